```python
import math
import jax
import jax.numpy as jnp
from jax import lax
import numpy as np

D_MODEL = 1024
BATCH = 4
SEQ = 4096
DEPTH = 1

N_META = 16
ROPE_THETA = 10000.0
EPS = 1e-6
A_HEAD_DIM = 64
A_HEADS = (D_MODEL // 2) // A_HEAD_DIM
A_WIDTH = A_HEADS * A_HEAD_DIM
KV_RANK = D_MODEL // 4
IDX_HEADS = 8
IDX_DIM = 64
INDEX_TOPK = 256
Q_BLOCK = 128
B_HEAD_DIM = 128
B_HEADS = (D_MODEL // 2) // B_HEAD_DIM
B_WIDTH = B_HEADS * B_HEAD_DIM
CONV_WIDTH = 4
CHUNK = 64
MIX_WIDTH = A_WIDTH + B_WIDTH
N_EXPERTS = 32
TOP_K = 4
D_FF = D_MODEL
SWIGLU_LIMIT = 7.0
SWIGLU_ALPHA = 1.702
MOE_BLOCK = 128
IN_SPLITS = (A_WIDTH, KV_RANK, IDX_HEADS * IDX_DIM, IDX_DIM, IDX_HEADS, 3 * B_WIDTH, B_WIDTH, B_HEADS, B_HEADS)
IN_COLS = A_WIDTH + KV_RANK + IDX_HEADS * IDX_DIM + IDX_DIM + IDX_HEADS + 4 * B_WIDTH + 2 * B_HEADS

kernel_name = "hybrid_dsa_gdn_moe_meta"


def rms_norm(x, w):
    xf = x.astype(jnp.float32)
    y = xf * lax.rsqrt(jnp.mean(xf * xf, axis=-1, keepdims=True) + EPS)
    return (y * w.astype(jnp.float32)).astype(x.dtype)


def l2_normalize(x):
    return x * lax.rsqrt(jnp.sum(x * x, axis=-1, keepdims=True) + EPS)


def rope_tables(n_pos, dim):
    inv_freq = ROPE_THETA ** (-jnp.arange(0, dim, 2, dtype=jnp.float32) / dim)
    ang = jnp.arange(n_pos, dtype=jnp.float32)[:, None] * inv_freq[None, :]
    return jnp.cos(ang), jnp.sin(ang)


def apply_rope(x, cos, sin):
    xf = x.astype(jnp.float32)
    x1, x2 = jnp.split(xf, 2, axis=-1)
    c = cos[:, None, :]
    s = sin[:, None, :]
    return jnp.concatenate([x1 * c - x2 * s, x2 * c + x1 * s], axis=-1).astype(x.dtype)


def sparse_index_attention(q, k, v, q_idx, k_idx, w_idx):
    B, T, H, hd = q.shape
    S = T - N_META
    scale = hd ** -0.5
    topk = min(INDEX_TOPK, S // 4)
    q_m, q_r = q[:, :N_META], q[:, N_META:]
    k_m, k_r = k[:, :N_META], k[:, N_META:]
    v_m, v_r = v[:, :N_META], v[:, N_META:]
    qi_r = q_idx[:, N_META:]
    ki_r = k_idx[:, N_META:]
    wi_r = w_idx[:, N_META:]
    causal_m = jnp.tril(jnp.ones((N_META, N_META), bool))
    s_mm = jnp.einsum('bqhd,bkhd->bhqk', q_m, k_m).astype(jnp.float32) * scale
    p_mm = jax.nn.softmax(jnp.where(causal_m, s_mm, -jnp.inf), axis=-1).astype(v.dtype)
    o_m = jnp.einsum('bhqk,bkhd->bqhd', p_mm, v_m)
    key_pos = jnp.arange(S)
    take = jax.vmap(lambda arr, idx: arr[idx])

    def query_block(j):
        start = j * Q_BLOCK
        qb = lax.dynamic_slice_in_dim(q_r, start, Q_BLOCK, axis=1)
        qib = lax.dynamic_slice_in_dim(qi_r, start, Q_BLOCK, axis=1)
        wb = lax.dynamic_slice_in_dim(wi_r, start, Q_BLOCK, axis=1)
        t_pos = start + jnp.arange(Q_BLOCK)
        rel = jax.nn.relu(jnp.einsum('bqhd,bsd->bqhs', qib, ki_r))
        iscore = jnp.einsum('bqhs,bqh->bqs', rel, wb).astype(jnp.float32)
        admissible = key_pos[None, :] <= t_pos[:, None]
        iscore = jnp.where(admissible[None], iscore, -jnp.inf)
        _, idx = lax.top_k(iscore, topk)
        valid = idx <= t_pos[None, :, None]
        kg = take(k_r, idx)
        vg = take(v_r, idx)
        s_sel = jnp.einsum('bqhd,bqkhd->bhqk', qb, kg).astype(jnp.float32) * scale
        s_sel = jnp.where(valid[:, None], s_sel, -jnp.inf)
        s_met = jnp.einsum('bqhd,bmhd->bhqm', qb, k_m).astype(jnp.float32) * scale
        p = jax.nn.softmax(jnp.concatenate([s_met, s_sel], axis=-1), axis=-1).astype(v.dtype)
        return (jnp.einsum('bhqm,bmhd->bqhd', p[..., :N_META], v_m)
                + jnp.einsum('bhqk,bqkhd->bqhd', p[..., N_META:], vg))

    o_r = lax.map(query_block, jnp.arange(S // Q_BLOCK))
    o_r = jnp.moveaxis(o_r, 0, 1).reshape(B, S, H, hd)
    return jnp.concatenate([o_m, o_r], axis=1)


def causal_depthwise_conv(x, w):
    C = x.shape[-1]
    return lax.conv_general_dilated(
        x, w[:, None, :].astype(x.dtype), window_strides=(1,),
        padding=((CONV_WIDTH - 1, 0),), dimension_numbers=('NWC', 'WIO', 'NWC'),
        feature_group_count=C)


def chunk_gated_delta_rule(q, k, v, g, beta):
    B, H, L, dk = q.shape
    dv = v.shape[-1]
    n = L // CHUNK
    q = q * dk ** -0.5
    k_beta = k * beta[..., None]
    v_beta = v * beta[..., None]

    def to_chunks(t):
        return t.reshape(B, H, n, CHUNK, t.shape[-1])

    q, k, k_beta, v_beta = to_chunks(q), to_chunks(k), to_chunks(k_beta), to_chunks(v_beta)
    g = jnp.cumsum(g.reshape(B, H, n, CHUNK), axis=-1)
    incl = jnp.tril(jnp.ones((CHUNK, CHUNK), bool))
    strict = jnp.tril(jnp.ones((CHUNK, CHUNK), bool), -1)
    decay = jnp.where(incl, jnp.exp(jnp.where(incl, g[..., :, None] - g[..., None, :], 0.0)), 0.0)
    a_mat = jnp.where(strict, jnp.einsum('bhncd,bhnsd->bhncs', k_beta, k) * decay, 0.0)
    eye = jnp.eye(CHUNK, dtype=jnp.float32)
    t_inv = lax.linalg.triangular_solve(a_mat + eye, jnp.broadcast_to(eye, a_mat.shape),
                                        left_side=True, lower=True, unit_diagonal=True)
    u = t_inv @ v_beta
    w = t_inv @ (k_beta * jnp.exp(g)[..., None])
    intra = jnp.where(incl, jnp.einsum('bhncd,bhnsd->bhncs', q, k) * decay, 0.0)

    def step(state, xs):
        q_c, k_c, u_c, w_c, g_c, intra_c = xs
        v_new = u_c - w_c @ state
        o_c = (q_c * jnp.exp(g_c)[..., None]) @ state + intra_c @ v_new
        g_last = g_c[..., -1:]
        state = state * jnp.exp(g_last)[..., None] + jnp.einsum(
            'bhcd,bhce->bhde', k_c * jnp.exp(g_last - g_c)[..., None], v_new)
        return state, o_c

    xs = tuple(jnp.moveaxis(t, 2, 0) for t in (q, k, u, w, g, intra))
    state0 = jnp.zeros((B, H, dk, dv), jnp.float32)
    _, o = lax.scan(step, state0, xs)
    return jnp.moveaxis(o, 0, 2).reshape(B, H, L, dv)


def gated_deltanet(qkv, z, beta_in, decay_in, conv_w, a_log, dt_bias, norm_w):
    B, T, _ = qkv.shape
    f32 = jnp.float32
    qkv = jax.nn.silu(causal_depthwise_conv(qkv, conv_w)).astype(f32)
    q, k, v = (t.reshape(B, T, B_HEADS, B_HEAD_DIM) for t in jnp.split(qkv, 3, axis=-1))
    q, k = l2_normalize(q), l2_normalize(k)
    beta = jax.nn.sigmoid(beta_in.astype(f32))
    g = -jnp.exp(a_log.astype(f32)) * jax.nn.softplus(decay_in.astype(f32) + dt_bias.astype(f32))
    n_pad = CHUNK - N_META

    def lead_pad(t):
        return jnp.pad(t, [(0, 0), (n_pad, 0)] + [(0, 0)] * (t.ndim - 2))

    q, k, v = (jnp.moveaxis(lead_pad(t), 1, 2) for t in (q, k, v))
    beta, g = (jnp.moveaxis(lead_pad(t), 1, 2) for t in (beta, g))
    o = chunk_gated_delta_rule(q, k, v, g, beta)
    o = jnp.moveaxis(o, 1, 2)[:, n_pad:]
    zf = z.astype(f32).reshape(B, T, B_HEADS, B_HEAD_DIM)
    o = rms_norm(o, norm_w) * jax.nn.silu(zf)
    return o.reshape(B, T, B_WIDTH).astype(z.dtype)


def moe_ffn(u, w_router, b_router, w_gate_up, b_gate_up, w_down, b_down):
    B, T, D = u.shape
    n_tok = B * T
    xt = u.reshape(n_tok, D)
    logits = (xt @ w_router + b_router).astype(jnp.float32)
    top_logit, top_e = lax.top_k(logits, TOP_K)
    gates = jax.nn.softmax(top_logit, axis=-1)
    n_assign = n_tok * TOP_K
    flat_e = top_e.reshape(-1)
    flat_tok = jnp.repeat(jnp.arange(n_tok, dtype=jnp.int32), TOP_K)
    flat_gate = gates.reshape(-1)
    order = jnp.argsort(flat_e)
    sorted_e = flat_e[order]
    counts = jnp.bincount(flat_e, length=N_EXPERTS)
    padded = (counts + MOE_BLOCK - 1) // MOE_BLOCK * MOE_BLOCK
    start = jnp.cumsum(counts) - counts
    pend = jnp.cumsum(padded)
    pstart = pend - padded
    dest = pstart[sorted_e] + jnp.arange(n_assign) - start[sorted_e]
    n_blocks = -(-(n_assign + N_EXPERTS * (MOE_BLOCK - 1)) // MOE_BLOCK)
    n_rows = n_blocks * MOE_BLOCK
    row_tok = jnp.full((n_rows,), n_tok, jnp.int32).at[dest].set(flat_tok[order])
    row_gate = jnp.zeros((n_rows,), jnp.float32).at[dest].set(flat_gate[order])
    block_e = jnp.minimum(jnp.searchsorted(pend, jnp.arange(n_blocks) * MOE_BLOCK, side='right'),
                          N_EXPERTS - 1)
    x_pad = jnp.concatenate([xt, jnp.zeros((1, D), xt.dtype)], axis=0)
    xb = x_pad[row_tok].reshape(n_blocks, MOE_BLOCK, D)

    def expert_block(args):
        xe, e = args
        gu = xe @ w_gate_up[e] + b_gate_up[e]
        gate = jnp.minimum(gu[:, 0::2], SWIGLU_LIMIT)
        up = jnp.clip(gu[:, 1::2], -SWIGLU_LIMIT, SWIGLU_LIMIT)
        glu = gate * jax.nn.sigmoid(gate * SWIGLU_ALPHA)
        return (glu * (up + 1.0)) @ w_down[e] + b_down[e]

    y = lax.map(expert_block, (xb, block_e)).reshape(n_rows, D)
    y = y * row_gate[:, None].astype(y.dtype)
    out = jnp.zeros((n_tok + 1, D), y.dtype).at[row_tok].add(y)[:n_tok]
    return out.reshape(B, T, D)


def hybrid_layer(h, norm_mix_w, w_in, q_norm_w, k_norm_w, kv_norm_w, w_kv_up, conv_w, a_log,
                 dt_bias, delta_norm_w, w_out, norm_ffn_w, w_router, b_router, w_gate_up,
                 b_gate_up, w_down, b_down):
    B, T, _ = h.shape
    u = rms_norm(h, norm_mix_w)
    points = np.cumsum(IN_SPLITS)[:-1].tolist()
    a_q, a_ckv, i_q, i_k, i_w, b_qkv, b_z, b_beta, b_a = jnp.split(u @ w_in, points, axis=-1)
    cos_a, sin_a = rope_tables(T, A_HEAD_DIM)
    cos_i, sin_i = rope_tables(T, IDX_DIM)
    q = apply_rope(rms_norm(a_q.reshape(B, T, A_HEADS, A_HEAD_DIM), q_norm_w), cos_a, sin_a)
    k, v = jnp.split(rms_norm(a_ckv, kv_norm_w) @ w_kv_up, 2, axis=-1)
    k = apply_rope(rms_norm(k.reshape(B, T, A_HEADS, A_HEAD_DIM), k_norm_w), cos_a, sin_a)
    v = v.reshape(B, T, A_HEADS, A_HEAD_DIM)
    q_idx = apply_rope(i_q.reshape(B, T, IDX_HEADS, IDX_DIM), cos_i, sin_i)
    k_idx = apply_rope(i_k[:, :, None, :], cos_i, sin_i)[:, :, 0]
    w_idx = i_w * (IDX_HEADS ** -0.5 * IDX_DIM ** -0.5)
    o_a = sparse_index_attention(q, k, v, q_idx, k_idx, w_idx).reshape(B, T, A_WIDTH)
    o_b = gated_deltanet(b_qkv, b_z, b_beta, b_a, conv_w, a_log, dt_bias, delta_norm_w)
    h = h + jnp.concatenate([o_a, o_b], axis=-1) @ w_out
    return h + moe_ffn(rms_norm(h, norm_ffn_w), w_router, b_router, w_gate_up, b_gate_up,
                       w_down, b_down)


def setup_inputs(seed: int = 0) -> dict:
    key = jax.random.key(seed)
    ks = jax.random.split(key, 20)
    f32 = jnp.float32

    def nrm(k, shape, scale):
        return jax.random.normal(k, shape, f32) * scale

    def gain(k, shape):
        return 1.0 + 0.02 * jax.random.normal(k, shape, f32)

    x = nrm(ks[0], (BATCH, SEQ, D_MODEL), 1.0)
    meta_tokens = nrm(ks[1], (N_META, D_MODEL), 1.0)
    norm_mix_w = gain(ks[2], (DEPTH, D_MODEL))
    w_in = nrm(ks[3], (DEPTH, D_MODEL, IN_COLS), D_MODEL ** -0.5)
    q_norm_w = gain(ks[4], (DEPTH, A_HEAD_DIM))
    k_norm_w = gain(ks[5], (DEPTH, A_HEAD_DIM))
    kv_norm_w = gain(ks[6], (DEPTH, KV_RANK))
    w_kv_up = nrm(ks[7], (DEPTH, KV_RANK, 2 * A_WIDTH), KV_RANK ** -0.5)
    conv_w = nrm(ks[8], (DEPTH, CONV_WIDTH, 3 * B_WIDTH), CONV_WIDTH ** -0.5)
    a_log = jnp.log(jax.random.uniform(ks[9], (DEPTH, B_HEADS), f32, 1.0, 16.0))
    dt = jnp.exp(jax.random.uniform(ks[10], (DEPTH, B_HEADS), f32, math.log(1e-3), math.log(0.1)))
    dt_bias = dt + jnp.log(-jnp.expm1(-dt))
    delta_norm_w = gain(ks[11], (DEPTH, B_HEAD_DIM))
    w_out = nrm(ks[12], (DEPTH, MIX_WIDTH, D_MODEL), MIX_WIDTH ** -0.5)
    norm_ffn_w = gain(ks[13], (DEPTH, D_MODEL))
    w_router = nrm(ks[14], (DEPTH, D_MODEL, N_EXPERTS), D_MODEL ** -0.5)
    b_router = nrm(ks[15], (DEPTH, N_EXPERTS), 0.01)
    w_gate_up = nrm(ks[16], (DEPTH, N_EXPERTS, D_MODEL, 2 * D_FF), D_MODEL ** -0.5)
    b_gate_up = nrm(ks[17], (DEPTH, N_EXPERTS, 2 * D_FF), 0.01)
    w_down = nrm(ks[18], (DEPTH, N_EXPERTS, D_FF, D_MODEL), D_FF ** -0.5)
    b_down = nrm(ks[19], (DEPTH, N_EXPERTS, D_MODEL), 0.01)
    return {"x": x, "meta_tokens": meta_tokens, "norm_mix_w": norm_mix_w, "w_in": w_in,
            "q_norm_w": q_norm_w, "k_norm_w": k_norm_w, "kv_norm_w": kv_norm_w,
            "w_kv_up": w_kv_up, "conv_w": conv_w, "a_log": a_log, "dt_bias": dt_bias,
            "delta_norm_w": delta_norm_w, "w_out": w_out, "norm_ffn_w": norm_ffn_w,
            "w_router": w_router, "b_router": b_router, "w_gate_up": w_gate_up,
            "b_gate_up": b_gate_up, "w_down": w_down, "b_down": b_down}


def reference(x, meta_tokens, norm_mix_w, w_in, q_norm_w, k_norm_w, kv_norm_w, w_kv_up, conv_w,
              a_log, dt_bias, delta_norm_w, w_out, norm_ffn_w, w_router, b_router, w_gate_up,
              b_gate_up, w_down, b_down):
    B = x.shape[0]
    meta = jnp.broadcast_to(meta_tokens.astype(x.dtype)[None], (B, N_META, x.shape[-1]))
    h = jnp.concatenate([meta, x], axis=1)
    for l in range(DEPTH):
        h = hybrid_layer(h, norm_mix_w[l], w_in[l], q_norm_w[l], k_norm_w[l], kv_norm_w[l],
                         w_kv_up[l], conv_w[l], a_log[l], dt_bias[l], delta_norm_w[l], w_out[l],
                         norm_ffn_w[l], w_router[l], b_router[l], w_gate_up[l], b_gate_up[l],
                         w_down[l], b_down[l])
    return h[:, N_META:]
```

```python
import functools
import math

import jax
import jax.numpy as jnp
from jax import lax
from jax.experimental import pallas as pl
from jax.experimental.pallas import tpu as pltpu

F32 = jnp.float32
BF16 = jnp.bfloat16
HIGHEST = lax.Precision.HIGHEST

D_MODEL = 1024
N_META = 16
ROPE_THETA = 10000.0
EPS = 1e-6
A_HEAD_DIM = 64
A_HEADS = 8
A_WIDTH = A_HEADS * A_HEAD_DIM
KV_RANK = 256
IDX_HEADS = 8
IDX_DIM = 64
INDEX_TOPK = 256
B_HEAD_DIM = 128
B_HEADS = 4
B_WIDTH = B_HEADS * B_HEAD_DIM
CONV_WIDTH = 4
CHUNK = 64
N_EXPERTS = 32
TOP_K = 4
D_FF = D_MODEL
SWIGLU_LIMIT = 7.0
SWIGLU_ALPHA = 1.702
IN_SPLITS = (A_WIDTH, KV_RANK, IDX_HEADS * IDX_DIM, IDX_DIM, IDX_HEADS, 3 * B_WIDTH, B_WIDTH,
             B_HEADS, B_HEADS)

LANES = 128

C_Q = 0
C_CKV = C_Q + A_WIDTH
C_IQ = C_CKV + KV_RANK
C_IK = C_IQ + IDX_HEADS * IDX_DIM
C_MISC = C_IK + LANES
C_QKV = C_MISC + LANES
C_Z = C_QKV + 3 * B_WIDTH
C_END = C_Z + B_WIDTH
MISC_W, MISC_BETA, MISC_DECAY = 0, IDX_HEADS, IDX_HEADS + B_HEADS


def _rope_partner(a):
    lane = lax.broadcasted_iota(jnp.int32, a.shape, 1)
    first_half = (lane % A_HEAD_DIM) < (A_HEAD_DIM // 2)
    return jnp.where(first_half, pltpu.roll(a, LANES - A_HEAD_DIM // 2, 1),
                     pltpu.roll(a, A_HEAD_DIM // 2, 1))


def _rope(a, cos, sin_signed):
    return a * cos + _rope_partner(a) * sin_signed


def _head_rms(a, head_mean, gain):
    msq = jnp.dot((a * a).astype(BF16), head_mean, preferred_element_type=F32)
    return a * lax.rsqrt(msq + EPS) * gain


def _in_proj_kernel(x_ref, nw_ref, w_ref, wkv_ref, qnw_ref, knw_ref, kvnw_ref, cos_ref, sin_ref,
                    hm_ref, q_ref, k_ref, v_ref, iq_ref, ik_ref, misc_ref, qkv_ref, z_ref):
    x = x_ref[...]
    u = x * lax.rsqrt(jnp.mean(x * x, axis=-1, keepdims=True) + EPS) * nw_ref[...]
    ub = u.astype(BF16)

    def proj(c0, c1):
        return jnp.dot(ub, w_ref[:, c0:c1], preferred_element_type=F32)

    cos = cos_ref[...]
    sin = sin_ref[...]
    hm = hm_ref[...]

    def rope_groups(a):
        return jnp.concatenate(
            [_rope(a[:, g * LANES:(g + 1) * LANES], cos, sin) for g in range(a.shape[1] // LANES)],
            axis=1)

    q = _head_rms(proj(C_Q, C_CKV), hm, qnw_ref[...])
    q_ref[...] = (rope_groups(q) * (A_HEAD_DIM ** -0.5)).astype(BF16)

    ckv = proj(C_CKV, C_IQ)
    ckv = ckv * lax.rsqrt(jnp.mean(ckv * ckv, axis=-1, keepdims=True) + EPS) * kvnw_ref[...]
    kv = jnp.dot(ckv.astype(BF16), wkv_ref[...], preferred_element_type=F32)
    k = _head_rms(kv[:, :A_WIDTH], hm, knw_ref[...])
    k_ref[...] = rope_groups(k).astype(BF16)
    v_ref[...] = kv[:, A_WIDTH:].astype(BF16)

    iq_ref[...] = rope_groups(proj(C_IQ, C_IK)).astype(BF16)
    ik_ref[...] = _rope(proj(C_IK, C_MISC), cos, sin).astype(BF16)

    lane = lax.broadcasted_iota(jnp.int32, (1, LANES), 1)
    w_scale = jnp.where(lane < IDX_HEADS, IDX_HEADS ** -0.5 * IDX_DIM ** -0.5, 1.0)
    misc_ref[...] = proj(C_MISC, C_QKV) * w_scale
    qkv_ref[...] = proj(C_QKV, C_Z)
    z_ref[...] = proj(C_Z, C_END)


def _in_proj(x2d, cos, sin, consts, tm):
    n = x2d.shape[0]
    n_pos_blocks = cos.shape[0] // tm
    nw, w_pack, wkv, qnw, knw, kvnw, hm = consts
    row = lambda i: (i, 0)
    fixed = lambda i: (0, 0)
    pos = lambda i: (i % n_pos_blocks, 0)
    out_widths = (A_WIDTH, A_WIDTH, A_WIDTH, IDX_HEADS * IDX_DIM, LANES, LANES, 3 * B_WIDTH, B_WIDTH)
    out_dtypes = (BF16, BF16, BF16, BF16, BF16, F32, F32, F32)
    return pl.pallas_call(
        _in_proj_kernel,
        grid=(n // tm,),
        in_specs=[
            pl.BlockSpec((tm, D_MODEL), row),
            pl.BlockSpec(nw.shape, fixed),
            pl.BlockSpec(w_pack.shape, fixed),
            pl.BlockSpec(wkv.shape, fixed),
            pl.BlockSpec(qnw.shape, fixed),
            pl.BlockSpec(knw.shape, fixed),
            pl.BlockSpec(kvnw.shape, fixed),
            pl.BlockSpec((tm, LANES), pos),
            pl.BlockSpec((tm, LANES), pos),
            pl.BlockSpec(hm.shape, fixed),
        ],
        out_specs=[pl.BlockSpec((tm, w), row) for w in out_widths],
        out_shape=[jax.ShapeDtypeStruct((n, w), dt) for w, dt in zip(out_widths, out_dtypes)],
        compiler_params=pltpu.CompilerParams(dimension_semantics=("parallel",)),
        name="in_proj",
    )(x2d, nw, w_pack, wkv, qnw, knw, kvnw, cos, sin, hm)


def _pack_in_proj_weights(norm_mix_w, w_in, q_norm_w, k_norm_w, kv_norm_w, w_kv_up):
    points = []
    acc = 0
    for s in IN_SPLITS:
        points.append((acc, acc + s))
        acc += s
    a_q, a_ckv, i_q, i_k, i_w, b_qkv, b_z, b_beta, b_a = (w_in[:, a:b] for a, b in points)
    misc = jnp.concatenate(
        [i_w, b_beta, b_a, jnp.zeros((D_MODEL, LANES - IDX_HEADS - 2 * B_HEADS), w_in.dtype)], axis=1)
    w_pack = jnp.concatenate([a_q, a_ckv, i_q, i_k, i_k, misc, b_qkv, b_z], axis=1).astype(BF16)
    head_mean = jnp.kron(jnp.eye(A_HEADS, dtype=F32),
                         jnp.full((A_HEAD_DIM, A_HEAD_DIM), 1.0 / A_HEAD_DIM, F32)).astype(BF16)
    return (norm_mix_w.reshape(1, D_MODEL), w_pack, w_kv_up.astype(BF16),
            jnp.tile(q_norm_w, A_HEADS).reshape(1, A_WIDTH),
            jnp.tile(k_norm_w, A_HEADS).reshape(1, A_WIDTH),
            kv_norm_w.reshape(1, KV_RANK), head_mean)


def _rope_tables(n_pos):
    half = A_HEAD_DIM // 2
    inv_freq = ROPE_THETA ** (-jnp.arange(0, A_HEAD_DIM, 2, dtype=F32) / A_HEAD_DIM)
    ang = jnp.arange(n_pos, dtype=F32)[:, None] * inv_freq[None, :]
    cos, sin = jnp.cos(ang), jnp.sin(ang)
    cos128 = jnp.tile(cos, (1, LANES // half))
    sin128 = jnp.tile(jnp.concatenate([-sin, sin], axis=1), (1, LANES // A_HEAD_DIM))
    return cos128, sin128


NEG_INF = float("-inf")
F32_MAX = float(jnp.finfo(jnp.float32).max)
INT_MIN = -2 ** 31


def _repeat_lanes(a, n):
    return a if n == 1 else jnp.concatenate([a] * n, axis=1)


def _fold_lanes(a):
    out = a[:, :LANES]
    for t in range(1, a.shape[1] // LANES):
        out = out + a[:, t * LANES:(t + 1) * LANES]
    return out


def _attn_kernel(q_ref, k_ref, v_ref, km_ref, vm_ref, iq_ref, ik_ref, misc_ref, o_ref,
                 isc_ref, lhs_ref, wb_ref, qm_ref, m_ref, l_ref, acc_ref, *, tq, kc, topk, pos_bits):
    j = pl.program_id(1)
    n_kc = lax.div((j + 1) * tq + (kc - 1), kc)
    rep = kc // LANES
    lane = lax.broadcasted_iota(jnp.int32, (1, LANES), 1)
    lo_half = lane < A_HEAD_DIM
    n_pairs = A_WIDTH // LANES
    nt = (((1,), (1,)), ((), ()))

    iq = iq_ref[0]
    q = q_ref[0]
    misc = misc_ref[0]
    zero = jnp.zeros((), BF16)
    for p in range(n_pairs):
        blk = iq[:, p * LANES:(p + 1) * LANES]
        lhs_ref[(2 * p) * tq:(2 * p + 1) * tq, :] = jnp.where(lo_half, blk, zero)
        lhs_ref[(2 * p + 1) * tq:(2 * p + 2) * tq, :] = jnp.where(lo_half, zero, blk)
        qb = q[:, p * LANES:(p + 1) * LANES]
        qm_ref[2 * p] = jnp.where(lo_half, qb, zero)
        qm_ref[2 * p + 1] = jnp.where(lo_half, zero, qb)
    for h in range(IDX_HEADS):
        wb_ref[h * tq:(h + 1) * tq, :] = jnp.broadcast_to(misc[:, MISC_W + h:MISC_W + h + 1], (tq, LANES))

    row = j * tq + lax.broadcasted_iota(jnp.int32, (tq, 1), 0)

    sub = 2 * LANES
    def idx_body(c, carry):
        k0 = pl.multiple_of(c * kc, kc)
        for t in range(kc // sub):
            ik_t = ik_ref[0, pl.ds(k0 + t * sub, sub), :]
            r = lax.dot_general(lhs_ref[...], ik_t, nt, preferred_element_type=F32)
            s = None
            for h in range(IDX_HEADS):
                w2 = _repeat_lanes(wb_ref[h * tq:(h + 1) * tq, :], sub // LANES)
                term = jnp.maximum(r[h * tq:(h + 1) * tq, :], 0.0) * w2
                s = term if s is None else s + term
            pos = k0 + t * sub + lax.broadcasted_iota(jnp.int32, (1, sub), 1)
            isc_ref[c, :, t * sub:(t + 1) * sub] = jnp.where(pos <= row, s, NEG_INF)
        return carry
    lax.fori_loop(0, n_kc, idx_body, 0)

    def count(pred):
        def body(c, acc):
            x = isc_ref[c]
            return acc + _fold_lanes(jnp.where(pred(x, c), 1.0, 0.0))
        acc = lax.fori_loop(0, n_kc, body, jnp.zeros((tq, LANES), F32))
        return jnp.sum(acc, axis=-1, keepdims=True)

    def key_to_float(u):
        key = u ^ jnp.int32(INT_MIN)
        bits = jnp.where(key >= 0, key, key ^ jnp.int32(0x7FFFFFFF))
        return lax.bitcast_convert_type(bits, F32)

    def bit_body(i, u):
        u2 = u | lax.shift_left(jnp.int32(1), 31 - i)
        cand = _repeat_lanes(key_to_float(u2), rep)
        n_ge = count(lambda x, c: x >= cand)
        return jnp.where(n_ge >= topk, u2, u)
    u = lax.fori_loop(0, 32, bit_body, jnp.zeros((tq, LANES), jnp.int32))
    few = (u >= 0) & (u < 0x00800000)
    tau = jnp.where(few, -F32_MAX, key_to_float(u))
    tau_rep = _repeat_lanes(tau, rep)

    n_ge = count(lambda x, c: x >= tau_rep)

    @pl.when(jnp.max(n_ge) > topk)
    def _():
        n_gt = count(lambda x, c: x > tau_rep)
        need = topk - n_gt
        def chunk_pos(c):
            return c * kc + lax.broadcasted_iota(jnp.int32, (1, kc), 1)
        def pos_body(i, cut):
            cut2 = cut | lax.shift_left(jnp.int32(1), pos_bits - 1 - i)
            cut2_rep = _repeat_lanes(cut2, rep)
            ties_before = count(lambda x, c: (x == tau_rep) & (chunk_pos(c) < cut2_rep))
            return jnp.where(ties_before < need, cut2, cut)
        cut = lax.fori_loop(0, pos_bits, pos_body, jnp.zeros((tq, LANES), jnp.int32))
        cut_rep = _repeat_lanes(cut, rep)
        def drop_body(c, carry):
            x = isc_ref[c]
            isc_ref[c] = jnp.where((x == tau_rep) & (chunk_pos(c) > cut_rep), NEG_INF, x)
            return carry
        lax.fori_loop(0, n_kc, drop_body, 0)

    meta_bias = jnp.where(lane < N_META, 0.0, NEG_INF)
    for p in range(n_pairs):
        kmp = km_ref[:, p * LANES:(p + 1) * LANES]
        vmp = vm_ref[:, p * LANES:(p + 1) * LANES]
        for half in range(2):
            hd = 2 * p + half
            s = lax.dot_general(qm_ref[hd], kmp, nt, preferred_element_type=F32) + meta_bias
            m = jnp.max(s, axis=-1, keepdims=True)
            e = jnp.exp(s - m)
            m_ref[hd] = jnp.broadcast_to(m, (tq, LANES))
            l_ref[hd] = jnp.broadcast_to(jnp.sum(e, axis=-1, keepdims=True), (tq, LANES))
            acc_ref[hd] = jnp.dot(e.astype(BF16), vmp, preferred_element_type=F32)

    def att_body(c, carry):
        k0 = pl.multiple_of(c * kc, kc)
        bias = jnp.where(isc_ref[c] >= tau_rep, 0.0, NEG_INF)
        for p in range(n_pairs):
            kp = k_ref[0, pl.ds(k0, kc), p * LANES:(p + 1) * LANES]
            vp = v_ref[0, pl.ds(k0, kc), p * LANES:(p + 1) * LANES]
            for half in range(2):
                hd = 2 * p + half
                s = lax.dot_general(qm_ref[hd], kp, nt, preferred_element_type=F32) + bias
                m_old = m_ref[hd]
                m_new = jnp.maximum(m_old, jnp.max(s, axis=-1, keepdims=True))
                alpha = jnp.exp(m_old - m_new)
                e = jnp.exp(s - _repeat_lanes(m_new, rep))
                l_ref[hd] = alpha * l_ref[hd] + jnp.sum(e, axis=-1, keepdims=True)
                acc_ref[hd] = alpha * acc_ref[hd] + jnp.dot(e.astype(BF16), vp,
                                                            preferred_element_type=F32)
                m_ref[hd] = m_new
        return carry
    lax.fori_loop(0, n_kc, att_body, 0)

    for p in range(n_pairs):
        o_lo = acc_ref[2 * p] / l_ref[2 * p]
        o_hi = acc_ref[2 * p + 1] / l_ref[2 * p + 1]
        o_ref[0, :, p * LANES:(p + 1) * LANES] = jnp.where(lo_half, o_lo, o_hi).astype(BF16)


def _sparse_attention(q, k, v, km, vm, iq, ik, misc, *, tq, kc, topk):
    b, s, _ = q.shape
    n_heads = A_HEADS
    kernel = functools.partial(_attn_kernel, tq=tq, kc=kc, topk=topk, pos_bits=int(math.log2(s)))
    qblk = lambda bi, j: (bi, j, 0)
    full = lambda bi, j: (bi, 0, 0)
    fixed = lambda bi, j: (0, 0)
    return pl.pallas_call(
        kernel,
        grid=(b, s // tq),
        in_specs=[
            pl.BlockSpec((1, tq, A_WIDTH), qblk),
            pl.BlockSpec((1, s, A_WIDTH), full),
            pl.BlockSpec((1, s, A_WIDTH), full),
            pl.BlockSpec(km.shape, fixed),
            pl.BlockSpec(vm.shape, fixed),
            pl.BlockSpec((1, tq, IDX_HEADS * IDX_DIM), qblk),
            pl.BlockSpec((1, s, LANES), full),
            pl.BlockSpec((1, tq, LANES), qblk),
        ],
        out_specs=pl.BlockSpec((1, tq, A_WIDTH), qblk),
        out_shape=jax.ShapeDtypeStruct((b, s, A_WIDTH), BF16),
        scratch_shapes=[
            pltpu.VMEM((s // kc, tq, kc), F32),
            pltpu.VMEM((IDX_HEADS * tq, LANES), BF16),
            pltpu.VMEM((IDX_HEADS * tq, LANES), F32),
            pltpu.VMEM((n_heads, tq, LANES), BF16),
            pltpu.VMEM((n_heads, tq, LANES), F32),
            pltpu.VMEM((n_heads, tq, LANES), F32),
            pltpu.VMEM((n_heads, tq, LANES), F32),
        ],
        compiler_params=pltpu.CompilerParams(dimension_semantics=("parallel", "arbitrary")),
        name="sparse_attention",
    )(q, k, v, km, vm, iq, ik, misc)


HALO = 8


def _softplus(x):
    return jnp.maximum(x, 0.0) + jnp.log1p(jnp.exp(-jnp.abs(x)))


def _bdot(a, b):
    return jnp.dot(a.astype(BF16), b.astype(BF16), preferred_element_type=F32)


def _hdot(a, b):
    return jnp.dot(a, b, precision=HIGHEST, preferred_element_type=F32)


def _delta_kernel(qkv_ref, z_ref, misc_ref, qkvm_ref, miscm_ref, convw_ref, alog_ref, dtb_ref,
                  nw_ref, o_ref, state_ref, halo_ref):
    c = pl.program_id(1)
    n_pad = CHUNK - N_META
    nt = (((1,), (1,)), ((), ()))
    tn = (((0,), (0,)), ((), ()))

    @pl.when(c == 0)
    def _():
        state_ref[...] = jnp.zeros_like(state_ref)
        halo_ref[...] = jnp.zeros_like(halo_ref)

    is_meta = c == 0
    xin = jnp.where(is_meta, qkvm_ref[...], qkv_ref[0])
    misc = jnp.where(is_meta, miscm_ref[...], misc_ref[0])

    xcat = jnp.concatenate([halo_ref[...], xin], axis=0)
    halo_ref[...] = xin[CHUNK - HALO:, :]
    conv = None
    for tap in range(CONV_WIDTH):
        off = HALO - (CONV_WIDTH - 1) + tap
        term = xcat[off:off + CHUNK, :] * convw_ref[tap:tap + 1, :]
        conv = term if conv is None else conv + term
    xc = conv * jax.nn.sigmoid(conv)

    row = lax.broadcasted_iota(jnp.int32, (CHUNK, 1), 0)
    neutral = jnp.logical_and(is_meta, row < n_pad)
    beta_all = jnp.where(neutral, 0.0, jax.nn.sigmoid(misc))
    g_all = jnp.where(neutral, 0.0, -jnp.exp(alog_ref[...]) * _softplus(misc + dtb_ref[...]))

    ri = lax.broadcasted_iota(jnp.int32, (CHUNK, CHUNK), 0)
    ci = lax.broadcasted_iota(jnp.int32, (CHUNK, CHUNK), 1)
    incl = ri >= ci
    strict = ri > ci
    eye = (ri == ci).astype(F32)
    tri = incl.astype(F32)
    ones = jnp.ones((CHUNK, CHUNK), F32)

    for h in range(B_HEADS):
        sl = slice(h * B_HEAD_DIM, (h + 1) * B_HEAD_DIM)
        qh = xc[:, sl]
        kh = xc[:, B_WIDTH + h * B_HEAD_DIM:B_WIDTH + (h + 1) * B_HEAD_DIM]
        vh = xc[:, 2 * B_WIDTH + h * B_HEAD_DIM:2 * B_WIDTH + (h + 1) * B_HEAD_DIM]
        qn = qh * lax.rsqrt(jnp.sum(qh * qh, axis=-1, keepdims=True) + EPS) * (B_HEAD_DIM ** -0.5)
        kn = kh * lax.rsqrt(jnp.sum(kh * kh, axis=-1, keepdims=True) + EPS)
        beta = jnp.broadcast_to(beta_all[:, MISC_BETA + h:MISC_BETA + h + 1], (CHUNK, B_HEAD_DIM))
        g = jnp.broadcast_to(g_all[:, MISC_DECAY + h:MISC_DECAY + h + 1], (CHUNK, B_HEAD_DIM))

        gc = _hdot(tri, g)
        g_col = gc[:, :CHUNK]
        g_row = _hdot(ones, eye * g_col)
        decay = jnp.where(incl, jnp.exp(jnp.where(incl, g_col - g_row, 0.0)), 0.0)

        kb = kn * beta
        vb = vh * beta
        a_mat = jnp.where(strict, lax.dot_general(kb.astype(BF16), kn.astype(BF16), nt,
                                                  preferred_element_type=F32) * decay, 0.0)
        pw = -a_mat
        t_inv = eye + pw
        for _ in range(int(math.log2(CHUNK)) - 1):
            pw = _hdot(pw, pw)
            t_inv = t_inv + _hdot(t_inv, pw)
        u = _bdot(t_inv, vb)
        w = _bdot(t_inv, kb * jnp.exp(gc))
        intra = jnp.where(incl, lax.dot_general(qn.astype(BF16), kn.astype(BF16), nt,
                                                preferred_element_type=F32) * decay, 0.0)

        state = state_ref[h]
        v_new = u - _bdot(w, state)
        o = _bdot(qn * jnp.exp(gc), state) + _bdot(intra, v_new)
        g_last = gc[CHUNK - 1:CHUNK, :]
        kd = kn * jnp.exp(g_last - gc)
        state_ref[h] = state * jnp.exp(g_last) + lax.dot_general(
            kd.astype(BF16), v_new.astype(BF16), tn, preferred_element_type=F32)

        @pl.when(c > 0)
        def _():
            y = o * lax.rsqrt(jnp.mean(o * o, axis=-1, keepdims=True) + EPS) * nw_ref[...]
            zh = z_ref[0, :, sl]
            o_ref[0, :, sl] = (y * (zh * jax.nn.sigmoid(zh))).astype(BF16)


def _gated_deltanet(qkv, z, misc, qkv_m, misc_m, conv_w, a_log, dt_bias, norm_w):
    b, s, _ = qkv.shape
    n_chunks = s // CHUNK + 1
    blk = lambda bi, c: (bi, jnp.maximum(c - 1, 0), 0)
    fixed = lambda bi, c: (0, 0)
    lane_vec = lambda v: jnp.zeros((1, LANES), F32).at[0, MISC_DECAY:MISC_DECAY + B_HEADS].set(v)
    return pl.pallas_call(
        _delta_kernel,
        grid=(b, n_chunks),
        in_specs=[
            pl.BlockSpec((1, CHUNK, 3 * B_WIDTH), blk),
            pl.BlockSpec((1, CHUNK, B_WIDTH), blk),
            pl.BlockSpec((1, CHUNK, LANES), blk),
            pl.BlockSpec((CHUNK, 3 * B_WIDTH), fixed),
            pl.BlockSpec((CHUNK, LANES), fixed),
            pl.BlockSpec((CONV_WIDTH, 3 * B_WIDTH), fixed),
            pl.BlockSpec((1, LANES), fixed),
            pl.BlockSpec((1, LANES), fixed),
            pl.BlockSpec((1, B_HEAD_DIM), fixed),
        ],
        out_specs=pl.BlockSpec((1, CHUNK, B_WIDTH), blk),
        out_shape=jax.ShapeDtypeStruct((b, s, B_WIDTH), BF16),
        scratch_shapes=[
            pltpu.VMEM((B_HEADS, B_HEAD_DIM, B_HEAD_DIM), F32),
            pltpu.VMEM((HALO, 3 * B_WIDTH), F32),
        ],
        compiler_params=pltpu.CompilerParams(dimension_semantics=("parallel", "arbitrary")),
        name="gated_deltanet",
    )(qkv, z, misc, qkv_m, misc_m, conv_w, lane_vec(a_log), lane_vec(dt_bias),
      norm_w.reshape(1, B_HEAD_DIM))


def _out_router_kernel(oa_ref, ob_ref, x_ref, wo_ref, nw_ref, wr_ref, br_ref,
                       h_ref, xn_ref, gate_ref, eid_ref):
    mix = (jnp.dot(oa_ref[...], wo_ref[:A_WIDTH, :], preferred_element_type=F32)
           + jnp.dot(ob_ref[...], wo_ref[A_WIDTH:, :], preferred_element_type=F32))
    h = x_ref[...] + mix
    h_ref[...] = h
    xn = h * lax.rsqrt(jnp.mean(h * h, axis=-1, keepdims=True) + EPS) * nw_ref[...]
    xn_ref[...] = xn
    logits = _hdot(xn, wr_ref[...]) + br_ref[...]
    tm = logits.shape[0]
    lane = lax.broadcasted_iota(jnp.int32, (tm, LANES), 1)
    lane_f = lane.astype(F32)
    work = logits
    vals, idxs = [], []
    for _ in range(TOP_K):
        m = jnp.max(work, axis=-1, keepdims=True)
        idx = jnp.min(jnp.where(work == m, lane_f, float(LANES)), axis=-1, keepdims=True)
        vals.append(m)
        idxs.append(idx)
        work = jnp.where(lane_f == idx, NEG_INF, work)
    exps = [jnp.exp(v - vals[0]) for v in vals]
    denom = exps[0]
    for e in exps[1:]:
        denom = denom + e
    gates = jnp.zeros((tm, LANES), F32)
    eids = jnp.zeros((tm, LANES), F32)
    for kk in range(TOP_K):
        gates = jnp.where(lane == kk, exps[kk] / denom, gates)
        eids = jnp.where(lane == kk, idxs[kk], eids)
    gate_ref[...] = gates
    eid_ref[...] = eids.astype(jnp.int32)


def _out_router(o_a, o_b, x2d, w_out, norm_w, w_router, b_router, tm):
    n = x2d.shape[0]
    row = lambda i: (i, 0)
    fixed = lambda i: (0, 0)
    wr = jnp.zeros((D_MODEL, LANES), F32).at[:, :N_EXPERTS].set(w_router)
    br = jnp.full((1, LANES), NEG_INF, F32).at[0, :N_EXPERTS].set(b_router)
    return pl.pallas_call(
        _out_router_kernel,
        grid=(n // tm,),
        in_specs=[
            pl.BlockSpec((tm, A_WIDTH), row),
            pl.BlockSpec((tm, B_WIDTH), row),
            pl.BlockSpec((tm, D_MODEL), row),
            pl.BlockSpec((A_WIDTH + B_WIDTH, D_MODEL), fixed),
            pl.BlockSpec((1, D_MODEL), fixed),
            pl.BlockSpec((D_MODEL, LANES), fixed),
            pl.BlockSpec((1, LANES), fixed),
        ],
        out_specs=[pl.BlockSpec((tm, D_MODEL), row), pl.BlockSpec((tm, D_MODEL), row),
                   pl.BlockSpec((tm, LANES), row), pl.BlockSpec((tm, LANES), row)],
        out_shape=[jax.ShapeDtypeStruct((n, D_MODEL), F32), jax.ShapeDtypeStruct((n, D_MODEL), F32),
                   jax.ShapeDtypeStruct((n, LANES), F32), jax.ShapeDtypeStruct((n, LANES), jnp.int32)],
        compiler_params=pltpu.CompilerParams(dimension_semantics=("parallel",)),
        name="out_proj_router",
    )(o_a, o_b, x2d, w_out.astype(BF16), norm_w.reshape(1, D_MODEL), wr, br)


MOE_BM = 256
MOE_BF = 512


def _route(eid, bm):
    n_tok = eid.shape[0]
    n_assign = n_tok * TOP_K
    flat_e = eid.T.reshape(-1)
    order = jnp.argsort(flat_e).astype(jnp.int32)
    counts = jnp.bincount(flat_e, length=N_EXPERTS).astype(jnp.int32)
    padded = (counts + bm - 1) // bm * bm
    start = jnp.cumsum(counts) - counts
    pend = jnp.cumsum(padded)
    pstart = pend - padded
    n_blocks = -(-(n_assign + N_EXPERTS * (bm - 1)) // bm)
    blk = jnp.arange(n_blocks, dtype=jnp.int32)
    block_e = jnp.minimum(jnp.searchsorted(pend, blk * bm, side='right'), N_EXPERTS - 1).astype(jnp.int32)
    n_valid = (pend[-1] // bm).astype(jnp.int32)
    r = jnp.arange(n_blocks * bm, dtype=jnp.int32)
    e_r = block_e[r // bm]
    off = r - pstart[e_r]
    valid = off < counts[e_r]
    a = order[jnp.clip(start[e_r] + off, 0, n_assign - 1)]
    row_tok = jnp.where(valid, a % n_tok, 0)
    row_dst = jnp.where(valid, a, 0)
    block_rows = jnp.clip(counts[block_e] - (blk * bm - pstart[block_e]), 0, bm).astype(jnp.int32)
    return (block_e, n_valid.reshape(1), block_rows, row_tok.reshape(n_blocks, 1, bm),
            row_dst.reshape(n_blocks, 1, bm))


def _moe_kernel(be_ref, nv_ref, rows_ref, tok_next_ref, tok_cur_ref, dst_prev_ref, dst_ref, xn_hbm,
                wg_ref, wu_ref, wd_ref, bg_ref, bu_ref, bd_ref, y_hbm, xbuf, ybuf, gsem, ssem, *, bm):
    i = pl.program_id(0)
    n_valid = nv_ref[0]
    slot = i % 2

    def for_valid_rows(blk, fn):
        def body(r, carry):
            fn(r)
            return carry
        lax.fori_loop(0, rows_ref[blk], body, 0)

    def gather_copy(tok_ref, s, r):
        return pltpu.make_async_copy(xn_hbm.at[pl.ds(tok_ref[0, 0, r], 1), :],
                                     xbuf.at[s, pl.ds(r, 1), :], gsem.at[s])

    def scatter_copy(d_ref, s, r):
        return pltpu.make_async_copy(ybuf.at[s, pl.ds(r, 1), :],
                                     y_hbm.at[pl.ds(d_ref[0, 0, r], 1), :], ssem.at[s])

    def for_rows(fn):
        def body(r, carry):
            fn(r)
            return carry
        lax.fori_loop(0, bm, body, 0, unroll=8)

    @pl.when(jnp.logical_and(i == 0, n_valid > 0))
    def _():
        for_rows(lambda r: gather_copy(tok_cur_ref, 0, r).start())

    @pl.when(i + 1 < n_valid)
    def _():
        for_rows(lambda r: gather_copy(tok_next_ref, 1 - slot, r).start())

    @pl.when(i < n_valid)
    def _():
        for_rows(lambda r: gather_copy(tok_cur_ref, slot, r).wait())
        x = xbuf[slot].astype(BF16)
        y = None
        for f0 in range(0, D_FF, MOE_BF):
            g = jnp.dot(x, wg_ref[0, :, f0:f0 + MOE_BF], preferred_element_type=F32) + bg_ref[0, :, f0:f0 + MOE_BF]
            u = jnp.dot(x, wu_ref[0, :, f0:f0 + MOE_BF], preferred_element_type=F32) + bu_ref[0, :, f0:f0 + MOE_BF]
            gate = jnp.minimum(g, SWIGLU_LIMIT)
            up = jnp.clip(u, -SWIGLU_LIMIT, SWIGLU_LIMIT)
            t = gate * jax.nn.sigmoid(gate * SWIGLU_ALPHA) * (up + 1.0)
            part = jnp.dot(t.astype(BF16), wd_ref[0, f0:f0 + MOE_BF, :], preferred_element_type=F32)
            y = part if y is None else y + part
        ybuf[slot] = y + bd_ref[0]

        @pl.when(i >= 1)
        def _():
            for_valid_rows(i - 1, lambda r: scatter_copy(dst_prev_ref, 1 - slot, r).wait())

        for_valid_rows(i, lambda r: scatter_copy(dst_ref, slot, r).start())

        @pl.when(i == n_valid - 1)
        def _():
            for_valid_rows(i, lambda r: scatter_copy(dst_ref, slot, r).wait())


def _moe_experts(xn, block_e, n_valid, block_rows, row_tok, row_dst, w_gate, w_up, w_down, b_gate,
                 b_up, b_down, bm):
    n_tok = xn.shape[0]
    n_blocks = block_e.shape[0]
    n_out = n_tok * TOP_K
    nxt = lambda i, be, nv, br: (jnp.minimum(i + 1, n_blocks - 1), 0, 0)
    cur = lambda i, be, nv, br: (i, 0, 0)
    prv = lambda i, be, nv, br: (jnp.maximum(i - 1, 0), 0, 0)
    wsel = lambda i, be, nv, br: (be[i], 0, 0)
    grid_spec = pltpu.PrefetchScalarGridSpec(
        num_scalar_prefetch=3,
        grid=(n_blocks,),
        in_specs=[
            pl.BlockSpec((1, 1, bm), nxt, memory_space=pltpu.SMEM),
            pl.BlockSpec((1, 1, bm), cur, memory_space=pltpu.SMEM),
            pl.BlockSpec((1, 1, bm), prv, memory_space=pltpu.SMEM),
            pl.BlockSpec((1, 1, bm), cur, memory_space=pltpu.SMEM),
            pl.BlockSpec(memory_space=pl.ANY),
            pl.BlockSpec((1, D_MODEL, D_FF), wsel),
            pl.BlockSpec((1, D_MODEL, D_FF), wsel),
            pl.BlockSpec((1, D_FF, D_MODEL), wsel),
            pl.BlockSpec((1, 1, D_FF), wsel),
            pl.BlockSpec((1, 1, D_FF), wsel),
            pl.BlockSpec((1, 1, D_MODEL), wsel),
        ],
        out_specs=pl.BlockSpec(memory_space=pl.ANY),
        scratch_shapes=[
            pltpu.VMEM((2, bm, D_MODEL), F32),
            pltpu.VMEM((2, bm, D_MODEL), F32),
            pltpu.SemaphoreType.DMA((2,)),
            pltpu.SemaphoreType.DMA((2,)),
        ],
    )
    return pl.pallas_call(
        functools.partial(_moe_kernel, bm=bm),
        grid_spec=grid_spec,
        out_shape=jax.ShapeDtypeStruct((n_out, D_MODEL), F32),
        compiler_params=pltpu.CompilerParams(dimension_semantics=("arbitrary",)),
        name="moe_experts",
    )(block_e, n_valid, block_rows, row_tok, row_tok, row_dst, row_dst, xn, w_gate, w_up, w_down,
      b_gate, b_up, b_down)


def _combine_kernel(h_ref, gate_ref, y0_ref, y1_ref, y2_ref, y3_ref, o_ref):
    gates = gate_ref[...]
    out = h_ref[...]
    for kk, y_ref in enumerate((y0_ref, y1_ref, y2_ref, y3_ref)):
        out = out + gates[:, kk:kk + 1] * y_ref[...]
    o_ref[...] = out


def _combine(h, gates, y, tm):
    n_tok = h.shape[0]
    nt = n_tok // tm
    row = lambda i: (i, 0)
    return pl.pallas_call(
        _combine_kernel,
        grid=(nt,),
        in_specs=[pl.BlockSpec((tm, D_MODEL), row), pl.BlockSpec((tm, LANES), row)]
        + [pl.BlockSpec((tm, D_MODEL), functools.partial(lambda kk, i: (kk * nt + i, 0), kk))
           for kk in range(TOP_K)],
        out_specs=pl.BlockSpec((tm, D_MODEL), row),
        out_shape=jax.ShapeDtypeStruct((n_tok, D_MODEL), F32),
        compiler_params=pltpu.CompilerParams(dimension_semantics=("parallel",)),
        name="moe_combine",
    )(h, gates, y, y, y, y)


def kernel(x, meta_tokens, norm_mix_w, w_in, q_norm_w, k_norm_w, kv_norm_w, w_kv_up, conv_w, a_log,
           dt_bias, delta_norm_w, w_out, norm_ffn_w, w_router, b_router, w_gate_up, b_gate_up,
           w_down, b_down):
    b, s, d = x.shape
    consts = _pack_in_proj_weights(norm_mix_w[0], w_in[0], q_norm_w[0], k_norm_w[0], kv_norm_w[0],
                                   w_kv_up[0])
    cos, sin = _rope_tables(N_META + s)
    real = _in_proj(x.reshape(b * s, d), cos[N_META:], sin[N_META:], consts, 256)
    meta = _in_proj(meta_tokens, cos[:N_META], sin[:N_META], consts, N_META)
    q, k, v, iq, ik, misc, qkv, z = (a.reshape(b, s, a.shape[-1]) for a in real)
    pad = lambda a: jnp.pad(a, ((0, LANES - N_META), (0, 0)))
    o_a = _sparse_attention(q, k, v, pad(meta[1]), pad(meta[2]), iq, ik, misc,
                            tq=128, kc=min(512, s), topk=min(INDEX_TOPK, s // 4))
    lead = lambda a: jnp.pad(a, ((CHUNK - N_META, 0), (0, 0)))
    o_b = _gated_deltanet(qkv, z, misc, lead(meta[6]), lead(meta[5]), conv_w[0], a_log[0],
                          dt_bias[0], delta_norm_w[0])
    n_tok = b * s
    h, xn, gates, eid = _out_router(o_a.reshape(n_tok, A_WIDTH), o_b.reshape(n_tok, B_WIDTH),
                                    x.reshape(n_tok, d), w_out[0], norm_ffn_w[0], w_router[0],
                                    b_router[0], 256)
    block_e, n_valid, block_rows, row_tok, row_dst = _route(eid[:, :TOP_K], MOE_BM)
    wgu = w_gate_up[0]
    bgu = b_gate_up[0].reshape(N_EXPERTS, 1, 2 * D_FF)
    y = _moe_experts(xn, block_e, n_valid, block_rows, row_tok, row_dst,
                     wgu[:, :, 0::2].astype(BF16), wgu[:, :, 1::2].astype(BF16), w_down[0].astype(BF16),
                     bgu[:, :, 0::2], bgu[:, :, 1::2], b_down[0].reshape(N_EXPERTS, 1, D_MODEL), MOE_BM)
    out = _combine(h, gates, y, 256)
    return out.reshape(b, s, d)
```

```python
import functools
import math

import jax
import jax.numpy as jnp
from jax import lax
from jax.experimental import pallas as pl
from jax.experimental.pallas import tpu as pltpu

F32 = jnp.float32
BF16 = jnp.bfloat16
HIGHEST = lax.Precision.HIGHEST

D_MODEL = 1024
N_META = 16
ROPE_THETA = 10000.0
EPS = 1e-6
A_HEAD_DIM = 64
A_HEADS = 8
A_WIDTH = A_HEADS * A_HEAD_DIM
KV_RANK = 256
IDX_HEADS = 8
IDX_DIM = 64
INDEX_TOPK = 256
B_HEAD_DIM = 128
B_HEADS = 4
B_WIDTH = B_HEADS * B_HEAD_DIM
CONV_WIDTH = 4
CHUNK = 64
N_EXPERTS = 32
TOP_K = 4
D_FF = D_MODEL
SWIGLU_LIMIT = 7.0
SWIGLU_ALPHA = 1.702
IN_SPLITS = (A_WIDTH, KV_RANK, IDX_HEADS * IDX_DIM, IDX_DIM, IDX_HEADS, 3 * B_WIDTH, B_WIDTH,
             B_HEADS, B_HEADS)

LANES = 128

C_Q = 0
C_CKV = C_Q + A_WIDTH
C_IQ = C_CKV + KV_RANK
C_IK = C_IQ + IDX_HEADS * IDX_DIM
C_MISC = C_IK + LANES
C_QKV = C_MISC + LANES
C_Z = C_QKV + 3 * B_WIDTH
C_END = C_Z + B_WIDTH
MISC_W, MISC_BETA, MISC_DECAY = 0, IDX_HEADS, IDX_HEADS + B_HEADS
LOG2E = math.log2(math.e)
VT_ROWS = A_HEAD_DIM + 16


def _rope_partner(a):
    lane = lax.broadcasted_iota(jnp.int32, a.shape, 1)
    first_half = (lane % A_HEAD_DIM) < (A_HEAD_DIM // 2)
    return jnp.where(first_half, pltpu.roll(a, LANES - A_HEAD_DIM // 2, 1),
                     pltpu.roll(a, A_HEAD_DIM // 2, 1))


def _rope(a, cos, sin_signed):
    return a * cos + _rope_partner(a) * sin_signed


def _head_rms(a, head_mean, gain):
    msq = jnp.dot((a * a).astype(BF16), head_mean, preferred_element_type=F32)
    return a * lax.rsqrt(msq + EPS) * gain


def _in_proj_kernel(x_ref, nw_ref, w_ref, wkv_ref, qnw_ref, knw_ref, kvnw_ref, cos_ref, sin_ref,
                    hm_ref, q_ref, k_ref, vt_ref, iq_ref, ik_ref, misc_ref, qkv_ref, z_ref):
    x = x_ref[...]
    u = x * lax.rsqrt(jnp.mean(x * x, axis=-1, keepdims=True) + EPS) * nw_ref[...]
    ub = u.astype(BF16)

    def proj(c0, c1):
        return jnp.dot(ub, w_ref[:, c0:c1], preferred_element_type=F32)

    cos = cos_ref[...]
    sin = sin_ref[...]
    hm = hm_ref[...]

    def rope_groups(a):
        return jnp.concatenate(
            [_rope(a[:, g * LANES:(g + 1) * LANES], cos, sin) for g in range(a.shape[1] // LANES)],
            axis=1)

    q = _head_rms(proj(C_Q, C_CKV), hm, qnw_ref[...])
    q_ref[...] = (rope_groups(q) * (A_HEAD_DIM ** -0.5 * LOG2E)).astype(BF16)

    ckv = proj(C_CKV, C_IQ)
    ckv = ckv * lax.rsqrt(jnp.mean(ckv * ckv, axis=-1, keepdims=True) + EPS) * kvnw_ref[...]
    kv = jnp.dot(ckv.astype(BF16), wkv_ref[...], preferred_element_type=F32)
    k = _head_rms(kv[:, :A_WIDTH], hm, knw_ref[...])
    k_ref[...] = rope_groups(k).astype(BF16)
    tm = x.shape[0]
    ones = jnp.ones((VT_ROWS - A_HEAD_DIM, tm), BF16)
    for g in range(A_WIDTH // LANES):
        vt = kv[:, A_WIDTH + g * LANES:A_WIDTH + (g + 1) * LANES].T.astype(BF16)
        for half in range(2):
            r0 = (2 * g + half) * VT_ROWS
            vt_ref[0, r0:r0 + A_HEAD_DIM, :] = vt[half * A_HEAD_DIM:(half + 1) * A_HEAD_DIM, :]
            vt_ref[0, r0 + A_HEAD_DIM:r0 + VT_ROWS, :] = ones

    iq_ref[...] = rope_groups(proj(C_IQ, C_IK)).astype(BF16)
    ik_ref[...] = _rope(proj(C_IK, C_MISC), cos, sin).astype(BF16)

    lane = lax.broadcasted_iota(jnp.int32, (1, LANES), 1)
    w_scale = jnp.where(lane < IDX_HEADS, IDX_HEADS ** -0.5 * IDX_DIM ** -0.5, 1.0)
    misc_ref[...] = proj(C_MISC, C_QKV) * w_scale
    qkv_ref[...] = proj(C_QKV, C_Z)
    z_ref[...] = proj(C_Z, C_END)


def _in_proj(x2d, cos, sin, consts, tm):
    n = x2d.shape[0]
    n_pos_blocks = cos.shape[0] // tm
    nw, w_pack, wkv, qnw, knw, kvnw, hm = consts
    row = lambda i: (i, 0)
    fixed = lambda i: (0, 0)
    pos = lambda i: (i % n_pos_blocks, 0)
    out_widths = (A_WIDTH, A_WIDTH, None, IDX_HEADS * IDX_DIM, LANES, LANES, 3 * B_WIDTH, B_WIDTH)
    out_dtypes = (BF16, BF16, BF16, BF16, BF16, F32, F32, F32)
    vt_rows = A_HEADS * VT_ROWS
    out_specs = [pl.BlockSpec((1, vt_rows, tm), lambda i: (i, 0, 0)) if w is None
                 else pl.BlockSpec((tm, w), row) for w in out_widths]
    out_shape = [jax.ShapeDtypeStruct((n // tm, vt_rows, tm) if w is None else (n, w), dt)
                 for w, dt in zip(out_widths, out_dtypes)]
    return pl.pallas_call(
        _in_proj_kernel,
        grid=(n // tm,),
        in_specs=[
            pl.BlockSpec((tm, D_MODEL), row),
            pl.BlockSpec(nw.shape, fixed),
            pl.BlockSpec(w_pack.shape, fixed),
            pl.BlockSpec(wkv.shape, fixed),
            pl.BlockSpec(qnw.shape, fixed),
            pl.BlockSpec(knw.shape, fixed),
            pl.BlockSpec(kvnw.shape, fixed),
            pl.BlockSpec((tm, LANES), pos),
            pl.BlockSpec((tm, LANES), pos),
            pl.BlockSpec(hm.shape, fixed),
        ],
        out_specs=out_specs,
        out_shape=out_shape,
        compiler_params=pltpu.CompilerParams(dimension_semantics=("parallel",)),
        name="in_proj",
    )(x2d, nw, w_pack, wkv, qnw, knw, kvnw, cos, sin, hm)


def _pack_in_proj_weights(norm_mix_w, w_in, q_norm_w, k_norm_w, kv_norm_w, w_kv_up):
    points = []
    acc = 0
    for s in IN_SPLITS:
        points.append((acc, acc + s))
        acc += s
    a_q, a_ckv, i_q, i_k, i_w, b_qkv, b_z, b_beta, b_a = (w_in[:, a:b] for a, b in points)
    misc = jnp.concatenate(
        [i_w, b_beta, b_a, jnp.zeros((D_MODEL, LANES - IDX_HEADS - 2 * B_HEADS), w_in.dtype)], axis=1)
    w_pack = jnp.concatenate([a_q, a_ckv, i_q, i_k, i_k, misc, b_qkv, b_z], axis=1).astype(BF16)
    head_mean = jnp.kron(jnp.eye(A_HEADS, dtype=F32),
                         jnp.full((A_HEAD_DIM, A_HEAD_DIM), 1.0 / A_HEAD_DIM, F32)).astype(BF16)
    return (norm_mix_w.reshape(1, D_MODEL), w_pack, w_kv_up.astype(BF16),
            jnp.tile(q_norm_w, A_HEADS).reshape(1, A_WIDTH),
            jnp.tile(k_norm_w, A_HEADS).reshape(1, A_WIDTH),
            kv_norm_w.reshape(1, KV_RANK), head_mean)


def _rope_tables(n_pos):
    half = A_HEAD_DIM // 2
    inv_freq = ROPE_THETA ** (-jnp.arange(0, A_HEAD_DIM, 2, dtype=F32) / A_HEAD_DIM)
    ang = jnp.arange(n_pos, dtype=F32)[:, None] * inv_freq[None, :]
    cos, sin = jnp.cos(ang), jnp.sin(ang)
    cos128 = jnp.tile(cos, (1, LANES // half))
    sin128 = jnp.tile(jnp.concatenate([-sin, sin], axis=1), (1, LANES // A_HEAD_DIM))
    return cos128, sin128


NEG_INF = float("-inf")
F32_MAX = float(jnp.finfo(jnp.float32).max)
INT_MIN = -2 ** 31


VT_TILE = 256
SUBLANES = 8


def _attn_kernel(q_ref, k_ref, vt_ref, km_ref, vmt_ref, iq_ref, ik_ref, misc_ref, o_ref,
                 isc_ref, lhs_ref, qm_ref, s_ref, m_ref, acc_ref, *, tq, kc, topk, pos_bits):
    j = pl.program_id(1)
    n_kc = lax.div((j + 1) * tq + (kc - 1), kc)
    g = kc // SUBLANES
    lane = lax.broadcasted_iota(jnp.int32, (1, LANES), 1)
    lo_half = lane < A_HEAD_DIM
    n_pairs = A_WIDTH // LANES
    nt = (((1,), (1,)), ((), ()))

    iq = iq_ref[0]
    q = q_ref[0]
    zero = jnp.zeros((), BF16)
    for p in range(n_pairs):
        blk = iq[:, p * LANES:(p + 1) * LANES]
        lhs_ref[(2 * p) * tq:(2 * p + 1) * tq, :] = jnp.where(lo_half, blk, zero)
        lhs_ref[(2 * p + 1) * tq:(2 * p + 2) * tq, :] = jnp.where(lo_half, zero, blk)
        qb = q[:, p * LANES:(p + 1) * LANES]
        qm_ref[p, :tq, :] = jnp.where(lo_half, qb, zero)
        qm_ref[p, tq:, :] = jnp.where(lo_half, zero, qb)
    w_t = misc_ref[0].T

    def fold0(x3, op):
        n = x3.shape[0]
        while n > 1:
            x3 = op(x3[:n // 2], x3[n // 2:n])
            n //= 2
        return x3[0]

    q_pos = j * tq + lax.broadcasted_iota(jnp.int32, (1, tq), 1)

    def idx_body(c, carry):
        k0 = pl.multiple_of(c * kc, kc)
        r = lax.dot_general(ik_ref[0, pl.ds(k0, kc), :], lhs_ref[...], nt,
                            preferred_element_type=F32)
        s = None
        for h in range(IDX_HEADS):
            term = jnp.maximum(r[:, h * tq:(h + 1) * tq], 0.0) * w_t[MISC_W + h:MISC_W + h + 1, :]
            s = term if s is None else s + term
        k_pos = k0 + lax.broadcasted_iota(jnp.int32, (kc, 1), 0)
        isc_ref[c] = jnp.where(k_pos <= q_pos, s, NEG_INF)
        return carry
    lax.fori_loop(0, n_kc, idx_body, 0)

    def count(pred):
        def body(c, acc):
            x3 = isc_ref[c].reshape(g, SUBLANES, tq)
            return acc + fold0(jnp.where(pred(x3, c), 1.0, 0.0), jnp.add)
        acc = lax.fori_loop(0, n_kc, body, jnp.zeros((SUBLANES, tq), F32))
        return jnp.broadcast_to(jnp.sum(acc, axis=0, keepdims=True), (SUBLANES, tq))

    def key_to_float(u):
        key = u ^ jnp.int32(INT_MIN)
        bits = jnp.where(key >= 0, key, key ^ jnp.int32(0x7FFFFFFF))
        return lax.bitcast_convert_type(bits, F32)

    def bit_body(i, u):
        u2 = u | lax.shift_left(jnp.int32(1), 31 - i)
        cand = key_to_float(u2)
        n_ge = count(lambda x3, c: x3 >= cand[None])
        return jnp.where(n_ge >= topk, u2, u)
    u = lax.fori_loop(0, 32, bit_body, jnp.zeros((SUBLANES, tq), jnp.int32))
    few = (u >= 0) & (u < 0x00800000)
    tau = jnp.where(few, -F32_MAX, key_to_float(u))

    n_ge = count(lambda x3, c: x3 >= tau[None])

    @pl.when(jnp.max(n_ge) > topk)
    def _():
        n_gt = count(lambda x3, c: x3 > tau[None])
        need = topk - n_gt
        def chunk_pos(c):
            return (c * kc + lax.broadcasted_iota(jnp.int32, (g, SUBLANES, tq), 0) * SUBLANES
                    + lax.broadcasted_iota(jnp.int32, (g, SUBLANES, tq), 1))
        def pos_body(i, cut):
            cut2 = cut | lax.shift_left(jnp.int32(1), pos_bits - 1 - i)
            ties_before = count(lambda x3, c: (x3 == tau[None]) & (chunk_pos(c) < cut2[None]))
            return jnp.where(ties_before < need, cut2, cut)
        cut = lax.fori_loop(0, pos_bits, pos_body, jnp.zeros((SUBLANES, tq), jnp.int32))
        def drop_body(c, carry):
            x3 = isc_ref[c].reshape(g, SUBLANES, tq)
            drop = (x3 == tau[None]) & (chunk_pos(c) > cut[None])
            isc_ref[c] = jnp.where(drop, NEG_INF, x3).reshape(kc, tq)
            return carry
        lax.fori_loop(0, n_kc, drop_body, 0)

    tau_row = tau[0:1, :]

    def attend(key_pairs, bias, vt_slabs, first):
        n = key_pairs[0].shape[0]
        m_cur = []
        for p in range(n_pairs):
            s2 = lax.dot_general(key_pairs[p], qm_ref[p], nt, preferred_element_type=F32)
            for half in range(2):
                s = s2[:, half * tq:(half + 1) * tq] + bias
                s_ref[2 * p + half, :n, :] = s
                m8 = fold0(s.reshape(n // SUBLANES, SUBLANES, tq), jnp.maximum)
                m_cur.append(jnp.max(m8, axis=0, keepdims=True))
        for hd in range(A_HEADS):
            m_new = m_cur[hd] if first else jnp.maximum(m_ref[hd][0:1, :], m_cur[hd])
            e = jnp.exp2(s_ref[hd, :n, :] - m_new).astype(BF16)
            pv = None
            off = 0
            for slab in vt_slabs(hd):
                part = jnp.dot(slab, e[off:off + slab.shape[1], :], preferred_element_type=F32)
                pv = part if pv is None else pv + part
                off += slab.shape[1]
            if first:
                acc_ref[hd] = pv
            else:
                acc_ref[hd] = jnp.exp2(m_ref[hd][0:1, :] - m_new) * acc_ref[hd] + pv
            m_ref[hd] = jnp.broadcast_to(m_new, (SUBLANES, tq))

    meta_bias = jnp.where(lax.broadcasted_iota(jnp.int32, (LANES, 1), 0) < N_META, 0.0, NEG_INF)
    attend([km_ref[:, p * LANES:(p + 1) * LANES] for p in range(n_pairs)], meta_bias,
           lambda hd: [vmt_ref[hd * VT_ROWS:(hd + 1) * VT_ROWS, :]], True)

    def att_body(c, carry):
        k0 = pl.multiple_of(c * kc, kc)
        bias = jnp.where(isc_ref[c] >= tau_row, 0.0, NEG_INF)
        attend([k_ref[0, pl.ds(k0, kc), p * LANES:(p + 1) * LANES] for p in range(n_pairs)], bias,
               lambda hd: [vt_ref[c * (kc // VT_TILE) + t, hd * VT_ROWS:(hd + 1) * VT_ROWS, :]
                           for t in range(kc // VT_TILE)], False)
        return carry
    lax.fori_loop(0, n_kc, att_body, 0)

    for p in range(n_pairs):
        halves = []
        for hd in (2 * p, 2 * p + 1):
            acc = acc_ref[hd]
            halves.append(acc[:A_HEAD_DIM, :] / acc[A_HEAD_DIM:A_HEAD_DIM + 1, :])
        o_ref[0, :, p * LANES:(p + 1) * LANES] = jnp.concatenate(halves, axis=0).T.astype(BF16)


def _sparse_attention(q, k, vt, km, vmt, iq, ik, misc, *, tq, kc, topk):
    b, s, _ = q.shape
    assert tq == LANES and kc % VT_TILE == 0 and s % kc == 0
    kernel = functools.partial(_attn_kernel, tq=tq, kc=kc, topk=topk, pos_bits=int(math.log2(s)))
    qblk = lambda bi, j: (bi, j, 0)
    full = lambda bi, j: (bi, 0, 0)
    fixed = lambda bi, j: (0, 0)
    vt_rows = A_HEADS * VT_ROWS
    return pl.pallas_call(
        kernel,
        grid=(b, s // tq),
        in_specs=[
            pl.BlockSpec((1, tq, A_WIDTH), qblk),
            pl.BlockSpec((1, s, A_WIDTH), full),
            pl.BlockSpec((s // VT_TILE, vt_rows, VT_TILE), full),
            pl.BlockSpec(km.shape, fixed),
            pl.BlockSpec(vmt.shape, fixed),
            pl.BlockSpec((1, tq, IDX_HEADS * IDX_DIM), qblk),
            pl.BlockSpec((1, s, LANES), full),
            pl.BlockSpec((1, tq, LANES), qblk),
        ],
        out_specs=pl.BlockSpec((1, tq, A_WIDTH), qblk),
        out_shape=jax.ShapeDtypeStruct((b, s, A_WIDTH), BF16),
        scratch_shapes=[
            pltpu.VMEM((s // kc, kc, tq), F32),
            pltpu.VMEM((IDX_HEADS * tq, LANES), BF16),
            pltpu.VMEM((A_WIDTH // LANES, 2 * tq, LANES), BF16),
            pltpu.VMEM((A_HEADS, max(kc, LANES), tq), F32),
            pltpu.VMEM((A_HEADS, SUBLANES, tq), F32),
            pltpu.VMEM((A_HEADS, VT_ROWS, tq), F32),
        ],
        compiler_params=pltpu.CompilerParams(dimension_semantics=("parallel", "arbitrary")),
        name="sparse_attention",
    )(q, k, vt, km, vmt, iq, ik, misc)


HALO = 8


def _softplus(x):
    return jnp.maximum(x, 0.0) + jnp.log1p(jnp.exp(-jnp.abs(x)))


def _bdot(a, b):
    return jnp.dot(a.astype(BF16), b.astype(BF16), preferred_element_type=F32)


def _hdot(a, b):
    return jnp.dot(a, b, precision=HIGHEST, preferred_element_type=F32)


def _dot3(a, b):
    a_hi = a.astype(BF16)
    b_hi = b.astype(BF16)
    a_lo = (a - a_hi.astype(F32)).astype(BF16)
    b_lo = (b - b_hi.astype(F32)).astype(BF16)
    dot = functools.partial(jnp.dot, preferred_element_type=F32)
    return dot(a_hi, b_hi) + (dot(a_hi, b_lo) + dot(a_lo, b_hi))


def _delta_kernel(qkv_ref, z_ref, misc_ref, qkvm_ref, miscm_ref, convw_ref, alog_ref, dtb_ref,
                  nw_ref, o_ref, state_ref, halo_ref):
    c = pl.program_id(1)
    n_pad = CHUNK - N_META
    nt = (((1,), (1,)), ((), ()))
    tn = (((0,), (0,)), ((), ()))

    @pl.when(c == 0)
    def _():
        state_ref[...] = jnp.zeros_like(state_ref)
        halo_ref[...] = jnp.zeros_like(halo_ref)

    is_meta = c == 0
    xin = jnp.where(is_meta, qkvm_ref[...], qkv_ref[0])
    misc = jnp.where(is_meta, miscm_ref[...], misc_ref[0])

    xcat = jnp.concatenate([halo_ref[...], xin], axis=0)
    halo_ref[...] = xin[CHUNK - HALO:, :]
    conv = None
    for tap in range(CONV_WIDTH):
        off = HALO - (CONV_WIDTH - 1) + tap
        term = xcat[off:off + CHUNK, :] * convw_ref[tap:tap + 1, :]
        conv = term if conv is None else conv + term
    xc = conv * jax.nn.sigmoid(conv)

    row = lax.broadcasted_iota(jnp.int32, (CHUNK, 1), 0)
    neutral = jnp.logical_and(is_meta, row < n_pad)
    beta_all = jnp.where(neutral, 0.0, jax.nn.sigmoid(misc))
    g_all = jnp.where(neutral, 0.0, -jnp.exp(alog_ref[...]) * _softplus(misc + dtb_ref[...]))

    ri = lax.broadcasted_iota(jnp.int32, (CHUNK, CHUNK), 0)
    ci = lax.broadcasted_iota(jnp.int32, (CHUNK, CHUNK), 1)
    incl = ri >= ci
    strict = ri > ci
    eye = (ri == ci).astype(F32)
    tri = incl.astype(F32)
    gc_all = _hdot(tri, g_all)
    gc_all_t = gc_all.T

    for h in range(B_HEADS):
        sl = slice(h * B_HEAD_DIM, (h + 1) * B_HEAD_DIM)
        qh = xc[:, sl]
        kh = xc[:, B_WIDTH + h * B_HEAD_DIM:B_WIDTH + (h + 1) * B_HEAD_DIM]
        vh = xc[:, 2 * B_WIDTH + h * B_HEAD_DIM:2 * B_WIDTH + (h + 1) * B_HEAD_DIM]
        qn = qh * lax.rsqrt(jnp.sum(qh * qh, axis=-1, keepdims=True) + EPS) * (B_HEAD_DIM ** -0.5)
        kn = kh * lax.rsqrt(jnp.sum(kh * kh, axis=-1, keepdims=True) + EPS)
        beta = jnp.broadcast_to(beta_all[:, MISC_BETA + h:MISC_BETA + h + 1], (CHUNK, B_HEAD_DIM))
        gc = jnp.broadcast_to(gc_all[:, MISC_DECAY + h:MISC_DECAY + h + 1], (CHUNK, B_HEAD_DIM))
        g_col = gc[:, :CHUNK]
        g_row = gc_all_t[MISC_DECAY + h:MISC_DECAY + h + 1, :]
        decay = jnp.where(incl, jnp.exp(jnp.where(incl, g_col - g_row, 0.0)), 0.0)

        kb = kn * beta
        vb = vh * beta
        a_mat = jnp.where(strict, lax.dot_general(kb.astype(BF16), kn.astype(BF16), nt,
                                                  preferred_element_type=F32) * decay, 0.0)
        pw = -a_mat
        t_inv = eye + pw
        for _ in range(int(math.log2(CHUNK)) - 1):
            pw = _dot3(pw, pw)
            t_inv = t_inv + _dot3(t_inv, pw)
        u = _bdot(t_inv, vb)
        w = _bdot(t_inv, kb * jnp.exp(gc))
        intra = jnp.where(incl, lax.dot_general(qn.astype(BF16), kn.astype(BF16), nt,
                                                preferred_element_type=F32) * decay, 0.0)

        state = state_ref[h]
        v_new = u - _bdot(w, state)
        o = _bdot(qn * jnp.exp(gc), state) + _bdot(intra, v_new)
        g_last = gc[CHUNK - 1:CHUNK, :]
        kd = kn * jnp.exp(g_last - gc)
        state_ref[h] = state * jnp.exp(g_last) + lax.dot_general(
            kd.astype(BF16), v_new.astype(BF16), tn, preferred_element_type=F32)

        y = o * lax.rsqrt(jnp.mean(o * o, axis=-1, keepdims=True) + EPS) * nw_ref[...]
        zh = z_ref[0, :, sl]
        o_ref[0, :, sl] = (y * (zh * jax.nn.sigmoid(zh))).astype(BF16)


def _gated_deltanet(qkv, z, misc, qkv_m, misc_m, conv_w, a_log, dt_bias, norm_w):
    b, s, _ = qkv.shape
    n_chunks = s // CHUNK + 1
    blk = lambda bi, c: (bi, jnp.maximum(c - 1, 0), 0)
    fixed = lambda bi, c: (0, 0)
    lane_vec = lambda v: jnp.zeros((1, LANES), F32).at[0, MISC_DECAY:MISC_DECAY + B_HEADS].set(v)
    return pl.pallas_call(
        _delta_kernel,
        grid=(b, n_chunks),
        in_specs=[
            pl.BlockSpec((1, CHUNK, 3 * B_WIDTH), blk),
            pl.BlockSpec((1, CHUNK, B_WIDTH), blk),
            pl.BlockSpec((1, CHUNK, LANES), blk),
            pl.BlockSpec((CHUNK, 3 * B_WIDTH), fixed),
            pl.BlockSpec((CHUNK, LANES), fixed),
            pl.BlockSpec((CONV_WIDTH, 3 * B_WIDTH), fixed),
            pl.BlockSpec((1, LANES), fixed),
            pl.BlockSpec((1, LANES), fixed),
            pl.BlockSpec((1, B_HEAD_DIM), fixed),
        ],
        out_specs=pl.BlockSpec((1, CHUNK, B_WIDTH), blk),
        out_shape=jax.ShapeDtypeStruct((b, s, B_WIDTH), BF16),
        scratch_shapes=[
            pltpu.VMEM((B_HEADS, B_HEAD_DIM, B_HEAD_DIM), F32),
            pltpu.VMEM((HALO, 3 * B_WIDTH), F32),
        ],
        compiler_params=pltpu.CompilerParams(dimension_semantics=("parallel", "arbitrary")),
        name="gated_deltanet",
    )(qkv, z, misc, qkv_m, misc_m, conv_w, lane_vec(a_log), lane_vec(dt_bias),
      norm_w.reshape(1, B_HEAD_DIM))


def _out_router_kernel(oa_ref, ob_ref, x_ref, wo_ref, nw_ref, wr_ref, br_ref,
                       h_ref, xn_ref, gate_ref, eid_ref):
    mix = (jnp.dot(oa_ref[...], wo_ref[:A_WIDTH, :], preferred_element_type=F32)
           + jnp.dot(ob_ref[...], wo_ref[A_WIDTH:, :], preferred_element_type=F32))
    h = x_ref[...] + mix
    h_ref[...] = h
    xn = h * lax.rsqrt(jnp.mean(h * h, axis=-1, keepdims=True) + EPS) * nw_ref[...]
    xn_ref[...] = xn
    logits = _hdot(xn, wr_ref[...]) + br_ref[...]
    tm = logits.shape[0]
    lane = lax.broadcasted_iota(jnp.int32, (tm, LANES), 1)
    lane_f = lane.astype(F32)
    work = logits
    vals, idxs = [], []
    for _ in range(TOP_K):
        m = jnp.max(work, axis=-1, keepdims=True)
        idx = jnp.min(jnp.where(work == m, lane_f, float(LANES)), axis=-1, keepdims=True)
        vals.append(m)
        idxs.append(idx)
        work = jnp.where(lane_f == idx, NEG_INF, work)
    exps = [jnp.exp(v - vals[0]) for v in vals]
    denom = exps[0]
    for e in exps[1:]:
        denom = denom + e
    gates = jnp.zeros((tm, LANES), F32)
    eids = jnp.zeros((tm, LANES), F32)
    for kk in range(TOP_K):
        gates = jnp.where(lane == kk, exps[kk] / denom, gates)
        eids = jnp.where(lane == kk, idxs[kk], eids)
    gate_ref[...] = gates
    eid_ref[...] = eids.astype(jnp.int32)


def _out_router(o_a, o_b, x2d, w_out, norm_w, w_router, b_router, tm):
    n = x2d.shape[0]
    row = lambda i: (i, 0)
    fixed = lambda i: (0, 0)
    wr = jnp.zeros((D_MODEL, LANES), F32).at[:, :N_EXPERTS].set(w_router)
    br = jnp.full((1, LANES), NEG_INF, F32).at[0, :N_EXPERTS].set(b_router)
    return pl.pallas_call(
        _out_router_kernel,
        grid=(n // tm,),
        in_specs=[
            pl.BlockSpec((tm, A_WIDTH), row),
            pl.BlockSpec((tm, B_WIDTH), row),
            pl.BlockSpec((tm, D_MODEL), row),
            pl.BlockSpec((A_WIDTH + B_WIDTH, D_MODEL), fixed),
            pl.BlockSpec((1, D_MODEL), fixed),
            pl.BlockSpec((D_MODEL, LANES), fixed),
            pl.BlockSpec((1, LANES), fixed),
        ],
        out_specs=[pl.BlockSpec((tm, D_MODEL), row), pl.BlockSpec((tm, D_MODEL), row),
                   pl.BlockSpec((tm, LANES), row), pl.BlockSpec((tm, LANES), row)],
        out_shape=[jax.ShapeDtypeStruct((n, D_MODEL), F32), jax.ShapeDtypeStruct((n, D_MODEL), F32),
                   jax.ShapeDtypeStruct((n, LANES), F32), jax.ShapeDtypeStruct((n, LANES), jnp.int32)],
        compiler_params=pltpu.CompilerParams(dimension_semantics=("parallel",)),
        name="out_proj_router",
    )(o_a, o_b, x2d, w_out.astype(BF16), norm_w.reshape(1, D_MODEL), wr, br)


MOE_BM = 256
MOE_BF = 512


def _route(eid, bm):
    n_tok = eid.shape[0]
    n_assign = n_tok * TOP_K
    flat_e = eid.T.reshape(-1)
    order = jnp.argsort(flat_e).astype(jnp.int32)
    counts = jnp.bincount(flat_e, length=N_EXPERTS).astype(jnp.int32)
    padded = (counts + bm - 1) // bm * bm
    start = jnp.cumsum(counts) - counts
    pend = jnp.cumsum(padded)
    pstart = pend - padded
    n_blocks = -(-(n_assign + N_EXPERTS * (bm - 1)) // bm)
    blk = jnp.arange(n_blocks, dtype=jnp.int32)
    block_e = jnp.minimum(jnp.searchsorted(pend, blk * bm, side='right'), N_EXPERTS - 1).astype(jnp.int32)
    n_valid = (pend[-1] // bm).astype(jnp.int32)
    r = jnp.arange(n_blocks * bm, dtype=jnp.int32)
    e_r = block_e[r // bm]
    off = r - pstart[e_r]
    valid = off < counts[e_r]
    a = order[jnp.clip(start[e_r] + off, 0, n_assign - 1)]
    row_tok = jnp.where(valid, a % n_tok, 0)
    row_dst = jnp.where(valid, a, 0)
    block_rows = jnp.clip(counts[block_e] - (blk * bm - pstart[block_e]), 0, bm).astype(jnp.int32)
    return (block_e, n_valid.reshape(1), block_rows, row_tok.reshape(n_blocks, 1, bm),
            row_dst.reshape(n_blocks, 1, bm))


def _moe_kernel(be_ref, nv_ref, rows_ref, tok_next_ref, tok_cur_ref, dst_prev_ref, dst_ref, xn_hbm,
                wgu_ref, wd_ref, bg_ref, bu_ref, bd_ref, perm_ref, y_hbm, xbuf, ybuf, wg_s, wu_s, wd_s,
                gsem, ssem, *, bm):
    i = pl.program_id(0)
    n_valid = nv_ref[0]
    slot = i % 2

    def for_valid_rows(blk, fn):
        def body(r, carry):
            fn(r)
            return carry
        lax.fori_loop(0, rows_ref[blk], body, 0)

    def gather_copy(tok_ref, s, r):
        return pltpu.make_async_copy(xn_hbm.at[pl.ds(tok_ref[0, 0, r], 1), :],
                                     xbuf.at[s, pl.ds(r, 1), :], gsem.at[s])

    def scatter_copy(d_ref, s, r):
        return pltpu.make_async_copy(ybuf.at[s, pl.ds(r, 1), :],
                                     y_hbm.at[pl.ds(d_ref[0, 0, r], 1), :], ssem.at[s])

    def for_rows(fn):
        def body(r, carry):
            fn(r)
            return carry
        lax.fori_loop(0, bm, body, 0, unroll=8)

    @pl.when(jnp.logical_and(i == 0, n_valid > 0))
    def _():
        for_rows(lambda r: gather_copy(tok_cur_ref, 0, r).start())

    @pl.when(i + 1 < n_valid)
    def _():
        for_rows(lambda r: gather_copy(tok_next_ref, 1 - slot, r).start())

    @pl.when(jnp.logical_and(i < n_valid, jnp.logical_or(i == 0, be_ref[i] != be_ref[jnp.maximum(i - 1, 0)])))
    def _():
        perm = perm_ref[...]
        grp = 2 * LANES
        for gidx in range(2 * D_FF // grp):
            blk = wgu_ref[0, :, gidx * grp:(gidx + 1) * grp].astype(BF16)
            split = jnp.dot(blk, perm, preferred_element_type=F32).astype(BF16)
            wg_s[:, gidx * LANES:(gidx + 1) * LANES] = split[:, :LANES]
            wu_s[:, gidx * LANES:(gidx + 1) * LANES] = split[:, LANES:]
        wd_s[...] = wd_ref[0].astype(BF16)

    @pl.when(i < n_valid)
    def _():
        for_rows(lambda r: gather_copy(tok_cur_ref, slot, r).wait())
        x = xbuf[slot].astype(BF16)
        y = None
        for f0 in range(0, D_FF, MOE_BF):
            g = jnp.dot(x, wg_s[:, f0:f0 + MOE_BF], preferred_element_type=F32) + bg_ref[0, :, f0:f0 + MOE_BF]
            u = jnp.dot(x, wu_s[:, f0:f0 + MOE_BF], preferred_element_type=F32) + bu_ref[0, :, f0:f0 + MOE_BF]
            gate = jnp.minimum(g, SWIGLU_LIMIT)
            up = jnp.clip(u, -SWIGLU_LIMIT, SWIGLU_LIMIT)
            t = gate * jax.nn.sigmoid(gate * SWIGLU_ALPHA) * (up + 1.0)
            part = jnp.dot(t.astype(BF16), wd_s[f0:f0 + MOE_BF, :], preferred_element_type=F32)
            y = part if y is None else y + part
        ybuf[slot] = y + bd_ref[0]

        @pl.when(i >= 1)
        def _():
            for_valid_rows(i - 1, lambda r: scatter_copy(dst_prev_ref, 1 - slot, r).wait())

        for_valid_rows(i, lambda r: scatter_copy(dst_ref, slot, r).start())

        @pl.when(i == n_valid - 1)
        def _():
            for_valid_rows(i, lambda r: scatter_copy(dst_ref, slot, r).wait())


def _moe_experts(xn, block_e, n_valid, block_rows, row_tok, row_dst, w_gate_up, w_down, b_gate,
                 b_up, b_down, bm):
    n_tok = xn.shape[0]
    n_blocks = block_e.shape[0]
    n_out = n_tok * TOP_K
    src = jnp.arange(2 * LANES)
    perm = jax.nn.one_hot((src % 2) * LANES + src // 2, 2 * LANES, dtype=BF16)
    fixed = lambda i, be, nv, br: (0, 0)
    nxt = lambda i, be, nv, br: (jnp.minimum(i + 1, n_blocks - 1), 0, 0)
    cur = lambda i, be, nv, br: (i, 0, 0)
    prv = lambda i, be, nv, br: (jnp.maximum(i - 1, 0), 0, 0)
    wsel = lambda i, be, nv, br: (be[i], 0, 0)
    grid_spec = pltpu.PrefetchScalarGridSpec(
        num_scalar_prefetch=3,
        grid=(n_blocks,),
        in_specs=[
            pl.BlockSpec((1, 1, bm), nxt, memory_space=pltpu.SMEM),
            pl.BlockSpec((1, 1, bm), cur, memory_space=pltpu.SMEM),
            pl.BlockSpec((1, 1, bm), prv, memory_space=pltpu.SMEM),
            pl.BlockSpec((1, 1, bm), cur, memory_space=pltpu.SMEM),
            pl.BlockSpec(memory_space=pl.ANY),
            pl.BlockSpec((1, D_MODEL, 2 * D_FF), wsel),
            pl.BlockSpec((1, D_FF, D_MODEL), wsel),
            pl.BlockSpec((1, 1, D_FF), wsel),
            pl.BlockSpec((1, 1, D_FF), wsel),
            pl.BlockSpec((1, 1, D_MODEL), wsel),
            pl.BlockSpec((2 * LANES, 2 * LANES), fixed),
        ],
        out_specs=pl.BlockSpec(memory_space=pl.ANY),
        scratch_shapes=[
            pltpu.VMEM((2, bm, D_MODEL), F32),
            pltpu.VMEM((2, bm, D_MODEL), F32),
            pltpu.VMEM((D_MODEL, D_FF), BF16),
            pltpu.VMEM((D_MODEL, D_FF), BF16),
            pltpu.VMEM((D_FF, D_MODEL), BF16),
            pltpu.SemaphoreType.DMA((2,)),
            pltpu.SemaphoreType.DMA((2,)),
        ],
    )
    return pl.pallas_call(
        functools.partial(_moe_kernel, bm=bm),
        grid_spec=grid_spec,
        out_shape=jax.ShapeDtypeStruct((n_out, D_MODEL), F32),
        compiler_params=pltpu.CompilerParams(dimension_semantics=("arbitrary",)),
        name="moe_experts",
    )(block_e, n_valid, block_rows, row_tok, row_tok, row_dst, row_dst, xn, w_gate_up, w_down,
      b_gate, b_up, b_down, perm)


def _combine_kernel(h_ref, gate_ref, y0_ref, y1_ref, y2_ref, y3_ref, o_ref):
    gates = gate_ref[...]
    out = h_ref[...]
    for kk, y_ref in enumerate((y0_ref, y1_ref, y2_ref, y3_ref)):
        out = out + gates[:, kk:kk + 1] * y_ref[...]
    o_ref[...] = out


def _combine(h, gates, y, tm):
    n_tok = h.shape[0]
    nt = n_tok // tm
    row = lambda i: (i, 0)
    return pl.pallas_call(
        _combine_kernel,
        grid=(nt,),
        in_specs=[pl.BlockSpec((tm, D_MODEL), row), pl.BlockSpec((tm, LANES), row)]
        + [pl.BlockSpec((tm, D_MODEL), functools.partial(lambda kk, i: (kk * nt + i, 0), kk))
           for kk in range(TOP_K)],
        out_specs=pl.BlockSpec((tm, D_MODEL), row),
        out_shape=jax.ShapeDtypeStruct((n_tok, D_MODEL), F32),
        compiler_params=pltpu.CompilerParams(dimension_semantics=("parallel",)),
        name="moe_combine",
    )(h, gates, y, y, y, y)


def kernel(x, meta_tokens, norm_mix_w, w_in, q_norm_w, k_norm_w, kv_norm_w, w_kv_up, conv_w, a_log,
           dt_bias, delta_norm_w, w_out, norm_ffn_w, w_router, b_router, w_gate_up, b_gate_up,
           w_down, b_down):
    b, s, d = x.shape
    consts = _pack_in_proj_weights(norm_mix_w[0], w_in[0], q_norm_w[0], k_norm_w[0], kv_norm_w[0],
                                   w_kv_up[0])
    cos, sin = _rope_tables(N_META + s)
    real = _in_proj(x.reshape(b * s, d), cos[N_META:], sin[N_META:], consts, VT_TILE)
    meta = _in_proj(meta_tokens, cos[:N_META], sin[:N_META], consts, N_META)
    vt = real[2]
    q, k, iq, ik, misc, qkv, z = (a.reshape(b, s, a.shape[-1]) for a in real[:2] + real[3:])
    km = jnp.pad(meta[1], ((0, LANES - N_META), (0, 0)))
    vmt = jnp.pad(meta[2][0], ((0, 0), (0, LANES - N_META)))
    o_a = _sparse_attention(q, k, vt, km, vmt, iq, ik, misc,
                            tq=LANES, kc=min(512, s), topk=min(INDEX_TOPK, s // 4))
    lead = lambda a: jnp.pad(a, ((CHUNK - N_META, 0), (0, 0)))
    o_b = _gated_deltanet(qkv, z, misc, lead(meta[6]), lead(meta[5]), conv_w[0], a_log[0],
                          dt_bias[0], delta_norm_w[0])
    n_tok = b * s
    h, xn, gates, eid = _out_router(o_a.reshape(n_tok, A_WIDTH), o_b.reshape(n_tok, B_WIDTH),
                                    x.reshape(n_tok, d), w_out[0], norm_ffn_w[0], w_router[0],
                                    b_router[0], 256)
    block_e, n_valid, block_rows, row_tok, row_dst = _route(eid[:, :TOP_K], MOE_BM)
    bgu = b_gate_up[0].reshape(N_EXPERTS, 1, 2 * D_FF)
    y = _moe_experts(xn, block_e, n_valid, block_rows, row_tok, row_dst, w_gate_up[0], w_down[0],
                     bgu[:, :, 0::2], bgu[:, :, 1::2], b_down[0].reshape(N_EXPERTS, 1, D_MODEL), MOE_BM)
    out = _combine(h, gates, y, 256)
    return out.reshape(b, s, d)
```

```python
import functools
import math

import jax
import jax.numpy as jnp
from jax import lax
from jax.experimental import pallas as pl
from jax.experimental.pallas import tpu as pltpu
from jax.experimental.pallas import tpu_sc as plsc

F32 = jnp.float32
BF16 = jnp.bfloat16
HIGHEST = lax.Precision.HIGHEST

D_MODEL = 1024
N_META = 16
ROPE_THETA = 10000.0
EPS = 1e-6
A_HEAD_DIM = 64
A_HEADS = 8
A_WIDTH = A_HEADS * A_HEAD_DIM
KV_RANK = 256
IDX_HEADS = 8
IDX_DIM = 64
INDEX_TOPK = 256
B_HEAD_DIM = 128
B_HEADS = 4
B_WIDTH = B_HEADS * B_HEAD_DIM
CONV_WIDTH = 4
CHUNK = 64
N_EXPERTS = 32
TOP_K = 4
D_FF = D_MODEL
SWIGLU_LIMIT = 7.0
SWIGLU_ALPHA = 1.702
IN_SPLITS = (A_WIDTH, KV_RANK, IDX_HEADS * IDX_DIM, IDX_DIM, IDX_HEADS, 3 * B_WIDTH, B_WIDTH,
             B_HEADS, B_HEADS)

LANES = 128

C_Q = 0
C_CKV = C_Q + A_WIDTH
C_IQ = C_CKV + KV_RANK
C_IK = C_IQ + IDX_HEADS * IDX_DIM
C_MISC = C_IK + LANES
C_QKV = C_MISC + LANES
C_Z = C_QKV + 3 * B_WIDTH
C_END = C_Z + B_WIDTH
MISC_W, MISC_BETA, MISC_DECAY = 0, IDX_HEADS, IDX_HEADS + B_HEADS
LOG2E = math.log2(math.e)
VT_ROWS = A_HEAD_DIM + 16


def _rope_partner(a):
    lane = lax.broadcasted_iota(jnp.int32, a.shape, 1)
    first_half = (lane % A_HEAD_DIM) < (A_HEAD_DIM // 2)
    return jnp.where(first_half, pltpu.roll(a, LANES - A_HEAD_DIM // 2, 1),
                     pltpu.roll(a, A_HEAD_DIM // 2, 1))


def _rope(a, cos, sin_signed):
    return a * cos + _rope_partner(a) * sin_signed


def _head_rms(a, head_mean, gain):
    msq = jnp.dot((a * a).astype(BF16), head_mean, preferred_element_type=F32)
    return a * lax.rsqrt(msq + EPS) * gain


def _in_proj_kernel(x_ref, nw_ref, w_ref, wkv_ref, qnw_ref, knw_ref, kvnw_ref, cos_ref, sin_ref,
                    hm_ref, q_ref, k_ref, vt_ref, iq_ref, ik_ref, misc_ref, qkv_ref, z_ref):
    x = x_ref[...]
    u = x * lax.rsqrt(jnp.mean(x * x, axis=-1, keepdims=True) + EPS) * nw_ref[...]
    ub = u.astype(BF16)

    def proj(c0, c1):
        return jnp.dot(ub, w_ref[:, c0:c1], preferred_element_type=F32)

    cos = cos_ref[...]
    sin = sin_ref[...]
    hm = hm_ref[...]

    def rope_groups(a):
        return jnp.concatenate(
            [_rope(a[:, g * LANES:(g + 1) * LANES], cos, sin) for g in range(a.shape[1] // LANES)],
            axis=1)

    q = _head_rms(proj(C_Q, C_CKV), hm, qnw_ref[...])
    q_ref[...] = (rope_groups(q) * (A_HEAD_DIM ** -0.5 * LOG2E)).astype(BF16)

    ckv = proj(C_CKV, C_IQ)
    ckv = ckv * lax.rsqrt(jnp.mean(ckv * ckv, axis=-1, keepdims=True) + EPS) * kvnw_ref[...]
    kv = jnp.dot(ckv.astype(BF16), wkv_ref[...], preferred_element_type=F32)
    k = _head_rms(kv[:, :A_WIDTH], hm, knw_ref[...])
    k_ref[...] = rope_groups(k).astype(BF16)
    tm = x.shape[0]
    ones = jnp.ones((VT_ROWS - A_HEAD_DIM, tm), BF16)
    for g in range(A_WIDTH // LANES):
        vt = kv[:, A_WIDTH + g * LANES:A_WIDTH + (g + 1) * LANES].T.astype(BF16)
        for half in range(2):
            r0 = (2 * g + half) * VT_ROWS
            vt_ref[0, r0:r0 + A_HEAD_DIM, :] = vt[half * A_HEAD_DIM:(half + 1) * A_HEAD_DIM, :]
            vt_ref[0, r0 + A_HEAD_DIM:r0 + VT_ROWS, :] = ones

    iq_ref[...] = rope_groups(proj(C_IQ, C_IK)).astype(BF16)
    ik_ref[...] = _rope(proj(C_IK, C_MISC), cos, sin).astype(BF16)

    lane = lax.broadcasted_iota(jnp.int32, (1, LANES), 1)
    w_scale = jnp.where(lane < IDX_HEADS, IDX_HEADS ** -0.5 * IDX_DIM ** -0.5, 1.0)
    misc_ref[...] = proj(C_MISC, C_QKV) * w_scale
    qkv_ref[...] = proj(C_QKV, C_Z)
    z_ref[...] = proj(C_Z, C_END)


def _in_proj(x2d, cos, sin, consts, tm):
    n = x2d.shape[0]
    n_pos_blocks = cos.shape[0] // tm
    nw, w_pack, wkv, qnw, knw, kvnw, hm = consts
    row = lambda i: (i, 0)
    fixed = lambda i: (0, 0)
    pos = lambda i: (i % n_pos_blocks, 0)
    out_widths = (A_WIDTH, A_WIDTH, None, IDX_HEADS * IDX_DIM, LANES, LANES, 3 * B_WIDTH, B_WIDTH)
    out_dtypes = (BF16, BF16, BF16, BF16, BF16, F32, F32, F32)
    vt_rows = A_HEADS * VT_ROWS
    out_specs = [pl.BlockSpec((1, vt_rows, tm), lambda i: (i, 0, 0)) if w is None
                 else pl.BlockSpec((tm, w), row) for w in out_widths]
    out_shape = [jax.ShapeDtypeStruct((n // tm, vt_rows, tm) if w is None else (n, w), dt)
                 for w, dt in zip(out_widths, out_dtypes)]
    return pl.pallas_call(
        _in_proj_kernel,
        grid=(n // tm,),
        in_specs=[
            pl.BlockSpec((tm, D_MODEL), row),
            pl.BlockSpec(nw.shape, fixed),
            pl.BlockSpec(w_pack.shape, fixed),
            pl.BlockSpec(wkv.shape, fixed),
            pl.BlockSpec(qnw.shape, fixed),
            pl.BlockSpec(knw.shape, fixed),
            pl.BlockSpec(kvnw.shape, fixed),
            pl.BlockSpec((tm, LANES), pos),
            pl.BlockSpec((tm, LANES), pos),
            pl.BlockSpec(hm.shape, fixed),
        ],
        out_specs=out_specs,
        out_shape=out_shape,
        compiler_params=pltpu.CompilerParams(dimension_semantics=("parallel",)),
        name="in_proj",
    )(x2d, nw, w_pack, wkv, qnw, knw, kvnw, cos, sin, hm)


def _pack_in_proj_weights(norm_mix_w, w_in, q_norm_w, k_norm_w, kv_norm_w, w_kv_up):
    points = []
    acc = 0
    for s in IN_SPLITS:
        points.append((acc, acc + s))
        acc += s
    a_q, a_ckv, i_q, i_k, i_w, b_qkv, b_z, b_beta, b_a = (w_in[:, a:b] for a, b in points)
    misc = jnp.concatenate(
        [i_w, b_beta, b_a, jnp.zeros((D_MODEL, LANES - IDX_HEADS - 2 * B_HEADS), w_in.dtype)], axis=1)
    w_pack = jnp.concatenate([a_q, a_ckv, i_q, i_k, i_k, misc, b_qkv, b_z], axis=1).astype(BF16)
    head_mean = jnp.kron(jnp.eye(A_HEADS, dtype=F32),
                         jnp.full((A_HEAD_DIM, A_HEAD_DIM), 1.0 / A_HEAD_DIM, F32)).astype(BF16)
    return (norm_mix_w.reshape(1, D_MODEL), w_pack, w_kv_up.astype(BF16),
            jnp.tile(q_norm_w, A_HEADS).reshape(1, A_WIDTH),
            jnp.tile(k_norm_w, A_HEADS).reshape(1, A_WIDTH),
            kv_norm_w.reshape(1, KV_RANK), head_mean)


def _rope_tables(n_pos):
    half = A_HEAD_DIM // 2
    inv_freq = ROPE_THETA ** (-jnp.arange(0, A_HEAD_DIM, 2, dtype=F32) / A_HEAD_DIM)
    ang = jnp.arange(n_pos, dtype=F32)[:, None] * inv_freq[None, :]
    cos, sin = jnp.cos(ang), jnp.sin(ang)
    cos128 = jnp.tile(cos, (1, LANES // half))
    sin128 = jnp.tile(jnp.concatenate([-sin, sin], axis=1), (1, LANES // A_HEAD_DIM))
    return cos128, sin128


NEG_INF = float("-inf")
F32_MAX = float(jnp.finfo(jnp.float32).max)
INT_MIN = -2 ** 31


VT_TILE = 256
SUBLANES = 8


def _attn_kernel(q_ref, k_ref, vt_ref, km_ref, vmt_ref, iq_ref, ik_ref, misc_ref, o_ref,
                 isc_ref, lhs_ref, qm_ref, s_ref, m_ref, acc_ref, *, tq, kc, topk, pos_bits):
    j = pl.program_id(1)
    n_kc = lax.div((j + 1) * tq + (kc - 1), kc)
    g = kc // SUBLANES
    lane = lax.broadcasted_iota(jnp.int32, (1, LANES), 1)
    lo_half = lane < A_HEAD_DIM
    n_pairs = A_WIDTH // LANES
    nt = (((1,), (1,)), ((), ()))

    iq = iq_ref[0]
    q = q_ref[0]
    zero = jnp.zeros((), BF16)
    for p in range(n_pairs):
        blk = iq[:, p * LANES:(p + 1) * LANES]
        lhs_ref[(2 * p) * tq:(2 * p + 1) * tq, :] = jnp.where(lo_half, blk, zero)
        lhs_ref[(2 * p + 1) * tq:(2 * p + 2) * tq, :] = jnp.where(lo_half, zero, blk)
        qb = q[:, p * LANES:(p + 1) * LANES]
        qm_ref[p, :tq, :] = jnp.where(lo_half, qb, zero)
        qm_ref[p, tq:, :] = jnp.where(lo_half, zero, qb)
    w_t = misc_ref[0].T

    def fold0(x3, op):
        n = x3.shape[0]
        while n > 1:
            x3 = op(x3[:n // 2], x3[n // 2:n])
            n //= 2
        return x3[0]

    q_pos = j * tq + lax.broadcasted_iota(jnp.int32, (1, tq), 1)

    def idx_body(c, carry):
        k0 = pl.multiple_of(c * kc, kc)
        r = lax.dot_general(ik_ref[0, pl.ds(k0, kc), :], lhs_ref[...], nt,
                            preferred_element_type=F32)
        s = None
        for h in range(IDX_HEADS):
            term = jnp.maximum(r[:, h * tq:(h + 1) * tq], 0.0) * w_t[MISC_W + h:MISC_W + h + 1, :]
            s = term if s is None else s + term
        k_pos = k0 + lax.broadcasted_iota(jnp.int32, (kc, 1), 0)
        isc_ref[c] = jnp.where(k_pos <= q_pos, s, NEG_INF)
        return carry
    lax.fori_loop(0, n_kc, idx_body, 0)

    def count(pred):
        def body(c, acc):
            x3 = isc_ref[c].reshape(g, SUBLANES, tq)
            return acc + fold0(jnp.where(pred(x3, c), 1.0, 0.0), jnp.add)
        acc = lax.fori_loop(0, n_kc, body, jnp.zeros((SUBLANES, tq), F32))
        return jnp.broadcast_to(jnp.sum(acc, axis=0, keepdims=True), (SUBLANES, tq))

    def key_to_float(u):
        key = u ^ jnp.int32(INT_MIN)
        bits = jnp.where(key >= 0, key, key ^ jnp.int32(0x7FFFFFFF))
        return lax.bitcast_convert_type(bits, F32)

    def bit_body(i, u):
        u2 = u | lax.shift_left(jnp.int32(1), 31 - i)
        cand = key_to_float(u2)
        n_ge = count(lambda x3, c: x3 >= cand[None])
        return jnp.where(n_ge >= topk, u2, u)
    u = lax.fori_loop(0, 32, bit_body, jnp.zeros((SUBLANES, tq), jnp.int32))
    few = (u >= 0) & (u < 0x00800000)
    tau = jnp.where(few, -F32_MAX, key_to_float(u))

    n_ge = count(lambda x3, c: x3 >= tau[None])

    @pl.when(jnp.max(n_ge) > topk)
    def _():
        n_gt = count(lambda x3, c: x3 > tau[None])
        need = topk - n_gt
        def chunk_pos(c):
            return (c * kc + lax.broadcasted_iota(jnp.int32, (g, SUBLANES, tq), 0) * SUBLANES
                    + lax.broadcasted_iota(jnp.int32, (g, SUBLANES, tq), 1))
        def pos_body(i, cut):
            cut2 = cut | lax.shift_left(jnp.int32(1), pos_bits - 1 - i)
            ties_before = count(lambda x3, c: (x3 == tau[None]) & (chunk_pos(c) < cut2[None]))
            return jnp.where(ties_before < need, cut2, cut)
        cut = lax.fori_loop(0, pos_bits, pos_body, jnp.zeros((SUBLANES, tq), jnp.int32))
        def drop_body(c, carry):
            x3 = isc_ref[c].reshape(g, SUBLANES, tq)
            drop = (x3 == tau[None]) & (chunk_pos(c) > cut[None])
            isc_ref[c] = jnp.where(drop, NEG_INF, x3).reshape(kc, tq)
            return carry
        lax.fori_loop(0, n_kc, drop_body, 0)

    tau_row = tau[0:1, :]

    def attend(key_pairs, bias, vt_slabs, first):
        n = key_pairs[0].shape[0]
        m_cur = []
        for p in range(n_pairs):
            s2 = lax.dot_general(key_pairs[p], qm_ref[p], nt, preferred_element_type=F32)
            for half in range(2):
                s = s2[:, half * tq:(half + 1) * tq] + bias
                s_ref[2 * p + half, :n, :] = s
                m8 = fold0(s.reshape(n // SUBLANES, SUBLANES, tq), jnp.maximum)
                m_cur.append(jnp.max(m8, axis=0, keepdims=True))
        for hd in range(A_HEADS):
            m_new = m_cur[hd] if first else jnp.maximum(m_ref[hd][0:1, :], m_cur[hd])
            e = jnp.exp2(s_ref[hd, :n, :] - m_new).astype(BF16)
            pv = None
            off = 0
            for slab in vt_slabs(hd):
                part = jnp.dot(slab, e[off:off + slab.shape[1], :], preferred_element_type=F32)
                pv = part if pv is None else pv + part
                off += slab.shape[1]
            if first:
                acc_ref[hd] = pv
            else:
                acc_ref[hd] = jnp.exp2(m_ref[hd][0:1, :] - m_new) * acc_ref[hd] + pv
            m_ref[hd] = jnp.broadcast_to(m_new, (SUBLANES, tq))

    meta_bias = jnp.where(lax.broadcasted_iota(jnp.int32, (LANES, 1), 0) < N_META, 0.0, NEG_INF)
    attend([km_ref[:, p * LANES:(p + 1) * LANES] for p in range(n_pairs)], meta_bias,
           lambda hd: [vmt_ref[hd * VT_ROWS:(hd + 1) * VT_ROWS, :]], True)

    def att_body(c, carry):
        k0 = pl.multiple_of(c * kc, kc)
        bias = jnp.where(isc_ref[c] >= tau_row, 0.0, NEG_INF)
        attend([k_ref[0, pl.ds(k0, kc), p * LANES:(p + 1) * LANES] for p in range(n_pairs)], bias,
               lambda hd: [vt_ref[c * (kc // VT_TILE) + t, hd * VT_ROWS:(hd + 1) * VT_ROWS, :]
                           for t in range(kc // VT_TILE)], False)
        return carry
    lax.fori_loop(0, n_kc, att_body, 0)

    for p in range(n_pairs):
        halves = []
        for hd in (2 * p, 2 * p + 1):
            acc = acc_ref[hd]
            halves.append(acc[:A_HEAD_DIM, :] / acc[A_HEAD_DIM:A_HEAD_DIM + 1, :])
        o_ref[0, :, p * LANES:(p + 1) * LANES] = jnp.concatenate(halves, axis=0).T.astype(BF16)


def _sparse_attention(q, k, vt, km, vmt, iq, ik, misc, *, tq, kc, topk):
    b, s, _ = q.shape
    assert tq == LANES and kc % VT_TILE == 0 and s % kc == 0
    kernel = functools.partial(_attn_kernel, tq=tq, kc=kc, topk=topk, pos_bits=int(math.log2(s)))
    qblk = lambda bi, j: (bi, j, 0)
    full = lambda bi, j: (bi, 0, 0)
    fixed = lambda bi, j: (0, 0)
    vt_rows = A_HEADS * VT_ROWS
    return pl.pallas_call(
        kernel,
        grid=(b, s // tq),
        in_specs=[
            pl.BlockSpec((1, tq, A_WIDTH), qblk),
            pl.BlockSpec((1, s, A_WIDTH), full),
            pl.BlockSpec((s // VT_TILE, vt_rows, VT_TILE), full),
            pl.BlockSpec(km.shape, fixed),
            pl.BlockSpec(vmt.shape, fixed),
            pl.BlockSpec((1, tq, IDX_HEADS * IDX_DIM), qblk),
            pl.BlockSpec((1, s, LANES), full),
            pl.BlockSpec((1, tq, LANES), qblk),
        ],
        out_specs=pl.BlockSpec((1, tq, A_WIDTH), qblk),
        out_shape=jax.ShapeDtypeStruct((b, s, A_WIDTH), BF16),
        scratch_shapes=[
            pltpu.VMEM((s // kc, kc, tq), F32),
            pltpu.VMEM((IDX_HEADS * tq, LANES), BF16),
            pltpu.VMEM((A_WIDTH // LANES, 2 * tq, LANES), BF16),
            pltpu.VMEM((A_HEADS, max(kc, LANES), tq), F32),
            pltpu.VMEM((A_HEADS, SUBLANES, tq), F32),
            pltpu.VMEM((A_HEADS, VT_ROWS, tq), F32),
        ],
        compiler_params=pltpu.CompilerParams(dimension_semantics=("parallel", "arbitrary")),
        name="sparse_attention",
    )(q, k, vt, km, vmt, iq, ik, misc)


HALO = 8


def _softplus(x):
    return jnp.maximum(x, 0.0) + jnp.log1p(jnp.exp(-jnp.abs(x)))


def _bdot(a, b):
    return jnp.dot(a.astype(BF16), b.astype(BF16), preferred_element_type=F32)


def _hdot(a, b):
    return jnp.dot(a, b, precision=HIGHEST, preferred_element_type=F32)


def _dot3(a, b):
    a_hi = a.astype(BF16)
    b_hi = b.astype(BF16)
    a_lo = (a - a_hi.astype(F32)).astype(BF16)
    b_lo = (b - b_hi.astype(F32)).astype(BF16)
    dot = functools.partial(jnp.dot, preferred_element_type=F32)
    return dot(a_hi, b_hi) + (dot(a_hi, b_lo) + dot(a_lo, b_hi))


def _delta_kernel(qkv_ref, z_ref, misc_ref, qkvm_ref, miscm_ref, convw_ref, alog_ref, dtb_ref,
                  nw_ref, o_ref, state_ref, halo_ref):
    c = pl.program_id(1)
    n_pad = CHUNK - N_META
    nt = (((1,), (1,)), ((), ()))
    tn = (((0,), (0,)), ((), ()))

    @pl.when(c == 0)
    def _():
        state_ref[...] = jnp.zeros_like(state_ref)
        halo_ref[...] = jnp.zeros_like(halo_ref)

    is_meta = c == 0
    xin = jnp.where(is_meta, qkvm_ref[...], qkv_ref[0])
    misc = jnp.where(is_meta, miscm_ref[...], misc_ref[0])

    xcat = jnp.concatenate([halo_ref[...], xin], axis=0)
    halo_ref[...] = xin[CHUNK - HALO:, :]
    conv = None
    for tap in range(CONV_WIDTH):
        off = HALO - (CONV_WIDTH - 1) + tap
        term = xcat[off:off + CHUNK, :] * convw_ref[tap:tap + 1, :]
        conv = term if conv is None else conv + term
    xc = conv * jax.nn.sigmoid(conv)

    row = lax.broadcasted_iota(jnp.int32, (CHUNK, 1), 0)
    neutral = jnp.logical_and(is_meta, row < n_pad)
    beta_all = jnp.where(neutral, 0.0, jax.nn.sigmoid(misc))
    g_all = jnp.where(neutral, 0.0, -jnp.exp(alog_ref[...]) * _softplus(misc + dtb_ref[...]))

    ri = lax.broadcasted_iota(jnp.int32, (CHUNK, CHUNK), 0)
    ci = lax.broadcasted_iota(jnp.int32, (CHUNK, CHUNK), 1)
    incl = ri >= ci
    strict = ri > ci
    eye = (ri == ci).astype(F32)
    tri = incl.astype(F32)
    gc_all = _hdot(tri, g_all)
    gc_all_t = gc_all.T

    heads = range(B_HEADS)
    cols = lambda base, h: slice(base + h * B_HEAD_DIM, base + (h + 1) * B_HEAD_DIM)
    qn, kn, vb, kb, gc, decay = [], [], [], [], [], []
    for h in heads:
        qh, kh, vh = xc[:, cols(0, h)], xc[:, cols(B_WIDTH, h)], xc[:, cols(2 * B_WIDTH, h)]
        qn.append(qh * lax.rsqrt(jnp.sum(qh * qh, axis=-1, keepdims=True) + EPS) * (B_HEAD_DIM ** -0.5))
        kn.append(kh * lax.rsqrt(jnp.sum(kh * kh, axis=-1, keepdims=True) + EPS))
        beta = jnp.broadcast_to(beta_all[:, MISC_BETA + h:MISC_BETA + h + 1], (CHUNK, B_HEAD_DIM))
        gc.append(jnp.broadcast_to(gc_all[:, MISC_DECAY + h:MISC_DECAY + h + 1], (CHUNK, B_HEAD_DIM)))
        g_col = gc[h][:, :CHUNK]
        g_row = gc_all_t[MISC_DECAY + h:MISC_DECAY + h + 1, :]
        decay.append(jnp.where(incl, jnp.exp(jnp.where(incl, g_col - g_row, 0.0)), 0.0))
        kb.append(kn[h] * beta)
        vb.append(vh * beta)

    kk = [lax.dot_general(kb[h].astype(BF16), kn[h].astype(BF16), nt, preferred_element_type=F32)
          for h in heads]
    qk = [lax.dot_general(qn[h].astype(BF16), kn[h].astype(BF16), nt, preferred_element_type=F32)
          for h in heads]
    intra = [jnp.where(incl, qk[h] * decay[h], 0.0) for h in heads]

    pw = [jnp.where(strict, -(kk[h] * decay[h]), 0.0) for h in heads]
    t_inv = [eye + pw[h] for h in heads]
    pw = [_dot3(pw[h], pw[h]) for h in heads]
    for _ in range(int(math.log2(CHUNK)) - 2):
        t_next = [t_inv[h] + _dot3(t_inv[h], pw[h]) for h in heads]
        pw = [_dot3(pw[h], pw[h]) for h in heads]
        t_inv = t_next
    t_inv = [t_inv[h] + _dot3(t_inv[h], pw[h]) for h in heads]

    u = [_bdot(t_inv[h], vb[h]) for h in heads]
    w = [_bdot(t_inv[h], kb[h] * jnp.exp(gc[h])) for h in heads]

    state = [state_ref[h] for h in heads]
    w_s = [_bdot(w[h], state[h]) for h in heads]
    q_s = [_bdot(qn[h] * jnp.exp(gc[h]), state[h]) for h in heads]
    v_new = [u[h] - w_s[h] for h in heads]
    o = [q_s[h] + _bdot(intra[h], v_new[h]) for h in heads]
    for h in heads:
        g_last = gc[h][CHUNK - 1:CHUNK, :]
        kd = kn[h] * jnp.exp(g_last - gc[h])
        state_ref[h] = state[h] * jnp.exp(g_last) + lax.dot_general(
            kd.astype(BF16), v_new[h].astype(BF16), tn, preferred_element_type=F32)

    for h in heads:
        y = o[h] * lax.rsqrt(jnp.mean(o[h] * o[h], axis=-1, keepdims=True) + EPS) * nw_ref[...]
        zh = z_ref[0, :, cols(0, h)]
        o_ref[0, :, cols(0, h)] = (y * (zh * jax.nn.sigmoid(zh))).astype(BF16)


def _gated_deltanet(qkv, z, misc, qkv_m, misc_m, conv_w, a_log, dt_bias, norm_w):
    b, s, _ = qkv.shape
    n_chunks = s // CHUNK + 1
    blk = lambda bi, c: (bi, jnp.maximum(c - 1, 0), 0)
    fixed = lambda bi, c: (0, 0)
    lane_vec = lambda v: jnp.zeros((1, LANES), F32).at[0, MISC_DECAY:MISC_DECAY + B_HEADS].set(v)
    return pl.pallas_call(
        _delta_kernel,
        grid=(b, n_chunks),
        in_specs=[
            pl.BlockSpec((1, CHUNK, 3 * B_WIDTH), blk),
            pl.BlockSpec((1, CHUNK, B_WIDTH), blk),
            pl.BlockSpec((1, CHUNK, LANES), blk),
            pl.BlockSpec((CHUNK, 3 * B_WIDTH), fixed),
            pl.BlockSpec((CHUNK, LANES), fixed),
            pl.BlockSpec((CONV_WIDTH, 3 * B_WIDTH), fixed),
            pl.BlockSpec((1, LANES), fixed),
            pl.BlockSpec((1, LANES), fixed),
            pl.BlockSpec((1, B_HEAD_DIM), fixed),
        ],
        out_specs=pl.BlockSpec((1, CHUNK, B_WIDTH), blk),
        out_shape=jax.ShapeDtypeStruct((b, s, B_WIDTH), BF16),
        scratch_shapes=[
            pltpu.VMEM((B_HEADS, B_HEAD_DIM, B_HEAD_DIM), F32),
            pltpu.VMEM((HALO, 3 * B_WIDTH), F32),
        ],
        compiler_params=pltpu.CompilerParams(dimension_semantics=("parallel", "arbitrary")),
        name="gated_deltanet",
    )(qkv, z, misc, qkv_m, misc_m, conv_w, lane_vec(a_log), lane_vec(dt_bias),
      norm_w.reshape(1, B_HEAD_DIM))


def _out_router_kernel(oa_ref, ob_ref, x_ref, wo_ref, nw_ref, wr_ref, br_ref,
                       h_ref, xn_ref, gate_ref, eid_ref):
    mix = (jnp.dot(oa_ref[...], wo_ref[:A_WIDTH, :], preferred_element_type=F32)
           + jnp.dot(ob_ref[...], wo_ref[A_WIDTH:, :], preferred_element_type=F32))
    h = x_ref[...] + mix
    h_ref[...] = h
    xn = h * lax.rsqrt(jnp.mean(h * h, axis=-1, keepdims=True) + EPS) * nw_ref[...]
    xn_ref[...] = xn
    logits = _hdot(xn, wr_ref[...]) + br_ref[...]
    tm = logits.shape[0]
    lane = lax.broadcasted_iota(jnp.int32, (tm, LANES), 1)
    lane_f = lane.astype(F32)
    work = logits
    vals, idxs = [], []
    for _ in range(TOP_K):
        m = jnp.max(work, axis=-1, keepdims=True)
        idx = jnp.min(jnp.where(work == m, lane_f, float(LANES)), axis=-1, keepdims=True)
        vals.append(m)
        idxs.append(idx)
        work = jnp.where(lane_f == idx, NEG_INF, work)
    exps = [jnp.exp(v - vals[0]) for v in vals]
    denom = exps[0]
    for e in exps[1:]:
        denom = denom + e
    gates = jnp.zeros((tm, LANES), F32)
    eids = jnp.zeros((tm, LANES), F32)
    for kk in range(TOP_K):
        gates = jnp.where(lane == kk, exps[kk] / denom, gates)
        eids = jnp.where(lane == kk, idxs[kk], eids)
    gate_ref[...] = gates
    eid_ref[...] = eids.astype(jnp.int32)


def _out_router(o_a, o_b, x2d, w_out, norm_w, w_router, b_router, tm):
    n = x2d.shape[0]
    row = lambda i: (i, 0)
    fixed = lambda i: (0, 0)
    wr = jnp.zeros((D_MODEL, LANES), F32).at[:, :N_EXPERTS].set(w_router)
    br = jnp.full((1, LANES), NEG_INF, F32).at[0, :N_EXPERTS].set(b_router)
    return pl.pallas_call(
        _out_router_kernel,
        grid=(n // tm,),
        in_specs=[
            pl.BlockSpec((tm, A_WIDTH), row),
            pl.BlockSpec((tm, B_WIDTH), row),
            pl.BlockSpec((tm, D_MODEL), row),
            pl.BlockSpec((A_WIDTH + B_WIDTH, D_MODEL), fixed),
            pl.BlockSpec((1, D_MODEL), fixed),
            pl.BlockSpec((D_MODEL, LANES), fixed),
            pl.BlockSpec((1, LANES), fixed),
        ],
        out_specs=[pl.BlockSpec((tm, D_MODEL), row), pl.BlockSpec((tm, D_MODEL), row),
                   pl.BlockSpec((tm, LANES), row), pl.BlockSpec((tm, LANES), row)],
        out_shape=[jax.ShapeDtypeStruct((n, D_MODEL), F32), jax.ShapeDtypeStruct((n, D_MODEL), F32),
                   jax.ShapeDtypeStruct((n, LANES), F32), jax.ShapeDtypeStruct((n, LANES), jnp.int32)],
        compiler_params=pltpu.CompilerParams(dimension_semantics=("parallel",)),
        name="out_proj_router",
    )(o_a, o_b, x2d, w_out.astype(BF16), norm_w.reshape(1, D_MODEL), wr, br)


MOE_BM = 256
MOE_BF = 512


SC_CORES = 2
SC_SUBCORES = 16
SC_ROWS = 64


def _sc_gather_rows(table, idx):
    n_workers = SC_CORES * SC_SUBCORES
    n_rows = idx.shape[0]
    d = table.shape[1]
    assert n_rows % (n_workers * SC_ROWS) == 0
    rows_per_worker = n_rows // n_workers
    mesh = plsc.VectorSubcoreMesh(core_axis_name="c", subcore_axis_name="s",
                                  num_cores=SC_CORES, num_subcores=SC_SUBCORES)

    @functools.partial(
        pl.kernel, mesh=mesh,
        out_type=jax.ShapeDtypeStruct((n_rows, d), table.dtype),
        scratch_types=[pltpu.VMEM((SC_ROWS,), jnp.int32), pltpu.VMEM((SC_ROWS, d), table.dtype),
                       pltpu.SemaphoreType.DMA],
        name="sc_gather_rows",
    )
    def gather(table_hbm, idx_hbm, out_hbm, idx_v, rows_v, sem):
        wid = lax.axis_index("s") * SC_CORES + lax.axis_index("c")
        base = wid * rows_per_worker

        @pl.loop(0, rows_per_worker // SC_ROWS)
        def _(i):
            off = base + i * SC_ROWS
            pltpu.sync_copy(idx_hbm.at[pl.ds(off, SC_ROWS)], idx_v)
            pltpu.async_copy(table_hbm.at[idx_v], rows_v, sem).wait()
            pltpu.sync_copy(rows_v, out_hbm.at[pl.ds(off, SC_ROWS)])

    return gather(table, idx)


def _route_blocks(eid, bm):
    n_tok = eid.shape[0]
    n_assign = n_tok * TOP_K
    experts = jnp.arange(N_EXPERTS, dtype=jnp.int32)
    flat_e = eid.T.reshape(-1)
    sorted_e, order = lax.sort((flat_e, jnp.arange(n_assign, dtype=jnp.int32)), num_keys=1)
    onehot = sorted_e[:, None] == experts[None, :]
    counts = jnp.sum(onehot, axis=0, dtype=jnp.int32)
    padded = (counts + bm - 1) // bm * bm
    start = jnp.cumsum(counts) - counts
    pend = jnp.cumsum(padded)
    pstart = pend - padded
    dest = jnp.arange(n_assign, dtype=jnp.int32) + jnp.sum(
        jnp.where(onehot, (pstart - start)[None, :], 0), axis=1)
    n_blocks = -(-(n_assign + N_EXPERTS * (bm - 1)) // bm)
    blk_start = jnp.arange(n_blocks, dtype=jnp.int32) * bm
    block_e = jnp.minimum(jnp.sum(blk_start[:, None] >= pend[None, :], axis=1), N_EXPERTS - 1)
    n_valid = pend[-1] // bm
    row_tok = (jnp.arange(n_blocks * bm, dtype=jnp.int32) % n_tok).at[dest].set(order % n_tok)
    pos = jnp.zeros((n_assign,), jnp.int32).at[order].set(dest)
    return block_e.astype(jnp.int32), n_valid.astype(jnp.int32).reshape(1), row_tok, pos


def _moe_dense_kernel(be_ref, nv_ref, x_ref, wgu_ref, wd_ref, bg_ref, bu_ref, bd_ref, perm_ref, y_ref,
                      wg_s, wu_s, wd_s):
    i = pl.program_id(0)
    live = i < nv_ref[0]

    @pl.when(jnp.logical_and(live, jnp.logical_or(i == 0, be_ref[i] != be_ref[jnp.maximum(i - 1, 0)])))
    def _():
        perm = perm_ref[...]
        grp = 2 * LANES
        for gidx in range(2 * D_FF // grp):
            blk = wgu_ref[0, :, gidx * grp:(gidx + 1) * grp].astype(BF16)
            split = jnp.dot(blk, perm, preferred_element_type=F32).astype(BF16)
            wg_s[:, gidx * LANES:(gidx + 1) * LANES] = split[:, :LANES]
            wu_s[:, gidx * LANES:(gidx + 1) * LANES] = split[:, LANES:]
        wd_s[...] = wd_ref[0].astype(BF16)

    @pl.when(live)
    def _():
        x = x_ref[...].astype(BF16)
        y = None
        for f0 in range(0, D_FF, MOE_BF):
            g = jnp.dot(x, wg_s[:, f0:f0 + MOE_BF], preferred_element_type=F32) + bg_ref[0, :, f0:f0 + MOE_BF]
            u = jnp.dot(x, wu_s[:, f0:f0 + MOE_BF], preferred_element_type=F32) + bu_ref[0, :, f0:f0 + MOE_BF]
            gate = jnp.minimum(g, SWIGLU_LIMIT)
            up = jnp.clip(u, -SWIGLU_LIMIT, SWIGLU_LIMIT)
            t = gate * jax.nn.sigmoid(gate * SWIGLU_ALPHA) * (up + 1.0)
            part = jnp.dot(t.astype(BF16), wd_s[f0:f0 + MOE_BF, :], preferred_element_type=F32)
            y = part if y is None else y + part
        y_ref[...] = y + bd_ref[0]


def _moe_dense(x_sorted, block_e, n_valid, w_gate_up, w_down, b_gate, b_up, b_down, bm):
    n_blocks = block_e.shape[0]
    src = jnp.arange(2 * LANES)
    perm = jax.nn.one_hot((src % 2) * LANES + src // 2, 2 * LANES, dtype=BF16)
    rows = lambda i, be, nv: (jnp.minimum(i, nv[0] - 1), 0)
    wsel = lambda i, be, nv: (be[i], 0, 0)
    fixed = lambda i, be, nv: (0, 0)
    grid_spec = pltpu.PrefetchScalarGridSpec(
        num_scalar_prefetch=2,
        grid=(n_blocks,),
        in_specs=[
            pl.BlockSpec((bm, D_MODEL), rows),
            pl.BlockSpec((1, D_MODEL, 2 * D_FF), wsel),
            pl.BlockSpec((1, D_FF, D_MODEL), wsel),
            pl.BlockSpec((1, 1, D_FF), wsel),
            pl.BlockSpec((1, 1, D_FF), wsel),
            pl.BlockSpec((1, 1, D_MODEL), wsel),
            pl.BlockSpec((2 * LANES, 2 * LANES), fixed),
        ],
        out_specs=pl.BlockSpec((bm, D_MODEL), rows),
        scratch_shapes=[
            pltpu.VMEM((D_MODEL, D_FF), BF16),
            pltpu.VMEM((D_MODEL, D_FF), BF16),
            pltpu.VMEM((D_FF, D_MODEL), BF16),
        ],
    )
    return pl.pallas_call(
        _moe_dense_kernel,
        grid_spec=grid_spec,
        out_shape=jax.ShapeDtypeStruct(x_sorted.shape, F32),
        compiler_params=pltpu.CompilerParams(dimension_semantics=("arbitrary",)),
        name="moe_experts",
    )(block_e, n_valid, x_sorted, w_gate_up, w_down, b_gate, b_up, b_down, perm)


def _combine_kernel(h_ref, gate_ref, y0_ref, y1_ref, y2_ref, y3_ref, o_ref):
    gates = gate_ref[...]
    out = h_ref[...]
    for kk, y_ref in enumerate((y0_ref, y1_ref, y2_ref, y3_ref)):
        out = out + gates[:, kk:kk + 1] * y_ref[...]
    o_ref[...] = out


def _combine(h, gates, y, tm):
    n_tok = h.shape[0]
    nt = n_tok // tm
    row = lambda i: (i, 0)
    return pl.pallas_call(
        _combine_kernel,
        grid=(nt,),
        in_specs=[pl.BlockSpec((tm, D_MODEL), row), pl.BlockSpec((tm, LANES), row)]
        + [pl.BlockSpec((tm, D_MODEL), functools.partial(lambda kk, i: (kk * nt + i, 0), kk))
           for kk in range(TOP_K)],
        out_specs=pl.BlockSpec((tm, D_MODEL), row),
        out_shape=jax.ShapeDtypeStruct((n_tok, D_MODEL), F32),
        compiler_params=pltpu.CompilerParams(dimension_semantics=("parallel",)),
        name="moe_combine",
    )(h, gates, y, y, y, y)


def kernel(x, meta_tokens, norm_mix_w, w_in, q_norm_w, k_norm_w, kv_norm_w, w_kv_up, conv_w, a_log,
           dt_bias, delta_norm_w, w_out, norm_ffn_w, w_router, b_router, w_gate_up, b_gate_up,
           w_down, b_down):
    b, s, d = x.shape
    consts = _pack_in_proj_weights(norm_mix_w[0], w_in[0], q_norm_w[0], k_norm_w[0], kv_norm_w[0],
                                   w_kv_up[0])
    cos, sin = _rope_tables(N_META + s)
    real = _in_proj(x.reshape(b * s, d), cos[N_META:], sin[N_META:], consts, VT_TILE)
    meta = _in_proj(meta_tokens, cos[:N_META], sin[:N_META], consts, N_META)
    vt = real[2]
    q, k, iq, ik, misc, qkv, z = (a.reshape(b, s, a.shape[-1]) for a in real[:2] + real[3:])
    km = jnp.pad(meta[1], ((0, LANES - N_META), (0, 0)))
    vmt = jnp.pad(meta[2][0], ((0, 0), (0, LANES - N_META)))
    o_a = _sparse_attention(q, k, vt, km, vmt, iq, ik, misc,
                            tq=LANES, kc=min(512, s), topk=min(INDEX_TOPK, s // 4))
    lead = lambda a: jnp.pad(a, ((CHUNK - N_META, 0), (0, 0)))
    o_b = _gated_deltanet(qkv, z, misc, lead(meta[6]), lead(meta[5]), conv_w[0], a_log[0],
                          dt_bias[0], delta_norm_w[0])
    n_tok = b * s
    h, xn, gates, eid = _out_router(o_a.reshape(n_tok, A_WIDTH), o_b.reshape(n_tok, B_WIDTH),
                                    x.reshape(n_tok, d), w_out[0], norm_ffn_w[0], w_router[0],
                                    b_router[0], 256)
    block_e, n_valid, row_tok, pos = _route_blocks(eid[:, :TOP_K], MOE_BM)
    bgu = b_gate_up[0].reshape(N_EXPERTS, 1, 2 * D_FF)
    x_sorted = _sc_gather_rows(xn, row_tok)
    y_sorted = _moe_dense(x_sorted, block_e, n_valid, w_gate_up[0], w_down[0], bgu[:, :, 0::2],
                          bgu[:, :, 1::2], b_down[0].reshape(N_EXPERTS, 1, D_MODEL), MOE_BM)
    y = _sc_gather_rows(y_sorted, pos)
    out = _combine(h, gates, y, 256)
    return out.reshape(b, s, d)
```

```python
import functools
import math

import jax
import jax.numpy as jnp
from jax import lax
from jax.experimental import pallas as pl
from jax.experimental.pallas import tpu as pltpu
from jax.experimental.pallas import tpu_sc as plsc

F32 = jnp.float32
BF16 = jnp.bfloat16
HIGHEST = lax.Precision.HIGHEST

D_MODEL = 1024
N_META = 16
ROPE_THETA = 10000.0
EPS = 1e-6
A_HEAD_DIM = 64
A_HEADS = 8
A_WIDTH = A_HEADS * A_HEAD_DIM
KV_RANK = 256
IDX_HEADS = 8
IDX_DIM = 64
INDEX_TOPK = 256
B_HEAD_DIM = 128
B_HEADS = 4
B_WIDTH = B_HEADS * B_HEAD_DIM
CONV_WIDTH = 4
CHUNK = 64
N_EXPERTS = 32
TOP_K = 4
D_FF = D_MODEL
SWIGLU_LIMIT = 7.0
SWIGLU_ALPHA = 1.702
IN_SPLITS = (A_WIDTH, KV_RANK, IDX_HEADS * IDX_DIM, IDX_DIM, IDX_HEADS, 3 * B_WIDTH, B_WIDTH,
             B_HEADS, B_HEADS)

LANES = 128

C_Q = 0
C_CKV = C_Q + A_WIDTH
C_IQ = C_CKV + KV_RANK
C_IK = C_IQ + IDX_HEADS * IDX_DIM
C_MISC = C_IK + LANES
C_QKV = C_MISC + LANES
C_Z = C_QKV + 3 * B_WIDTH
C_END = C_Z + B_WIDTH
MISC_W, MISC_BETA, MISC_DECAY = 0, IDX_HEADS, IDX_HEADS + B_HEADS
LOG2E = math.log2(math.e)
VT_ROWS = A_HEAD_DIM + 16


def _rope_partner(a):
    lane = lax.broadcasted_iota(jnp.int32, a.shape, 1)
    first_half = (lane % A_HEAD_DIM) < (A_HEAD_DIM // 2)
    return jnp.where(first_half, pltpu.roll(a, LANES - A_HEAD_DIM // 2, 1),
                     pltpu.roll(a, A_HEAD_DIM // 2, 1))


def _rope(a, cos, sin_signed):
    return a * cos + _rope_partner(a) * sin_signed


def _head_rms(a, head_mean, gain):
    msq = jnp.dot((a * a).astype(BF16), head_mean, preferred_element_type=F32)
    return a * lax.rsqrt(msq + EPS) * gain


def _in_proj_kernel(x_ref, nw_ref, w_ref, wkv_ref, qnw_ref, knw_ref, kvnw_ref, cos_ref, sin_ref,
                    hm_ref, q_ref, k_ref, vt_ref, iq_ref, ik_ref, misc_ref, qkv_ref, z_ref):
    x = x_ref[...]
    u = x * lax.rsqrt(jnp.mean(x * x, axis=-1, keepdims=True) + EPS) * nw_ref[...]
    ub = u.astype(BF16)

    def proj(c0, c1):
        return jnp.dot(ub, w_ref[:, c0:c1], preferred_element_type=F32)

    cos = cos_ref[...]
    sin = sin_ref[...]
    hm = hm_ref[...]

    def rope_groups(a):
        return jnp.concatenate(
            [_rope(a[:, g * LANES:(g + 1) * LANES], cos, sin) for g in range(a.shape[1] // LANES)],
            axis=1)

    q = _head_rms(proj(C_Q, C_CKV), hm, qnw_ref[...])
    q_ref[...] = (rope_groups(q) * (A_HEAD_DIM ** -0.5 * LOG2E)).astype(BF16)

    ckv = proj(C_CKV, C_IQ)
    ckv = ckv * lax.rsqrt(jnp.mean(ckv * ckv, axis=-1, keepdims=True) + EPS) * kvnw_ref[...]
    kv = jnp.dot(ckv.astype(BF16), wkv_ref[...], preferred_element_type=F32)
    k = _head_rms(kv[:, :A_WIDTH], hm, knw_ref[...])
    k_ref[...] = rope_groups(k).astype(BF16)
    tm = x.shape[0]
    ones = jnp.ones((VT_ROWS - A_HEAD_DIM, tm), BF16)
    for g in range(A_WIDTH // LANES):
        vt = kv[:, A_WIDTH + g * LANES:A_WIDTH + (g + 1) * LANES].T.astype(BF16)
        for half in range(2):
            r0 = (2 * g + half) * VT_ROWS
            vt_ref[0, r0:r0 + A_HEAD_DIM, :] = vt[half * A_HEAD_DIM:(half + 1) * A_HEAD_DIM, :]
            vt_ref[0, r0 + A_HEAD_DIM:r0 + VT_ROWS, :] = ones

    iq_ref[...] = rope_groups(proj(C_IQ, C_IK)).astype(BF16)
    ik_ref[...] = _rope(proj(C_IK, C_MISC), cos, sin).astype(BF16)

    lane = lax.broadcasted_iota(jnp.int32, (1, LANES), 1)
    w_scale = jnp.where(lane < IDX_HEADS, IDX_HEADS ** -0.5 * IDX_DIM ** -0.5, 1.0)
    misc_ref[...] = proj(C_MISC, C_QKV) * w_scale
    qkv_ref[...] = proj(C_QKV, C_Z)
    z_ref[...] = proj(C_Z, C_END)


def _in_proj(x2d, cos, sin, consts, tm):
    n = x2d.shape[0]
    n_pos_blocks = cos.shape[0] // tm
    nw, w_pack, wkv, qnw, knw, kvnw, hm = consts
    row = lambda i: (i, 0)
    fixed = lambda i: (0, 0)
    pos = lambda i: (i % n_pos_blocks, 0)
    out_widths = (A_WIDTH, A_WIDTH, None, IDX_HEADS * IDX_DIM, LANES, LANES, 3 * B_WIDTH, B_WIDTH)
    out_dtypes = (BF16, BF16, BF16, BF16, BF16, F32, F32, F32)
    vt_rows = A_HEADS * VT_ROWS
    out_specs = [pl.BlockSpec((1, vt_rows, tm), lambda i: (i, 0, 0)) if w is None
                 else pl.BlockSpec((tm, w), row) for w in out_widths]
    out_shape = [jax.ShapeDtypeStruct((n // tm, vt_rows, tm) if w is None else (n, w), dt)
                 for w, dt in zip(out_widths, out_dtypes)]
    return pl.pallas_call(
        _in_proj_kernel,
        grid=(n // tm,),
        in_specs=[
            pl.BlockSpec((tm, D_MODEL), row),
            pl.BlockSpec(nw.shape, fixed),
            pl.BlockSpec(w_pack.shape, fixed),
            pl.BlockSpec(wkv.shape, fixed),
            pl.BlockSpec(qnw.shape, fixed),
            pl.BlockSpec(knw.shape, fixed),
            pl.BlockSpec(kvnw.shape, fixed),
            pl.BlockSpec((tm, LANES), pos),
            pl.BlockSpec((tm, LANES), pos),
            pl.BlockSpec(hm.shape, fixed),
        ],
        out_specs=out_specs,
        out_shape=out_shape,
        compiler_params=pltpu.CompilerParams(dimension_semantics=("parallel",)),
        name="in_proj",
    )(x2d, nw, w_pack, wkv, qnw, knw, kvnw, cos, sin, hm)


def _pack_in_proj_weights(norm_mix_w, w_in, q_norm_w, k_norm_w, kv_norm_w, w_kv_up):
    points = []
    acc = 0
    for s in IN_SPLITS:
        points.append((acc, acc + s))
        acc += s
    a_q, a_ckv, i_q, i_k, i_w, b_qkv, b_z, b_beta, b_a = (w_in[:, a:b] for a, b in points)
    misc = jnp.concatenate(
        [i_w, b_beta, b_a, jnp.zeros((D_MODEL, LANES - IDX_HEADS - 2 * B_HEADS), w_in.dtype)], axis=1)
    w_pack = jnp.concatenate([a_q, a_ckv, i_q, i_k, i_k, misc, b_qkv, b_z], axis=1).astype(BF16)
    head_mean = jnp.kron(jnp.eye(A_HEADS, dtype=F32),
                         jnp.full((A_HEAD_DIM, A_HEAD_DIM), 1.0 / A_HEAD_DIM, F32)).astype(BF16)
    return (norm_mix_w.reshape(1, D_MODEL), w_pack, w_kv_up.astype(BF16),
            jnp.tile(q_norm_w, A_HEADS).reshape(1, A_WIDTH),
            jnp.tile(k_norm_w, A_HEADS).reshape(1, A_WIDTH),
            kv_norm_w.reshape(1, KV_RANK), head_mean)


def _rope_tables(n_pos):
    half = A_HEAD_DIM // 2
    inv_freq = ROPE_THETA ** (-jnp.arange(0, A_HEAD_DIM, 2, dtype=F32) / A_HEAD_DIM)
    ang = jnp.arange(n_pos, dtype=F32)[:, None] * inv_freq[None, :]
    cos, sin = jnp.cos(ang), jnp.sin(ang)
    cos128 = jnp.tile(cos, (1, LANES // half))
    sin128 = jnp.tile(jnp.concatenate([-sin, sin], axis=1), (1, LANES // A_HEAD_DIM))
    return cos128, sin128


NEG_INF = float("-inf")
F32_MAX = float(jnp.finfo(jnp.float32).max)
INT_MIN = -2 ** 31


VT_TILE = 256
SUBLANES = 8


def _attn_kernel(q_ref, k_ref, vt_ref, km_ref, vmt_ref, iq_ref, ik_ref, misc_ref, o_ref,
                 isc_ref, lhs_ref, qm_ref, s_ref, m_ref, acc_ref, *, tq, kc, topk, pos_bits):
    j = pl.program_id(1)
    n_kc = lax.div((j + 1) * tq + (kc - 1), kc)
    g = kc // SUBLANES
    lane = lax.broadcasted_iota(jnp.int32, (1, LANES), 1)
    lo_half = lane < A_HEAD_DIM
    n_pairs = A_WIDTH // LANES
    nt = (((1,), (1,)), ((), ()))

    iq = iq_ref[0]
    q = q_ref[0]
    zero = jnp.zeros((), BF16)
    for p in range(n_pairs):
        blk = iq[:, p * LANES:(p + 1) * LANES]
        lhs_ref[(2 * p) * tq:(2 * p + 1) * tq, :] = jnp.where(lo_half, blk, zero)
        lhs_ref[(2 * p + 1) * tq:(2 * p + 2) * tq, :] = jnp.where(lo_half, zero, blk)
        qb = q[:, p * LANES:(p + 1) * LANES]
        qm_ref[p, :tq, :] = jnp.where(lo_half, qb, zero)
        qm_ref[p, tq:, :] = jnp.where(lo_half, zero, qb)
    w_t = misc_ref[0].T

    def fold0(x3, op):
        n = x3.shape[0]
        while n > 1:
            x3 = op(x3[:n // 2], x3[n // 2:n])
            n //= 2
        return x3[0]

    q_pos = j * tq + lax.broadcasted_iota(jnp.int32, (1, tq), 1)

    def idx_body(c, carry):
        k0 = pl.multiple_of(c * kc, kc)
        r = lax.dot_general(ik_ref[0, pl.ds(k0, kc), :], lhs_ref[...], nt,
                            preferred_element_type=F32)
        s = None
        for h in range(IDX_HEADS):
            term = jnp.maximum(r[:, h * tq:(h + 1) * tq], 0.0) * w_t[MISC_W + h:MISC_W + h + 1, :]
            s = term if s is None else s + term
        k_pos = k0 + lax.broadcasted_iota(jnp.int32, (kc, 1), 0)
        isc_ref[c] = jnp.where(k_pos <= q_pos, s, NEG_INF)
        return carry
    lax.fori_loop(0, n_kc, idx_body, 0)

    def count(pred):
        def body(c, acc):
            x3 = isc_ref[c].reshape(g, SUBLANES, tq)
            return acc + fold0(jnp.where(pred(x3, c), 1.0, 0.0), jnp.add)
        acc = lax.fori_loop(0, n_kc, body, jnp.zeros((SUBLANES, tq), F32))
        return jnp.broadcast_to(jnp.sum(acc, axis=0, keepdims=True), (SUBLANES, tq))

    def key_to_float(u):
        key = u ^ jnp.int32(INT_MIN)
        bits = jnp.where(key >= 0, key, key ^ jnp.int32(0x7FFFFFFF))
        return lax.bitcast_convert_type(bits, F32)

    def bit_body(i, u):
        u2 = u | lax.shift_left(jnp.int32(1), 31 - i)
        cand = key_to_float(u2)
        n_ge = count(lambda x3, c: x3 >= cand[None])
        return jnp.where(n_ge >= topk, u2, u)
    u = lax.fori_loop(0, 32, bit_body, jnp.zeros((SUBLANES, tq), jnp.int32))
    few = (u >= 0) & (u < 0x00800000)
    tau = jnp.where(few, -F32_MAX, key_to_float(u))

    n_ge = count(lambda x3, c: x3 >= tau[None])

    @pl.when(jnp.max(n_ge) > topk)
    def _():
        n_gt = count(lambda x3, c: x3 > tau[None])
        need = topk - n_gt
        def chunk_pos(c):
            return (c * kc + lax.broadcasted_iota(jnp.int32, (g, SUBLANES, tq), 0) * SUBLANES
                    + lax.broadcasted_iota(jnp.int32, (g, SUBLANES, tq), 1))
        def pos_body(i, cut):
            cut2 = cut | lax.shift_left(jnp.int32(1), pos_bits - 1 - i)
            ties_before = count(lambda x3, c: (x3 == tau[None]) & (chunk_pos(c) < cut2[None]))
            return jnp.where(ties_before < need, cut2, cut)
        cut = lax.fori_loop(0, pos_bits, pos_body, jnp.zeros((SUBLANES, tq), jnp.int32))
        def drop_body(c, carry):
            x3 = isc_ref[c].reshape(g, SUBLANES, tq)
            drop = (x3 == tau[None]) & (chunk_pos(c) > cut[None])
            isc_ref[c] = jnp.where(drop, NEG_INF, x3).reshape(kc, tq)
            return carry
        lax.fori_loop(0, n_kc, drop_body, 0)

    tau_row = tau[0:1, :]

    def attend(key_pairs, bias, vt_slabs, first):
        n = key_pairs[0].shape[0]
        m_cur = []
        for p in range(n_pairs):
            s2 = lax.dot_general(key_pairs[p], qm_ref[p], nt, preferred_element_type=F32)
            for half in range(2):
                s = s2[:, half * tq:(half + 1) * tq] + bias
                s_ref[2 * p + half, :n, :] = s
                m8 = fold0(s.reshape(n // SUBLANES, SUBLANES, tq), jnp.maximum)
                m_cur.append(jnp.max(m8, axis=0, keepdims=True))
        for hd in range(A_HEADS):
            m_new = m_cur[hd] if first else jnp.maximum(m_ref[hd][0:1, :], m_cur[hd])
            e = jnp.exp2(s_ref[hd, :n, :] - m_new).astype(BF16)
            pv = None
            off = 0
            for slab in vt_slabs(hd):
                part = jnp.dot(slab, e[off:off + slab.shape[1], :], preferred_element_type=F32)
                pv = part if pv is None else pv + part
                off += slab.shape[1]
            if first:
                acc_ref[hd] = pv
            else:
                acc_ref[hd] = jnp.exp2(m_ref[hd][0:1, :] - m_new) * acc_ref[hd] + pv
            m_ref[hd] = jnp.broadcast_to(m_new, (SUBLANES, tq))

    meta_bias = jnp.where(lax.broadcasted_iota(jnp.int32, (LANES, 1), 0) < N_META, 0.0, NEG_INF)
    attend([km_ref[:, p * LANES:(p + 1) * LANES] for p in range(n_pairs)], meta_bias,
           lambda hd: [vmt_ref[hd * VT_ROWS:(hd + 1) * VT_ROWS, :]], True)

    def att_body(c, carry):
        k0 = pl.multiple_of(c * kc, kc)
        bias = jnp.where(isc_ref[c] >= tau_row, 0.0, NEG_INF)
        attend([k_ref[0, pl.ds(k0, kc), p * LANES:(p + 1) * LANES] for p in range(n_pairs)], bias,
               lambda hd: [vt_ref[c * (kc // VT_TILE) + t, hd * VT_ROWS:(hd + 1) * VT_ROWS, :]
                           for t in range(kc // VT_TILE)], False)
        return carry
    lax.fori_loop(0, n_kc, att_body, 0)

    for p in range(n_pairs):
        halves = []
        for hd in (2 * p, 2 * p + 1):
            acc = acc_ref[hd]
            halves.append(acc[:A_HEAD_DIM, :] / acc[A_HEAD_DIM:A_HEAD_DIM + 1, :])
        o_ref[0, :, p * LANES:(p + 1) * LANES] = jnp.concatenate(halves, axis=0).T.astype(BF16)


def _sparse_attention(q, k, vt, km, vmt, iq, ik, misc, *, tq, kc, topk):
    b, s, _ = q.shape
    assert tq == LANES and kc % VT_TILE == 0 and s % kc == 0
    kernel = functools.partial(_attn_kernel, tq=tq, kc=kc, topk=topk, pos_bits=int(math.log2(s)))
    qblk = lambda bi, j: (bi, j, 0)
    full = lambda bi, j: (bi, 0, 0)
    fixed = lambda bi, j: (0, 0)
    vt_rows = A_HEADS * VT_ROWS
    return pl.pallas_call(
        kernel,
        grid=(b, s // tq),
        in_specs=[
            pl.BlockSpec((1, tq, A_WIDTH), qblk),
            pl.BlockSpec((1, s, A_WIDTH), full),
            pl.BlockSpec((s // VT_TILE, vt_rows, VT_TILE), full),
            pl.BlockSpec(km.shape, fixed),
            pl.BlockSpec(vmt.shape, fixed),
            pl.BlockSpec((1, tq, IDX_HEADS * IDX_DIM), qblk),
            pl.BlockSpec((1, s, LANES), full),
            pl.BlockSpec((1, tq, LANES), qblk),
        ],
        out_specs=pl.BlockSpec((1, tq, A_WIDTH), qblk),
        out_shape=jax.ShapeDtypeStruct((b, s, A_WIDTH), BF16),
        scratch_shapes=[
            pltpu.VMEM((s // kc, kc, tq), F32),
            pltpu.VMEM((IDX_HEADS * tq, LANES), BF16),
            pltpu.VMEM((A_WIDTH // LANES, 2 * tq, LANES), BF16),
            pltpu.VMEM((A_HEADS, max(kc, LANES), tq), F32),
            pltpu.VMEM((A_HEADS, SUBLANES, tq), F32),
            pltpu.VMEM((A_HEADS, VT_ROWS, tq), F32),
        ],
        compiler_params=pltpu.CompilerParams(dimension_semantics=("parallel", "arbitrary")),
        name="sparse_attention",
    )(q, k, vt, km, vmt, iq, ik, misc)


HALO = 8


def _softplus(x):
    return jnp.maximum(x, 0.0) + jnp.log1p(jnp.exp(-jnp.abs(x)))


def _bdot(a, b):
    return jnp.dot(a.astype(BF16), b.astype(BF16), preferred_element_type=F32)


def _hdot(a, b):
    return jnp.dot(a, b, precision=HIGHEST, preferred_element_type=F32)


def _dot3(a, b):
    a_hi = a.astype(BF16)
    b_hi = b.astype(BF16)
    a_lo = (a - a_hi.astype(F32)).astype(BF16)
    b_lo = (b - b_hi.astype(F32)).astype(BF16)
    dot = functools.partial(jnp.dot, preferred_element_type=F32)
    return dot(a_hi, b_hi) + (dot(a_hi, b_lo) + dot(a_lo, b_hi))


def _delta_kernel(qkv_ref, z_ref, misc_ref, qkvm_ref, miscm_ref, convw_ref, alog_ref, dtb_ref,
                  nw_ref, o_ref, state_ref, halo_ref):
    c = pl.program_id(1)
    n_pad = CHUNK - N_META
    nt = (((1,), (1,)), ((), ()))
    tn = (((0,), (0,)), ((), ()))

    @pl.when(c == 0)
    def _():
        state_ref[...] = jnp.zeros_like(state_ref)
        halo_ref[...] = jnp.zeros_like(halo_ref)

    is_meta = c == 0
    xin = jnp.where(is_meta, qkvm_ref[...], qkv_ref[0])
    misc = jnp.where(is_meta, miscm_ref[...], misc_ref[0])

    xcat = jnp.concatenate([halo_ref[...], xin], axis=0)
    halo_ref[...] = xin[CHUNK - HALO:, :]
    conv = None
    for tap in range(CONV_WIDTH):
        off = HALO - (CONV_WIDTH - 1) + tap
        term = xcat[off:off + CHUNK, :] * convw_ref[tap:tap + 1, :]
        conv = term if conv is None else conv + term
    xc = conv * jax.nn.sigmoid(conv)

    row = lax.broadcasted_iota(jnp.int32, (CHUNK, 1), 0)
    neutral = jnp.logical_and(is_meta, row < n_pad)
    beta_all = jnp.where(neutral, 0.0, jax.nn.sigmoid(misc))
    g_all = jnp.where(neutral, 0.0, -jnp.exp(alog_ref[...]) * _softplus(misc + dtb_ref[...]))

    ri = lax.broadcasted_iota(jnp.int32, (CHUNK, CHUNK), 0)
    ci = lax.broadcasted_iota(jnp.int32, (CHUNK, CHUNK), 1)
    incl = ri >= ci
    strict = ri > ci
    eye = (ri == ci).astype(F32)
    tri = incl.astype(F32)
    gc_all = _hdot(tri, g_all)
    gc_all_t = gc_all.T

    heads = range(B_HEADS)
    cols = lambda base, h: slice(base + h * B_HEAD_DIM, base + (h + 1) * B_HEAD_DIM)
    qn, kn, vb, kb, gc, decay = [], [], [], [], [], []
    for h in heads:
        qh, kh, vh = xc[:, cols(0, h)], xc[:, cols(B_WIDTH, h)], xc[:, cols(2 * B_WIDTH, h)]
        qn.append(qh * lax.rsqrt(jnp.sum(qh * qh, axis=-1, keepdims=True) + EPS) * (B_HEAD_DIM ** -0.5))
        kn.append(kh * lax.rsqrt(jnp.sum(kh * kh, axis=-1, keepdims=True) + EPS))
        beta = jnp.broadcast_to(beta_all[:, MISC_BETA + h:MISC_BETA + h + 1], (CHUNK, B_HEAD_DIM))
        gc.append(jnp.broadcast_to(gc_all[:, MISC_DECAY + h:MISC_DECAY + h + 1], (CHUNK, B_HEAD_DIM)))
        g_col = gc[h][:, :CHUNK]
        g_row = gc_all_t[MISC_DECAY + h:MISC_DECAY + h + 1, :]
        decay.append(jnp.where(incl, jnp.exp(jnp.where(incl, g_col - g_row, 0.0)), 0.0))
        kb.append(kn[h] * beta)
        vb.append(vh * beta)

    kk = [lax.dot_general(kb[h].astype(BF16), kn[h].astype(BF16), nt, preferred_element_type=F32)
          for h in heads]
    qk = [lax.dot_general(qn[h].astype(BF16), kn[h].astype(BF16), nt, preferred_element_type=F32)
          for h in heads]
    intra = [jnp.where(incl, qk[h] * decay[h], 0.0) for h in heads]

    pw = [jnp.where(strict, -(kk[h] * decay[h]), 0.0) for h in heads]
    t_inv = [eye + pw[h] for h in heads]
    pw = [_dot3(pw[h], pw[h]) for h in heads]
    for _ in range(int(math.log2(CHUNK)) - 2):
        t_next = [t_inv[h] + _dot3(t_inv[h], pw[h]) for h in heads]
        pw = [_dot3(pw[h], pw[h]) for h in heads]
        t_inv = t_next
    t_inv = [t_inv[h] + _dot3(t_inv[h], pw[h]) for h in heads]

    u = [_bdot(t_inv[h], vb[h]) for h in heads]
    w = [_bdot(t_inv[h], kb[h] * jnp.exp(gc[h])) for h in heads]

    state = [state_ref[h] for h in heads]
    w_s = [_bdot(w[h], state[h]) for h in heads]
    q_s = [_bdot(qn[h] * jnp.exp(gc[h]), state[h]) for h in heads]
    v_new = [u[h] - w_s[h] for h in heads]
    o = [q_s[h] + _bdot(intra[h], v_new[h]) for h in heads]
    for h in heads:
        g_last = gc[h][CHUNK - 1:CHUNK, :]
        kd = kn[h] * jnp.exp(g_last - gc[h])
        state_ref[h] = state[h] * jnp.exp(g_last) + lax.dot_general(
            kd.astype(BF16), v_new[h].astype(BF16), tn, preferred_element_type=F32)

    for h in heads:
        y = o[h] * lax.rsqrt(jnp.mean(o[h] * o[h], axis=-1, keepdims=True) + EPS) * nw_ref[...]
        zh = z_ref[0, :, cols(0, h)]
        o_ref[0, :, cols(0, h)] = (y * (zh * jax.nn.sigmoid(zh))).astype(BF16)


def _gated_deltanet(qkv, z, misc, qkv_m, misc_m, conv_w, a_log, dt_bias, norm_w):
    b, s, _ = qkv.shape
    n_chunks = s // CHUNK + 1
    blk = lambda bi, c: (bi, jnp.maximum(c - 1, 0), 0)
    fixed = lambda bi, c: (0, 0)
    lane_vec = lambda v: jnp.zeros((1, LANES), F32).at[0, MISC_DECAY:MISC_DECAY + B_HEADS].set(v)
    return pl.pallas_call(
        _delta_kernel,
        grid=(b, n_chunks),
        in_specs=[
            pl.BlockSpec((1, CHUNK, 3 * B_WIDTH), blk),
            pl.BlockSpec((1, CHUNK, B_WIDTH), blk),
            pl.BlockSpec((1, CHUNK, LANES), blk),
            pl.BlockSpec((CHUNK, 3 * B_WIDTH), fixed),
            pl.BlockSpec((CHUNK, LANES), fixed),
            pl.BlockSpec((CONV_WIDTH, 3 * B_WIDTH), fixed),
            pl.BlockSpec((1, LANES), fixed),
            pl.BlockSpec((1, LANES), fixed),
            pl.BlockSpec((1, B_HEAD_DIM), fixed),
        ],
        out_specs=pl.BlockSpec((1, CHUNK, B_WIDTH), blk),
        out_shape=jax.ShapeDtypeStruct((b, s, B_WIDTH), BF16),
        scratch_shapes=[
            pltpu.VMEM((B_HEADS, B_HEAD_DIM, B_HEAD_DIM), F32),
            pltpu.VMEM((HALO, 3 * B_WIDTH), F32),
        ],
        compiler_params=pltpu.CompilerParams(dimension_semantics=("parallel", "arbitrary")),
        name="gated_deltanet",
    )(qkv, z, misc, qkv_m, misc_m, conv_w, lane_vec(a_log), lane_vec(dt_bias),
      norm_w.reshape(1, B_HEAD_DIM))


def _out_router_kernel(oa_ref, ob_ref, x_ref, wo_ref, nw_ref, wr_ref, br_ref,
                       h_ref, xn_ref, gate_ref, eid_ref):
    mix = (jnp.dot(oa_ref[...], wo_ref[:A_WIDTH, :], preferred_element_type=F32)
           + jnp.dot(ob_ref[...], wo_ref[A_WIDTH:, :], preferred_element_type=F32))
    h = x_ref[...] + mix
    h_ref[...] = h
    xn = h * lax.rsqrt(jnp.mean(h * h, axis=-1, keepdims=True) + EPS) * nw_ref[...]
    xn_ref[...] = xn
    logits = _hdot(xn, wr_ref[...]) + br_ref[...]
    tm = logits.shape[0]
    lane = lax.broadcasted_iota(jnp.int32, (tm, LANES), 1)
    lane_f = lane.astype(F32)
    work = logits
    vals, idxs = [], []
    for _ in range(TOP_K):
        m = jnp.max(work, axis=-1, keepdims=True)
        idx = jnp.min(jnp.where(work == m, lane_f, float(LANES)), axis=-1, keepdims=True)
        vals.append(m)
        idxs.append(idx)
        work = jnp.where(lane_f == idx, NEG_INF, work)
    exps = [jnp.exp(v - vals[0]) for v in vals]
    denom = exps[0]
    for e in exps[1:]:
        denom = denom + e
    gates = jnp.zeros((tm, LANES), F32)
    eids = jnp.zeros((tm, LANES), F32)
    for kk in range(TOP_K):
        gates = jnp.where(lane == kk, exps[kk] / denom, gates)
        eids = jnp.where(lane == kk, idxs[kk], eids)
    gate_ref[...] = gates
    eid_ref[...] = eids.astype(jnp.int32)


def _out_router(o_a, o_b, x2d, w_out, norm_w, w_router, b_router, tm):
    n = x2d.shape[0]
    row = lambda i: (i, 0)
    fixed = lambda i: (0, 0)
    wr = jnp.zeros((D_MODEL, LANES), F32).at[:, :N_EXPERTS].set(w_router)
    br = jnp.full((1, LANES), NEG_INF, F32).at[0, :N_EXPERTS].set(b_router)
    return pl.pallas_call(
        _out_router_kernel,
        grid=(n // tm,),
        in_specs=[
            pl.BlockSpec((tm, A_WIDTH), row),
            pl.BlockSpec((tm, B_WIDTH), row),
            pl.BlockSpec((tm, D_MODEL), row),
            pl.BlockSpec((A_WIDTH + B_WIDTH, D_MODEL), fixed),
            pl.BlockSpec((1, D_MODEL), fixed),
            pl.BlockSpec((D_MODEL, LANES), fixed),
            pl.BlockSpec((1, LANES), fixed),
        ],
        out_specs=[pl.BlockSpec((tm, D_MODEL), row), pl.BlockSpec((tm, D_MODEL), row),
                   pl.BlockSpec((tm, LANES), row), pl.BlockSpec((tm, LANES), row)],
        out_shape=[jax.ShapeDtypeStruct((n, D_MODEL), F32), jax.ShapeDtypeStruct((n, D_MODEL), F32),
                   jax.ShapeDtypeStruct((n, LANES), F32), jax.ShapeDtypeStruct((n, LANES), jnp.int32)],
        compiler_params=pltpu.CompilerParams(dimension_semantics=("parallel",)),
        name="out_proj_router",
    )(o_a, o_b, x2d, w_out.astype(BF16), norm_w.reshape(1, D_MODEL), wr, br)


MOE_BM = 256
MOE_BF = 512


SC_CORES = 2
SC_SUBCORES = 16
SC_ROWS = 64


def _sc_gather_rows(table, idx):
    n_workers = SC_CORES * SC_SUBCORES
    n_rows = idx.shape[0]
    d = table.shape[1]
    assert n_rows % (n_workers * SC_ROWS) == 0
    rows_per_worker = n_rows // n_workers
    mesh = plsc.VectorSubcoreMesh(core_axis_name="c", subcore_axis_name="s",
                                  num_cores=SC_CORES, num_subcores=SC_SUBCORES)

    @functools.partial(
        pl.kernel, mesh=mesh,
        out_type=jax.ShapeDtypeStruct((n_rows, d), table.dtype),
        scratch_types=[pltpu.VMEM((SC_ROWS,), jnp.int32), pltpu.VMEM((SC_ROWS, d), table.dtype),
                       pltpu.SemaphoreType.DMA],
        name="sc_gather_rows",
    )
    def gather(table_hbm, idx_hbm, out_hbm, idx_v, rows_v, sem):
        wid = lax.axis_index("s") * SC_CORES + lax.axis_index("c")
        base = wid * rows_per_worker

        @pl.loop(0, rows_per_worker // SC_ROWS)
        def _(i):
            off = base + i * SC_ROWS
            pltpu.sync_copy(idx_hbm.at[pl.ds(off, SC_ROWS)], idx_v)
            pltpu.async_copy(table_hbm.at[idx_v], rows_v, sem).wait()
            pltpu.sync_copy(rows_v, out_hbm.at[pl.ds(off, SC_ROWS)])

    return gather(table, idx)


def _route_blocks(eid, bm):
    n_tok = eid.shape[0]
    n_assign = n_tok * TOP_K
    experts = jnp.arange(N_EXPERTS, dtype=jnp.int32)
    flat_e = eid.T.reshape(-1)
    sorted_e, order = lax.sort((flat_e, jnp.arange(n_assign, dtype=jnp.int32)), num_keys=1)
    onehot = sorted_e[:, None] == experts[None, :]
    counts = jnp.sum(onehot, axis=0, dtype=jnp.int32)
    padded = (counts + bm - 1) // bm * bm
    start = jnp.cumsum(counts) - counts
    pend = jnp.cumsum(padded)
    pstart = pend - padded
    dest = jnp.arange(n_assign, dtype=jnp.int32) + jnp.sum(
        jnp.where(onehot, (pstart - start)[None, :], 0), axis=1)
    n_blocks = -(-(n_assign + N_EXPERTS * (bm - 1)) // bm)
    blk_start = jnp.arange(n_blocks, dtype=jnp.int32) * bm
    block_e = jnp.minimum(jnp.sum(blk_start[:, None] >= pend[None, :], axis=1), N_EXPERTS - 1)
    n_valid = pend[-1] // bm
    _, pos = lax.sort((order, dest), num_keys=1)
    t = jnp.arange(bm, dtype=jnp.int32)[None, :]
    pad_key = jnp.where(t < (padded - counts)[:, None], (pstart + counts)[:, None] + t,
                        n_blocks * bm).reshape(-1)
    pad_tok = jnp.arange(N_EXPERTS * bm, dtype=jnp.int32) % n_tok
    assert n_blocks * bm - n_assign == N_EXPERTS * bm
    _, row_tok = lax.sort((jnp.concatenate([dest, pad_key]),
                           jnp.concatenate([order % n_tok, pad_tok])), num_keys=1)
    return block_e.astype(jnp.int32), n_valid.astype(jnp.int32).reshape(1), row_tok, pos


def _moe_dense_kernel(be_ref, nv_ref, x_ref, wgu_ref, wd_ref, bg_ref, bu_ref, bd_ref, perm_ref, y_ref,
                      wg_s, wu_s, wd_s):
    i = pl.program_id(0)
    live = i < nv_ref[0]

    @pl.when(jnp.logical_and(live, jnp.logical_or(i == 0, be_ref[i] != be_ref[jnp.maximum(i - 1, 0)])))
    def _():
        perm = perm_ref[...]
        grp = 2 * LANES
        for gidx in range(2 * D_FF // grp):
            blk = wgu_ref[0, :, gidx * grp:(gidx + 1) * grp].astype(BF16)
            split = jnp.dot(blk, perm, preferred_element_type=F32).astype(BF16)
            wg_s[:, gidx * LANES:(gidx + 1) * LANES] = split[:, :LANES]
            wu_s[:, gidx * LANES:(gidx + 1) * LANES] = split[:, LANES:]
        wd_s[...] = wd_ref[0].astype(BF16)

    @pl.when(live)
    def _():
        x = x_ref[...].astype(BF16)
        y = None
        for f0 in range(0, D_FF, MOE_BF):
            g = jnp.dot(x, wg_s[:, f0:f0 + MOE_BF], preferred_element_type=F32) + bg_ref[0, :, f0:f0 + MOE_BF]
            u = jnp.dot(x, wu_s[:, f0:f0 + MOE_BF], preferred_element_type=F32) + bu_ref[0, :, f0:f0 + MOE_BF]
            gate = jnp.minimum(g, SWIGLU_LIMIT)
            up = jnp.clip(u, -SWIGLU_LIMIT, SWIGLU_LIMIT)
            t = gate * jax.nn.sigmoid(gate * SWIGLU_ALPHA) * (up + 1.0)
            part = jnp.dot(t.astype(BF16), wd_s[f0:f0 + MOE_BF, :], preferred_element_type=F32)
            y = part if y is None else y + part
        y_ref[...] = y + bd_ref[0]


def _moe_dense(x_sorted, block_e, n_valid, w_gate_up, w_down, b_gate, b_up, b_down, bm):
    n_blocks = block_e.shape[0]
    src = jnp.arange(2 * LANES)
    perm = jax.nn.one_hot((src % 2) * LANES + src // 2, 2 * LANES, dtype=BF16)
    rows = lambda i, be, nv: (jnp.minimum(i, nv[0] - 1), 0)
    wsel = lambda i, be, nv: (be[i], 0, 0)
    fixed = lambda i, be, nv: (0, 0)
    grid_spec = pltpu.PrefetchScalarGridSpec(
        num_scalar_prefetch=2,
        grid=(n_blocks,),
        in_specs=[
            pl.BlockSpec((bm, D_MODEL), rows),
            pl.BlockSpec((1, D_MODEL, 2 * D_FF), wsel),
            pl.BlockSpec((1, D_FF, D_MODEL), wsel),
            pl.BlockSpec((1, 1, D_FF), wsel),
            pl.BlockSpec((1, 1, D_FF), wsel),
            pl.BlockSpec((1, 1, D_MODEL), wsel),
            pl.BlockSpec((2 * LANES, 2 * LANES), fixed),
        ],
        out_specs=pl.BlockSpec((bm, D_MODEL), rows),
        scratch_shapes=[
            pltpu.VMEM((D_MODEL, D_FF), BF16),
            pltpu.VMEM((D_MODEL, D_FF), BF16),
            pltpu.VMEM((D_FF, D_MODEL), BF16),
        ],
    )
    return pl.pallas_call(
        _moe_dense_kernel,
        grid_spec=grid_spec,
        out_shape=jax.ShapeDtypeStruct(x_sorted.shape, F32),
        compiler_params=pltpu.CompilerParams(dimension_semantics=("arbitrary",)),
        name="moe_experts",
    )(block_e, n_valid, x_sorted, w_gate_up, w_down, b_gate, b_up, b_down, perm)


def _combine_kernel(h_ref, gate_ref, y0_ref, y1_ref, y2_ref, y3_ref, o_ref):
    gates = gate_ref[...]
    out = h_ref[...]
    for kk, y_ref in enumerate((y0_ref, y1_ref, y2_ref, y3_ref)):
        out = out + gates[:, kk:kk + 1] * y_ref[...]
    o_ref[...] = out


def _combine(h, gates, y, tm):
    n_tok = h.shape[0]
    nt = n_tok // tm
    row = lambda i: (i, 0)
    return pl.pallas_call(
        _combine_kernel,
        grid=(nt,),
        in_specs=[pl.BlockSpec((tm, D_MODEL), row), pl.BlockSpec((tm, LANES), row)]
        + [pl.BlockSpec((tm, D_MODEL), functools.partial(lambda kk, i: (kk * nt + i, 0), kk))
           for kk in range(TOP_K)],
        out_specs=pl.BlockSpec((tm, D_MODEL), row),
        out_shape=jax.ShapeDtypeStruct((n_tok, D_MODEL), F32),
        compiler_params=pltpu.CompilerParams(dimension_semantics=("parallel",)),
        name="moe_combine",
    )(h, gates, y, y, y, y)


def kernel(x, meta_tokens, norm_mix_w, w_in, q_norm_w, k_norm_w, kv_norm_w, w_kv_up, conv_w, a_log,
           dt_bias, delta_norm_w, w_out, norm_ffn_w, w_router, b_router, w_gate_up, b_gate_up,
           w_down, b_down):
    b, s, d = x.shape
    consts = _pack_in_proj_weights(norm_mix_w[0], w_in[0], q_norm_w[0], k_norm_w[0], kv_norm_w[0],
                                   w_kv_up[0])
    cos, sin = _rope_tables(N_META + s)
    real = _in_proj(x.reshape(b * s, d), cos[N_META:], sin[N_META:], consts, VT_TILE)
    meta = _in_proj(meta_tokens, cos[:N_META], sin[:N_META], consts, N_META)
    vt = real[2]
    q, k, iq, ik, misc, qkv, z = (a.reshape(b, s, a.shape[-1]) for a in real[:2] + real[3:])
    km = jnp.pad(meta[1], ((0, LANES - N_META), (0, 0)))
    vmt = jnp.pad(meta[2][0], ((0, 0), (0, LANES - N_META)))
    o_a = _sparse_attention(q, k, vt, km, vmt, iq, ik, misc,
                            tq=LANES, kc=min(512, s), topk=min(INDEX_TOPK, s // 4))
    lead = lambda a: jnp.pad(a, ((CHUNK - N_META, 0), (0, 0)))
    o_b = _gated_deltanet(qkv, z, misc, lead(meta[6]), lead(meta[5]), conv_w[0], a_log[0],
                          dt_bias[0], delta_norm_w[0])
    n_tok = b * s
    h, xn, gates, eid = _out_router(o_a.reshape(n_tok, A_WIDTH), o_b.reshape(n_tok, B_WIDTH),
                                    x.reshape(n_tok, d), w_out[0], norm_ffn_w[0], w_router[0],
                                    b_router[0], 256)
    block_e, n_valid, row_tok, pos = _route_blocks(eid[:, :TOP_K], MOE_BM)
    bgu = b_gate_up[0].reshape(N_EXPERTS, 1, 2 * D_FF)
    x_sorted = _sc_gather_rows(xn, row_tok)
    y_sorted = _moe_dense(x_sorted, block_e, n_valid, w_gate_up[0], w_down[0], bgu[:, :, 0::2],
                          bgu[:, :, 1::2], b_down[0].reshape(N_EXPERTS, 1, D_MODEL), MOE_BM)
    y = _sc_gather_rows(y_sorted, pos)
    out = _combine(h, gates, y, 256)
    return out.reshape(b, s, d)
```

```python
import functools
import math

import jax
import jax.numpy as jnp
from jax import lax
from jax.experimental import pallas as pl
from jax.experimental.pallas import tpu as pltpu
from jax.experimental.pallas import tpu_sc as plsc

F32 = jnp.float32
BF16 = jnp.bfloat16
HIGHEST = lax.Precision.HIGHEST

D_MODEL = 1024
N_META = 16
ROPE_THETA = 10000.0
EPS = 1e-6
A_HEAD_DIM = 64
A_HEADS = 8
A_WIDTH = A_HEADS * A_HEAD_DIM
KV_RANK = 256
IDX_HEADS = 8
IDX_DIM = 64
INDEX_TOPK = 256
B_HEAD_DIM = 128
B_HEADS = 4
B_WIDTH = B_HEADS * B_HEAD_DIM
CONV_WIDTH = 4
CHUNK = 64
N_EXPERTS = 32
TOP_K = 4
D_FF = D_MODEL
SWIGLU_LIMIT = 7.0
SWIGLU_ALPHA = 1.702
IN_SPLITS = (A_WIDTH, KV_RANK, IDX_HEADS * IDX_DIM, IDX_DIM, IDX_HEADS, 3 * B_WIDTH, B_WIDTH,
             B_HEADS, B_HEADS)

LANES = 128

C_Q = 0
C_CKV = C_Q + A_WIDTH
C_IQ = C_CKV + KV_RANK
C_IK = C_IQ + IDX_HEADS * IDX_DIM
C_MISC = C_IK + LANES
C_QKV = C_MISC + LANES
C_Z = C_QKV + 3 * B_WIDTH
C_END = C_Z + B_WIDTH
MISC_W, MISC_BETA, MISC_DECAY = 0, IDX_HEADS, IDX_HEADS + B_HEADS
LOG2E = math.log2(math.e)
VT_ROWS = A_HEAD_DIM + 16


def _rope_partner(a):
    lane = lax.broadcasted_iota(jnp.int32, a.shape, 1)
    first_half = (lane % A_HEAD_DIM) < (A_HEAD_DIM // 2)
    return jnp.where(first_half, pltpu.roll(a, LANES - A_HEAD_DIM // 2, 1),
                     pltpu.roll(a, A_HEAD_DIM // 2, 1))


def _rope(a, cos, sin_signed):
    return a * cos + _rope_partner(a) * sin_signed


def _head_rms(a, head_mean, gain):
    msq = jnp.dot((a * a).astype(BF16), head_mean, preferred_element_type=F32)
    return a * lax.rsqrt(msq + EPS) * gain


def _in_proj_kernel(x_ref, nw_ref, w_ref, wkv_ref, qnw_ref, knw_ref, kvnw_ref, cos_ref, sin_ref,
                    hm_ref, q_ref, k_ref, vt_ref, iq_ref, ik_ref, misc_ref, qkv_ref, z_ref):
    x = x_ref[...]
    u = x * lax.rsqrt(jnp.mean(x * x, axis=-1, keepdims=True) + EPS) * nw_ref[...]
    ub = u.astype(BF16)

    def proj(c0, c1):
        return jnp.dot(ub, w_ref[:, c0:c1], preferred_element_type=F32)

    cos = cos_ref[...]
    sin = sin_ref[...]
    hm = hm_ref[...]

    def rope_groups(a):
        return jnp.concatenate(
            [_rope(a[:, g * LANES:(g + 1) * LANES], cos, sin) for g in range(a.shape[1] // LANES)],
            axis=1)

    q = _head_rms(proj(C_Q, C_CKV), hm, qnw_ref[...])
    q_ref[...] = (rope_groups(q) * (A_HEAD_DIM ** -0.5 * LOG2E)).astype(BF16)

    ckv = proj(C_CKV, C_IQ)
    ckv = ckv * lax.rsqrt(jnp.mean(ckv * ckv, axis=-1, keepdims=True) + EPS) * kvnw_ref[...]
    kv = jnp.dot(ckv.astype(BF16), wkv_ref[...], preferred_element_type=F32)
    k = _head_rms(kv[:, :A_WIDTH], hm, knw_ref[...])
    k_ref[...] = rope_groups(k).astype(BF16)
    tm = x.shape[0]
    ones = jnp.ones((VT_ROWS - A_HEAD_DIM, tm), BF16)
    for g in range(A_WIDTH // LANES):
        vt = kv[:, A_WIDTH + g * LANES:A_WIDTH + (g + 1) * LANES].T.astype(BF16)
        for half in range(2):
            r0 = (2 * g + half) * VT_ROWS
            vt_ref[0, r0:r0 + A_HEAD_DIM, :] = vt[half * A_HEAD_DIM:(half + 1) * A_HEAD_DIM, :]
            vt_ref[0, r0 + A_HEAD_DIM:r0 + VT_ROWS, :] = ones

    iq_ref[...] = rope_groups(proj(C_IQ, C_IK)).astype(BF16)
    ik_ref[...] = _rope(proj(C_IK, C_MISC), cos, sin).astype(BF16)

    lane = lax.broadcasted_iota(jnp.int32, (1, LANES), 1)
    w_scale = jnp.where(lane < IDX_HEADS, IDX_HEADS ** -0.5 * IDX_DIM ** -0.5, 1.0)
    misc_ref[...] = proj(C_MISC, C_QKV) * w_scale
    qkv_ref[...] = proj(C_QKV, C_Z)
    z_ref[...] = proj(C_Z, C_END)


def _in_proj(x2d, cos, sin, consts, tm):
    n = x2d.shape[0]
    n_pos_blocks = cos.shape[0] // tm
    nw, w_pack, wkv, qnw, knw, kvnw, hm = consts
    row = lambda i: (i, 0)
    fixed = lambda i: (0, 0)
    pos = lambda i: (i % n_pos_blocks, 0)
    out_widths = (A_WIDTH, A_WIDTH, None, IDX_HEADS * IDX_DIM, LANES, LANES, 3 * B_WIDTH, B_WIDTH)
    out_dtypes = (BF16, BF16, BF16, BF16, BF16, F32, F32, F32)
    vt_rows = A_HEADS * VT_ROWS
    out_specs = [pl.BlockSpec((1, vt_rows, tm), lambda i: (i, 0, 0)) if w is None
                 else pl.BlockSpec((tm, w), row) for w in out_widths]
    out_shape = [jax.ShapeDtypeStruct((n // tm, vt_rows, tm) if w is None else (n, w), dt)
                 for w, dt in zip(out_widths, out_dtypes)]
    return pl.pallas_call(
        _in_proj_kernel,
        grid=(n // tm,),
        in_specs=[
            pl.BlockSpec((tm, D_MODEL), row),
            pl.BlockSpec(nw.shape, fixed),
            pl.BlockSpec(w_pack.shape, fixed),
            pl.BlockSpec(wkv.shape, fixed),
            pl.BlockSpec(qnw.shape, fixed),
            pl.BlockSpec(knw.shape, fixed),
            pl.BlockSpec(kvnw.shape, fixed),
            pl.BlockSpec((tm, LANES), pos),
            pl.BlockSpec((tm, LANES), pos),
            pl.BlockSpec(hm.shape, fixed),
        ],
        out_specs=out_specs,
        out_shape=out_shape,
        compiler_params=pltpu.CompilerParams(dimension_semantics=("parallel",)),
        name="in_proj",
    )(x2d, nw, w_pack, wkv, qnw, knw, kvnw, cos, sin, hm)


def _pack_in_proj_weights(norm_mix_w, w_in, q_norm_w, k_norm_w, kv_norm_w, w_kv_up):
    points = []
    acc = 0
    for s in IN_SPLITS:
        points.append((acc, acc + s))
        acc += s
    a_q, a_ckv, i_q, i_k, i_w, b_qkv, b_z, b_beta, b_a = (w_in[:, a:b] for a, b in points)
    misc = jnp.concatenate(
        [i_w, b_beta, b_a, jnp.zeros((D_MODEL, LANES - IDX_HEADS - 2 * B_HEADS), w_in.dtype)], axis=1)
    w_pack = jnp.concatenate([a_q, a_ckv, i_q, i_k, i_k, misc, b_qkv, b_z], axis=1).astype(BF16)
    head_mean = jnp.kron(jnp.eye(A_HEADS, dtype=F32),
                         jnp.full((A_HEAD_DIM, A_HEAD_DIM), 1.0 / A_HEAD_DIM, F32)).astype(BF16)
    return (norm_mix_w.reshape(1, D_MODEL), w_pack, w_kv_up.astype(BF16),
            jnp.tile(q_norm_w, A_HEADS).reshape(1, A_WIDTH),
            jnp.tile(k_norm_w, A_HEADS).reshape(1, A_WIDTH),
            kv_norm_w.reshape(1, KV_RANK), head_mean)


def _rope_tables(n_pos):
    half = A_HEAD_DIM // 2
    inv_freq = ROPE_THETA ** (-jnp.arange(0, A_HEAD_DIM, 2, dtype=F32) / A_HEAD_DIM)
    ang = jnp.arange(n_pos, dtype=F32)[:, None] * inv_freq[None, :]
    cos, sin = jnp.cos(ang), jnp.sin(ang)
    cos128 = jnp.tile(cos, (1, LANES // half))
    sin128 = jnp.tile(jnp.concatenate([-sin, sin], axis=1), (1, LANES // A_HEAD_DIM))
    return cos128, sin128


NEG_INF = float("-inf")
F32_MAX = float(jnp.finfo(jnp.float32).max)
INT_MIN = -2 ** 31


VT_TILE = 256
SUBLANES = 8


def _attn_kernel(q_ref, k_ref, vt_ref, km_ref, vmt_ref, iq_ref, ik_ref, misc_ref, o_ref,
                 isc_ref, lhs_ref, qm_ref, s_ref, m_ref, acc_ref, *, tq, kc, topk, pos_bits):
    j = pl.program_id(1)
    n_kc = lax.div((j + 1) * tq + (kc - 1), kc)
    g = kc // SUBLANES
    lane = lax.broadcasted_iota(jnp.int32, (1, LANES), 1)
    lo_half = lane < A_HEAD_DIM
    n_pairs = A_WIDTH // LANES
    nt = (((1,), (1,)), ((), ()))

    iq = iq_ref[0]
    q = q_ref[0]
    zero = jnp.zeros((), BF16)
    for p in range(n_pairs):
        blk = iq[:, p * LANES:(p + 1) * LANES]
        lhs_ref[(2 * p) * tq:(2 * p + 1) * tq, :] = jnp.where(lo_half, blk, zero)
        lhs_ref[(2 * p + 1) * tq:(2 * p + 2) * tq, :] = jnp.where(lo_half, zero, blk)
        qb = q[:, p * LANES:(p + 1) * LANES]
        qm_ref[p, :tq, :] = jnp.where(lo_half, qb, zero)
        qm_ref[p, tq:, :] = jnp.where(lo_half, zero, qb)
    w_t = misc_ref[0].T

    def fold0(x3, op):
        n = x3.shape[0]
        while n > 1:
            x3 = op(x3[:n // 2], x3[n // 2:n])
            n //= 2
        return x3[0]

    q_pos = j * tq + lax.broadcasted_iota(jnp.int32, (1, tq), 1)

    def idx_body(c, carry):
        k0 = pl.multiple_of(c * kc, kc)
        r = lax.dot_general(ik_ref[0, pl.ds(k0, kc), :], lhs_ref[...], nt,
                            preferred_element_type=F32)
        s = None
        for h in range(IDX_HEADS):
            term = jnp.maximum(r[:, h * tq:(h + 1) * tq], 0.0) * w_t[MISC_W + h:MISC_W + h + 1, :]
            s = term if s is None else s + term
        k_pos = k0 + lax.broadcasted_iota(jnp.int32, (kc, 1), 0)
        isc_ref[c] = jnp.where(k_pos <= q_pos, s, NEG_INF)
        return carry
    lax.fori_loop(0, n_kc, idx_body, 0)

    def count(pred):
        def body(c, acc):
            x3 = isc_ref[c].reshape(g, SUBLANES, tq)
            return acc + fold0(jnp.where(pred(x3, c), 1.0, 0.0), jnp.add)
        acc = lax.fori_loop(0, n_kc, body, jnp.zeros((SUBLANES, tq), F32))
        return jnp.broadcast_to(jnp.sum(acc, axis=0, keepdims=True), (SUBLANES, tq))

    def key_to_float(u):
        key = u ^ jnp.int32(INT_MIN)
        bits = jnp.where(key >= 0, key, key ^ jnp.int32(0x7FFFFFFF))
        return lax.bitcast_convert_type(bits, F32)

    def bit_body(i, u):
        u2 = u | lax.shift_left(jnp.int32(1), 31 - i)
        cand = key_to_float(u2)
        n_ge = count(lambda x3, c: x3 >= cand[None])
        return jnp.where(n_ge >= topk, u2, u)
    u = lax.fori_loop(0, 32, bit_body, jnp.zeros((SUBLANES, tq), jnp.int32))
    few = (u >= 0) & (u < 0x00800000)
    tau = jnp.where(few, -F32_MAX, key_to_float(u))

    n_ge = count(lambda x3, c: x3 >= tau[None])

    @pl.when(jnp.max(n_ge) > topk)
    def _():
        n_gt = count(lambda x3, c: x3 > tau[None])
        need = topk - n_gt
        def chunk_pos(c):
            return (c * kc + lax.broadcasted_iota(jnp.int32, (g, SUBLANES, tq), 0) * SUBLANES
                    + lax.broadcasted_iota(jnp.int32, (g, SUBLANES, tq), 1))
        def pos_body(i, cut):
            cut2 = cut | lax.shift_left(jnp.int32(1), pos_bits - 1 - i)
            ties_before = count(lambda x3, c: (x3 == tau[None]) & (chunk_pos(c) < cut2[None]))
            return jnp.where(ties_before < need, cut2, cut)
        cut = lax.fori_loop(0, pos_bits, pos_body, jnp.zeros((SUBLANES, tq), jnp.int32))
        def drop_body(c, carry):
            x3 = isc_ref[c].reshape(g, SUBLANES, tq)
            drop = (x3 == tau[None]) & (chunk_pos(c) > cut[None])
            isc_ref[c] = jnp.where(drop, NEG_INF, x3).reshape(kc, tq)
            return carry
        lax.fori_loop(0, n_kc, drop_body, 0)

    tau_row = tau[0:1, :]

    def attend(key_pairs, bias, vt_slabs, first):
        n = key_pairs[0].shape[0]
        m_cur = []
        for p in range(n_pairs):
            s2 = lax.dot_general(key_pairs[p], qm_ref[p], nt, preferred_element_type=F32)
            for half in range(2):
                s = s2[:, half * tq:(half + 1) * tq] + bias
                s_ref[2 * p + half, :n, :] = s
                m8 = fold0(s.reshape(n // SUBLANES, SUBLANES, tq), jnp.maximum)
                m_cur.append(jnp.max(m8, axis=0, keepdims=True))
        for hd in range(A_HEADS):
            m_new = m_cur[hd] if first else jnp.maximum(m_ref[hd][0:1, :], m_cur[hd])
            e = jnp.exp2(s_ref[hd, :n, :] - m_new).astype(BF16)
            pv = None
            off = 0
            for slab in vt_slabs(hd):
                part = jnp.dot(slab, e[off:off + slab.shape[1], :], preferred_element_type=F32)
                pv = part if pv is None else pv + part
                off += slab.shape[1]
            if first:
                acc_ref[hd] = pv
            else:
                acc_ref[hd] = jnp.exp2(m_ref[hd][0:1, :] - m_new) * acc_ref[hd] + pv
            m_ref[hd] = jnp.broadcast_to(m_new, (SUBLANES, tq))

    meta_bias = jnp.where(lax.broadcasted_iota(jnp.int32, (LANES, 1), 0) < N_META, 0.0, NEG_INF)
    attend([km_ref[:, p * LANES:(p + 1) * LANES] for p in range(n_pairs)], meta_bias,
           lambda hd: [vmt_ref[hd * VT_ROWS:(hd + 1) * VT_ROWS, :]], True)

    def att_body(c, carry):
        k0 = pl.multiple_of(c * kc, kc)
        bias = jnp.where(isc_ref[c] >= tau_row, 0.0, NEG_INF)
        attend([k_ref[0, pl.ds(k0, kc), p * LANES:(p + 1) * LANES] for p in range(n_pairs)], bias,
               lambda hd: [vt_ref[c * (kc // VT_TILE) + t, hd * VT_ROWS:(hd + 1) * VT_ROWS, :]
                           for t in range(kc // VT_TILE)], False)
        return carry
    lax.fori_loop(0, n_kc, att_body, 0)

    for p in range(n_pairs):
        halves = []
        for hd in (2 * p, 2 * p + 1):
            acc = acc_ref[hd]
            halves.append(acc[:A_HEAD_DIM, :] / acc[A_HEAD_DIM:A_HEAD_DIM + 1, :])
        o_ref[0, :, p * LANES:(p + 1) * LANES] = jnp.concatenate(halves, axis=0).T.astype(BF16)


def _sparse_attention(q, k, vt, km, vmt, iq, ik, misc, *, tq, kc, topk):
    b, s, _ = q.shape
    assert tq == LANES and kc % VT_TILE == 0 and s % kc == 0
    kernel = functools.partial(_attn_kernel, tq=tq, kc=kc, topk=topk, pos_bits=int(math.log2(s)))
    qblk = lambda bi, j: (bi, j, 0)
    full = lambda bi, j: (bi, 0, 0)
    fixed = lambda bi, j: (0, 0)
    vt_rows = A_HEADS * VT_ROWS
    return pl.pallas_call(
        kernel,
        grid=(b, s // tq),
        in_specs=[
            pl.BlockSpec((1, tq, A_WIDTH), qblk),
            pl.BlockSpec((1, s, A_WIDTH), full),
            pl.BlockSpec((s // VT_TILE, vt_rows, VT_TILE), full),
            pl.BlockSpec(km.shape, fixed),
            pl.BlockSpec(vmt.shape, fixed),
            pl.BlockSpec((1, tq, IDX_HEADS * IDX_DIM), qblk),
            pl.BlockSpec((1, s, LANES), full),
            pl.BlockSpec((1, tq, LANES), qblk),
        ],
        out_specs=pl.BlockSpec((1, tq, A_WIDTH), qblk),
        out_shape=jax.ShapeDtypeStruct((b, s, A_WIDTH), BF16),
        scratch_shapes=[
            pltpu.VMEM((s // kc, kc, tq), F32),
            pltpu.VMEM((IDX_HEADS * tq, LANES), BF16),
            pltpu.VMEM((A_WIDTH // LANES, 2 * tq, LANES), BF16),
            pltpu.VMEM((A_HEADS, max(kc, LANES), tq), F32),
            pltpu.VMEM((A_HEADS, SUBLANES, tq), F32),
            pltpu.VMEM((A_HEADS, VT_ROWS, tq), F32),
        ],
        compiler_params=pltpu.CompilerParams(dimension_semantics=("parallel", "arbitrary")),
        name="sparse_attention",
    )(q, k, vt, km, vmt, iq, ik, misc)


HALO = 8


def _softplus(x):
    return jnp.maximum(x, 0.0) + jnp.log1p(jnp.exp(-jnp.abs(x)))


def _bdot(a, b):
    return jnp.dot(a.astype(BF16), b.astype(BF16), preferred_element_type=F32)


def _hdot(a, b):
    return jnp.dot(a, b, precision=HIGHEST, preferred_element_type=F32)


def _dot3(a, b):
    a_hi = a.astype(BF16)
    b_hi = b.astype(BF16)
    a_lo = (a - a_hi.astype(F32)).astype(BF16)
    b_lo = (b - b_hi.astype(F32)).astype(BF16)
    dot = functools.partial(jnp.dot, preferred_element_type=F32)
    return dot(a_hi, b_hi) + (dot(a_hi, b_lo) + dot(a_lo, b_hi))


def _delta_kernel(qkv_ref, z_ref, misc_ref, qkvm_ref, miscm_ref, convw_ref, alog_ref, dtb_ref,
                  nw_ref, o_ref, state_ref, halo_ref):
    c = pl.program_id(1)
    n_pad = CHUNK - N_META
    nt = (((1,), (1,)), ((), ()))
    tn = (((0,), (0,)), ((), ()))

    @pl.when(c == 0)
    def _():
        state_ref[...] = jnp.zeros_like(state_ref)
        halo_ref[...] = jnp.zeros_like(halo_ref)

    ri = lax.broadcasted_iota(jnp.int32, (CHUNK, CHUNK), 0)
    ci = lax.broadcasted_iota(jnp.int32, (CHUNK, CHUNK), 1)
    incl = ri >= ci
    strict = ri > ci
    eye = (ri == ci).astype(F32)
    tri = incl.astype(F32)
    row = lax.broadcasted_iota(jnp.int32, (CHUNK, 1), 0)
    is_meta = c == 0
    neutral = jnp.logical_and(is_meta, row < n_pad)
    cols = lambda base, h: slice(base + h * B_HEAD_DIM, base + (h + 1) * B_HEAD_DIM)

    nb = qkv_ref.shape[0]
    heads = range(nb * B_HEADS)
    qn, kn, vb, kb, gc, decay = [], [], [], [], [], []
    for bi in range(nb):
        xin = jnp.where(is_meta, qkvm_ref[...], qkv_ref[bi])
        misc = jnp.where(is_meta, miscm_ref[...], misc_ref[bi])
        xcat = jnp.concatenate([halo_ref[bi], xin], axis=0)
        halo_ref[bi] = xin[CHUNK - HALO:, :]
        conv = None
        for tap in range(CONV_WIDTH):
            off = HALO - (CONV_WIDTH - 1) + tap
            term = xcat[off:off + CHUNK, :] * convw_ref[tap:tap + 1, :]
            conv = term if conv is None else conv + term
        xc = conv * jax.nn.sigmoid(conv)
        beta_all = jnp.where(neutral, 0.0, jax.nn.sigmoid(misc))
        g_all = jnp.where(neutral, 0.0, -jnp.exp(alog_ref[...]) * _softplus(misc + dtb_ref[...]))
        gc_all = _hdot(tri, g_all)
        gc_all_t = gc_all.T
        for h in range(B_HEADS):
            qh, kh, vh = xc[:, cols(0, h)], xc[:, cols(B_WIDTH, h)], xc[:, cols(2 * B_WIDTH, h)]
            qn.append(qh * lax.rsqrt(jnp.sum(qh * qh, axis=-1, keepdims=True) + EPS) * (B_HEAD_DIM ** -0.5))
            kn.append(kh * lax.rsqrt(jnp.sum(kh * kh, axis=-1, keepdims=True) + EPS))
            beta = jnp.broadcast_to(beta_all[:, MISC_BETA + h:MISC_BETA + h + 1], (CHUNK, B_HEAD_DIM))
            gc.append(jnp.broadcast_to(gc_all[:, MISC_DECAY + h:MISC_DECAY + h + 1], (CHUNK, B_HEAD_DIM)))
            g_col = gc[-1][:, :CHUNK]
            g_row = gc_all_t[MISC_DECAY + h:MISC_DECAY + h + 1, :]
            decay.append(jnp.where(incl, jnp.exp(jnp.where(incl, g_col - g_row, 0.0)), 0.0))
            kb.append(kn[-1] * beta)
            vb.append(vh * beta)

    kk = [lax.dot_general(kb[h].astype(BF16), kn[h].astype(BF16), nt, preferred_element_type=F32)
          for h in heads]
    qk = [lax.dot_general(qn[h].astype(BF16), kn[h].astype(BF16), nt, preferred_element_type=F32)
          for h in heads]
    intra = [jnp.where(incl, qk[h] * decay[h], 0.0) for h in heads]

    pw = [jnp.where(strict, -(kk[h] * decay[h]), 0.0) for h in heads]
    t_inv = [eye + pw[h] for h in heads]
    pw = [_dot3(pw[h], pw[h]) for h in heads]
    for _ in range(int(math.log2(CHUNK)) - 2):
        t_next = [t_inv[h] + _dot3(t_inv[h], pw[h]) for h in heads]
        pw = [_dot3(pw[h], pw[h]) for h in heads]
        t_inv = t_next
    t_inv = [t_inv[h] + _dot3(t_inv[h], pw[h]) for h in heads]

    u = [_bdot(t_inv[h], vb[h]) for h in heads]
    w = [_bdot(t_inv[h], kb[h] * jnp.exp(gc[h])) for h in heads]

    state = [state_ref[h] for h in heads]
    w_s = [_bdot(w[h], state[h]) for h in heads]
    q_s = [_bdot(qn[h] * jnp.exp(gc[h]), state[h]) for h in heads]
    v_new = [u[h] - w_s[h] for h in heads]
    o = [q_s[h] + _bdot(intra[h], v_new[h]) for h in heads]
    for h in heads:
        g_last = gc[h][CHUNK - 1:CHUNK, :]
        kd = kn[h] * jnp.exp(g_last - gc[h])
        state_ref[h] = state[h] * jnp.exp(g_last) + lax.dot_general(
            kd.astype(BF16), v_new[h].astype(BF16), tn, preferred_element_type=F32)

    for h in heads:
        bi, hh = divmod(h, B_HEADS)
        y = o[h] * lax.rsqrt(jnp.mean(o[h] * o[h], axis=-1, keepdims=True) + EPS) * nw_ref[...]
        zh = z_ref[bi, :, cols(0, hh)]
        o_ref[bi, :, cols(0, hh)] = (y * (zh * jax.nn.sigmoid(zh))).astype(BF16)


DELTA_BATCH = 2


def _gated_deltanet(qkv, z, misc, qkv_m, misc_m, conv_w, a_log, dt_bias, norm_w):
    b, s, _ = qkv.shape
    n_chunks = s // CHUNK + 1
    nb = DELTA_BATCH if b % DELTA_BATCH == 0 else 1
    blk = lambda bi, c: (bi, jnp.maximum(c - 1, 0), 0)
    fixed = lambda bi, c: (0, 0)
    lane_vec = lambda v: jnp.zeros((1, LANES), F32).at[0, MISC_DECAY:MISC_DECAY + B_HEADS].set(v)
    return pl.pallas_call(
        _delta_kernel,
        grid=(b // nb, n_chunks),
        in_specs=[
            pl.BlockSpec((nb, CHUNK, 3 * B_WIDTH), blk),
            pl.BlockSpec((nb, CHUNK, B_WIDTH), blk),
            pl.BlockSpec((nb, CHUNK, LANES), blk),
            pl.BlockSpec((CHUNK, 3 * B_WIDTH), fixed),
            pl.BlockSpec((CHUNK, LANES), fixed),
            pl.BlockSpec((CONV_WIDTH, 3 * B_WIDTH), fixed),
            pl.BlockSpec((1, LANES), fixed),
            pl.BlockSpec((1, LANES), fixed),
            pl.BlockSpec((1, B_HEAD_DIM), fixed),
        ],
        out_specs=pl.BlockSpec((nb, CHUNK, B_WIDTH), blk),
        out_shape=jax.ShapeDtypeStruct((b, s, B_WIDTH), BF16),
        scratch_shapes=[
            pltpu.VMEM((nb * B_HEADS, B_HEAD_DIM, B_HEAD_DIM), F32),
            pltpu.VMEM((nb, HALO, 3 * B_WIDTH), F32),
        ],
        compiler_params=pltpu.CompilerParams(dimension_semantics=("parallel", "arbitrary")),
        name="gated_deltanet",
    )(qkv, z, misc, qkv_m, misc_m, conv_w, lane_vec(a_log), lane_vec(dt_bias),
      norm_w.reshape(1, B_HEAD_DIM))


def _pack_rows(a):
    half = a.shape[1] // 2
    bits = lax.bitcast_convert_type(a.astype(BF16).astype(F32), jnp.int32)
    return bits[:, :half] | lax.shift_right_logical(bits[:, half:], 16)


def _unpack_rows(w):
    hi = lax.bitcast_convert_type(w & jnp.int32(-65536), F32)
    lo = lax.bitcast_convert_type(lax.shift_left(w, 16), F32)
    return jnp.concatenate([hi, lo], axis=1)


def _out_router_kernel(oa_ref, ob_ref, x_ref, wo_ref, nw_ref, wr_ref, br_ref,
                       h_ref, xn_ref, gate_ref, eid_ref):
    mix = (jnp.dot(oa_ref[...], wo_ref[:A_WIDTH, :], preferred_element_type=F32)
           + jnp.dot(ob_ref[...], wo_ref[A_WIDTH:, :], preferred_element_type=F32))
    h = x_ref[...] + mix
    h_ref[...] = h
    xn = h * lax.rsqrt(jnp.mean(h * h, axis=-1, keepdims=True) + EPS) * nw_ref[...]
    xn_ref[...] = _pack_rows(xn)
    logits = _hdot(xn, wr_ref[...]) + br_ref[...]
    tm = logits.shape[0]
    lane = lax.broadcasted_iota(jnp.int32, (tm, LANES), 1)
    lane_f = lane.astype(F32)
    work = logits
    vals, idxs = [], []
    for _ in range(TOP_K):
        m = jnp.max(work, axis=-1, keepdims=True)
        idx = jnp.min(jnp.where(work == m, lane_f, float(LANES)), axis=-1, keepdims=True)
        vals.append(m)
        idxs.append(idx)
        work = jnp.where(lane_f == idx, NEG_INF, work)
    exps = [jnp.exp(v - vals[0]) for v in vals]
    denom = exps[0]
    for e in exps[1:]:
        denom = denom + e
    gates = jnp.zeros((tm, LANES), F32)
    eids = jnp.zeros((tm, LANES), F32)
    for kk in range(TOP_K):
        gates = jnp.where(lane == kk, exps[kk] / denom, gates)
        eids = jnp.where(lane == kk, idxs[kk], eids)
    gate_ref[...] = gates
    eid_ref[...] = eids.astype(jnp.int32)


def _out_router(o_a, o_b, x2d, w_out, norm_w, w_router, b_router, tm):
    n = x2d.shape[0]
    row = lambda i: (i, 0)
    fixed = lambda i: (0, 0)
    wr = jnp.zeros((D_MODEL, LANES), F32).at[:, :N_EXPERTS].set(w_router)
    br = jnp.full((1, LANES), NEG_INF, F32).at[0, :N_EXPERTS].set(b_router)
    return pl.pallas_call(
        _out_router_kernel,
        grid=(n // tm,),
        in_specs=[
            pl.BlockSpec((tm, A_WIDTH), row),
            pl.BlockSpec((tm, B_WIDTH), row),
            pl.BlockSpec((tm, D_MODEL), row),
            pl.BlockSpec((A_WIDTH + B_WIDTH, D_MODEL), fixed),
            pl.BlockSpec((1, D_MODEL), fixed),
            pl.BlockSpec((D_MODEL, LANES), fixed),
            pl.BlockSpec((1, LANES), fixed),
        ],
        out_specs=[pl.BlockSpec((tm, D_MODEL), row), pl.BlockSpec((tm, D_MODEL // 2), row),
                   pl.BlockSpec((tm, LANES), row), pl.BlockSpec((tm, LANES), row)],
        out_shape=[jax.ShapeDtypeStruct((n, D_MODEL), F32), jax.ShapeDtypeStruct((n, D_MODEL // 2), jnp.int32),
                   jax.ShapeDtypeStruct((n, LANES), F32), jax.ShapeDtypeStruct((n, LANES), jnp.int32)],
        compiler_params=pltpu.CompilerParams(dimension_semantics=("parallel",)),
        name="out_proj_router",
    )(o_a, o_b, x2d, w_out.astype(BF16), norm_w.reshape(1, D_MODEL), wr, br)


MOE_BM = 256
MOE_BF = 512


SC_CORES = 2
SC_SUBCORES = 16
SC_ROWS = 128


def _sc_gather_rows(table, idx):
    n_workers = SC_CORES * SC_SUBCORES
    n_rows = idx.shape[0]
    d = table.shape[1]
    assert n_rows % (n_workers * SC_ROWS) == 0
    rows_per_worker = n_rows // n_workers
    mesh = plsc.VectorSubcoreMesh(core_axis_name="c", subcore_axis_name="s",
                                  num_cores=SC_CORES, num_subcores=SC_SUBCORES)

    @functools.partial(
        pl.kernel, mesh=mesh,
        out_type=jax.ShapeDtypeStruct((n_rows, d), table.dtype),
        scratch_types=[pltpu.VMEM((SC_ROWS,), jnp.int32), pltpu.VMEM((SC_ROWS, d), table.dtype),
                       pltpu.SemaphoreType.DMA],
        name="sc_gather_rows",
    )
    def gather(table_hbm, idx_hbm, out_hbm, idx_v, rows_v, sem):
        wid = lax.axis_index("s") * SC_CORES + lax.axis_index("c")
        base = wid * rows_per_worker

        @pl.loop(0, rows_per_worker // SC_ROWS)
        def _(i):
            off = base + i * SC_ROWS
            pltpu.sync_copy(idx_hbm.at[pl.ds(off, SC_ROWS)], idx_v)
            pltpu.async_copy(table_hbm.at[idx_v], rows_v, sem).wait()
            pltpu.sync_copy(rows_v, out_hbm.at[pl.ds(off, SC_ROWS)])

    return gather(table, idx)


def _route_blocks(eid, bm):
    n_tok = eid.shape[0]
    n_assign = n_tok * TOP_K
    experts = jnp.arange(N_EXPERTS, dtype=jnp.int32)
    flat_e = eid.T.reshape(-1)
    sorted_e, order = lax.sort((flat_e, jnp.arange(n_assign, dtype=jnp.int32)), num_keys=1)
    onehot = sorted_e[:, None] == experts[None, :]
    counts = jnp.sum(onehot, axis=0, dtype=jnp.int32)
    padded = (counts + bm - 1) // bm * bm
    start = jnp.cumsum(counts) - counts
    pend = jnp.cumsum(padded)
    pstart = pend - padded
    dest = jnp.arange(n_assign, dtype=jnp.int32) + jnp.sum(
        jnp.where(onehot, (pstart - start)[None, :], 0), axis=1)
    n_blocks = -(-(n_assign + N_EXPERTS * (bm - 1)) // bm)
    blk_start = jnp.arange(n_blocks, dtype=jnp.int32) * bm
    block_e = jnp.minimum(jnp.sum(blk_start[:, None] >= pend[None, :], axis=1), N_EXPERTS - 1)
    n_valid = pend[-1] // bm
    _, pos = lax.sort((order, dest), num_keys=1)
    t = jnp.arange(bm, dtype=jnp.int32)[None, :]
    pad_key = jnp.where(t < (padded - counts)[:, None], (pstart + counts)[:, None] + t,
                        n_blocks * bm).reshape(-1)
    pad_tok = jnp.arange(N_EXPERTS * bm, dtype=jnp.int32) % n_tok
    assert n_blocks * bm - n_assign == N_EXPERTS * bm
    _, row_tok = lax.sort((jnp.concatenate([dest, pad_key]),
                           jnp.concatenate([order % n_tok, pad_tok])), num_keys=1)
    return block_e.astype(jnp.int32), n_valid.astype(jnp.int32).reshape(1), row_tok, pos


def _moe_dense_kernel(be_ref, nv_ref, x_ref, wgu_ref, wd_ref, bg_ref, bu_ref, bd_ref, perm_ref, y_ref,
                      wg_s, wu_s, wd_s):
    i = pl.program_id(0)
    live = i < nv_ref[0]

    @pl.when(jnp.logical_and(live, jnp.logical_or(i == 0, be_ref[i] != be_ref[jnp.maximum(i - 1, 0)])))
    def _():
        perm = perm_ref[...]
        grp = 2 * LANES
        for gidx in range(2 * D_FF // grp):
            blk = wgu_ref[0, :, gidx * grp:(gidx + 1) * grp].astype(BF16)
            split = jnp.dot(blk, perm, preferred_element_type=F32).astype(BF16)
            wg_s[:, gidx * LANES:(gidx + 1) * LANES] = split[:, :LANES]
            wu_s[:, gidx * LANES:(gidx + 1) * LANES] = split[:, LANES:]
        wd_s[...] = wd_ref[0].astype(BF16)

    @pl.when(live)
    def _():
        x = _unpack_rows(x_ref[...]).astype(BF16)
        y = None
        for f0 in range(0, D_FF, MOE_BF):
            g = jnp.dot(x, wg_s[:, f0:f0 + MOE_BF], preferred_element_type=F32) + bg_ref[0, :, f0:f0 + MOE_BF]
            u = jnp.dot(x, wu_s[:, f0:f0 + MOE_BF], preferred_element_type=F32) + bu_ref[0, :, f0:f0 + MOE_BF]
            gate = jnp.minimum(g, SWIGLU_LIMIT)
            up = jnp.clip(u, -SWIGLU_LIMIT, SWIGLU_LIMIT)
            t = gate * jax.nn.sigmoid(gate * SWIGLU_ALPHA) * (up + 1.0)
            part = jnp.dot(t.astype(BF16), wd_s[f0:f0 + MOE_BF, :], preferred_element_type=F32)
            y = part if y is None else y + part
        y_ref[...] = _pack_rows(y + bd_ref[0])


def _moe_dense(x_sorted, block_e, n_valid, w_gate_up, w_down, b_gate, b_up, b_down, bm):
    n_blocks = block_e.shape[0]
    src = jnp.arange(2 * LANES)
    perm = jax.nn.one_hot((src % 2) * LANES + src // 2, 2 * LANES, dtype=BF16)
    rows = lambda i, be, nv: (jnp.minimum(i, nv[0] - 1), 0)
    wsel = lambda i, be, nv: (be[i], 0, 0)
    fixed = lambda i, be, nv: (0, 0)
    grid_spec = pltpu.PrefetchScalarGridSpec(
        num_scalar_prefetch=2,
        grid=(n_blocks,),
        in_specs=[
            pl.BlockSpec((bm, D_MODEL // 2), rows),
            pl.BlockSpec((1, D_MODEL, 2 * D_FF), wsel),
            pl.BlockSpec((1, D_FF, D_MODEL), wsel),
            pl.BlockSpec((1, 1, D_FF), wsel),
            pl.BlockSpec((1, 1, D_FF), wsel),
            pl.BlockSpec((1, 1, D_MODEL), wsel),
            pl.BlockSpec((2 * LANES, 2 * LANES), fixed),
        ],
        out_specs=pl.BlockSpec((bm, D_MODEL // 2), rows),
        scratch_shapes=[
            pltpu.VMEM((D_MODEL, D_FF), BF16),
            pltpu.VMEM((D_MODEL, D_FF), BF16),
            pltpu.VMEM((D_FF, D_MODEL), BF16),
        ],
    )
    return pl.pallas_call(
        _moe_dense_kernel,
        grid_spec=grid_spec,
        out_shape=jax.ShapeDtypeStruct(x_sorted.shape, jnp.int32),
        compiler_params=pltpu.CompilerParams(dimension_semantics=("arbitrary",)),
        name="moe_experts",
    )(block_e, n_valid, x_sorted, w_gate_up, w_down, b_gate, b_up, b_down, perm)


def _combine_kernel(h_ref, gate_ref, y0_ref, y1_ref, y2_ref, y3_ref, o_ref):
    gates = gate_ref[...]
    out = h_ref[...]
    for kk, y_ref in enumerate((y0_ref, y1_ref, y2_ref, y3_ref)):
        out = out + gates[:, kk:kk + 1] * _unpack_rows(y_ref[...])
    o_ref[...] = out


def _combine(h, gates, y, tm):
    n_tok = h.shape[0]
    nt = n_tok // tm
    row = lambda i: (i, 0)
    return pl.pallas_call(
        _combine_kernel,
        grid=(nt,),
        in_specs=[pl.BlockSpec((tm, D_MODEL), row), pl.BlockSpec((tm, LANES), row)]
        + [pl.BlockSpec((tm, D_MODEL // 2), functools.partial(lambda kk, i: (kk * nt + i, 0), kk))
           for kk in range(TOP_K)],
        out_specs=pl.BlockSpec((tm, D_MODEL), row),
        out_shape=jax.ShapeDtypeStruct((n_tok, D_MODEL), F32),
        compiler_params=pltpu.CompilerParams(dimension_semantics=("parallel",)),
        name="moe_combine",
    )(h, gates, y, y, y, y)


def kernel(x, meta_tokens, norm_mix_w, w_in, q_norm_w, k_norm_w, kv_norm_w, w_kv_up, conv_w, a_log,
           dt_bias, delta_norm_w, w_out, norm_ffn_w, w_router, b_router, w_gate_up, b_gate_up,
           w_down, b_down):
    b, s, d = x.shape
    consts = _pack_in_proj_weights(norm_mix_w[0], w_in[0], q_norm_w[0], k_norm_w[0], kv_norm_w[0],
                                   w_kv_up[0])
    cos, sin = _rope_tables(N_META + s)
    real = _in_proj(x.reshape(b * s, d), cos[N_META:], sin[N_META:], consts, VT_TILE)
    meta = _in_proj(meta_tokens, cos[:N_META], sin[:N_META], consts, N_META)
    vt = real[2]
    q, k, iq, ik, misc, qkv, z = (a.reshape(b, s, a.shape[-1]) for a in real[:2] + real[3:])
    km = jnp.pad(meta[1], ((0, LANES - N_META), (0, 0)))
    vmt = jnp.pad(meta[2][0], ((0, 0), (0, LANES - N_META)))
    o_a = _sparse_attention(q, k, vt, km, vmt, iq, ik, misc,
                            tq=LANES, kc=min(512, s), topk=min(INDEX_TOPK, s // 4))
    lead = lambda a: jnp.pad(a, ((CHUNK - N_META, 0), (0, 0)))
    o_b = _gated_deltanet(qkv, z, misc, lead(meta[6]), lead(meta[5]), conv_w[0], a_log[0],
                          dt_bias[0], delta_norm_w[0])
    n_tok = b * s
    h, xn, gates, eid = _out_router(o_a.reshape(n_tok, A_WIDTH), o_b.reshape(n_tok, B_WIDTH),
                                    x.reshape(n_tok, d), w_out[0], norm_ffn_w[0], w_router[0],
                                    b_router[0], 256)
    block_e, n_valid, row_tok, pos = _route_blocks(eid[:, :TOP_K], MOE_BM)
    bgu = b_gate_up[0].reshape(N_EXPERTS, 1, 2 * D_FF)
    x_sorted = _sc_gather_rows(xn, row_tok)
    y_sorted = _moe_dense(x_sorted, block_e, n_valid, w_gate_up[0], w_down[0], bgu[:, :, 0::2],
                          bgu[:, :, 1::2], b_down[0].reshape(N_EXPERTS, 1, D_MODEL), MOE_BM)
    y = _sc_gather_rows(y_sorted, pos)
    out = _combine(h, gates, y, 256)
    return out.reshape(b, s, d)
```

```python
import functools
import math

import jax
import jax.numpy as jnp
from jax import lax
from jax.experimental import pallas as pl
from jax.experimental.pallas import tpu as pltpu
from jax.experimental.pallas import tpu_sc as plsc

F32 = jnp.float32
BF16 = jnp.bfloat16
HIGHEST = lax.Precision.HIGHEST

D_MODEL = 1024
N_META = 16
ROPE_THETA = 10000.0
EPS = 1e-6
A_HEAD_DIM = 64
A_HEADS = 8
A_WIDTH = A_HEADS * A_HEAD_DIM
KV_RANK = 256
IDX_HEADS = 8
IDX_DIM = 64
INDEX_TOPK = 256
B_HEAD_DIM = 128
B_HEADS = 4
B_WIDTH = B_HEADS * B_HEAD_DIM
CONV_WIDTH = 4
CHUNK = 64
N_EXPERTS = 32
TOP_K = 4
D_FF = D_MODEL
SWIGLU_LIMIT = 7.0
SWIGLU_ALPHA = 1.702
IN_SPLITS = (A_WIDTH, KV_RANK, IDX_HEADS * IDX_DIM, IDX_DIM, IDX_HEADS, 3 * B_WIDTH, B_WIDTH,
             B_HEADS, B_HEADS)

LANES = 128

C_Q = 0
C_CKV = C_Q + A_WIDTH
C_IQ = C_CKV + KV_RANK
C_IK = C_IQ + IDX_HEADS * IDX_DIM
C_MISC = C_IK + LANES
C_QKV = C_MISC + LANES
C_Z = C_QKV + 3 * B_WIDTH
C_END = C_Z + B_WIDTH
MISC_W, MISC_BETA, MISC_DECAY = 0, IDX_HEADS, IDX_HEADS + B_HEADS
LOG2E = math.log2(math.e)
VT_ROWS = A_HEAD_DIM + 16


def _rope_partner(a):
    lane = lax.broadcasted_iota(jnp.int32, a.shape, 1)
    first_half = (lane % A_HEAD_DIM) < (A_HEAD_DIM // 2)
    return jnp.where(first_half, pltpu.roll(a, LANES - A_HEAD_DIM // 2, 1),
                     pltpu.roll(a, A_HEAD_DIM // 2, 1))


def _rope(a, cos, sin_signed):
    return a * cos + _rope_partner(a) * sin_signed


def _head_rms(a, head_mean, gain):
    msq = jnp.dot((a * a).astype(BF16), head_mean, preferred_element_type=F32)
    return a * lax.rsqrt(msq + EPS) * gain


def _in_proj_kernel(x_ref, nw_ref, w_ref, wkv_ref, qnw_ref, knw_ref, kvnw_ref, cos_ref, sin_ref,
                    hm_ref, q_ref, k_ref, vt_ref, iq_ref, ik_ref, misc_ref, qkv_ref, z_ref):
    x = x_ref[...]
    u = x * lax.rsqrt(jnp.mean(x * x, axis=-1, keepdims=True) + EPS) * nw_ref[...]
    ub = u.astype(BF16)

    def proj(c0, c1):
        return jnp.dot(ub, w_ref[:, c0:c1], preferred_element_type=F32)

    cos = cos_ref[...]
    sin = sin_ref[...]
    hm = hm_ref[...]

    def rope_groups(a):
        return jnp.concatenate(
            [_rope(a[:, g * LANES:(g + 1) * LANES], cos, sin) for g in range(a.shape[1] // LANES)],
            axis=1)

    q = _head_rms(proj(C_Q, C_CKV), hm, qnw_ref[...])
    q_ref[...] = (rope_groups(q) * (A_HEAD_DIM ** -0.5 * LOG2E)).astype(BF16)

    ckv = proj(C_CKV, C_IQ)
    ckv = ckv * lax.rsqrt(jnp.mean(ckv * ckv, axis=-1, keepdims=True) + EPS) * kvnw_ref[...]
    kv = jnp.dot(ckv.astype(BF16), wkv_ref[...], preferred_element_type=F32)
    k = _head_rms(kv[:, :A_WIDTH], hm, knw_ref[...])
    k_ref[...] = rope_groups(k).astype(BF16)
    tm = x.shape[0]
    ones = jnp.ones((VT_ROWS - A_HEAD_DIM, tm), BF16)
    for g in range(A_WIDTH // LANES):
        vt = kv[:, A_WIDTH + g * LANES:A_WIDTH + (g + 1) * LANES].T.astype(BF16)
        for half in range(2):
            r0 = (2 * g + half) * VT_ROWS
            vt_ref[0, r0:r0 + A_HEAD_DIM, :] = vt[half * A_HEAD_DIM:(half + 1) * A_HEAD_DIM, :]
            vt_ref[0, r0 + A_HEAD_DIM:r0 + VT_ROWS, :] = ones

    iq_ref[...] = rope_groups(proj(C_IQ, C_IK)).astype(BF16)
    ik_ref[...] = _rope(proj(C_IK, C_MISC), cos, sin).astype(BF16)

    lane = lax.broadcasted_iota(jnp.int32, (1, LANES), 1)
    w_scale = jnp.where(lane < IDX_HEADS, IDX_HEADS ** -0.5 * IDX_DIM ** -0.5, 1.0)
    misc_ref[...] = proj(C_MISC, C_QKV) * w_scale
    qkv_ref[...] = proj(C_QKV, C_Z)
    z_ref[...] = proj(C_Z, C_END)


def _in_proj(x2d, cos, sin, consts, tm):
    n = x2d.shape[0]
    n_pos_blocks = cos.shape[0] // tm
    nw, w_pack, wkv, qnw, knw, kvnw, hm = consts
    row = lambda i: (i, 0)
    fixed = lambda i: (0, 0)
    pos = lambda i: (i % n_pos_blocks, 0)
    out_widths = (A_WIDTH, A_WIDTH, None, IDX_HEADS * IDX_DIM, LANES, LANES, 3 * B_WIDTH, B_WIDTH)
    out_dtypes = (BF16, BF16, BF16, BF16, BF16, F32, F32, F32)
    vt_rows = A_HEADS * VT_ROWS
    out_specs = [pl.BlockSpec((1, vt_rows, tm), lambda i: (i, 0, 0)) if w is None
                 else pl.BlockSpec((tm, w), row) for w in out_widths]
    out_shape = [jax.ShapeDtypeStruct((n // tm, vt_rows, tm) if w is None else (n, w), dt)
                 for w, dt in zip(out_widths, out_dtypes)]
    return pl.pallas_call(
        _in_proj_kernel,
        grid=(n // tm,),
        in_specs=[
            pl.BlockSpec((tm, D_MODEL), row),
            pl.BlockSpec(nw.shape, fixed),
            pl.BlockSpec(w_pack.shape, fixed),
            pl.BlockSpec(wkv.shape, fixed),
            pl.BlockSpec(qnw.shape, fixed),
            pl.BlockSpec(knw.shape, fixed),
            pl.BlockSpec(kvnw.shape, fixed),
            pl.BlockSpec((tm, LANES), pos),
            pl.BlockSpec((tm, LANES), pos),
            pl.BlockSpec(hm.shape, fixed),
        ],
        out_specs=out_specs,
        out_shape=out_shape,
        compiler_params=pltpu.CompilerParams(dimension_semantics=("parallel",)),
        name="in_proj",
    )(x2d, nw, w_pack, wkv, qnw, knw, kvnw, cos, sin, hm)


def _pack_in_proj_weights(norm_mix_w, w_in, q_norm_w, k_norm_w, kv_norm_w, w_kv_up):
    points = []
    acc = 0
    for s in IN_SPLITS:
        points.append((acc, acc + s))
        acc += s
    a_q, a_ckv, i_q, i_k, i_w, b_qkv, b_z, b_beta, b_a = (w_in[:, a:b] for a, b in points)
    misc = jnp.concatenate(
        [i_w, b_beta, b_a, jnp.zeros((D_MODEL, LANES - IDX_HEADS - 2 * B_HEADS), w_in.dtype)], axis=1)
    w_pack = jnp.concatenate([a_q, a_ckv, i_q, i_k, i_k, misc, b_qkv, b_z], axis=1).astype(BF16)
    head_mean = jnp.kron(jnp.eye(A_HEADS, dtype=F32),
                         jnp.full((A_HEAD_DIM, A_HEAD_DIM), 1.0 / A_HEAD_DIM, F32)).astype(BF16)
    return (norm_mix_w.reshape(1, D_MODEL), w_pack, w_kv_up.astype(BF16),
            jnp.tile(q_norm_w, A_HEADS).reshape(1, A_WIDTH),
            jnp.tile(k_norm_w, A_HEADS).reshape(1, A_WIDTH),
            kv_norm_w.reshape(1, KV_RANK), head_mean)


def _rope_tables(n_pos):
    half = A_HEAD_DIM // 2
    inv_freq = ROPE_THETA ** (-jnp.arange(0, A_HEAD_DIM, 2, dtype=F32) / A_HEAD_DIM)
    ang = jnp.arange(n_pos, dtype=F32)[:, None] * inv_freq[None, :]
    cos, sin = jnp.cos(ang), jnp.sin(ang)
    cos128 = jnp.tile(cos, (1, LANES // half))
    sin128 = jnp.tile(jnp.concatenate([-sin, sin], axis=1), (1, LANES // A_HEAD_DIM))
    return cos128, sin128


NEG_INF = float("-inf")
F32_MAX = float(jnp.finfo(jnp.float32).max)
INT_MIN = -2 ** 31


VT_TILE = 256
SUBLANES = 8


def _attn_kernel(q_ref, k_ref, vt_ref, km_ref, vmt_ref, iq_ref, ik_ref, misc_ref, o_ref,
                 isc_ref, isc_hi_ref, lhs_ref, qm_ref, s_ref, m_ref, acc_ref, *, tq, kc, topk, pos_bits):
    j = pl.program_id(1)
    n_kc = lax.div((j + 1) * tq + (kc - 1), kc)
    g = kc // SUBLANES
    lane = lax.broadcasted_iota(jnp.int32, (1, LANES), 1)
    lo_half = lane < A_HEAD_DIM
    n_pairs = A_WIDTH // LANES
    nt = (((1,), (1,)), ((), ()))

    iq = iq_ref[0]
    q = q_ref[0]
    zero = jnp.zeros((), BF16)
    for p in range(n_pairs):
        blk = iq[:, p * LANES:(p + 1) * LANES]
        lhs_ref[(2 * p) * tq:(2 * p + 1) * tq, :] = jnp.where(lo_half, blk, zero)
        lhs_ref[(2 * p + 1) * tq:(2 * p + 2) * tq, :] = jnp.where(lo_half, zero, blk)
        qb = q[:, p * LANES:(p + 1) * LANES]
        qm_ref[p, :tq, :] = jnp.where(lo_half, qb, zero)
        qm_ref[p, tq:, :] = jnp.where(lo_half, zero, qb)
    w_t = misc_ref[0].T

    def fold0(x3, op):
        n = x3.shape[0]
        while n > 1:
            x3 = op(x3[:n // 2], x3[n // 2:n])
            n //= 2
        return x3[0]

    q_pos = j * tq + lax.broadcasted_iota(jnp.int32, (1, tq), 1)

    def idx_body(c, carry):
        k0 = pl.multiple_of(c * kc, kc)
        r = lax.dot_general(ik_ref[0, pl.ds(k0, kc), :], lhs_ref[...], nt,
                            preferred_element_type=F32)
        s = None
        for h in range(IDX_HEADS):
            term = jnp.maximum(r[:, h * tq:(h + 1) * tq], 0.0) * w_t[MISC_W + h:MISC_W + h + 1, :]
            s = term if s is None else s + term
        k_pos = k0 + lax.broadcasted_iota(jnp.int32, (kc, 1), 0)
        s = jnp.where(k_pos <= q_pos, s, NEG_INF)
        isc_ref[c] = s
        top = lax.bitcast_convert_type(s, jnp.int32) & jnp.int32(-65536)
        isc_hi_ref[c] = lax.bitcast_convert_type(top, F32).astype(BF16)
        return carry
    lax.fori_loop(0, n_kc, idx_body, 0)

    def count(pred):
        def body(c, acc):
            x3 = isc_ref[c].reshape(g, SUBLANES, tq)
            return acc + fold0(jnp.where(pred(x3, c), 1.0, 0.0), jnp.add)
        acc = lax.fori_loop(0, n_kc, body, jnp.zeros((SUBLANES, tq), F32))
        return jnp.broadcast_to(jnp.sum(acc, axis=0, keepdims=True), (SUBLANES, tq))

    def key_to_float(u):
        key = u ^ jnp.int32(INT_MIN)
        bits = jnp.where(key >= 0, key, key ^ jnp.int32(0x7FFFFFFF))
        return lax.bitcast_convert_type(bits, F32)

    packed = 2 * SUBLANES
    def count_hi(cand_hi):
        one = jnp.ones((), BF16)
        zero_b = jnp.zeros((), BF16)
        def body(c, acc):
            x3 = isc_hi_ref[c].reshape(kc // packed, packed, tq)
            return acc + fold0(jnp.where(x3 >= cand_hi[None], one, zero_b), jnp.add).astype(F32)
        acc = lax.fori_loop(0, n_kc, body, jnp.zeros((packed, tq), F32))
        return jnp.broadcast_to(jnp.sum(acc, axis=0, keepdims=True), (packed, tq))

    def hi_body(i, u):
        u2 = u | lax.shift_left(jnp.int32(1), 31 - i)
        top = lax.bitcast_convert_type(key_to_float(u2), jnp.int32) & jnp.int32(-65536)
        n_ge = count_hi(lax.bitcast_convert_type(top, F32).astype(BF16))
        return jnp.where(n_ge >= topk, u2, u)
    u = lax.fori_loop(0, 16, hi_body, jnp.zeros((packed, tq), jnp.int32))[:SUBLANES]

    def bit_body(i, u):
        u2 = u | lax.shift_left(jnp.int32(1), 31 - i)
        cand = key_to_float(u2)
        n_ge = count(lambda x3, c: x3 >= cand[None])
        return jnp.where(n_ge >= topk, u2, u)
    u = lax.fori_loop(16, 32, bit_body, u)
    few = (u >= 0) & (u < 0x00800000)
    tau = jnp.where(few, -F32_MAX, key_to_float(u))

    n_ge = count(lambda x3, c: x3 >= tau[None])

    @pl.when(jnp.max(n_ge) > topk)
    def _():
        n_gt = count(lambda x3, c: x3 > tau[None])
        need = topk - n_gt
        def chunk_pos(c):
            return (c * kc + lax.broadcasted_iota(jnp.int32, (g, SUBLANES, tq), 0) * SUBLANES
                    + lax.broadcasted_iota(jnp.int32, (g, SUBLANES, tq), 1))
        def pos_body(i, cut):
            cut2 = cut | lax.shift_left(jnp.int32(1), pos_bits - 1 - i)
            ties_before = count(lambda x3, c: (x3 == tau[None]) & (chunk_pos(c) < cut2[None]))
            return jnp.where(ties_before < need, cut2, cut)
        cut = lax.fori_loop(0, pos_bits, pos_body, jnp.zeros((SUBLANES, tq), jnp.int32))
        def drop_body(c, carry):
            x3 = isc_ref[c].reshape(g, SUBLANES, tq)
            drop = (x3 == tau[None]) & (chunk_pos(c) > cut[None])
            isc_ref[c] = jnp.where(drop, NEG_INF, x3).reshape(kc, tq)
            return carry
        lax.fori_loop(0, n_kc, drop_body, 0)

    tau_row = tau[0:1, :]

    def attend(key_pairs, bias, vt_slabs, first):
        n = key_pairs[0].shape[0]
        m_cur = []
        for p in range(n_pairs):
            s2 = lax.dot_general(key_pairs[p], qm_ref[p], nt, preferred_element_type=F32)
            for half in range(2):
                s = s2[:, half * tq:(half + 1) * tq] + bias
                s_ref[2 * p + half, :n, :] = s
                m8 = fold0(s.reshape(n // SUBLANES, SUBLANES, tq), jnp.maximum)
                m_cur.append(jnp.max(m8, axis=0, keepdims=True))
        for hd in range(A_HEADS):
            m_new = m_cur[hd] if first else jnp.maximum(m_ref[hd][0:1, :], m_cur[hd])
            e = jnp.exp2(s_ref[hd, :n, :] - m_new).astype(BF16)
            pv = None
            off = 0
            for slab in vt_slabs(hd):
                part = jnp.dot(slab, e[off:off + slab.shape[1], :], preferred_element_type=F32)
                pv = part if pv is None else pv + part
                off += slab.shape[1]
            if first:
                acc_ref[hd] = pv
            else:
                acc_ref[hd] = jnp.exp2(m_ref[hd][0:1, :] - m_new) * acc_ref[hd] + pv
            m_ref[hd] = jnp.broadcast_to(m_new, (SUBLANES, tq))

    meta_bias = jnp.where(lax.broadcasted_iota(jnp.int32, (LANES, 1), 0) < N_META, 0.0, NEG_INF)
    attend([km_ref[:, p * LANES:(p + 1) * LANES] for p in range(n_pairs)], meta_bias,
           lambda hd: [vmt_ref[hd * VT_ROWS:(hd + 1) * VT_ROWS, :]], True)

    def att_body(c, carry):
        k0 = pl.multiple_of(c * kc, kc)
        bias = jnp.where(isc_ref[c] >= tau_row, 0.0, NEG_INF)
        attend([k_ref[0, pl.ds(k0, kc), p * LANES:(p + 1) * LANES] for p in range(n_pairs)], bias,
               lambda hd: [vt_ref[c * (kc // VT_TILE) + t, hd * VT_ROWS:(hd + 1) * VT_ROWS, :]
                           for t in range(kc // VT_TILE)], False)
        return carry
    lax.fori_loop(0, n_kc, att_body, 0)

    for p in range(n_pairs):
        halves = []
        for hd in (2 * p, 2 * p + 1):
            acc = acc_ref[hd]
            halves.append(acc[:A_HEAD_DIM, :] / acc[A_HEAD_DIM:A_HEAD_DIM + 1, :])
        o_ref[0, :, p * LANES:(p + 1) * LANES] = jnp.concatenate(halves, axis=0).T.astype(BF16)


def _sparse_attention(q, k, vt, km, vmt, iq, ik, misc, *, tq, kc, topk):
    b, s, _ = q.shape
    assert tq == LANES and kc % VT_TILE == 0 and s % kc == 0
    kernel = functools.partial(_attn_kernel, tq=tq, kc=kc, topk=topk, pos_bits=int(math.log2(s)))
    qblk = lambda bi, j: (bi, j, 0)
    full = lambda bi, j: (bi, 0, 0)
    fixed = lambda bi, j: (0, 0)
    vt_rows = A_HEADS * VT_ROWS
    return pl.pallas_call(
        kernel,
        grid=(b, s // tq),
        in_specs=[
            pl.BlockSpec((1, tq, A_WIDTH), qblk),
            pl.BlockSpec((1, s, A_WIDTH), full),
            pl.BlockSpec((s // VT_TILE, vt_rows, VT_TILE), full),
            pl.BlockSpec(km.shape, fixed),
            pl.BlockSpec(vmt.shape, fixed),
            pl.BlockSpec((1, tq, IDX_HEADS * IDX_DIM), qblk),
            pl.BlockSpec((1, s, LANES), full),
            pl.BlockSpec((1, tq, LANES), qblk),
        ],
        out_specs=pl.BlockSpec((1, tq, A_WIDTH), qblk),
        out_shape=jax.ShapeDtypeStruct((b, s, A_WIDTH), BF16),
        scratch_shapes=[
            pltpu.VMEM((s // kc, kc, tq), F32),
            pltpu.VMEM((s // kc, kc, tq), BF16),
            pltpu.VMEM((IDX_HEADS * tq, LANES), BF16),
            pltpu.VMEM((A_WIDTH // LANES, 2 * tq, LANES), BF16),
            pltpu.VMEM((A_HEADS, max(kc, LANES), tq), F32),
            pltpu.VMEM((A_HEADS, SUBLANES, tq), F32),
            pltpu.VMEM((A_HEADS, VT_ROWS, tq), F32),
        ],
        compiler_params=pltpu.CompilerParams(dimension_semantics=("parallel", "arbitrary")),
        name="sparse_attention",
    )(q, k, vt, km, vmt, iq, ik, misc)


HALO = 8


def _softplus(x):
    return jnp.maximum(x, 0.0) + jnp.log1p(jnp.exp(-jnp.abs(x)))


def _bdot(a, b):
    return jnp.dot(a.astype(BF16), b.astype(BF16), preferred_element_type=F32)


def _hdot(a, b):
    return jnp.dot(a, b, precision=HIGHEST, preferred_element_type=F32)


def _dot3(a, b):
    a_hi = a.astype(BF16)
    b_hi = b.astype(BF16)
    a_lo = (a - a_hi.astype(F32)).astype(BF16)
    b_lo = (b - b_hi.astype(F32)).astype(BF16)
    dot = functools.partial(jnp.dot, preferred_element_type=F32)
    return dot(a_hi, b_hi) + (dot(a_hi, b_lo) + dot(a_lo, b_hi))


def _delta_kernel(qkv_ref, z_ref, misc_ref, qkvm_ref, miscm_ref, convw_ref, alog_ref, dtb_ref,
                  nw_ref, o_ref, state_ref, halo_ref):
    c = pl.program_id(1)
    n_pad = CHUNK - N_META
    nt = (((1,), (1,)), ((), ()))
    tn = (((0,), (0,)), ((), ()))

    @pl.when(c == 0)
    def _():
        state_ref[...] = jnp.zeros_like(state_ref)
        halo_ref[...] = jnp.zeros_like(halo_ref)

    ri = lax.broadcasted_iota(jnp.int32, (CHUNK, CHUNK), 0)
    ci = lax.broadcasted_iota(jnp.int32, (CHUNK, CHUNK), 1)
    incl = ri >= ci
    strict = ri > ci
    eye = (ri == ci).astype(F32)
    tri = incl.astype(F32)
    row = lax.broadcasted_iota(jnp.int32, (CHUNK, 1), 0)
    is_meta = c == 0
    neutral = jnp.logical_and(is_meta, row < n_pad)
    cols = lambda base, h: slice(base + h * B_HEAD_DIM, base + (h + 1) * B_HEAD_DIM)

    nb = qkv_ref.shape[0]
    heads = range(nb * B_HEADS)
    qn, kn, vb, kb, gc, decay = [], [], [], [], [], []
    for bi in range(nb):
        xin = jnp.where(is_meta, qkvm_ref[...], qkv_ref[bi])
        misc = jnp.where(is_meta, miscm_ref[...], misc_ref[bi])
        xcat = jnp.concatenate([halo_ref[bi], xin], axis=0)
        halo_ref[bi] = xin[CHUNK - HALO:, :]
        conv = None
        for tap in range(CONV_WIDTH):
            off = HALO - (CONV_WIDTH - 1) + tap
            term = xcat[off:off + CHUNK, :] * convw_ref[tap:tap + 1, :]
            conv = term if conv is None else conv + term
        xc = conv * jax.nn.sigmoid(conv)
        beta_all = jnp.where(neutral, 0.0, jax.nn.sigmoid(misc))
        g_all = jnp.where(neutral, 0.0, -jnp.exp(alog_ref[...]) * _softplus(misc + dtb_ref[...]))
        gc_all = _hdot(tri, g_all)
        gc_all_t = gc_all.T
        for h in range(B_HEADS):
            qh, kh, vh = xc[:, cols(0, h)], xc[:, cols(B_WIDTH, h)], xc[:, cols(2 * B_WIDTH, h)]
            qn.append(qh * lax.rsqrt(jnp.sum(qh * qh, axis=-1, keepdims=True) + EPS) * (B_HEAD_DIM ** -0.5))
            kn.append(kh * lax.rsqrt(jnp.sum(kh * kh, axis=-1, keepdims=True) + EPS))
            beta = jnp.broadcast_to(beta_all[:, MISC_BETA + h:MISC_BETA + h + 1], (CHUNK, B_HEAD_DIM))
            gc.append(jnp.broadcast_to(gc_all[:, MISC_DECAY + h:MISC_DECAY + h + 1], (CHUNK, B_HEAD_DIM)))
            g_col = gc[-1][:, :CHUNK]
            g_row = gc_all_t[MISC_DECAY + h:MISC_DECAY + h + 1, :]
            decay.append(jnp.where(incl, jnp.exp(jnp.where(incl, g_col - g_row, 0.0)), 0.0))
            kb.append(kn[-1] * beta)
            vb.append(vh * beta)

    kk = [lax.dot_general(kb[h].astype(BF16), kn[h].astype(BF16), nt, preferred_element_type=F32)
          for h in heads]
    qk = [lax.dot_general(qn[h].astype(BF16), kn[h].astype(BF16), nt, preferred_element_type=F32)
          for h in heads]
    intra = [jnp.where(incl, qk[h] * decay[h], 0.0) for h in heads]

    pw = [jnp.where(strict, -(kk[h] * decay[h]), 0.0) for h in heads]
    t_inv = [eye + pw[h] for h in heads]
    pw = [_dot3(pw[h], pw[h]) for h in heads]
    for _ in range(int(math.log2(CHUNK)) - 2):
        t_next = [t_inv[h] + _dot3(t_inv[h], pw[h]) for h in heads]
        pw = [_dot3(pw[h], pw[h]) for h in heads]
        t_inv = t_next
    t_inv = [t_inv[h] + _dot3(t_inv[h], pw[h]) for h in heads]

    u = [_bdot(t_inv[h], vb[h]) for h in heads]
    w = [_bdot(t_inv[h], kb[h] * jnp.exp(gc[h])) for h in heads]

    state = [state_ref[h] for h in heads]
    w_s = [_bdot(w[h], state[h]) for h in heads]
    q_s = [_bdot(qn[h] * jnp.exp(gc[h]), state[h]) for h in heads]
    v_new = [u[h] - w_s[h] for h in heads]
    o = [q_s[h] + _bdot(intra[h], v_new[h]) for h in heads]
    for h in heads:
        g_last = gc[h][CHUNK - 1:CHUNK, :]
        kd = kn[h] * jnp.exp(g_last - gc[h])
        state_ref[h] = state[h] * jnp.exp(g_last) + lax.dot_general(
            kd.astype(BF16), v_new[h].astype(BF16), tn, preferred_element_type=F32)

    for h in heads:
        bi, hh = divmod(h, B_HEADS)
        y = o[h] * lax.rsqrt(jnp.mean(o[h] * o[h], axis=-1, keepdims=True) + EPS) * nw_ref[...]
        zh = z_ref[bi, :, cols(0, hh)]
        o_ref[bi, :, cols(0, hh)] = (y * (zh * jax.nn.sigmoid(zh))).astype(BF16)


DELTA_BATCH = 2


def _gated_deltanet(qkv, z, misc, qkv_m, misc_m, conv_w, a_log, dt_bias, norm_w):
    b, s, _ = qkv.shape
    n_chunks = s // CHUNK + 1
    nb = DELTA_BATCH if b % DELTA_BATCH == 0 else 1
    blk = lambda bi, c: (bi, jnp.maximum(c - 1, 0), 0)
    fixed = lambda bi, c: (0, 0)
    lane_vec = lambda v: jnp.zeros((1, LANES), F32).at[0, MISC_DECAY:MISC_DECAY + B_HEADS].set(v)
    return pl.pallas_call(
        _delta_kernel,
        grid=(b // nb, n_chunks),
        in_specs=[
            pl.BlockSpec((nb, CHUNK, 3 * B_WIDTH), blk),
            pl.BlockSpec((nb, CHUNK, B_WIDTH), blk),
            pl.BlockSpec((nb, CHUNK, LANES), blk),
            pl.BlockSpec((CHUNK, 3 * B_WIDTH), fixed),
            pl.BlockSpec((CHUNK, LANES), fixed),
            pl.BlockSpec((CONV_WIDTH, 3 * B_WIDTH), fixed),
            pl.BlockSpec((1, LANES), fixed),
            pl.BlockSpec((1, LANES), fixed),
            pl.BlockSpec((1, B_HEAD_DIM), fixed),
        ],
        out_specs=pl.BlockSpec((nb, CHUNK, B_WIDTH), blk),
        out_shape=jax.ShapeDtypeStruct((b, s, B_WIDTH), BF16),
        scratch_shapes=[
            pltpu.VMEM((nb * B_HEADS, B_HEAD_DIM, B_HEAD_DIM), F32),
            pltpu.VMEM((nb, HALO, 3 * B_WIDTH), F32),
        ],
        compiler_params=pltpu.CompilerParams(dimension_semantics=("parallel", "arbitrary")),
        name="gated_deltanet",
    )(qkv, z, misc, qkv_m, misc_m, conv_w, lane_vec(a_log), lane_vec(dt_bias),
      norm_w.reshape(1, B_HEAD_DIM))


def _pack_rows(a):
    half = a.shape[1] // 2
    bits = lax.bitcast_convert_type(a.astype(BF16).astype(F32), jnp.int32)
    return bits[:, :half] | lax.shift_right_logical(bits[:, half:], 16)


def _unpack_rows(w):
    hi = lax.bitcast_convert_type(w & jnp.int32(-65536), F32)
    lo = lax.bitcast_convert_type(lax.shift_left(w, 16), F32)
    return jnp.concatenate([hi, lo], axis=1)


def _out_router_kernel(oa_ref, ob_ref, x_ref, wo_ref, nw_ref, wr_ref, br_ref,
                       h_ref, xn_ref, gate_ref, eid_ref):
    mix = (jnp.dot(oa_ref[...], wo_ref[:A_WIDTH, :], preferred_element_type=F32)
           + jnp.dot(ob_ref[...], wo_ref[A_WIDTH:, :], preferred_element_type=F32))
    h = x_ref[...] + mix
    h_ref[...] = h
    xn = h * lax.rsqrt(jnp.mean(h * h, axis=-1, keepdims=True) + EPS) * nw_ref[...]
    xn_ref[...] = _pack_rows(xn)
    tm = xn.shape[0]
    logits = lax.dot_general(wr_ref[...], xn, (((1,), (1,)), ((), ())), precision=HIGHEST,
                             preferred_element_type=F32) + br_ref[...]
    e_idx = lax.broadcasted_iota(jnp.int32, (N_EXPERTS, tm), 0).astype(F32)
    work = logits
    vals, idxs = [], []
    for _ in range(TOP_K):
        m = jnp.max(work, axis=0, keepdims=True)
        idx = jnp.min(jnp.where(work == m, e_idx, float(N_EXPERTS)), axis=0, keepdims=True)
        vals.append(m)
        idxs.append(idx)
        work = jnp.where(e_idx == idx, NEG_INF, work)
    exps = [jnp.exp(v - vals[0]) for v in vals]
    denom = exps[0]
    for e in exps[1:]:
        denom = denom + e
    eid_ref[...] = jnp.concatenate(idxs + [jnp.zeros((SUBLANES - TOP_K, tm), F32)], axis=0).astype(jnp.int32)
    gates_t = jnp.concatenate([e / denom for e in exps] + [jnp.zeros((LANES - TOP_K, tm), F32)], axis=0)
    gate_ref[...] = gates_t.T


def _out_router(o_a, o_b, x2d, w_out, norm_w, w_router, b_router, tm):
    n = x2d.shape[0]
    row = lambda i: (i, 0)
    fixed = lambda i: (0, 0)
    wr = w_router.T
    br = b_router.reshape(N_EXPERTS, 1)
    return pl.pallas_call(
        _out_router_kernel,
        grid=(n // tm,),
        in_specs=[
            pl.BlockSpec((tm, A_WIDTH), row),
            pl.BlockSpec((tm, B_WIDTH), row),
            pl.BlockSpec((tm, D_MODEL), row),
            pl.BlockSpec((A_WIDTH + B_WIDTH, D_MODEL), fixed),
            pl.BlockSpec((1, D_MODEL), fixed),
            pl.BlockSpec((N_EXPERTS, D_MODEL), fixed),
            pl.BlockSpec((N_EXPERTS, 1), fixed),
        ],
        out_specs=[pl.BlockSpec((tm, D_MODEL), row), pl.BlockSpec((tm, D_MODEL // 2), row),
                   pl.BlockSpec((tm, LANES), row), pl.BlockSpec((SUBLANES, tm), lambda i: (0, i))],
        out_shape=[jax.ShapeDtypeStruct((n, D_MODEL), F32), jax.ShapeDtypeStruct((n, D_MODEL // 2), jnp.int32),
                   jax.ShapeDtypeStruct((n, LANES), F32), jax.ShapeDtypeStruct((SUBLANES, n), jnp.int32)],
        compiler_params=pltpu.CompilerParams(dimension_semantics=("parallel",)),
        name="out_proj_router",
    )(o_a, o_b, x2d, w_out.astype(BF16), norm_w.reshape(1, D_MODEL), wr, br)


MOE_BM = 256
MOE_BF = 512


SC_CORES = 2
SC_SUBCORES = 16
SC_ROWS = 128


def _sc_gather_rows(table, idx):
    n_workers = SC_CORES * SC_SUBCORES
    n_rows = idx.shape[0]
    d = table.shape[1]
    assert n_rows % (n_workers * SC_ROWS) == 0
    rows_per_worker = n_rows // n_workers
    mesh = plsc.VectorSubcoreMesh(core_axis_name="c", subcore_axis_name="s",
                                  num_cores=SC_CORES, num_subcores=SC_SUBCORES)

    @functools.partial(
        pl.kernel, mesh=mesh,
        out_type=jax.ShapeDtypeStruct((n_rows, d), table.dtype),
        scratch_types=[pltpu.VMEM((SC_ROWS,), jnp.int32), pltpu.VMEM((SC_ROWS, d), table.dtype),
                       pltpu.SemaphoreType.DMA],
        name="sc_gather_rows",
    )
    def gather(table_hbm, idx_hbm, out_hbm, idx_v, rows_v, sem):
        wid = lax.axis_index("s") * SC_CORES + lax.axis_index("c")
        base = wid * rows_per_worker

        @pl.loop(0, rows_per_worker // SC_ROWS)
        def _(i):
            off = base + i * SC_ROWS
            pltpu.sync_copy(idx_hbm.at[pl.ds(off, SC_ROWS)], idx_v)
            pltpu.async_copy(table_hbm.at[idx_v], rows_v, sem).wait()
            pltpu.sync_copy(rows_v, out_hbm.at[pl.ds(off, SC_ROWS)])

    return gather(table, idx)


def _route_blocks(eid_t, bm):
    n_tok = eid_t.shape[1]
    n_assign = n_tok * TOP_K
    experts = jnp.arange(N_EXPERTS, dtype=jnp.int32)
    flat_e = eid_t.reshape(-1)
    sorted_e, order = lax.sort((flat_e, jnp.arange(n_assign, dtype=jnp.int32)), num_keys=1)
    onehot = sorted_e[:, None] == experts[None, :]
    counts = jnp.sum(onehot, axis=0, dtype=jnp.int32)
    padded = (counts + bm - 1) // bm * bm
    start = jnp.cumsum(counts) - counts
    pend = jnp.cumsum(padded)
    pstart = pend - padded
    dest = jnp.arange(n_assign, dtype=jnp.int32) + jnp.sum(
        jnp.where(onehot, (pstart - start)[None, :], 0), axis=1)
    n_blocks = -(-(n_assign + N_EXPERTS * (bm - 1)) // bm)
    blk_start = jnp.arange(n_blocks, dtype=jnp.int32) * bm
    block_e = jnp.minimum(jnp.sum(blk_start[:, None] >= pend[None, :], axis=1), N_EXPERTS - 1)
    n_valid = pend[-1] // bm
    _, pos = lax.sort((order, dest), num_keys=1)
    t = jnp.arange(bm, dtype=jnp.int32)[None, :]
    pad_key = jnp.where(t < (padded - counts)[:, None], (pstart + counts)[:, None] + t,
                        n_blocks * bm).reshape(-1)
    pad_tok = jnp.arange(N_EXPERTS * bm, dtype=jnp.int32) % n_tok
    assert n_blocks * bm - n_assign == N_EXPERTS * bm
    _, row_tok = lax.sort((jnp.concatenate([dest, pad_key]),
                           jnp.concatenate([order % n_tok, pad_tok])), num_keys=1)
    return block_e.astype(jnp.int32), n_valid.astype(jnp.int32).reshape(1), row_tok, pos


def _moe_dense_kernel(be_ref, nv_ref, x_ref, wgu_ref, wd_ref, bg_ref, bu_ref, bd_ref, perm_ref, y_ref,
                      wg_s, wu_s, wd_s):
    i = pl.program_id(0)
    live = i < nv_ref[0]

    @pl.when(jnp.logical_and(live, jnp.logical_or(i == 0, be_ref[i] != be_ref[jnp.maximum(i - 1, 0)])))
    def _():
        perm = perm_ref[...]
        grp = 2 * LANES
        for gidx in range(2 * D_FF // grp):
            blk = wgu_ref[0, :, gidx * grp:(gidx + 1) * grp].astype(BF16)
            split = jnp.dot(blk, perm, preferred_element_type=F32).astype(BF16)
            wg_s[:, gidx * LANES:(gidx + 1) * LANES] = split[:, :LANES]
            wu_s[:, gidx * LANES:(gidx + 1) * LANES] = split[:, LANES:]
        wd_s[...] = wd_ref[0].astype(BF16)

    @pl.when(live)
    def _():
        x = _unpack_rows(x_ref[...]).astype(BF16)
        y = None
        for f0 in range(0, D_FF, MOE_BF):
            g = jnp.dot(x, wg_s[:, f0:f0 + MOE_BF], preferred_element_type=F32) + bg_ref[0, :, f0:f0 + MOE_BF]
            u = jnp.dot(x, wu_s[:, f0:f0 + MOE_BF], preferred_element_type=F32) + bu_ref[0, :, f0:f0 + MOE_BF]
            gate = jnp.minimum(g, SWIGLU_LIMIT)
            up = jnp.clip(u, -SWIGLU_LIMIT, SWIGLU_LIMIT)
            t = gate * jax.nn.sigmoid(gate * SWIGLU_ALPHA) * (up + 1.0)
            part = jnp.dot(t.astype(BF16), wd_s[f0:f0 + MOE_BF, :], preferred_element_type=F32)
            y = part if y is None else y + part
        y_ref[...] = _pack_rows(y + bd_ref[0])


def _moe_dense(x_sorted, block_e, n_valid, w_gate_up, w_down, b_gate, b_up, b_down, bm):
    n_blocks = block_e.shape[0]
    src = jnp.arange(2 * LANES)
    perm = jax.nn.one_hot((src % 2) * LANES + src // 2, 2 * LANES, dtype=BF16)
    rows = lambda i, be, nv: (jnp.minimum(i, nv[0] - 1), 0)
    wsel = lambda i, be, nv: (be[i], 0, 0)
    fixed = lambda i, be, nv: (0, 0)
    grid_spec = pltpu.PrefetchScalarGridSpec(
        num_scalar_prefetch=2,
        grid=(n_blocks,),
        in_specs=[
            pl.BlockSpec((bm, D_MODEL // 2), rows),
            pl.BlockSpec((1, D_MODEL, 2 * D_FF), wsel),
            pl.BlockSpec((1, D_FF, D_MODEL), wsel),
            pl.BlockSpec((1, 1, D_FF), wsel),
            pl.BlockSpec((1, 1, D_FF), wsel),
            pl.BlockSpec((1, 1, D_MODEL), wsel),
            pl.BlockSpec((2 * LANES, 2 * LANES), fixed),
        ],
        out_specs=pl.BlockSpec((bm, D_MODEL // 2), rows),
        scratch_shapes=[
            pltpu.VMEM((D_MODEL, D_FF), BF16),
            pltpu.VMEM((D_MODEL, D_FF), BF16),
            pltpu.VMEM((D_FF, D_MODEL), BF16),
        ],
    )
    return pl.pallas_call(
        _moe_dense_kernel,
        grid_spec=grid_spec,
        out_shape=jax.ShapeDtypeStruct(x_sorted.shape, jnp.int32),
        compiler_params=pltpu.CompilerParams(dimension_semantics=("arbitrary",)),
        name="moe_experts",
    )(block_e, n_valid, x_sorted, w_gate_up, w_down, b_gate, b_up, b_down, perm)


def _combine_kernel(h_ref, gate_ref, y0_ref, y1_ref, y2_ref, y3_ref, o_ref):
    gates = gate_ref[...]
    out = h_ref[...]
    for kk, y_ref in enumerate((y0_ref, y1_ref, y2_ref, y3_ref)):
        out = out + gates[:, kk:kk + 1] * _unpack_rows(y_ref[...])
    o_ref[...] = out


def _combine(h, gates, y, tm):
    n_tok = h.shape[0]
    nt = n_tok // tm
    row = lambda i: (i, 0)
    return pl.pallas_call(
        _combine_kernel,
        grid=(nt,),
        in_specs=[pl.BlockSpec((tm, D_MODEL), row), pl.BlockSpec((tm, LANES), row)]
        + [pl.BlockSpec((tm, D_MODEL // 2), functools.partial(lambda kk, i: (kk * nt + i, 0), kk))
           for kk in range(TOP_K)],
        out_specs=pl.BlockSpec((tm, D_MODEL), row),
        out_shape=jax.ShapeDtypeStruct((n_tok, D_MODEL), F32),
        compiler_params=pltpu.CompilerParams(dimension_semantics=("parallel",)),
        name="moe_combine",
    )(h, gates, y, y, y, y)


def kernel(x, meta_tokens, norm_mix_w, w_in, q_norm_w, k_norm_w, kv_norm_w, w_kv_up, conv_w, a_log,
           dt_bias, delta_norm_w, w_out, norm_ffn_w, w_router, b_router, w_gate_up, b_gate_up,
           w_down, b_down):
    b, s, d = x.shape
    consts = _pack_in_proj_weights(norm_mix_w[0], w_in[0], q_norm_w[0], k_norm_w[0], kv_norm_w[0],
                                   w_kv_up[0])
    cos, sin = _rope_tables(N_META + s)
    real = _in_proj(x.reshape(b * s, d), cos[N_META:], sin[N_META:], consts, VT_TILE)
    meta = _in_proj(meta_tokens, cos[:N_META], sin[:N_META], consts, N_META)
    vt = real[2]
    q, k, iq, ik, misc, qkv, z = (a.reshape(b, s, a.shape[-1]) for a in real[:2] + real[3:])
    km = jnp.pad(meta[1], ((0, LANES - N_META), (0, 0)))
    vmt = jnp.pad(meta[2][0], ((0, 0), (0, LANES - N_META)))
    o_a = _sparse_attention(q, k, vt, km, vmt, iq, ik, misc,
                            tq=LANES, kc=min(512, s), topk=min(INDEX_TOPK, s // 4))
    lead = lambda a: jnp.pad(a, ((CHUNK - N_META, 0), (0, 0)))
    o_b = _gated_deltanet(qkv, z, misc, lead(meta[6]), lead(meta[5]), conv_w[0], a_log[0],
                          dt_bias[0], delta_norm_w[0])
    n_tok = b * s
    h, xn, gates, eid = _out_router(o_a.reshape(n_tok, A_WIDTH), o_b.reshape(n_tok, B_WIDTH),
                                    x.reshape(n_tok, d), w_out[0], norm_ffn_w[0], w_router[0],
                                    b_router[0], 256)
    block_e, n_valid, row_tok, pos = _route_blocks(eid[:TOP_K], MOE_BM)
    bgu = b_gate_up[0].reshape(N_EXPERTS, 1, 2 * D_FF)
    x_sorted = _sc_gather_rows(xn, row_tok)
    y_sorted = _moe_dense(x_sorted, block_e, n_valid, w_gate_up[0], w_down[0], bgu[:, :, 0::2],
                          bgu[:, :, 1::2], b_down[0].reshape(N_EXPERTS, 1, D_MODEL), MOE_BM)
    y = _sc_gather_rows(y_sorted, pos)
    out = _combine(h, gates, y, 256)
    return out.reshape(b, s, d)
```

```python
import functools
import math

import jax
import jax.numpy as jnp
from jax import lax
from jax.experimental import pallas as pl
from jax.experimental.pallas import tpu as pltpu
from jax.experimental.pallas import tpu_sc as plsc

F32 = jnp.float32
BF16 = jnp.bfloat16
HIGHEST = lax.Precision.HIGHEST

D_MODEL = 1024
N_META = 16
ROPE_THETA = 10000.0
EPS = 1e-6
A_HEAD_DIM = 64
A_HEADS = 8
A_WIDTH = A_HEADS * A_HEAD_DIM
KV_RANK = 256
IDX_HEADS = 8
IDX_DIM = 64
INDEX_TOPK = 256
B_HEAD_DIM = 128
B_HEADS = 4
B_WIDTH = B_HEADS * B_HEAD_DIM
CONV_WIDTH = 4
CHUNK = 64
N_EXPERTS = 32
TOP_K = 4
D_FF = D_MODEL
SWIGLU_LIMIT = 7.0
SWIGLU_ALPHA = 1.702
IN_SPLITS = (A_WIDTH, KV_RANK, IDX_HEADS * IDX_DIM, IDX_DIM, IDX_HEADS, 3 * B_WIDTH, B_WIDTH,
             B_HEADS, B_HEADS)

LANES = 128

C_Q = 0
C_CKV = C_Q + A_WIDTH
C_IQ = C_CKV + KV_RANK
C_IK = C_IQ + IDX_HEADS * IDX_DIM
C_MISC = C_IK + LANES
C_QKV = C_MISC + LANES
C_Z = C_QKV + 3 * B_WIDTH
C_END = C_Z + B_WIDTH
MISC_W, MISC_BETA, MISC_DECAY = 0, IDX_HEADS, IDX_HEADS + B_HEADS
LOG2E = math.log2(math.e)
VT_ROWS = A_HEAD_DIM + 16


def _rope_partner(a):
    lane = lax.broadcasted_iota(jnp.int32, a.shape, 1)
    first_half = (lane % A_HEAD_DIM) < (A_HEAD_DIM // 2)
    return jnp.where(first_half, pltpu.roll(a, LANES - A_HEAD_DIM // 2, 1),
                     pltpu.roll(a, A_HEAD_DIM // 2, 1))


def _rope(a, cos, sin_signed):
    return a * cos + _rope_partner(a) * sin_signed


def _head_rms(a, head_mean, gain):
    msq = jnp.dot((a * a).astype(BF16), head_mean, preferred_element_type=F32)
    return a * lax.rsqrt(msq + EPS) * gain


def _in_proj_kernel(x_ref, nw_ref, w_ref, wkv_ref, qnw_ref, knw_ref, kvnw_ref, cos_ref, sin_ref,
                    hm_ref, q_ref, k_ref, vt_ref, iq_ref, ik_ref, misc_ref, qkv_ref, z_ref):
    x = x_ref[...]
    u = x * lax.rsqrt(jnp.mean(x * x, axis=-1, keepdims=True) + EPS) * nw_ref[...]
    ub = u.astype(BF16)

    def proj(c0, c1):
        return jnp.dot(ub, w_ref[:, c0:c1], preferred_element_type=F32)

    cos = cos_ref[...]
    sin = sin_ref[...]
    hm = hm_ref[...]

    def rope_groups(a):
        return jnp.concatenate(
            [_rope(a[:, g * LANES:(g + 1) * LANES], cos, sin) for g in range(a.shape[1] // LANES)],
            axis=1)

    q = _head_rms(proj(C_Q, C_CKV), hm, qnw_ref[...])
    q_ref[...] = (rope_groups(q) * (A_HEAD_DIM ** -0.5 * LOG2E)).astype(BF16)

    ckv = proj(C_CKV, C_IQ)
    ckv = ckv * lax.rsqrt(jnp.mean(ckv * ckv, axis=-1, keepdims=True) + EPS) * kvnw_ref[...]
    kv = jnp.dot(ckv.astype(BF16), wkv_ref[...], preferred_element_type=F32)
    k = _head_rms(kv[:, :A_WIDTH], hm, knw_ref[...])
    k_ref[...] = rope_groups(k).astype(BF16)
    tm = x.shape[0]
    ones = jnp.ones((VT_ROWS - A_HEAD_DIM, tm), BF16)
    for g in range(A_WIDTH // LANES):
        vt = kv[:, A_WIDTH + g * LANES:A_WIDTH + (g + 1) * LANES].T.astype(BF16)
        for half in range(2):
            r0 = (2 * g + half) * VT_ROWS
            vt_ref[0, r0:r0 + A_HEAD_DIM, :] = vt[half * A_HEAD_DIM:(half + 1) * A_HEAD_DIM, :]
            vt_ref[0, r0 + A_HEAD_DIM:r0 + VT_ROWS, :] = ones

    iq_ref[...] = rope_groups(proj(C_IQ, C_IK)).astype(BF16)
    ik_ref[...] = _rope(proj(C_IK, C_MISC), cos, sin).astype(BF16)

    lane = lax.broadcasted_iota(jnp.int32, (1, LANES), 1)
    w_scale = jnp.where(lane < IDX_HEADS, IDX_HEADS ** -0.5 * IDX_DIM ** -0.5, 1.0)
    misc_ref[...] = proj(C_MISC, C_QKV) * w_scale
    qkv_ref[...] = proj(C_QKV, C_Z)
    z_ref[...] = proj(C_Z, C_END)


def _in_proj(x2d, cos, sin, consts, tm):
    n = x2d.shape[0]
    n_pos_blocks = cos.shape[0] // tm
    nw, w_pack, wkv, qnw, knw, kvnw, hm = consts
    row = lambda i: (i, 0)
    fixed = lambda i: (0, 0)
    pos = lambda i: (i % n_pos_blocks, 0)
    out_widths = (A_WIDTH, A_WIDTH, None, IDX_HEADS * IDX_DIM, LANES, LANES, 3 * B_WIDTH, B_WIDTH)
    out_dtypes = (BF16, BF16, BF16, BF16, BF16, F32, F32, F32)
    vt_rows = A_HEADS * VT_ROWS
    out_specs = [pl.BlockSpec((1, vt_rows, tm), lambda i: (i, 0, 0)) if w is None
                 else pl.BlockSpec((tm, w), row) for w in out_widths]
    out_shape = [jax.ShapeDtypeStruct((n // tm, vt_rows, tm) if w is None else (n, w), dt)
                 for w, dt in zip(out_widths, out_dtypes)]
    return pl.pallas_call(
        _in_proj_kernel,
        grid=(n // tm,),
        in_specs=[
            pl.BlockSpec((tm, D_MODEL), row),
            pl.BlockSpec(nw.shape, fixed),
            pl.BlockSpec(w_pack.shape, fixed),
            pl.BlockSpec(wkv.shape, fixed),
            pl.BlockSpec(qnw.shape, fixed),
            pl.BlockSpec(knw.shape, fixed),
            pl.BlockSpec(kvnw.shape, fixed),
            pl.BlockSpec((tm, LANES), pos),
            pl.BlockSpec((tm, LANES), pos),
            pl.BlockSpec(hm.shape, fixed),
        ],
        out_specs=out_specs,
        out_shape=out_shape,
        compiler_params=pltpu.CompilerParams(dimension_semantics=("parallel",)),
        name="in_proj",
    )(x2d, nw, w_pack, wkv, qnw, knw, kvnw, cos, sin, hm)


def _pack_in_proj_weights(norm_mix_w, w_in, q_norm_w, k_norm_w, kv_norm_w, w_kv_up):
    points = []
    acc = 0
    for s in IN_SPLITS:
        points.append((acc, acc + s))
        acc += s
    a_q, a_ckv, i_q, i_k, i_w, b_qkv, b_z, b_beta, b_a = (w_in[:, a:b] for a, b in points)
    misc = jnp.concatenate(
        [i_w, b_beta, b_a, jnp.zeros((D_MODEL, LANES - IDX_HEADS - 2 * B_HEADS), w_in.dtype)], axis=1)
    w_pack = jnp.concatenate([a_q, a_ckv, i_q, i_k, i_k, misc, b_qkv, b_z], axis=1).astype(BF16)
    head_mean = jnp.kron(jnp.eye(A_HEADS, dtype=F32),
                         jnp.full((A_HEAD_DIM, A_HEAD_DIM), 1.0 / A_HEAD_DIM, F32)).astype(BF16)
    return (norm_mix_w.reshape(1, D_MODEL), w_pack, w_kv_up.astype(BF16),
            jnp.tile(q_norm_w, A_HEADS).reshape(1, A_WIDTH),
            jnp.tile(k_norm_w, A_HEADS).reshape(1, A_WIDTH),
            kv_norm_w.reshape(1, KV_RANK), head_mean)


def _rope_tables(n_pos):
    half = A_HEAD_DIM // 2
    inv_freq = ROPE_THETA ** (-jnp.arange(0, A_HEAD_DIM, 2, dtype=F32) / A_HEAD_DIM)
    ang = jnp.arange(n_pos, dtype=F32)[:, None] * inv_freq[None, :]
    cos, sin = jnp.cos(ang), jnp.sin(ang)
    cos128 = jnp.tile(cos, (1, LANES // half))
    sin128 = jnp.tile(jnp.concatenate([-sin, sin], axis=1), (1, LANES // A_HEAD_DIM))
    return cos128, sin128


NEG_INF = float("-inf")
F32_MAX = float(jnp.finfo(jnp.float32).max)
INT_MIN = -2 ** 31


VT_TILE = 256
SUBLANES = 8


def _attn_kernel(q_ref, k_ref, vt_ref, km_ref, vmt_ref, iq_ref, ik_ref, misc_ref, o_ref,
                 isc_ref, lhs_ref, qm_ref, s_ref, m_ref, acc_ref, *, tq, kc, topk, pos_bits):
    j = pl.program_id(1)
    n_kc = lax.div((j + 1) * tq + (kc - 1), kc)
    g = kc // SUBLANES
    lane = lax.broadcasted_iota(jnp.int32, (1, LANES), 1)
    lo_half = lane < A_HEAD_DIM
    n_pairs = A_WIDTH // LANES
    nt = (((1,), (1,)), ((), ()))

    iq = iq_ref[0]
    q = q_ref[0]
    zero = jnp.zeros((), BF16)
    for p in range(n_pairs):
        blk = iq[:, p * LANES:(p + 1) * LANES]
        lhs_ref[(2 * p) * tq:(2 * p + 1) * tq, :] = jnp.where(lo_half, blk, zero)
        lhs_ref[(2 * p + 1) * tq:(2 * p + 2) * tq, :] = jnp.where(lo_half, zero, blk)
        qb = q[:, p * LANES:(p + 1) * LANES]
        qm_ref[p, :tq, :] = jnp.where(lo_half, qb, zero)
        qm_ref[p, tq:, :] = jnp.where(lo_half, zero, qb)
    w_t = misc_ref[0].T

    def fold0(x3, op):
        n = x3.shape[0]
        while n > 1:
            x3 = op(x3[:n // 2], x3[n // 2:n])
            n //= 2
        return x3[0]

    q_pos = j * tq + lax.broadcasted_iota(jnp.int32, (1, tq), 1)

    def idx_body(c, carry):
        k0 = pl.multiple_of(c * kc, kc)
        r = lax.dot_general(ik_ref[0, pl.ds(k0, kc), :], lhs_ref[...], nt,
                            preferred_element_type=F32)
        s = None
        for h in range(IDX_HEADS):
            term = jnp.maximum(r[:, h * tq:(h + 1) * tq], 0.0) * w_t[MISC_W + h:MISC_W + h + 1, :]
            s = term if s is None else s + term
        k_pos = k0 + lax.broadcasted_iota(jnp.int32, (kc, 1), 0)
        isc_ref[c] = jnp.where(k_pos <= q_pos, s, NEG_INF)
        return carry
    lax.fori_loop(0, n_kc, idx_body, 0)

    def count(pred):
        def body(c, acc):
            x3 = isc_ref[c].reshape(g, SUBLANES, tq)
            return acc + fold0(jnp.where(pred(x3, c), 1.0, 0.0), jnp.add)
        acc = lax.fori_loop(0, n_kc, body, jnp.zeros((SUBLANES, tq), F32))
        return jnp.broadcast_to(jnp.sum(acc, axis=0, keepdims=True), (SUBLANES, tq))

    def key_to_float(u):
        key = u ^ jnp.int32(INT_MIN)
        bits = jnp.where(key >= 0, key, key ^ jnp.int32(0x7FFFFFFF))
        return lax.bitcast_convert_type(bits, F32)

    def bit_body(i, carry):
        u, n_u = carry
        u2 = u | lax.shift_left(jnp.int32(1), 31 - i)
        cand = key_to_float(u2)
        n_ge = count(lambda x3, c: x3 >= cand[None])
        keep = n_ge >= topk
        return jnp.where(keep, u2, u), jnp.where(keep, n_ge, n_u)
    u, n_u = lax.fori_loop(0, 32, bit_body, (jnp.zeros((SUBLANES, tq), jnp.int32),
                                             jnp.zeros((SUBLANES, tq), F32)))
    few = (u >= 0) & (u < 0x00800000)
    tau = jnp.where(few, -F32_MAX, key_to_float(u))

    n_ge = jnp.where(few, 0.0, n_u)

    @pl.when(jnp.max(n_ge) > topk)
    def _():
        n_gt = count(lambda x3, c: x3 > tau[None])
        need = topk - n_gt
        def chunk_pos(c):
            return (c * kc + lax.broadcasted_iota(jnp.int32, (g, SUBLANES, tq), 0) * SUBLANES
                    + lax.broadcasted_iota(jnp.int32, (g, SUBLANES, tq), 1))
        def pos_body(i, cut):
            cut2 = cut | lax.shift_left(jnp.int32(1), pos_bits - 1 - i)
            ties_before = count(lambda x3, c: (x3 == tau[None]) & (chunk_pos(c) < cut2[None]))
            return jnp.where(ties_before < need, cut2, cut)
        cut = lax.fori_loop(0, pos_bits, pos_body, jnp.zeros((SUBLANES, tq), jnp.int32))
        def drop_body(c, carry):
            x3 = isc_ref[c].reshape(g, SUBLANES, tq)
            drop = (x3 == tau[None]) & (chunk_pos(c) > cut[None])
            isc_ref[c] = jnp.where(drop, NEG_INF, x3).reshape(kc, tq)
            return carry
        lax.fori_loop(0, n_kc, drop_body, 0)

    tau_row = tau[0:1, :]

    def attend(key_pairs, bias, vt_slabs, first):
        n = key_pairs[0].shape[0]
        m_cur = []
        for p in range(n_pairs):
            s2 = lax.dot_general(key_pairs[p], qm_ref[p], nt, preferred_element_type=F32)
            for half in range(2):
                s = s2[:, half * tq:(half + 1) * tq] + bias
                s_ref[2 * p + half, :n, :] = s
                m8 = fold0(s.reshape(n // SUBLANES, SUBLANES, tq), jnp.maximum)
                m_cur.append(jnp.max(m8, axis=0, keepdims=True))
        for hd in range(A_HEADS):
            m_new = m_cur[hd] if first else jnp.maximum(m_ref[hd][0:1, :], m_cur[hd])
            e = jnp.exp2(s_ref[hd, :n, :] - m_new).astype(BF16)
            pv = None
            off = 0
            for slab in vt_slabs(hd):
                part = jnp.dot(slab, e[off:off + slab.shape[1], :], preferred_element_type=F32)
                pv = part if pv is None else pv + part
                off += slab.shape[1]
            if first:
                acc_ref[hd] = pv
            else:
                acc_ref[hd] = jnp.exp2(m_ref[hd][0:1, :] - m_new) * acc_ref[hd] + pv
            m_ref[hd] = jnp.broadcast_to(m_new, (SUBLANES, tq))

    meta_bias = jnp.where(lax.broadcasted_iota(jnp.int32, (LANES, 1), 0) < N_META, 0.0, NEG_INF)
    attend([km_ref[:, p * LANES:(p + 1) * LANES] for p in range(n_pairs)], meta_bias,
           lambda hd: [vmt_ref[hd * VT_ROWS:(hd + 1) * VT_ROWS, :]], True)

    def att_body(c, carry):
        k0 = pl.multiple_of(c * kc, kc)
        bias = jnp.where(isc_ref[c] >= tau_row, 0.0, NEG_INF)
        attend([k_ref[0, pl.ds(k0, kc), p * LANES:(p + 1) * LANES] for p in range(n_pairs)], bias,
               lambda hd: [vt_ref[c * (kc // VT_TILE) + t, hd * VT_ROWS:(hd + 1) * VT_ROWS, :]
                           for t in range(kc // VT_TILE)], False)
        return carry
    lax.fori_loop(0, n_kc, att_body, 0)

    for p in range(n_pairs):
        halves = []
        for hd in (2 * p, 2 * p + 1):
            acc = acc_ref[hd]
            halves.append(acc[:A_HEAD_DIM, :] / acc[A_HEAD_DIM:A_HEAD_DIM + 1, :])
        o_ref[0, :, p * LANES:(p + 1) * LANES] = jnp.concatenate(halves, axis=0).T.astype(BF16)


def _sparse_attention(q, k, vt, km, vmt, iq, ik, misc, *, tq, kc, topk):
    b, s, _ = q.shape
    assert tq == LANES and kc % VT_TILE == 0 and s % kc == 0
    kernel = functools.partial(_attn_kernel, tq=tq, kc=kc, topk=topk, pos_bits=int(math.log2(s)))
    qblk = lambda bi, j: (bi, j, 0)
    full = lambda bi, j: (bi, 0, 0)
    fixed = lambda bi, j: (0, 0)
    vt_rows = A_HEADS * VT_ROWS
    return pl.pallas_call(
        kernel,
        grid=(b, s // tq),
        in_specs=[
            pl.BlockSpec((1, tq, A_WIDTH), qblk),
            pl.BlockSpec((1, s, A_WIDTH), full),
            pl.BlockSpec((s // VT_TILE, vt_rows, VT_TILE), full),
            pl.BlockSpec(km.shape, fixed),
            pl.BlockSpec(vmt.shape, fixed),
            pl.BlockSpec((1, tq, IDX_HEADS * IDX_DIM), qblk),
            pl.BlockSpec((1, s, LANES), full),
            pl.BlockSpec((1, tq, LANES), qblk),
        ],
        out_specs=pl.BlockSpec((1, tq, A_WIDTH), qblk),
        out_shape=jax.ShapeDtypeStruct((b, s, A_WIDTH), BF16),
        scratch_shapes=[
            pltpu.VMEM((s // kc, kc, tq), F32),
            pltpu.VMEM((IDX_HEADS * tq, LANES), BF16),
            pltpu.VMEM((A_WIDTH // LANES, 2 * tq, LANES), BF16),
            pltpu.VMEM((A_HEADS, max(kc, LANES), tq), F32),
            pltpu.VMEM((A_HEADS, SUBLANES, tq), F32),
            pltpu.VMEM((A_HEADS, VT_ROWS, tq), F32),
        ],
        compiler_params=pltpu.CompilerParams(dimension_semantics=("parallel", "arbitrary")),
        name="sparse_attention",
    )(q, k, vt, km, vmt, iq, ik, misc)


HALO = 8


def _softplus(x):
    return jnp.maximum(x, 0.0) + jnp.log1p(jnp.exp(-jnp.abs(x)))


def _bdot(a, b):
    return jnp.dot(a.astype(BF16), b.astype(BF16), preferred_element_type=F32)


def _hdot(a, b):
    return jnp.dot(a, b, precision=HIGHEST, preferred_element_type=F32)


def _dot3(a, b):
    a_hi = a.astype(BF16)
    b_hi = b.astype(BF16)
    a_lo = (a - a_hi.astype(F32)).astype(BF16)
    b_lo = (b - b_hi.astype(F32)).astype(BF16)
    dot = functools.partial(jnp.dot, preferred_element_type=F32)
    return dot(a_hi, b_hi) + (dot(a_hi, b_lo) + dot(a_lo, b_hi))


def _delta_kernel(qkv_ref, z_ref, misc_ref, qkvm_ref, miscm_ref, convw_ref, alog_ref, dtb_ref,
                  nw_ref, o_ref, state_ref, halo_ref):
    c = pl.program_id(1)
    n_pad = CHUNK - N_META
    nt = (((1,), (1,)), ((), ()))
    tn = (((0,), (0,)), ((), ()))

    @pl.when(c == 0)
    def _():
        state_ref[...] = jnp.zeros_like(state_ref)
        halo_ref[...] = jnp.zeros_like(halo_ref)

    ri = lax.broadcasted_iota(jnp.int32, (CHUNK, CHUNK), 0)
    ci = lax.broadcasted_iota(jnp.int32, (CHUNK, CHUNK), 1)
    incl = ri >= ci
    strict = ri > ci
    eye = (ri == ci).astype(F32)
    tri = incl.astype(F32)
    row = lax.broadcasted_iota(jnp.int32, (CHUNK, 1), 0)
    is_meta = c == 0
    neutral = jnp.logical_and(is_meta, row < n_pad)
    cols = lambda base, h: slice(base + h * B_HEAD_DIM, base + (h + 1) * B_HEAD_DIM)

    nb = qkv_ref.shape[0]
    heads = range(nb * B_HEADS)
    qn, kn, vb, kb, gc, decay = [], [], [], [], [], []
    for bi in range(nb):
        xin = jnp.where(is_meta, qkvm_ref[...], qkv_ref[bi])
        misc = jnp.where(is_meta, miscm_ref[...], misc_ref[bi])
        xcat = jnp.concatenate([halo_ref[bi], xin], axis=0)
        halo_ref[bi] = xin[CHUNK - HALO:, :]
        conv = None
        for tap in range(CONV_WIDTH):
            off = HALO - (CONV_WIDTH - 1) + tap
            term = xcat[off:off + CHUNK, :] * convw_ref[tap:tap + 1, :]
            conv = term if conv is None else conv + term
        xc = conv * jax.nn.sigmoid(conv)
        beta_all = jnp.where(neutral, 0.0, jax.nn.sigmoid(misc))
        g_all = jnp.where(neutral, 0.0, -jnp.exp(alog_ref[...]) * _softplus(misc + dtb_ref[...]))
        gc_all = _hdot(tri, g_all)
        gc_all_t = gc_all.T
        for h in range(B_HEADS):
            qh, kh, vh = xc[:, cols(0, h)], xc[:, cols(B_WIDTH, h)], xc[:, cols(2 * B_WIDTH, h)]
            qn.append(qh * lax.rsqrt(jnp.sum(qh * qh, axis=-1, keepdims=True) + EPS) * (B_HEAD_DIM ** -0.5))
            kn.append(kh * lax.rsqrt(jnp.sum(kh * kh, axis=-1, keepdims=True) + EPS))
            beta = jnp.broadcast_to(beta_all[:, MISC_BETA + h:MISC_BETA + h + 1], (CHUNK, B_HEAD_DIM))
            gc.append(jnp.broadcast_to(gc_all[:, MISC_DECAY + h:MISC_DECAY + h + 1], (CHUNK, B_HEAD_DIM)))
            g_col = gc[-1][:, :CHUNK]
            g_row = gc_all_t[MISC_DECAY + h:MISC_DECAY + h + 1, :]
            decay.append(jnp.where(incl, jnp.exp(jnp.where(incl, g_col - g_row, 0.0)), 0.0))
            kb.append(kn[-1] * beta)
            vb.append(vh * beta)

    kk = [lax.dot_general(kb[h].astype(BF16), kn[h].astype(BF16), nt, preferred_element_type=F32)
          for h in heads]
    qk = [lax.dot_general(qn[h].astype(BF16), kn[h].astype(BF16), nt, preferred_element_type=F32)
          for h in heads]
    intra = [jnp.where(incl, qk[h] * decay[h], 0.0) for h in heads]

    pw = [jnp.where(strict, -(kk[h] * decay[h]), 0.0) for h in heads]
    t_inv = [eye + pw[h] for h in heads]
    pw = [_dot3(pw[h], pw[h]) for h in heads]
    for _ in range(int(math.log2(CHUNK)) - 2):
        t_next = [t_inv[h] + _dot3(t_inv[h], pw[h]) for h in heads]
        pw = [_dot3(pw[h], pw[h]) for h in heads]
        t_inv = t_next
    t_inv = [t_inv[h] + _dot3(t_inv[h], pw[h]) for h in heads]

    u = [_bdot(t_inv[h], vb[h]) for h in heads]
    w = [_bdot(t_inv[h], kb[h] * jnp.exp(gc[h])) for h in heads]

    state = [state_ref[h] for h in heads]
    w_s = [_bdot(w[h], state[h]) for h in heads]
    q_s = [_bdot(qn[h] * jnp.exp(gc[h]), state[h]) for h in heads]
    v_new = [u[h] - w_s[h] for h in heads]
    o = [q_s[h] + _bdot(intra[h], v_new[h]) for h in heads]
    for h in heads:
        g_last = gc[h][CHUNK - 1:CHUNK, :]
        kd = kn[h] * jnp.exp(g_last - gc[h])
        state_ref[h] = state[h] * jnp.exp(g_last) + lax.dot_general(
            kd.astype(BF16), v_new[h].astype(BF16), tn, preferred_element_type=F32)

    for h in heads:
        bi, hh = divmod(h, B_HEADS)
        y = o[h] * lax.rsqrt(jnp.mean(o[h] * o[h], axis=-1, keepdims=True) + EPS) * nw_ref[...]
        zh = z_ref[bi, :, cols(0, hh)]
        o_ref[bi, :, cols(0, hh)] = (y * (zh * jax.nn.sigmoid(zh))).astype(BF16)


DELTA_BATCH = 4


def _gated_deltanet(qkv, z, misc, qkv_m, misc_m, conv_w, a_log, dt_bias, norm_w):
    b, s, _ = qkv.shape
    n_chunks = s // CHUNK + 1
    nb = DELTA_BATCH if b % DELTA_BATCH == 0 else 1
    blk = lambda bi, c: (bi, jnp.maximum(c - 1, 0), 0)
    fixed = lambda bi, c: (0, 0)
    lane_vec = lambda v: jnp.zeros((1, LANES), F32).at[0, MISC_DECAY:MISC_DECAY + B_HEADS].set(v)
    return pl.pallas_call(
        _delta_kernel,
        grid=(b // nb, n_chunks),
        in_specs=[
            pl.BlockSpec((nb, CHUNK, 3 * B_WIDTH), blk),
            pl.BlockSpec((nb, CHUNK, B_WIDTH), blk),
            pl.BlockSpec((nb, CHUNK, LANES), blk),
            pl.BlockSpec((CHUNK, 3 * B_WIDTH), fixed),
            pl.BlockSpec((CHUNK, LANES), fixed),
            pl.BlockSpec((CONV_WIDTH, 3 * B_WIDTH), fixed),
            pl.BlockSpec((1, LANES), fixed),
            pl.BlockSpec((1, LANES), fixed),
            pl.BlockSpec((1, B_HEAD_DIM), fixed),
        ],
        out_specs=pl.BlockSpec((nb, CHUNK, B_WIDTH), blk),
        out_shape=jax.ShapeDtypeStruct((b, s, B_WIDTH), BF16),
        scratch_shapes=[
            pltpu.VMEM((nb * B_HEADS, B_HEAD_DIM, B_HEAD_DIM), F32),
            pltpu.VMEM((nb, HALO, 3 * B_WIDTH), F32),
        ],
        compiler_params=pltpu.CompilerParams(dimension_semantics=("parallel", "arbitrary")),
        name="gated_deltanet",
    )(qkv, z, misc, qkv_m, misc_m, conv_w, lane_vec(a_log), lane_vec(dt_bias),
      norm_w.reshape(1, B_HEAD_DIM))


def _pack_rows(a):
    half = a.shape[1] // 2
    bits = lax.bitcast_convert_type(a.astype(BF16).astype(F32), jnp.int32)
    return bits[:, :half] | lax.shift_right_logical(bits[:, half:], 16)


def _unpack_rows(w):
    hi = lax.bitcast_convert_type(w & jnp.int32(-65536), F32)
    lo = lax.bitcast_convert_type(lax.shift_left(w, 16), F32)
    return jnp.concatenate([hi, lo], axis=1)


def _out_router_kernel(oa_ref, ob_ref, x_ref, wo_ref, nw_ref, wr_ref, br_ref,
                       h_ref, xn_ref, gate_ref, eid_ref):
    mix = (jnp.dot(oa_ref[...], wo_ref[:A_WIDTH, :], preferred_element_type=F32)
           + jnp.dot(ob_ref[...], wo_ref[A_WIDTH:, :], preferred_element_type=F32))
    h = x_ref[...] + mix
    h_ref[...] = h
    xn = h * lax.rsqrt(jnp.mean(h * h, axis=-1, keepdims=True) + EPS) * nw_ref[...]
    xn_ref[...] = _pack_rows(xn)
    tm = xn.shape[0]
    logits = lax.dot_general(wr_ref[...], xn, (((1,), (1,)), ((), ())), precision=HIGHEST,
                             preferred_element_type=F32) + br_ref[...]
    e_idx = lax.broadcasted_iota(jnp.int32, (N_EXPERTS, tm), 0).astype(F32)
    work = logits
    vals, idxs = [], []
    for _ in range(TOP_K):
        m = jnp.max(work, axis=0, keepdims=True)
        idx = jnp.min(jnp.where(work == m, e_idx, float(N_EXPERTS)), axis=0, keepdims=True)
        vals.append(m)
        idxs.append(idx)
        work = jnp.where(e_idx == idx, NEG_INF, work)
    exps = [jnp.exp(v - vals[0]) for v in vals]
    denom = exps[0]
    for e in exps[1:]:
        denom = denom + e
    eid_ref[...] = jnp.concatenate(idxs + [jnp.zeros((SUBLANES - TOP_K, tm), F32)], axis=0).astype(jnp.int32)
    gates_t = jnp.concatenate([e / denom for e in exps] + [jnp.zeros((LANES - TOP_K, tm), F32)], axis=0)
    gate_ref[...] = gates_t.T


def _out_router(o_a, o_b, x2d, w_out, norm_w, w_router, b_router, tm):
    n = x2d.shape[0]
    row = lambda i: (i, 0)
    fixed = lambda i: (0, 0)
    wr = w_router.T
    br = b_router.reshape(N_EXPERTS, 1)
    return pl.pallas_call(
        _out_router_kernel,
        grid=(n // tm,),
        in_specs=[
            pl.BlockSpec((tm, A_WIDTH), row),
            pl.BlockSpec((tm, B_WIDTH), row),
            pl.BlockSpec((tm, D_MODEL), row),
            pl.BlockSpec((A_WIDTH + B_WIDTH, D_MODEL), fixed),
            pl.BlockSpec((1, D_MODEL), fixed),
            pl.BlockSpec((N_EXPERTS, D_MODEL), fixed),
            pl.BlockSpec((N_EXPERTS, 1), fixed),
        ],
        out_specs=[pl.BlockSpec((tm, D_MODEL), row), pl.BlockSpec((tm, D_MODEL // 2), row),
                   pl.BlockSpec((tm, LANES), row), pl.BlockSpec((SUBLANES, tm), lambda i: (0, i))],
        out_shape=[jax.ShapeDtypeStruct((n, D_MODEL), F32), jax.ShapeDtypeStruct((n, D_MODEL // 2), jnp.int32),
                   jax.ShapeDtypeStruct((n, LANES), F32), jax.ShapeDtypeStruct((SUBLANES, n), jnp.int32)],
        compiler_params=pltpu.CompilerParams(dimension_semantics=("parallel",)),
        name="out_proj_router",
    )(o_a, o_b, x2d, w_out.astype(BF16), norm_w.reshape(1, D_MODEL), wr, br)


MOE_BM = 256
MOE_BF = 512


SC_CORES = 2
SC_SUBCORES = 16
SC_ROWS = 128


def _sc_gather_rows(table, idx):
    n_workers = SC_CORES * SC_SUBCORES
    n_rows = idx.shape[0]
    d = table.shape[1]
    assert n_rows % (n_workers * SC_ROWS) == 0
    rows_per_worker = n_rows // n_workers
    mesh = plsc.VectorSubcoreMesh(core_axis_name="c", subcore_axis_name="s",
                                  num_cores=SC_CORES, num_subcores=SC_SUBCORES)

    @functools.partial(
        pl.kernel, mesh=mesh,
        out_type=jax.ShapeDtypeStruct((n_rows, d), table.dtype),
        scratch_types=[pltpu.VMEM((SC_ROWS,), jnp.int32), pltpu.VMEM((SC_ROWS, d), table.dtype),
                       pltpu.SemaphoreType.DMA],
        name="sc_gather_rows",
    )
    def gather(table_hbm, idx_hbm, out_hbm, idx_v, rows_v, sem):
        wid = lax.axis_index("s") * SC_CORES + lax.axis_index("c")
        base = wid * rows_per_worker

        @pl.loop(0, rows_per_worker // SC_ROWS)
        def _(i):
            off = base + i * SC_ROWS
            pltpu.sync_copy(idx_hbm.at[pl.ds(off, SC_ROWS)], idx_v)
            pltpu.async_copy(table_hbm.at[idx_v], rows_v, sem).wait()
            pltpu.sync_copy(rows_v, out_hbm.at[pl.ds(off, SC_ROWS)])

    return gather(table, idx)


def _route_blocks(eid_t, bm):
    n_tok = eid_t.shape[1]
    n_assign = n_tok * TOP_K
    experts = jnp.arange(N_EXPERTS, dtype=jnp.int32)
    flat_e = eid_t.reshape(-1)
    sorted_e, order = lax.sort((flat_e, jnp.arange(n_assign, dtype=jnp.int32)), num_keys=1)
    onehot = sorted_e[:, None] == experts[None, :]
    counts = jnp.sum(onehot, axis=0, dtype=jnp.int32)
    padded = (counts + bm - 1) // bm * bm
    start = jnp.cumsum(counts) - counts
    pend = jnp.cumsum(padded)
    pstart = pend - padded
    dest = jnp.arange(n_assign, dtype=jnp.int32) + jnp.sum(
        jnp.where(onehot, (pstart - start)[None, :], 0), axis=1)
    n_blocks = -(-(n_assign + N_EXPERTS * (bm - 1)) // bm)
    blk_start = jnp.arange(n_blocks, dtype=jnp.int32) * bm
    block_e = jnp.minimum(jnp.sum(blk_start[:, None] >= pend[None, :], axis=1), N_EXPERTS - 1)
    n_valid = pend[-1] // bm
    _, pos = lax.sort((order, dest), num_keys=1)
    t = jnp.arange(bm, dtype=jnp.int32)[None, :]
    pad_key = jnp.where(t < (padded - counts)[:, None], (pstart + counts)[:, None] + t,
                        n_blocks * bm).reshape(-1)
    pad_tok = jnp.arange(N_EXPERTS * bm, dtype=jnp.int32) % n_tok
    assert n_blocks * bm - n_assign == N_EXPERTS * bm
    _, row_tok = lax.sort((jnp.concatenate([dest, pad_key]),
                           jnp.concatenate([order % n_tok, pad_tok])), num_keys=1)
    return block_e.astype(jnp.int32), n_valid.astype(jnp.int32).reshape(1), row_tok, pos


def _moe_dense_kernel(be_ref, nv_ref, x_ref, wgu_ref, wd_ref, bg_ref, bu_ref, bd_ref, perm_ref, y_ref,
                      wg_s, wu_s, wd_s):
    i = pl.program_id(0)
    live = i < nv_ref[0]

    @pl.when(jnp.logical_and(live, jnp.logical_or(i == 0, be_ref[i] != be_ref[jnp.maximum(i - 1, 0)])))
    def _():
        perm = perm_ref[...]
        grp = 2 * LANES
        for gidx in range(2 * D_FF // grp):
            blk = wgu_ref[0, :, gidx * grp:(gidx + 1) * grp].astype(BF16)
            split = jnp.dot(blk, perm, preferred_element_type=F32).astype(BF16)
            wg_s[:, gidx * LANES:(gidx + 1) * LANES] = split[:, :LANES]
            wu_s[:, gidx * LANES:(gidx + 1) * LANES] = split[:, LANES:]
        wd_s[...] = wd_ref[0].astype(BF16)

    @pl.when(live)
    def _():
        x = _unpack_rows(x_ref[...]).astype(BF16)
        y = None
        for f0 in range(0, D_FF, MOE_BF):
            g = jnp.dot(x, wg_s[:, f0:f0 + MOE_BF], preferred_element_type=F32) + bg_ref[0, :, f0:f0 + MOE_BF]
            u = jnp.dot(x, wu_s[:, f0:f0 + MOE_BF], preferred_element_type=F32) + bu_ref[0, :, f0:f0 + MOE_BF]
            gate = jnp.minimum(g, SWIGLU_LIMIT)
            up = jnp.clip(u, -SWIGLU_LIMIT, SWIGLU_LIMIT)
            t = gate * jax.nn.sigmoid(gate * SWIGLU_ALPHA) * (up + 1.0)
            part = jnp.dot(t.astype(BF16), wd_s[f0:f0 + MOE_BF, :], preferred_element_type=F32)
            y = part if y is None else y + part
        y_ref[...] = _pack_rows(y + bd_ref[0])


def _moe_dense(x_sorted, block_e, n_valid, w_gate_up, w_down, b_gate, b_up, b_down, bm):
    n_blocks = block_e.shape[0]
    src = jnp.arange(2 * LANES)
    perm = jax.nn.one_hot((src % 2) * LANES + src // 2, 2 * LANES, dtype=BF16)
    rows = lambda i, be, nv: (jnp.minimum(i, nv[0] - 1), 0)
    wsel = lambda i, be, nv: (be[i], 0, 0)
    fixed = lambda i, be, nv: (0, 0)
    grid_spec = pltpu.PrefetchScalarGridSpec(
        num_scalar_prefetch=2,
        grid=(n_blocks,),
        in_specs=[
            pl.BlockSpec((bm, D_MODEL // 2), rows),
            pl.BlockSpec((1, D_MODEL, 2 * D_FF), wsel),
            pl.BlockSpec((1, D_FF, D_MODEL), wsel),
            pl.BlockSpec((1, 1, D_FF), wsel),
            pl.BlockSpec((1, 1, D_FF), wsel),
            pl.BlockSpec((1, 1, D_MODEL), wsel),
            pl.BlockSpec((2 * LANES, 2 * LANES), fixed),
        ],
        out_specs=pl.BlockSpec((bm, D_MODEL // 2), rows),
        scratch_shapes=[
            pltpu.VMEM((D_MODEL, D_FF), BF16),
            pltpu.VMEM((D_MODEL, D_FF), BF16),
            pltpu.VMEM((D_FF, D_MODEL), BF16),
        ],
    )
    return pl.pallas_call(
        _moe_dense_kernel,
        grid_spec=grid_spec,
        out_shape=jax.ShapeDtypeStruct(x_sorted.shape, jnp.int32),
        compiler_params=pltpu.CompilerParams(dimension_semantics=("arbitrary",)),
        name="moe_experts",
    )(block_e, n_valid, x_sorted, w_gate_up, w_down, b_gate, b_up, b_down, perm)


def _combine_kernel(h_ref, gate_ref, y0_ref, y1_ref, y2_ref, y3_ref, o_ref):
    gates = gate_ref[...]
    out = h_ref[...]
    for kk, y_ref in enumerate((y0_ref, y1_ref, y2_ref, y3_ref)):
        out = out + gates[:, kk:kk + 1] * _unpack_rows(y_ref[...])
    o_ref[...] = out


def _combine(h, gates, y, tm):
    n_tok = h.shape[0]
    nt = n_tok // tm
    row = lambda i: (i, 0)
    return pl.pallas_call(
        _combine_kernel,
        grid=(nt,),
        in_specs=[pl.BlockSpec((tm, D_MODEL), row), pl.BlockSpec((tm, LANES), row)]
        + [pl.BlockSpec((tm, D_MODEL // 2), functools.partial(lambda kk, i: (kk * nt + i, 0), kk))
           for kk in range(TOP_K)],
        out_specs=pl.BlockSpec((tm, D_MODEL), row),
        out_shape=jax.ShapeDtypeStruct((n_tok, D_MODEL), F32),
        compiler_params=pltpu.CompilerParams(dimension_semantics=("parallel",)),
        name="moe_combine",
    )(h, gates, y, y, y, y)


def kernel(x, meta_tokens, norm_mix_w, w_in, q_norm_w, k_norm_w, kv_norm_w, w_kv_up, conv_w, a_log,
           dt_bias, delta_norm_w, w_out, norm_ffn_w, w_router, b_router, w_gate_up, b_gate_up,
           w_down, b_down):
    b, s, d = x.shape
    consts = _pack_in_proj_weights(norm_mix_w[0], w_in[0], q_norm_w[0], k_norm_w[0], kv_norm_w[0],
                                   w_kv_up[0])
    cos, sin = _rope_tables(N_META + s)
    real = _in_proj(x.reshape(b * s, d), cos[N_META:], sin[N_META:], consts, VT_TILE)
    meta = _in_proj(meta_tokens, cos[:N_META], sin[:N_META], consts, N_META)
    vt = real[2]
    q, k, iq, ik, misc, qkv, z = (a.reshape(b, s, a.shape[-1]) for a in real[:2] + real[3:])
    km = jnp.pad(meta[1], ((0, LANES - N_META), (0, 0)))
    vmt = jnp.pad(meta[2][0], ((0, 0), (0, LANES - N_META)))
    o_a = _sparse_attention(q, k, vt, km, vmt, iq, ik, misc,
                            tq=LANES, kc=min(512, s), topk=min(INDEX_TOPK, s // 4))
    lead = lambda a: jnp.pad(a, ((CHUNK - N_META, 0), (0, 0)))
    o_b = _gated_deltanet(qkv, z, misc, lead(meta[6]), lead(meta[5]), conv_w[0], a_log[0],
                          dt_bias[0], delta_norm_w[0])
    n_tok = b * s
    h, xn, gates, eid = _out_router(o_a.reshape(n_tok, A_WIDTH), o_b.reshape(n_tok, B_WIDTH),
                                    x.reshape(n_tok, d), w_out[0], norm_ffn_w[0], w_router[0],
                                    b_router[0], 256)
    block_e, n_valid, row_tok, pos = _route_blocks(eid[:TOP_K], MOE_BM)
    bgu = b_gate_up[0].reshape(N_EXPERTS, 1, 2 * D_FF)
    x_sorted = _sc_gather_rows(xn, row_tok)
    y_sorted = _moe_dense(x_sorted, block_e, n_valid, w_gate_up[0], w_down[0], bgu[:, :, 0::2],
                          bgu[:, :, 1::2], b_down[0].reshape(N_EXPERTS, 1, D_MODEL), MOE_BM)
    y = _sc_gather_rows(y_sorted, pos)
    out = _combine(h, gates, y, 256)
    return out.reshape(b, s, d)
```

```python
import functools
import math

import jax
import jax.numpy as jnp
from jax import lax
from jax.experimental import pallas as pl
from jax.experimental.pallas import tpu as pltpu
from jax.experimental.pallas import tpu_sc as plsc

F32 = jnp.float32
BF16 = jnp.bfloat16
HIGHEST = lax.Precision.HIGHEST

D_MODEL = 1024
N_META = 16
ROPE_THETA = 10000.0
EPS = 1e-6
A_HEAD_DIM = 64
A_HEADS = 8
A_WIDTH = A_HEADS * A_HEAD_DIM
KV_RANK = 256
IDX_HEADS = 8
IDX_DIM = 64
INDEX_TOPK = 256
B_HEAD_DIM = 128
B_HEADS = 4
B_WIDTH = B_HEADS * B_HEAD_DIM
CONV_WIDTH = 4
CHUNK = 64
N_EXPERTS = 32
TOP_K = 4
D_FF = D_MODEL
SWIGLU_LIMIT = 7.0
SWIGLU_ALPHA = 1.702
IN_SPLITS = (A_WIDTH, KV_RANK, IDX_HEADS * IDX_DIM, IDX_DIM, IDX_HEADS, 3 * B_WIDTH, B_WIDTH,
             B_HEADS, B_HEADS)

LANES = 128

C_Q = 0
C_CKV = C_Q + A_WIDTH
C_IQ = C_CKV + KV_RANK
C_IK = C_IQ + IDX_HEADS * IDX_DIM
C_MISC = C_IK + LANES
C_QKV = C_MISC + LANES
C_Z = C_QKV + 3 * B_WIDTH
C_END = C_Z + B_WIDTH
MISC_W, MISC_BETA, MISC_DECAY = 0, IDX_HEADS, IDX_HEADS + B_HEADS
LOG2E = math.log2(math.e)
VT_ROWS = A_HEAD_DIM + 16


def _rope_partner(a):
    lane = lax.broadcasted_iota(jnp.int32, a.shape, 1)
    first_half = (lane % A_HEAD_DIM) < (A_HEAD_DIM // 2)
    return jnp.where(first_half, pltpu.roll(a, LANES - A_HEAD_DIM // 2, 1),
                     pltpu.roll(a, A_HEAD_DIM // 2, 1))


def _rope(a, cos, sin_signed):
    return a * cos + _rope_partner(a) * sin_signed


def _head_rms(a, head_mean, gain):
    msq = jnp.dot((a * a).astype(BF16), head_mean, preferred_element_type=F32)
    return a * lax.rsqrt(msq + EPS) * gain


def _in_proj_kernel(x_ref, nw_ref, w_ref, wkv_ref, qnw_ref, knw_ref, kvnw_ref, cos_ref, sin_ref,
                    hm_ref, q_ref, k_ref, vt_ref, iq_ref, ik_ref, misc_ref, qkv_ref, z_ref):
    x = x_ref[...]
    u = x * lax.rsqrt(jnp.mean(x * x, axis=-1, keepdims=True) + EPS) * nw_ref[...]
    ub = u.astype(BF16)

    def proj(c0, c1):
        return jnp.dot(ub, w_ref[:, c0:c1], preferred_element_type=F32)

    cos = cos_ref[...]
    sin = sin_ref[...]
    hm = hm_ref[...]

    def rope_groups(a):
        return jnp.concatenate(
            [_rope(a[:, g * LANES:(g + 1) * LANES], cos, sin) for g in range(a.shape[1] // LANES)],
            axis=1)

    q = _head_rms(proj(C_Q, C_CKV), hm, qnw_ref[...])
    q_ref[...] = (rope_groups(q) * (A_HEAD_DIM ** -0.5 * LOG2E)).astype(BF16)

    ckv = proj(C_CKV, C_IQ)
    ckv = ckv * lax.rsqrt(jnp.mean(ckv * ckv, axis=-1, keepdims=True) + EPS) * kvnw_ref[...]
    kv = jnp.dot(ckv.astype(BF16), wkv_ref[...], preferred_element_type=F32)
    k = _head_rms(kv[:, :A_WIDTH], hm, knw_ref[...])
    k_ref[...] = rope_groups(k).astype(BF16)
    tm = x.shape[0]
    ones = jnp.ones((VT_ROWS - A_HEAD_DIM, tm), BF16)
    for g in range(A_WIDTH // LANES):
        vt = kv[:, A_WIDTH + g * LANES:A_WIDTH + (g + 1) * LANES].T.astype(BF16)
        for half in range(2):
            r0 = (2 * g + half) * VT_ROWS
            vt_ref[0, r0:r0 + A_HEAD_DIM, :] = vt[half * A_HEAD_DIM:(half + 1) * A_HEAD_DIM, :]
            vt_ref[0, r0 + A_HEAD_DIM:r0 + VT_ROWS, :] = ones

    iq_ref[...] = rope_groups(proj(C_IQ, C_IK)).astype(BF16)
    ik_ref[...] = _rope(proj(C_IK, C_MISC), cos, sin).astype(BF16)

    lane = lax.broadcasted_iota(jnp.int32, (1, LANES), 1)
    w_scale = jnp.where(lane < IDX_HEADS, IDX_HEADS ** -0.5 * IDX_DIM ** -0.5, 1.0)
    misc_ref[...] = proj(C_MISC, C_QKV) * w_scale
    qkv_ref[...] = proj(C_QKV, C_Z)
    z_ref[...] = proj(C_Z, C_END)


def _in_proj(x2d, cos, sin, consts, tm):
    n = x2d.shape[0]
    n_pos_blocks = cos.shape[0] // tm
    nw, w_pack, wkv, qnw, knw, kvnw, hm = consts
    row = lambda i: (i, 0)
    fixed = lambda i: (0, 0)
    pos = lambda i: (i % n_pos_blocks, 0)
    out_widths = (A_WIDTH, A_WIDTH, None, IDX_HEADS * IDX_DIM, LANES, LANES, 3 * B_WIDTH, B_WIDTH)
    out_dtypes = (BF16, BF16, BF16, BF16, BF16, F32, F32, F32)
    vt_rows = A_HEADS * VT_ROWS
    out_specs = [pl.BlockSpec((1, vt_rows, tm), lambda i: (i, 0, 0)) if w is None
                 else pl.BlockSpec((tm, w), row) for w in out_widths]
    out_shape = [jax.ShapeDtypeStruct((n // tm, vt_rows, tm) if w is None else (n, w), dt)
                 for w, dt in zip(out_widths, out_dtypes)]
    return pl.pallas_call(
        _in_proj_kernel,
        grid=(n // tm,),
        in_specs=[
            pl.BlockSpec((tm, D_MODEL), row),
            pl.BlockSpec(nw.shape, fixed),
            pl.BlockSpec(w_pack.shape, fixed),
            pl.BlockSpec(wkv.shape, fixed),
            pl.BlockSpec(qnw.shape, fixed),
            pl.BlockSpec(knw.shape, fixed),
            pl.BlockSpec(kvnw.shape, fixed),
            pl.BlockSpec((tm, LANES), pos),
            pl.BlockSpec((tm, LANES), pos),
            pl.BlockSpec(hm.shape, fixed),
        ],
        out_specs=out_specs,
        out_shape=out_shape,
        compiler_params=pltpu.CompilerParams(dimension_semantics=("parallel",)),
        name="in_proj",
    )(x2d, nw, w_pack, wkv, qnw, knw, kvnw, cos, sin, hm)


def _pack_in_proj_weights(norm_mix_w, w_in, q_norm_w, k_norm_w, kv_norm_w, w_kv_up):
    points = []
    acc = 0
    for s in IN_SPLITS:
        points.append((acc, acc + s))
        acc += s
    a_q, a_ckv, i_q, i_k, i_w, b_qkv, b_z, b_beta, b_a = (w_in[:, a:b] for a, b in points)
    misc = jnp.concatenate(
        [i_w, b_beta, b_a, jnp.zeros((D_MODEL, LANES - IDX_HEADS - 2 * B_HEADS), w_in.dtype)], axis=1)
    w_pack = jnp.concatenate([a_q, a_ckv, i_q, i_k, i_k, misc, b_qkv, b_z], axis=1).astype(BF16)
    head_mean = jnp.kron(jnp.eye(A_HEADS, dtype=F32),
                         jnp.full((A_HEAD_DIM, A_HEAD_DIM), 1.0 / A_HEAD_DIM, F32)).astype(BF16)
    return (norm_mix_w.reshape(1, D_MODEL), w_pack, w_kv_up.astype(BF16),
            jnp.tile(q_norm_w, A_HEADS).reshape(1, A_WIDTH),
            jnp.tile(k_norm_w, A_HEADS).reshape(1, A_WIDTH),
            kv_norm_w.reshape(1, KV_RANK), head_mean)


def _rope_tables(n_pos):
    half = A_HEAD_DIM // 2
    inv_freq = ROPE_THETA ** (-jnp.arange(0, A_HEAD_DIM, 2, dtype=F32) / A_HEAD_DIM)
    ang = jnp.arange(n_pos, dtype=F32)[:, None] * inv_freq[None, :]
    cos, sin = jnp.cos(ang), jnp.sin(ang)
    cos128 = jnp.tile(cos, (1, LANES // half))
    sin128 = jnp.tile(jnp.concatenate([-sin, sin], axis=1), (1, LANES // A_HEAD_DIM))
    return cos128, sin128


NEG_INF = float("-inf")
F32_MAX = float(jnp.finfo(jnp.float32).max)
INT_MIN = -2 ** 31


VT_TILE = 256
SUBLANES = 8


def _attn_kernel(q_ref, k_ref, vt_ref, km_ref, vmt_ref, iq_ref, ik_ref, misc_ref, o_ref,
                 isc_ref, lhs_ref, qm_ref, s_ref, m_ref, acc_ref, *, tq, kc, topk, pos_bits):
    j = pl.program_id(1)
    n_kc = lax.div((j + 1) * tq + (kc - 1), kc)
    g = kc // SUBLANES
    lane = lax.broadcasted_iota(jnp.int32, (1, LANES), 1)
    lo_half = lane < A_HEAD_DIM
    n_pairs = A_WIDTH // LANES
    nt = (((1,), (1,)), ((), ()))

    iq = iq_ref[0]
    q = q_ref[0]
    zero = jnp.zeros((), BF16)
    for p in range(n_pairs):
        blk = iq[:, p * LANES:(p + 1) * LANES]
        lhs_ref[(2 * p) * tq:(2 * p + 1) * tq, :] = jnp.where(lo_half, blk, zero)
        lhs_ref[(2 * p + 1) * tq:(2 * p + 2) * tq, :] = jnp.where(lo_half, zero, blk)
        qb = q[:, p * LANES:(p + 1) * LANES]
        qm_ref[p, :tq, :] = jnp.where(lo_half, qb, zero)
        qm_ref[p, tq:, :] = jnp.where(lo_half, zero, qb)
    w_t = misc_ref[0].T

    def fold0(x3, op):
        n = x3.shape[0]
        while n > 1:
            x3 = op(x3[:n // 2], x3[n // 2:n])
            n //= 2
        return x3[0]

    q_pos = j * tq + lax.broadcasted_iota(jnp.int32, (1, tq), 1)

    def idx_body(c, carry):
        k0 = pl.multiple_of(c * kc, kc)
        r = lax.dot_general(ik_ref[0, pl.ds(k0, kc), :], lhs_ref[...], nt,
                            preferred_element_type=F32)
        s = None
        for h in range(IDX_HEADS):
            term = jnp.maximum(r[:, h * tq:(h + 1) * tq], 0.0) * w_t[MISC_W + h:MISC_W + h + 1, :]
            s = term if s is None else s + term
        k_pos = k0 + lax.broadcasted_iota(jnp.int32, (kc, 1), 0)
        isc_ref[c] = jnp.where(k_pos <= q_pos, s, NEG_INF)
        return carry
    lax.fori_loop(0, n_kc, idx_body, 0)

    def count(pred):
        def body(c, acc):
            x3 = isc_ref[c].reshape(g, SUBLANES, tq)
            return acc + fold0(jnp.where(pred(x3, c), 1.0, 0.0), jnp.add)
        acc = lax.fori_loop(0, n_kc, body, jnp.zeros((SUBLANES, tq), F32))
        return jnp.broadcast_to(jnp.sum(acc, axis=0, keepdims=True), (SUBLANES, tq))

    def key_to_float(u):
        key = u ^ jnp.int32(INT_MIN)
        bits = jnp.where(key >= 0, key, key ^ jnp.int32(0x7FFFFFFF))
        return lax.bitcast_convert_type(bits, F32)

    def bit_body(i, carry):
        u, n_u = carry
        u2 = u | lax.shift_left(jnp.int32(1), 31 - i)
        cand = key_to_float(u2)
        n_ge = count(lambda x3, c: x3 >= cand[None])
        keep = n_ge >= topk
        return jnp.where(keep, u2, u), jnp.where(keep, n_ge, n_u)
    u, n_u = lax.fori_loop(0, 32, bit_body, (jnp.zeros((SUBLANES, tq), jnp.int32),
                                             jnp.zeros((SUBLANES, tq), F32)))
    few = (u >= 0) & (u < 0x00800000)
    tau = jnp.where(few, -F32_MAX, key_to_float(u))

    n_ge = jnp.where(few, 0.0, n_u)

    @pl.when(jnp.max(n_ge) > topk)
    def _():
        n_gt = count(lambda x3, c: x3 > tau[None])
        need = topk - n_gt
        def chunk_pos(c):
            return (c * kc + lax.broadcasted_iota(jnp.int32, (g, SUBLANES, tq), 0) * SUBLANES
                    + lax.broadcasted_iota(jnp.int32, (g, SUBLANES, tq), 1))
        def pos_body(i, cut):
            cut2 = cut | lax.shift_left(jnp.int32(1), pos_bits - 1 - i)
            ties_before = count(lambda x3, c: (x3 == tau[None]) & (chunk_pos(c) < cut2[None]))
            return jnp.where(ties_before < need, cut2, cut)
        cut = lax.fori_loop(0, pos_bits, pos_body, jnp.zeros((SUBLANES, tq), jnp.int32))
        def drop_body(c, carry):
            x3 = isc_ref[c].reshape(g, SUBLANES, tq)
            drop = (x3 == tau[None]) & (chunk_pos(c) > cut[None])
            isc_ref[c] = jnp.where(drop, NEG_INF, x3).reshape(kc, tq)
            return carry
        lax.fori_loop(0, n_kc, drop_body, 0)

    tau_row = tau[0:1, :]

    def attend(key_pairs, bias, vt_slabs, first):
        n = key_pairs[0].shape[0]
        m_cur = []
        for p in range(n_pairs):
            s2 = lax.dot_general(key_pairs[p], qm_ref[p], nt, preferred_element_type=F32)
            for half in range(2):
                s = s2[:, half * tq:(half + 1) * tq] + bias
                s_ref[2 * p + half, :n, :] = s
                m8 = fold0(s.reshape(n // SUBLANES, SUBLANES, tq), jnp.maximum)
                m_cur.append(jnp.max(m8, axis=0, keepdims=True))
        for hd in range(A_HEADS):
            m_new = m_cur[hd] if first else jnp.maximum(m_ref[hd][0:1, :], m_cur[hd])
            e = jnp.exp2(s_ref[hd, :n, :] - m_new).astype(BF16)
            pv = None
            off = 0
            for slab in vt_slabs(hd):
                part = jnp.dot(slab, e[off:off + slab.shape[1], :], preferred_element_type=F32)
                pv = part if pv is None else pv + part
                off += slab.shape[1]
            if first:
                acc_ref[hd] = pv
            else:
                acc_ref[hd] = jnp.exp2(m_ref[hd][0:1, :] - m_new) * acc_ref[hd] + pv
            m_ref[hd] = jnp.broadcast_to(m_new, (SUBLANES, tq))

    meta_bias = jnp.where(lax.broadcasted_iota(jnp.int32, (LANES, 1), 0) < N_META, 0.0, NEG_INF)
    attend([km_ref[:, p * LANES:(p + 1) * LANES] for p in range(n_pairs)], meta_bias,
           lambda hd: [vmt_ref[hd * VT_ROWS:(hd + 1) * VT_ROWS, :]], True)

    def att_body(c, carry):
        k0 = pl.multiple_of(c * kc, kc)
        bias = jnp.where(isc_ref[c] >= tau_row, 0.0, NEG_INF)
        attend([k_ref[0, pl.ds(k0, kc), p * LANES:(p + 1) * LANES] for p in range(n_pairs)], bias,
               lambda hd: [vt_ref[c * (kc // VT_TILE) + t, hd * VT_ROWS:(hd + 1) * VT_ROWS, :]
                           for t in range(kc // VT_TILE)], False)
        return carry
    lax.fori_loop(0, n_kc, att_body, 0)

    for p in range(n_pairs):
        halves = []
        for hd in (2 * p, 2 * p + 1):
            acc = acc_ref[hd]
            halves.append(acc[:A_HEAD_DIM, :] / acc[A_HEAD_DIM:A_HEAD_DIM + 1, :])
        o_ref[0, :, p * LANES:(p + 1) * LANES] = jnp.concatenate(halves, axis=0).T.astype(BF16)


def _sparse_attention(q, k, vt, km, vmt, iq, ik, misc, *, tq, kc, topk):
    b, s, _ = q.shape
    assert tq == LANES and kc % VT_TILE == 0 and s % kc == 0
    kernel = functools.partial(_attn_kernel, tq=tq, kc=kc, topk=topk, pos_bits=int(math.log2(s)))
    qblk = lambda bi, j: (bi, j, 0)
    full = lambda bi, j: (bi, 0, 0)
    fixed = lambda bi, j: (0, 0)
    vt_rows = A_HEADS * VT_ROWS
    return pl.pallas_call(
        kernel,
        grid=(b, s // tq),
        in_specs=[
            pl.BlockSpec((1, tq, A_WIDTH), qblk),
            pl.BlockSpec((1, s, A_WIDTH), full),
            pl.BlockSpec((s // VT_TILE, vt_rows, VT_TILE), full),
            pl.BlockSpec(km.shape, fixed),
            pl.BlockSpec(vmt.shape, fixed),
            pl.BlockSpec((1, tq, IDX_HEADS * IDX_DIM), qblk),
            pl.BlockSpec((1, s, LANES), full),
            pl.BlockSpec((1, tq, LANES), qblk),
        ],
        out_specs=pl.BlockSpec((1, tq, A_WIDTH), qblk),
        out_shape=jax.ShapeDtypeStruct((b, s, A_WIDTH), BF16),
        scratch_shapes=[
            pltpu.VMEM((s // kc, kc, tq), F32),
            pltpu.VMEM((IDX_HEADS * tq, LANES), BF16),
            pltpu.VMEM((A_WIDTH // LANES, 2 * tq, LANES), BF16),
            pltpu.VMEM((A_HEADS, max(kc, LANES), tq), F32),
            pltpu.VMEM((A_HEADS, SUBLANES, tq), F32),
            pltpu.VMEM((A_HEADS, VT_ROWS, tq), F32),
        ],
        compiler_params=pltpu.CompilerParams(dimension_semantics=("parallel", "arbitrary")),
        name="sparse_attention",
    )(q, k, vt, km, vmt, iq, ik, misc)


HALO = 8


def _softplus(x):
    return jnp.maximum(x, 0.0) + jnp.log1p(jnp.exp(-jnp.abs(x)))


def _bdot(a, b):
    return jnp.dot(a.astype(BF16), b.astype(BF16), preferred_element_type=F32)


def _hdot(a, b):
    return jnp.dot(a, b, precision=HIGHEST, preferred_element_type=F32)


def _dot3(a, b):
    a_hi = a.astype(BF16)
    b_hi = b.astype(BF16)
    a_lo = (a - a_hi.astype(F32)).astype(BF16)
    b_lo = (b - b_hi.astype(F32)).astype(BF16)
    dot = functools.partial(jnp.dot, preferred_element_type=F32)
    return dot(a_hi, b_hi) + (dot(a_hi, b_lo) + dot(a_lo, b_hi))


def _delta_kernel(qkv_ref, z_ref, misc_ref, qkvm_ref, miscm_ref, convw_ref, alog_ref, dtb_ref,
                  nw_ref, o_ref, state_ref, halo_ref):
    c = pl.program_id(1)
    n_pad = CHUNK - N_META
    nt = (((1,), (1,)), ((), ()))
    tn = (((0,), (0,)), ((), ()))

    @pl.when(c == 0)
    def _():
        state_ref[...] = jnp.zeros_like(state_ref)
        halo_ref[...] = jnp.zeros_like(halo_ref)

    ri = lax.broadcasted_iota(jnp.int32, (CHUNK, CHUNK), 0)
    ci = lax.broadcasted_iota(jnp.int32, (CHUNK, CHUNK), 1)
    incl = ri >= ci
    strict = ri > ci
    eye = (ri == ci).astype(F32)
    tri = incl.astype(F32)
    row = lax.broadcasted_iota(jnp.int32, (CHUNK, 1), 0)
    is_meta = c == 0
    neutral = jnp.logical_and(is_meta, row < n_pad)
    cols = lambda base, h: slice(base + h * B_HEAD_DIM, base + (h + 1) * B_HEAD_DIM)

    nb = qkv_ref.shape[0]
    heads = range(nb * B_HEADS)
    qn, kn, vb, kb, gc, decay = [], [], [], [], [], []
    for bi in range(nb):
        xin = jnp.where(is_meta, qkvm_ref[...], qkv_ref[bi])
        misc = jnp.where(is_meta, miscm_ref[...], misc_ref[bi])
        xcat = jnp.concatenate([halo_ref[bi], xin], axis=0)
        halo_ref[bi] = xin[CHUNK - HALO:, :]
        conv = None
        for tap in range(CONV_WIDTH):
            off = HALO - (CONV_WIDTH - 1) + tap
            term = xcat[off:off + CHUNK, :] * convw_ref[tap:tap + 1, :]
            conv = term if conv is None else conv + term
        xc = conv * jax.nn.sigmoid(conv)
        beta_all = jnp.where(neutral, 0.0, jax.nn.sigmoid(misc))
        g_all = jnp.where(neutral, 0.0, -jnp.exp(alog_ref[...]) * _softplus(misc + dtb_ref[...]))
        gc_all = _hdot(tri, g_all)
        gc_all_t = gc_all.T
        for h in range(B_HEADS):
            qh, kh, vh = xc[:, cols(0, h)], xc[:, cols(B_WIDTH, h)], xc[:, cols(2 * B_WIDTH, h)]
            qn.append(qh * lax.rsqrt(jnp.sum(qh * qh, axis=-1, keepdims=True) + EPS) * (B_HEAD_DIM ** -0.5))
            kn.append(kh * lax.rsqrt(jnp.sum(kh * kh, axis=-1, keepdims=True) + EPS))
            beta = jnp.broadcast_to(beta_all[:, MISC_BETA + h:MISC_BETA + h + 1], (CHUNK, B_HEAD_DIM))
            gc.append(jnp.broadcast_to(gc_all[:, MISC_DECAY + h:MISC_DECAY + h + 1], (CHUNK, B_HEAD_DIM)))
            g_col = gc[-1][:, :CHUNK]
            g_row = gc_all_t[MISC_DECAY + h:MISC_DECAY + h + 1, :]
            decay.append(jnp.where(incl, jnp.exp(jnp.where(incl, g_col - g_row, 0.0)), 0.0))
            kb.append(kn[-1] * beta)
            vb.append(vh * beta)

    kk = [lax.dot_general(kb[h].astype(BF16), kn[h].astype(BF16), nt, preferred_element_type=F32)
          for h in heads]
    qk = [lax.dot_general(qn[h].astype(BF16), kn[h].astype(BF16), nt, preferred_element_type=F32)
          for h in heads]
    intra = [jnp.where(incl, qk[h] * decay[h], 0.0) for h in heads]

    pw = [jnp.where(strict, -(kk[h] * decay[h]), 0.0) for h in heads]
    t_inv = [eye + pw[h] for h in heads]
    pw = [_dot3(pw[h], pw[h]) for h in heads]
    for _ in range(int(math.log2(CHUNK)) - 2):
        t_next = [t_inv[h] + _dot3(t_inv[h], pw[h]) for h in heads]
        pw = [_dot3(pw[h], pw[h]) for h in heads]
        t_inv = t_next
    t_inv = [t_inv[h] + _dot3(t_inv[h], pw[h]) for h in heads]

    u = [_bdot(t_inv[h], vb[h]) for h in heads]
    w = [_bdot(t_inv[h], kb[h] * jnp.exp(gc[h])) for h in heads]

    state = [state_ref[h] for h in heads]
    w_s = [_bdot(w[h], state[h]) for h in heads]
    q_s = [_bdot(qn[h] * jnp.exp(gc[h]), state[h]) for h in heads]
    v_new = [u[h] - w_s[h] for h in heads]
    o = [q_s[h] + _bdot(intra[h], v_new[h]) for h in heads]
    for h in heads:
        g_last = gc[h][CHUNK - 1:CHUNK, :]
        kd = kn[h] * jnp.exp(g_last - gc[h])
        state_ref[h] = state[h] * jnp.exp(g_last) + lax.dot_general(
            kd.astype(BF16), v_new[h].astype(BF16), tn, preferred_element_type=F32)

    for h in heads:
        bi, hh = divmod(h, B_HEADS)
        y = o[h] * lax.rsqrt(jnp.mean(o[h] * o[h], axis=-1, keepdims=True) + EPS) * nw_ref[...]
        zh = z_ref[bi, :, cols(0, hh)]
        o_ref[bi, :, cols(0, hh)] = (y * (zh * jax.nn.sigmoid(zh))).astype(BF16)


DELTA_BATCH = 4


def _gated_deltanet(qkv, z, misc, qkv_m, misc_m, conv_w, a_log, dt_bias, norm_w):
    b, s, _ = qkv.shape
    n_chunks = s // CHUNK + 1
    nb = DELTA_BATCH if b % DELTA_BATCH == 0 else 1
    blk = lambda bi, c: (bi, jnp.maximum(c - 1, 0), 0)
    fixed = lambda bi, c: (0, 0)
    lane_vec = lambda v: jnp.zeros((1, LANES), F32).at[0, MISC_DECAY:MISC_DECAY + B_HEADS].set(v)
    return pl.pallas_call(
        _delta_kernel,
        grid=(b // nb, n_chunks),
        in_specs=[
            pl.BlockSpec((nb, CHUNK, 3 * B_WIDTH), blk),
            pl.BlockSpec((nb, CHUNK, B_WIDTH), blk),
            pl.BlockSpec((nb, CHUNK, LANES), blk),
            pl.BlockSpec((CHUNK, 3 * B_WIDTH), fixed),
            pl.BlockSpec((CHUNK, LANES), fixed),
            pl.BlockSpec((CONV_WIDTH, 3 * B_WIDTH), fixed),
            pl.BlockSpec((1, LANES), fixed),
            pl.BlockSpec((1, LANES), fixed),
            pl.BlockSpec((1, B_HEAD_DIM), fixed),
        ],
        out_specs=pl.BlockSpec((nb, CHUNK, B_WIDTH), blk),
        out_shape=jax.ShapeDtypeStruct((b, s, B_WIDTH), BF16),
        scratch_shapes=[
            pltpu.VMEM((nb * B_HEADS, B_HEAD_DIM, B_HEAD_DIM), F32),
            pltpu.VMEM((nb, HALO, 3 * B_WIDTH), F32),
        ],
        compiler_params=pltpu.CompilerParams(dimension_semantics=("parallel", "arbitrary")),
        name="gated_deltanet",
    )(qkv, z, misc, qkv_m, misc_m, conv_w, lane_vec(a_log), lane_vec(dt_bias),
      norm_w.reshape(1, B_HEAD_DIM))


def _pack_rows(a):
    half = a.shape[1] // 2
    bits = lax.bitcast_convert_type(a.astype(BF16).astype(F32), jnp.int32)
    return bits[:, :half] | lax.shift_right_logical(bits[:, half:], 16)


def _unpack_rows(w):
    hi = lax.bitcast_convert_type(w & jnp.int32(-65536), F32)
    lo = lax.bitcast_convert_type(lax.shift_left(w, 16), F32)
    return jnp.concatenate([hi, lo], axis=1)


def _out_router_kernel(oa_ref, ob_ref, x_ref, wo_ref, nw_ref, wr_ref, br_ref,
                       h_ref, xn_ref, gate_ref, eid_ref):
    mix = (jnp.dot(oa_ref[...], wo_ref[:A_WIDTH, :], preferred_element_type=F32)
           + jnp.dot(ob_ref[...], wo_ref[A_WIDTH:, :], preferred_element_type=F32))
    h = x_ref[...] + mix
    h_ref[...] = h
    xn = h * lax.rsqrt(jnp.mean(h * h, axis=-1, keepdims=True) + EPS) * nw_ref[...]
    xn_ref[...] = _pack_rows(xn)
    tm = xn.shape[0]
    logits = lax.dot_general(wr_ref[...], xn, (((1,), (1,)), ((), ())), precision=HIGHEST,
                             preferred_element_type=F32) + br_ref[...]
    e_idx = lax.broadcasted_iota(jnp.int32, (N_EXPERTS, tm), 0).astype(F32)
    work = logits
    vals, idxs = [], []
    for _ in range(TOP_K):
        m = jnp.max(work, axis=0, keepdims=True)
        idx = jnp.min(jnp.where(work == m, e_idx, float(N_EXPERTS)), axis=0, keepdims=True)
        vals.append(m)
        idxs.append(idx)
        work = jnp.where(e_idx == idx, NEG_INF, work)
    exps = [jnp.exp(v - vals[0]) for v in vals]
    denom = exps[0]
    for e in exps[1:]:
        denom = denom + e
    eid_ref[...] = jnp.concatenate(idxs + [jnp.zeros((SUBLANES - TOP_K, tm), F32)], axis=0).astype(jnp.int32)
    gates_t = jnp.concatenate([e / denom for e in exps] + [jnp.zeros((LANES - TOP_K, tm), F32)], axis=0)
    gate_ref[...] = gates_t.T


def _out_router(o_a, o_b, x2d, w_out, norm_w, w_router, b_router, tm):
    n = x2d.shape[0]
    row = lambda i: (i, 0)
    fixed = lambda i: (0, 0)
    wr = w_router.T
    br = b_router.reshape(N_EXPERTS, 1)
    return pl.pallas_call(
        _out_router_kernel,
        grid=(n // tm,),
        in_specs=[
            pl.BlockSpec((tm, A_WIDTH), row),
            pl.BlockSpec((tm, B_WIDTH), row),
            pl.BlockSpec((tm, D_MODEL), row),
            pl.BlockSpec((A_WIDTH + B_WIDTH, D_MODEL), fixed),
            pl.BlockSpec((1, D_MODEL), fixed),
            pl.BlockSpec((N_EXPERTS, D_MODEL), fixed),
            pl.BlockSpec((N_EXPERTS, 1), fixed),
        ],
        out_specs=[pl.BlockSpec((tm, D_MODEL), row), pl.BlockSpec((tm, D_MODEL // 2), row),
                   pl.BlockSpec((tm, LANES), row), pl.BlockSpec((SUBLANES, tm), lambda i: (0, i))],
        out_shape=[jax.ShapeDtypeStruct((n, D_MODEL), F32), jax.ShapeDtypeStruct((n, D_MODEL // 2), jnp.int32),
                   jax.ShapeDtypeStruct((n, LANES), F32), jax.ShapeDtypeStruct((SUBLANES, n), jnp.int32)],
        compiler_params=pltpu.CompilerParams(dimension_semantics=("parallel",)),
        name="out_proj_router",
    )(o_a, o_b, x2d, w_out.astype(BF16), norm_w.reshape(1, D_MODEL), wr, br)


MOE_BM = 512
TOKEN_TILE = 512
ATT_KC = 512
MOE_BF = 512


SC_CORES = 2
SC_SUBCORES = 16
SC_ROWS = 128


def _sc_gather_rows(table, idx):
    n_workers = SC_CORES * SC_SUBCORES
    n_rows = idx.shape[0]
    d = table.shape[1]
    assert n_rows % (n_workers * SC_ROWS) == 0
    rows_per_worker = n_rows // n_workers
    mesh = plsc.VectorSubcoreMesh(core_axis_name="c", subcore_axis_name="s",
                                  num_cores=SC_CORES, num_subcores=SC_SUBCORES)

    @functools.partial(
        pl.kernel, mesh=mesh,
        out_type=jax.ShapeDtypeStruct((n_rows, d), table.dtype),
        scratch_types=[pltpu.VMEM((SC_ROWS,), jnp.int32), pltpu.VMEM((SC_ROWS, d), table.dtype),
                       pltpu.SemaphoreType.DMA],
        name="sc_gather_rows",
    )
    def gather(table_hbm, idx_hbm, out_hbm, idx_v, rows_v, sem):
        wid = lax.axis_index("s") * SC_CORES + lax.axis_index("c")
        base = wid * rows_per_worker

        @pl.loop(0, rows_per_worker // SC_ROWS)
        def _(i):
            off = base + i * SC_ROWS
            pltpu.sync_copy(idx_hbm.at[pl.ds(off, SC_ROWS)], idx_v)
            pltpu.async_copy(table_hbm.at[idx_v], rows_v, sem).wait()
            pltpu.sync_copy(rows_v, out_hbm.at[pl.ds(off, SC_ROWS)])

    return gather(table, idx)


def _route_blocks(eid_t, bm):
    n_tok = eid_t.shape[1]
    n_assign = n_tok * TOP_K
    experts = jnp.arange(N_EXPERTS, dtype=jnp.int32)
    flat_e = eid_t.reshape(-1)
    sorted_e, order = lax.sort((flat_e, jnp.arange(n_assign, dtype=jnp.int32)), num_keys=1)
    onehot = sorted_e[:, None] == experts[None, :]
    counts = jnp.sum(onehot, axis=0, dtype=jnp.int32)
    padded = (counts + bm - 1) // bm * bm
    start = jnp.cumsum(counts) - counts
    pend = jnp.cumsum(padded)
    pstart = pend - padded
    dest = jnp.arange(n_assign, dtype=jnp.int32) + jnp.sum(
        jnp.where(onehot, (pstart - start)[None, :], 0), axis=1)
    n_blocks = -(-(n_assign + N_EXPERTS * (bm - 1)) // bm)
    blk_start = jnp.arange(n_blocks, dtype=jnp.int32) * bm
    block_e = jnp.minimum(jnp.sum(blk_start[:, None] >= pend[None, :], axis=1), N_EXPERTS - 1)
    n_valid = pend[-1] // bm
    _, pos = lax.sort((order, dest), num_keys=1)
    t = jnp.arange(bm, dtype=jnp.int32)[None, :]
    pad_key = jnp.where(t < (padded - counts)[:, None], (pstart + counts)[:, None] + t,
                        n_blocks * bm).reshape(-1)
    pad_tok = jnp.arange(N_EXPERTS * bm, dtype=jnp.int32) % n_tok
    assert n_blocks * bm - n_assign == N_EXPERTS * bm
    _, row_tok = lax.sort((jnp.concatenate([dest, pad_key]),
                           jnp.concatenate([order % n_tok, pad_tok])), num_keys=1)
    return block_e.astype(jnp.int32), n_valid.astype(jnp.int32).reshape(1), row_tok, pos


def _moe_dense_kernel(be_ref, nv_ref, x_ref, wgu_ref, wd_ref, bg_ref, bu_ref, bd_ref, perm_ref, y_ref,
                      wg_s, wu_s, wd_s):
    i = pl.program_id(0)
    live = i < nv_ref[0]

    @pl.when(jnp.logical_and(live, jnp.logical_or(i == 0, be_ref[i] != be_ref[jnp.maximum(i - 1, 0)])))
    def _():
        perm = perm_ref[...]
        grp = 2 * LANES
        for gidx in range(2 * D_FF // grp):
            blk = wgu_ref[0, :, gidx * grp:(gidx + 1) * grp].astype(BF16)
            split = jnp.dot(blk, perm, preferred_element_type=F32).astype(BF16)
            wg_s[:, gidx * LANES:(gidx + 1) * LANES] = split[:, :LANES]
            wu_s[:, gidx * LANES:(gidx + 1) * LANES] = split[:, LANES:]
        wd_s[...] = wd_ref[0].astype(BF16)

    @pl.when(live)
    def _():
        x = _unpack_rows(x_ref[...]).astype(BF16)
        y = None
        for f0 in range(0, D_FF, MOE_BF):
            g = jnp.dot(x, wg_s[:, f0:f0 + MOE_BF], preferred_element_type=F32) + bg_ref[0, :, f0:f0 + MOE_BF]
            u = jnp.dot(x, wu_s[:, f0:f0 + MOE_BF], preferred_element_type=F32) + bu_ref[0, :, f0:f0 + MOE_BF]
            gate = jnp.minimum(g, SWIGLU_LIMIT)
            up = jnp.clip(u, -SWIGLU_LIMIT, SWIGLU_LIMIT)
            t = gate * jax.nn.sigmoid(gate * SWIGLU_ALPHA) * (up + 1.0)
            part = jnp.dot(t.astype(BF16), wd_s[f0:f0 + MOE_BF, :], preferred_element_type=F32)
            y = part if y is None else y + part
        y_ref[...] = _pack_rows(y + bd_ref[0])


def _moe_dense(x_sorted, block_e, n_valid, w_gate_up, w_down, b_gate, b_up, b_down, bm):
    n_blocks = block_e.shape[0]
    src = jnp.arange(2 * LANES)
    perm = jax.nn.one_hot((src % 2) * LANES + src // 2, 2 * LANES, dtype=BF16)
    rows = lambda i, be, nv: (jnp.minimum(i, nv[0] - 1), 0)
    wsel = lambda i, be, nv: (be[i], 0, 0)
    fixed = lambda i, be, nv: (0, 0)
    grid_spec = pltpu.PrefetchScalarGridSpec(
        num_scalar_prefetch=2,
        grid=(n_blocks,),
        in_specs=[
            pl.BlockSpec((bm, D_MODEL // 2), rows),
            pl.BlockSpec((1, D_MODEL, 2 * D_FF), wsel),
            pl.BlockSpec((1, D_FF, D_MODEL), wsel),
            pl.BlockSpec((1, 1, D_FF), wsel),
            pl.BlockSpec((1, 1, D_FF), wsel),
            pl.BlockSpec((1, 1, D_MODEL), wsel),
            pl.BlockSpec((2 * LANES, 2 * LANES), fixed),
        ],
        out_specs=pl.BlockSpec((bm, D_MODEL // 2), rows),
        scratch_shapes=[
            pltpu.VMEM((D_MODEL, D_FF), BF16),
            pltpu.VMEM((D_MODEL, D_FF), BF16),
            pltpu.VMEM((D_FF, D_MODEL), BF16),
        ],
    )
    return pl.pallas_call(
        _moe_dense_kernel,
        grid_spec=grid_spec,
        out_shape=jax.ShapeDtypeStruct(x_sorted.shape, jnp.int32),
        compiler_params=pltpu.CompilerParams(dimension_semantics=("arbitrary",)),
        name="moe_experts",
    )(block_e, n_valid, x_sorted, w_gate_up, w_down, b_gate, b_up, b_down, perm)


def _combine_kernel(h_ref, gate_ref, y0_ref, y1_ref, y2_ref, y3_ref, o_ref):
    gates = gate_ref[...]
    out = h_ref[...]
    for kk, y_ref in enumerate((y0_ref, y1_ref, y2_ref, y3_ref)):
        out = out + gates[:, kk:kk + 1] * _unpack_rows(y_ref[...])
    o_ref[...] = out


def _combine(h, gates, y, tm):
    n_tok = h.shape[0]
    nt = n_tok // tm
    row = lambda i: (i, 0)
    return pl.pallas_call(
        _combine_kernel,
        grid=(nt,),
        in_specs=[pl.BlockSpec((tm, D_MODEL), row), pl.BlockSpec((tm, LANES), row)]
        + [pl.BlockSpec((tm, D_MODEL // 2), functools.partial(lambda kk, i: (kk * nt + i, 0), kk))
           for kk in range(TOP_K)],
        out_specs=pl.BlockSpec((tm, D_MODEL), row),
        out_shape=jax.ShapeDtypeStruct((n_tok, D_MODEL), F32),
        compiler_params=pltpu.CompilerParams(dimension_semantics=("parallel",)),
        name="moe_combine",
    )(h, gates, y, y, y, y)


def kernel(x, meta_tokens, norm_mix_w, w_in, q_norm_w, k_norm_w, kv_norm_w, w_kv_up, conv_w, a_log,
           dt_bias, delta_norm_w, w_out, norm_ffn_w, w_router, b_router, w_gate_up, b_gate_up,
           w_down, b_down):
    b, s, d = x.shape
    consts = _pack_in_proj_weights(norm_mix_w[0], w_in[0], q_norm_w[0], k_norm_w[0], kv_norm_w[0],
                                   w_kv_up[0])
    cos, sin = _rope_tables(N_META + s)
    real = _in_proj(x.reshape(b * s, d), cos[N_META:], sin[N_META:], consts, VT_TILE)
    meta = _in_proj(meta_tokens, cos[:N_META], sin[:N_META], consts, N_META)
    vt = real[2]
    q, k, iq, ik, misc, qkv, z = (a.reshape(b, s, a.shape[-1]) for a in real[:2] + real[3:])
    km = jnp.pad(meta[1], ((0, LANES - N_META), (0, 0)))
    vmt = jnp.pad(meta[2][0], ((0, 0), (0, LANES - N_META)))
    o_a = _sparse_attention(q, k, vt, km, vmt, iq, ik, misc,
                            tq=LANES, kc=min(ATT_KC, s), topk=min(INDEX_TOPK, s // 4))
    lead = lambda a: jnp.pad(a, ((CHUNK - N_META, 0), (0, 0)))
    o_b = _gated_deltanet(qkv, z, misc, lead(meta[6]), lead(meta[5]), conv_w[0], a_log[0],
                          dt_bias[0], delta_norm_w[0])
    n_tok = b * s
    h, xn, gates, eid = _out_router(o_a.reshape(n_tok, A_WIDTH), o_b.reshape(n_tok, B_WIDTH),
                                    x.reshape(n_tok, d), w_out[0], norm_ffn_w[0], w_router[0],
                                    b_router[0], TOKEN_TILE)
    block_e, n_valid, row_tok, pos = _route_blocks(eid[:TOP_K], MOE_BM)
    bgu = b_gate_up[0].reshape(N_EXPERTS, 1, 2 * D_FF)
    x_sorted = _sc_gather_rows(xn, row_tok)
    y_sorted = _moe_dense(x_sorted, block_e, n_valid, w_gate_up[0], w_down[0], bgu[:, :, 0::2],
                          bgu[:, :, 1::2], b_down[0].reshape(N_EXPERTS, 1, D_MODEL), MOE_BM)
    y = _sc_gather_rows(y_sorted, pos)
    out = _combine(h, gates, y, TOKEN_TILE)
    return out.reshape(b, s, d)
```

```python
import functools
import math

import jax
import jax.numpy as jnp
from jax import lax
from jax.experimental import pallas as pl
from jax.experimental.pallas import tpu as pltpu
from jax.experimental.pallas import tpu_sc as plsc

F32 = jnp.float32
BF16 = jnp.bfloat16
HIGHEST = lax.Precision.HIGHEST

D_MODEL = 1024
N_META = 16
ROPE_THETA = 10000.0
EPS = 1e-6
A_HEAD_DIM = 64
A_HEADS = 8
A_WIDTH = A_HEADS * A_HEAD_DIM
KV_RANK = 256
IDX_HEADS = 8
IDX_DIM = 64
INDEX_TOPK = 256
B_HEAD_DIM = 128
B_HEADS = 4
B_WIDTH = B_HEADS * B_HEAD_DIM
CONV_WIDTH = 4
CHUNK = 64
N_EXPERTS = 32
TOP_K = 4
D_FF = D_MODEL
SWIGLU_LIMIT = 7.0
SWIGLU_ALPHA = 1.702
IN_SPLITS = (A_WIDTH, KV_RANK, IDX_HEADS * IDX_DIM, IDX_DIM, IDX_HEADS, 3 * B_WIDTH, B_WIDTH,
             B_HEADS, B_HEADS)

LANES = 128

C_Q = 0
C_CKV = C_Q + A_WIDTH
C_IQ = C_CKV + KV_RANK
C_IK = C_IQ + IDX_HEADS * IDX_DIM
C_MISC = C_IK + LANES
C_QKV = C_MISC + LANES
C_Z = C_QKV + 3 * B_WIDTH
C_END = C_Z + B_WIDTH
MISC_W, MISC_BETA, MISC_DECAY = 0, IDX_HEADS, IDX_HEADS + B_HEADS
LOG2E = math.log2(math.e)
VT_ROWS = A_HEAD_DIM + 16


def _rope_partner(a):
    lane = lax.broadcasted_iota(jnp.int32, a.shape, 1)
    first_half = (lane % A_HEAD_DIM) < (A_HEAD_DIM // 2)
    return jnp.where(first_half, pltpu.roll(a, LANES - A_HEAD_DIM // 2, 1),
                     pltpu.roll(a, A_HEAD_DIM // 2, 1))


def _rope(a, cos, sin_signed):
    return a * cos + _rope_partner(a) * sin_signed


def _head_rms(a, head_mean, gain):
    msq = jnp.dot((a * a).astype(BF16), head_mean, preferred_element_type=F32)
    return a * lax.rsqrt(msq + EPS) * gain


def _in_proj_kernel(x_ref, nw_ref, w_ref, wkv_ref, qnw_ref, knw_ref, kvnw_ref, cos_ref, sin_ref,
                    hm_ref, q_ref, k_ref, vt_ref, iq_ref, ik_ref, misc_ref, qkv_ref, z_ref):
    x = x_ref[...]
    u = x * lax.rsqrt(jnp.mean(x * x, axis=-1, keepdims=True) + EPS) * nw_ref[...]
    ub = u.astype(BF16)

    def proj(c0, c1):
        return jnp.dot(ub, w_ref[:, c0:c1], preferred_element_type=F32)

    cos = cos_ref[...]
    sin = sin_ref[...]
    hm = hm_ref[...]

    def rope_groups(a):
        return jnp.concatenate(
            [_rope(a[:, g * LANES:(g + 1) * LANES], cos, sin) for g in range(a.shape[1] // LANES)],
            axis=1)

    q = _head_rms(proj(C_Q, C_CKV), hm, qnw_ref[...])
    q_ref[...] = (rope_groups(q) * (A_HEAD_DIM ** -0.5 * LOG2E)).astype(BF16)

    ckv = proj(C_CKV, C_IQ)
    ckv = ckv * lax.rsqrt(jnp.mean(ckv * ckv, axis=-1, keepdims=True) + EPS) * kvnw_ref[...]
    kv = jnp.dot(ckv.astype(BF16), wkv_ref[...], preferred_element_type=F32)
    k = _head_rms(kv[:, :A_WIDTH], hm, knw_ref[...])
    k_ref[...] = rope_groups(k).astype(BF16)
    tm = x.shape[0]
    ones = jnp.ones((VT_ROWS - A_HEAD_DIM, tm), BF16)
    for g in range(A_WIDTH // LANES):
        vt = kv[:, A_WIDTH + g * LANES:A_WIDTH + (g + 1) * LANES].T.astype(BF16)
        for half in range(2):
            r0 = (2 * g + half) * VT_ROWS
            vt_ref[0, r0:r0 + A_HEAD_DIM, :] = vt[half * A_HEAD_DIM:(half + 1) * A_HEAD_DIM, :]
            vt_ref[0, r0 + A_HEAD_DIM:r0 + VT_ROWS, :] = ones

    iq_ref[...] = rope_groups(proj(C_IQ, C_IK)).astype(BF16)
    ik_ref[...] = _rope(proj(C_IK, C_MISC), cos, sin).astype(BF16)

    lane = lax.broadcasted_iota(jnp.int32, (1, LANES), 1)
    w_scale = jnp.where(lane < IDX_HEADS, IDX_HEADS ** -0.5 * IDX_DIM ** -0.5, 1.0)
    misc_ref[...] = proj(C_MISC, C_QKV) * w_scale
    qkv_ref[...] = proj(C_QKV, C_Z)
    z_ref[...] = proj(C_Z, C_END)


def _in_proj(x2d, cos, sin, consts, tm):
    n = x2d.shape[0]
    n_pos_blocks = cos.shape[0] // tm
    nw, w_pack, wkv, qnw, knw, kvnw, hm = consts
    row = lambda i: (i, 0)
    fixed = lambda i: (0, 0)
    pos = lambda i: (i % n_pos_blocks, 0)
    out_widths = (A_WIDTH, A_WIDTH, None, IDX_HEADS * IDX_DIM, LANES, LANES, 3 * B_WIDTH, B_WIDTH)
    out_dtypes = (BF16, BF16, BF16, BF16, BF16, F32, F32, F32)
    vt_rows = A_HEADS * VT_ROWS
    out_specs = [pl.BlockSpec((1, vt_rows, tm), lambda i: (i, 0, 0)) if w is None
                 else pl.BlockSpec((tm, w), row) for w in out_widths]
    out_shape = [jax.ShapeDtypeStruct((n // tm, vt_rows, tm) if w is None else (n, w), dt)
                 for w, dt in zip(out_widths, out_dtypes)]
    return pl.pallas_call(
        _in_proj_kernel,
        grid=(n // tm,),
        in_specs=[
            pl.BlockSpec((tm, D_MODEL), row),
            pl.BlockSpec(nw.shape, fixed),
            pl.BlockSpec(w_pack.shape, fixed),
            pl.BlockSpec(wkv.shape, fixed),
            pl.BlockSpec(qnw.shape, fixed),
            pl.BlockSpec(knw.shape, fixed),
            pl.BlockSpec(kvnw.shape, fixed),
            pl.BlockSpec((tm, LANES), pos),
            pl.BlockSpec((tm, LANES), pos),
            pl.BlockSpec(hm.shape, fixed),
        ],
        out_specs=out_specs,
        out_shape=out_shape,
        compiler_params=pltpu.CompilerParams(dimension_semantics=("parallel",)),
        name="in_proj",
    )(x2d, nw, w_pack, wkv, qnw, knw, kvnw, cos, sin, hm)


def _pack_in_proj_weights(norm_mix_w, w_in, q_norm_w, k_norm_w, kv_norm_w, w_kv_up):
    points = []
    acc = 0
    for s in IN_SPLITS:
        points.append((acc, acc + s))
        acc += s
    a_q, a_ckv, i_q, i_k, i_w, b_qkv, b_z, b_beta, b_a = (w_in[:, a:b] for a, b in points)
    misc = jnp.concatenate(
        [i_w, b_beta, b_a, jnp.zeros((D_MODEL, LANES - IDX_HEADS - 2 * B_HEADS), w_in.dtype)], axis=1)
    w_pack = jnp.concatenate([a_q, a_ckv, i_q, i_k, i_k, misc, b_qkv, b_z], axis=1).astype(BF16)
    head_mean = jnp.kron(jnp.eye(A_HEADS, dtype=F32),
                         jnp.full((A_HEAD_DIM, A_HEAD_DIM), 1.0 / A_HEAD_DIM, F32)).astype(BF16)
    return (norm_mix_w.reshape(1, D_MODEL), w_pack, w_kv_up.astype(BF16),
            jnp.tile(q_norm_w, A_HEADS).reshape(1, A_WIDTH),
            jnp.tile(k_norm_w, A_HEADS).reshape(1, A_WIDTH),
            kv_norm_w.reshape(1, KV_RANK), head_mean)


def _rope_tables(n_pos):
    half = A_HEAD_DIM // 2
    inv_freq = ROPE_THETA ** (-jnp.arange(0, A_HEAD_DIM, 2, dtype=F32) / A_HEAD_DIM)
    ang = jnp.arange(n_pos, dtype=F32)[:, None] * inv_freq[None, :]
    cos, sin = jnp.cos(ang), jnp.sin(ang)
    cos128 = jnp.tile(cos, (1, LANES // half))
    sin128 = jnp.tile(jnp.concatenate([-sin, sin], axis=1), (1, LANES // A_HEAD_DIM))
    return cos128, sin128


NEG_INF = float("-inf")
F32_MAX = float(jnp.finfo(jnp.float32).max)
INT_MIN = -2 ** 31


VT_TILE = 512
SUBLANES = 8


def _attn_kernel(q_ref, k_ref, vt_ref, km_ref, vmt_ref, iq_ref, ik_ref, misc_ref, o_ref,
                 isc_ref, lhs_ref, qm_ref, s_ref, m_ref, acc_ref, *, tq, kc, topk, pos_bits):
    j = pl.program_id(1)
    n_kc = lax.div((j + 1) * tq + (kc - 1), kc)
    g = kc // SUBLANES
    lane = lax.broadcasted_iota(jnp.int32, (1, LANES), 1)
    lo_half = lane < A_HEAD_DIM
    n_pairs = A_WIDTH // LANES
    nt = (((1,), (1,)), ((), ()))

    iq = iq_ref[0]
    q = q_ref[0]
    zero = jnp.zeros((), BF16)
    for p in range(n_pairs):
        blk = iq[:, p * LANES:(p + 1) * LANES]
        lhs_ref[(2 * p) * tq:(2 * p + 1) * tq, :] = jnp.where(lo_half, blk, zero)
        lhs_ref[(2 * p + 1) * tq:(2 * p + 2) * tq, :] = jnp.where(lo_half, zero, blk)
        qb = q[:, p * LANES:(p + 1) * LANES]
        qm_ref[p, :tq, :] = jnp.where(lo_half, qb, zero)
        qm_ref[p, tq:, :] = jnp.where(lo_half, zero, qb)
    w_t = misc_ref[0].T

    def fold0(x3, op):
        n = x3.shape[0]
        while n > 1:
            x3 = op(x3[:n // 2], x3[n // 2:n])
            n //= 2
        return x3[0]

    q_pos = j * tq + lax.broadcasted_iota(jnp.int32, (1, tq), 1)

    def idx_body(c, carry):
        k0 = pl.multiple_of(c * kc, kc)
        r = lax.dot_general(ik_ref[0, pl.ds(k0, kc), :], lhs_ref[...], nt,
                            preferred_element_type=F32)
        s = None
        for h in range(IDX_HEADS):
            term = jnp.maximum(r[:, h * tq:(h + 1) * tq], 0.0) * w_t[MISC_W + h:MISC_W + h + 1, :]
            s = term if s is None else s + term
        k_pos = k0 + lax.broadcasted_iota(jnp.int32, (kc, 1), 0)
        isc_ref[c] = jnp.where(k_pos <= q_pos, s, NEG_INF)
        return carry
    lax.fori_loop(0, n_kc, idx_body, 0)

    def count(pred):
        def body(c, acc):
            x3 = isc_ref[c].reshape(g, SUBLANES, tq)
            return acc + fold0(jnp.where(pred(x3, c), 1.0, 0.0), jnp.add)
        acc = lax.fori_loop(0, n_kc, body, jnp.zeros((SUBLANES, tq), F32))
        return jnp.broadcast_to(jnp.sum(acc, axis=0, keepdims=True), (SUBLANES, tq))

    def key_to_float(u):
        key = u ^ jnp.int32(INT_MIN)
        bits = jnp.where(key >= 0, key, key ^ jnp.int32(0x7FFFFFFF))
        return lax.bitcast_convert_type(bits, F32)

    def bit_body(i, carry):
        u, n_u = carry
        u2 = u | lax.shift_left(jnp.int32(1), 31 - i)
        cand = key_to_float(u2)
        n_ge = count(lambda x3, c: x3 >= cand[None])
        keep = n_ge >= topk
        return jnp.where(keep, u2, u), jnp.where(keep, n_ge, n_u)
    u, n_u = lax.fori_loop(0, 32, bit_body, (jnp.zeros((SUBLANES, tq), jnp.int32),
                                             jnp.zeros((SUBLANES, tq), F32)))
    few = (u >= 0) & (u < 0x00800000)
    tau = jnp.where(few, -F32_MAX, key_to_float(u))

    n_ge = jnp.where(few, 0.0, n_u)

    @pl.when(jnp.max(n_ge) > topk)
    def _():
        n_gt = count(lambda x3, c: x3 > tau[None])
        need = topk - n_gt
        def chunk_pos(c):
            return (c * kc + lax.broadcasted_iota(jnp.int32, (g, SUBLANES, tq), 0) * SUBLANES
                    + lax.broadcasted_iota(jnp.int32, (g, SUBLANES, tq), 1))
        def pos_body(i, cut):
            cut2 = cut | lax.shift_left(jnp.int32(1), pos_bits - 1 - i)
            ties_before = count(lambda x3, c: (x3 == tau[None]) & (chunk_pos(c) < cut2[None]))
            return jnp.where(ties_before < need, cut2, cut)
        cut = lax.fori_loop(0, pos_bits, pos_body, jnp.zeros((SUBLANES, tq), jnp.int32))
        def drop_body(c, carry):
            x3 = isc_ref[c].reshape(g, SUBLANES, tq)
            drop = (x3 == tau[None]) & (chunk_pos(c) > cut[None])
            isc_ref[c] = jnp.where(drop, NEG_INF, x3).reshape(kc, tq)
            return carry
        lax.fori_loop(0, n_kc, drop_body, 0)

    tau_row = tau[0:1, :]

    def attend(key_pairs, bias, vt_slabs, first):
        n = key_pairs[0].shape[0]
        m_cur = []
        for p in range(n_pairs):
            s2 = lax.dot_general(key_pairs[p], qm_ref[p], nt, preferred_element_type=F32)
            for half in range(2):
                s = s2[:, half * tq:(half + 1) * tq] + bias
                s_ref[2 * p + half, :n, :] = s
                m8 = fold0(s.reshape(n // SUBLANES, SUBLANES, tq), jnp.maximum)
                m_cur.append(jnp.max(m8, axis=0, keepdims=True))
        for hd in range(A_HEADS):
            m_new = m_cur[hd] if first else jnp.maximum(m_ref[hd][0:1, :], m_cur[hd])
            e = jnp.exp2(s_ref[hd, :n, :] - m_new).astype(BF16)
            pv = None
            off = 0
            for slab in vt_slabs(hd):
                part = jnp.dot(slab, e[off:off + slab.shape[1], :], preferred_element_type=F32)
                pv = part if pv is None else pv + part
                off += slab.shape[1]
            if first:
                acc_ref[hd] = pv
            else:
                acc_ref[hd] = jnp.exp2(m_ref[hd][0:1, :] - m_new) * acc_ref[hd] + pv
            m_ref[hd] = jnp.broadcast_to(m_new, (SUBLANES, tq))

    meta_bias = jnp.where(lax.broadcasted_iota(jnp.int32, (LANES, 1), 0) < N_META, 0.0, NEG_INF)
    attend([km_ref[:, p * LANES:(p + 1) * LANES] for p in range(n_pairs)], meta_bias,
           lambda hd: [vmt_ref[hd * VT_ROWS:(hd + 1) * VT_ROWS, :]], True)

    def att_body(c, carry):
        k0 = pl.multiple_of(c * kc, kc)
        bias = jnp.where(isc_ref[c] >= tau_row, 0.0, NEG_INF)
        attend([k_ref[0, pl.ds(k0, kc), p * LANES:(p + 1) * LANES] for p in range(n_pairs)], bias,
               lambda hd: [vt_ref[c * (kc // VT_TILE) + t, hd * VT_ROWS:(hd + 1) * VT_ROWS, :]
                           for t in range(kc // VT_TILE)], False)
        return carry
    lax.fori_loop(0, n_kc, att_body, 0)

    for p in range(n_pairs):
        halves = []
        for hd in (2 * p, 2 * p + 1):
            acc = acc_ref[hd]
            halves.append(acc[:A_HEAD_DIM, :] / acc[A_HEAD_DIM:A_HEAD_DIM + 1, :])
        o_ref[0, :, p * LANES:(p + 1) * LANES] = jnp.concatenate(halves, axis=0).T.astype(BF16)


def _sparse_attention(q, k, vt, km, vmt, iq, ik, misc, *, tq, kc, topk):
    b, s, _ = q.shape
    assert tq == LANES and kc % VT_TILE == 0 and s % kc == 0
    kernel = functools.partial(_attn_kernel, tq=tq, kc=kc, topk=topk, pos_bits=int(math.log2(s)))
    qblk = lambda bi, j: (bi, j, 0)
    full = lambda bi, j: (bi, 0, 0)
    fixed = lambda bi, j: (0, 0)
    vt_rows = A_HEADS * VT_ROWS
    return pl.pallas_call(
        kernel,
        grid=(b, s // tq),
        in_specs=[
            pl.BlockSpec((1, tq, A_WIDTH), qblk),
            pl.BlockSpec((1, s, A_WIDTH), full),
            pl.BlockSpec((s // VT_TILE, vt_rows, VT_TILE), full),
            pl.BlockSpec(km.shape, fixed),
            pl.BlockSpec(vmt.shape, fixed),
            pl.BlockSpec((1, tq, IDX_HEADS * IDX_DIM), qblk),
            pl.BlockSpec((1, s, LANES), full),
            pl.BlockSpec((1, tq, LANES), qblk),
        ],
        out_specs=pl.BlockSpec((1, tq, A_WIDTH), qblk),
        out_shape=jax.ShapeDtypeStruct((b, s, A_WIDTH), BF16),
        scratch_shapes=[
            pltpu.VMEM((s // kc, kc, tq), F32),
            pltpu.VMEM((IDX_HEADS * tq, LANES), BF16),
            pltpu.VMEM((A_WIDTH // LANES, 2 * tq, LANES), BF16),
            pltpu.VMEM((A_HEADS, max(kc, LANES), tq), F32),
            pltpu.VMEM((A_HEADS, SUBLANES, tq), F32),
            pltpu.VMEM((A_HEADS, VT_ROWS, tq), F32),
        ],
        compiler_params=pltpu.CompilerParams(dimension_semantics=("parallel", "arbitrary")),
        name="sparse_attention",
    )(q, k, vt, km, vmt, iq, ik, misc)


HALO = 8


def _softplus(x):
    return jnp.maximum(x, 0.0) + jnp.log1p(jnp.exp(-jnp.abs(x)))


def _bdot(a, b):
    return jnp.dot(a.astype(BF16), b.astype(BF16), preferred_element_type=F32)


def _hdot(a, b):
    return jnp.dot(a, b, precision=HIGHEST, preferred_element_type=F32)


def _dot3(a, b):
    a_hi = a.astype(BF16)
    b_hi = b.astype(BF16)
    a_lo = (a - a_hi.astype(F32)).astype(BF16)
    b_lo = (b - b_hi.astype(F32)).astype(BF16)
    dot = functools.partial(jnp.dot, preferred_element_type=F32)
    return dot(a_hi, b_hi) + (dot(a_hi, b_lo) + dot(a_lo, b_hi))


def _delta_kernel(qkv_ref, z_ref, misc_ref, qkvm_ref, miscm_ref, convw_ref, alog_ref, dtb_ref,
                  nw_ref, o_ref, state_ref, halo_ref):
    c = pl.program_id(1)
    n_pad = CHUNK - N_META
    nt = (((1,), (1,)), ((), ()))
    tn = (((0,), (0,)), ((), ()))

    @pl.when(c == 0)
    def _():
        state_ref[...] = jnp.zeros_like(state_ref)
        halo_ref[...] = jnp.zeros_like(halo_ref)

    ri = lax.broadcasted_iota(jnp.int32, (CHUNK, CHUNK), 0)
    ci = lax.broadcasted_iota(jnp.int32, (CHUNK, CHUNK), 1)
    incl = ri >= ci
    strict = ri > ci
    eye = (ri == ci).astype(F32)
    tri = incl.astype(F32)
    row = lax.broadcasted_iota(jnp.int32, (CHUNK, 1), 0)
    is_meta = c == 0
    neutral = jnp.logical_and(is_meta, row < n_pad)
    cols = lambda base, h: slice(base + h * B_HEAD_DIM, base + (h + 1) * B_HEAD_DIM)

    nb = qkv_ref.shape[0]
    heads = range(nb * B_HEADS)
    qn, kn, vb, kb, gc, decay = [], [], [], [], [], []
    for bi in range(nb):
        xin = jnp.where(is_meta, qkvm_ref[...], qkv_ref[bi])
        misc = jnp.where(is_meta, miscm_ref[...], misc_ref[bi])
        xcat = jnp.concatenate([halo_ref[bi], xin], axis=0)
        halo_ref[bi] = xin[CHUNK - HALO:, :]
        conv = None
        for tap in range(CONV_WIDTH):
            off = HALO - (CONV_WIDTH - 1) + tap
            term = xcat[off:off + CHUNK, :] * convw_ref[tap:tap + 1, :]
            conv = term if conv is None else conv + term
        xc = conv * jax.nn.sigmoid(conv)
        beta_all = jnp.where(neutral, 0.0, jax.nn.sigmoid(misc))
        g_all = jnp.where(neutral, 0.0, -jnp.exp(alog_ref[...]) * _softplus(misc + dtb_ref[...]))
        gc_all = _hdot(tri, g_all)
        gc_all_t = gc_all.T
        for h in range(B_HEADS):
            qh, kh, vh = xc[:, cols(0, h)], xc[:, cols(B_WIDTH, h)], xc[:, cols(2 * B_WIDTH, h)]
            qn.append(qh * lax.rsqrt(jnp.sum(qh * qh, axis=-1, keepdims=True) + EPS) * (B_HEAD_DIM ** -0.5))
            kn.append(kh * lax.rsqrt(jnp.sum(kh * kh, axis=-1, keepdims=True) + EPS))
            beta = jnp.broadcast_to(beta_all[:, MISC_BETA + h:MISC_BETA + h + 1], (CHUNK, B_HEAD_DIM))
            gc.append(jnp.broadcast_to(gc_all[:, MISC_DECAY + h:MISC_DECAY + h + 1], (CHUNK, B_HEAD_DIM)))
            g_col = gc[-1][:, :CHUNK]
            g_row = gc_all_t[MISC_DECAY + h:MISC_DECAY + h + 1, :]
            decay.append(jnp.where(incl, jnp.exp(jnp.where(incl, g_col - g_row, 0.0)), 0.0))
            kb.append(kn[-1] * beta)
            vb.append(vh * beta)

    kk = [lax.dot_general(kb[h].astype(BF16), kn[h].astype(BF16), nt, preferred_element_type=F32)
          for h in heads]
    qk = [lax.dot_general(qn[h].astype(BF16), kn[h].astype(BF16), nt, preferred_element_type=F32)
          for h in heads]
    intra = [jnp.where(incl, qk[h] * decay[h], 0.0) for h in heads]

    pw = [jnp.where(strict, -(kk[h] * decay[h]), 0.0) for h in heads]
    t_inv = [eye + pw[h] for h in heads]
    pw = [_dot3(pw[h], pw[h]) for h in heads]
    for _ in range(int(math.log2(CHUNK)) - 2):
        t_next = [t_inv[h] + _dot3(t_inv[h], pw[h]) for h in heads]
        pw = [_dot3(pw[h], pw[h]) for h in heads]
        t_inv = t_next
    t_inv = [t_inv[h] + _dot3(t_inv[h], pw[h]) for h in heads]

    u = [_bdot(t_inv[h], vb[h]) for h in heads]
    w = [_bdot(t_inv[h], kb[h] * jnp.exp(gc[h])) for h in heads]

    state = [state_ref[h] for h in heads]
    w_s = [_bdot(w[h], state[h]) for h in heads]
    q_s = [_bdot(qn[h] * jnp.exp(gc[h]), state[h]) for h in heads]
    v_new = [u[h] - w_s[h] for h in heads]
    o = [q_s[h] + _bdot(intra[h], v_new[h]) for h in heads]
    for h in heads:
        g_last = gc[h][CHUNK - 1:CHUNK, :]
        kd = kn[h] * jnp.exp(g_last - gc[h])
        state_ref[h] = state[h] * jnp.exp(g_last) + lax.dot_general(
            kd.astype(BF16), v_new[h].astype(BF16), tn, preferred_element_type=F32)

    for h in heads:
        bi, hh = divmod(h, B_HEADS)
        y = o[h] * lax.rsqrt(jnp.mean(o[h] * o[h], axis=-1, keepdims=True) + EPS) * nw_ref[...]
        zh = z_ref[bi, :, cols(0, hh)]
        o_ref[bi, :, cols(0, hh)] = (y * (zh * jax.nn.sigmoid(zh))).astype(BF16)


DELTA_BATCH = 4


def _gated_deltanet(qkv, z, misc, qkv_m, misc_m, conv_w, a_log, dt_bias, norm_w):
    b, s, _ = qkv.shape
    n_chunks = s // CHUNK + 1
    nb = DELTA_BATCH if b % DELTA_BATCH == 0 else 1
    blk = lambda bi, c: (bi, jnp.maximum(c - 1, 0), 0)
    fixed = lambda bi, c: (0, 0)
    lane_vec = lambda v: jnp.zeros((1, LANES), F32).at[0, MISC_DECAY:MISC_DECAY + B_HEADS].set(v)
    return pl.pallas_call(
        _delta_kernel,
        grid=(b // nb, n_chunks),
        in_specs=[
            pl.BlockSpec((nb, CHUNK, 3 * B_WIDTH), blk),
            pl.BlockSpec((nb, CHUNK, B_WIDTH), blk),
            pl.BlockSpec((nb, CHUNK, LANES), blk),
            pl.BlockSpec((CHUNK, 3 * B_WIDTH), fixed),
            pl.BlockSpec((CHUNK, LANES), fixed),
            pl.BlockSpec((CONV_WIDTH, 3 * B_WIDTH), fixed),
            pl.BlockSpec((1, LANES), fixed),
            pl.BlockSpec((1, LANES), fixed),
            pl.BlockSpec((1, B_HEAD_DIM), fixed),
        ],
        out_specs=pl.BlockSpec((nb, CHUNK, B_WIDTH), blk),
        out_shape=jax.ShapeDtypeStruct((b, s, B_WIDTH), BF16),
        scratch_shapes=[
            pltpu.VMEM((nb * B_HEADS, B_HEAD_DIM, B_HEAD_DIM), F32),
            pltpu.VMEM((nb, HALO, 3 * B_WIDTH), F32),
        ],
        compiler_params=pltpu.CompilerParams(dimension_semantics=("parallel", "arbitrary")),
        name="gated_deltanet",
    )(qkv, z, misc, qkv_m, misc_m, conv_w, lane_vec(a_log), lane_vec(dt_bias),
      norm_w.reshape(1, B_HEAD_DIM))


def _pack_rows(a):
    half = a.shape[1] // 2
    bits = lax.bitcast_convert_type(a.astype(BF16).astype(F32), jnp.int32)
    return bits[:, :half] | lax.shift_right_logical(bits[:, half:], 16)


def _unpack_rows(w):
    hi = lax.bitcast_convert_type(w & jnp.int32(-65536), F32)
    lo = lax.bitcast_convert_type(lax.shift_left(w, 16), F32)
    return jnp.concatenate([hi, lo], axis=1)


def _out_router_kernel(oa_ref, ob_ref, x_ref, wo_ref, nw_ref, wr_ref, br_ref,
                       h_ref, xn_ref, gate_ref, eid_ref):
    mix = (jnp.dot(oa_ref[...], wo_ref[:A_WIDTH, :], preferred_element_type=F32)
           + jnp.dot(ob_ref[...], wo_ref[A_WIDTH:, :], preferred_element_type=F32))
    h = x_ref[...] + mix
    h_ref[...] = h
    xn = h * lax.rsqrt(jnp.mean(h * h, axis=-1, keepdims=True) + EPS) * nw_ref[...]
    xn_ref[...] = _pack_rows(xn)
    tm = xn.shape[0]
    logits = lax.dot_general(wr_ref[...], xn, (((1,), (1,)), ((), ())), precision=HIGHEST,
                             preferred_element_type=F32) + br_ref[...]
    e_idx = lax.broadcasted_iota(jnp.int32, (N_EXPERTS, tm), 0).astype(F32)
    work = logits
    vals, idxs = [], []
    for _ in range(TOP_K):
        m = jnp.max(work, axis=0, keepdims=True)
        idx = jnp.min(jnp.where(work == m, e_idx, float(N_EXPERTS)), axis=0, keepdims=True)
        vals.append(m)
        idxs.append(idx)
        work = jnp.where(e_idx == idx, NEG_INF, work)
    exps = [jnp.exp(v - vals[0]) for v in vals]
    denom = exps[0]
    for e in exps[1:]:
        denom = denom + e
    eid_ref[...] = jnp.concatenate(idxs + [jnp.zeros((SUBLANES - TOP_K, tm), F32)], axis=0).astype(jnp.int32)
    gates_t = jnp.concatenate([e / denom for e in exps] + [jnp.zeros((LANES - TOP_K, tm), F32)], axis=0)
    gate_ref[...] = gates_t.T


def _out_router(o_a, o_b, x2d, w_out, norm_w, w_router, b_router, tm):
    n = x2d.shape[0]
    row = lambda i: (i, 0)
    fixed = lambda i: (0, 0)
    wr = w_router.T
    br = b_router.reshape(N_EXPERTS, 1)
    return pl.pallas_call(
        _out_router_kernel,
        grid=(n // tm,),
        in_specs=[
            pl.BlockSpec((tm, A_WIDTH), row),
            pl.BlockSpec((tm, B_WIDTH), row),
            pl.BlockSpec((tm, D_MODEL), row),
            pl.BlockSpec((A_WIDTH + B_WIDTH, D_MODEL), fixed),
            pl.BlockSpec((1, D_MODEL), fixed),
            pl.BlockSpec((N_EXPERTS, D_MODEL), fixed),
            pl.BlockSpec((N_EXPERTS, 1), fixed),
        ],
        out_specs=[pl.BlockSpec((tm, D_MODEL), row), pl.BlockSpec((tm, D_MODEL // 2), row),
                   pl.BlockSpec((tm, LANES), row), pl.BlockSpec((SUBLANES, tm), lambda i: (0, i))],
        out_shape=[jax.ShapeDtypeStruct((n, D_MODEL), F32), jax.ShapeDtypeStruct((n, D_MODEL // 2), jnp.int32),
                   jax.ShapeDtypeStruct((n, LANES), F32), jax.ShapeDtypeStruct((SUBLANES, n), jnp.int32)],
        compiler_params=pltpu.CompilerParams(dimension_semantics=("parallel",)),
        name="out_proj_router",
    )(o_a, o_b, x2d, w_out.astype(BF16), norm_w.reshape(1, D_MODEL), wr, br)


MOE_BM = 512
TOKEN_TILE = 512
COMBINE_TILE = 1024
ATT_KC = 512
MOE_BF = 512


SC_CORES = 2
SC_SUBCORES = 16
SC_ROWS = 128


def _sc_gather_rows(table, idx):
    n_workers = SC_CORES * SC_SUBCORES
    n_rows = idx.shape[0]
    d = table.shape[1]
    assert n_rows % (n_workers * SC_ROWS) == 0
    rows_per_worker = n_rows // n_workers
    mesh = plsc.VectorSubcoreMesh(core_axis_name="c", subcore_axis_name="s",
                                  num_cores=SC_CORES, num_subcores=SC_SUBCORES)

    @functools.partial(
        pl.kernel, mesh=mesh,
        out_type=jax.ShapeDtypeStruct((n_rows, d), table.dtype),
        scratch_types=[pltpu.VMEM((SC_ROWS,), jnp.int32), pltpu.VMEM((SC_ROWS, d), table.dtype),
                       pltpu.SemaphoreType.DMA],
        name="sc_gather_rows",
    )
    def gather(table_hbm, idx_hbm, out_hbm, idx_v, rows_v, sem):
        wid = lax.axis_index("s") * SC_CORES + lax.axis_index("c")
        base = wid * rows_per_worker

        @pl.loop(0, rows_per_worker // SC_ROWS)
        def _(i):
            off = base + i * SC_ROWS
            pltpu.sync_copy(idx_hbm.at[pl.ds(off, SC_ROWS)], idx_v)
            pltpu.async_copy(table_hbm.at[idx_v], rows_v, sem).wait()
            pltpu.sync_copy(rows_v, out_hbm.at[pl.ds(off, SC_ROWS)])

    return gather(table, idx)


def _route_blocks(eid_t, bm):
    n_tok = eid_t.shape[1]
    n_assign = n_tok * TOP_K
    experts = jnp.arange(N_EXPERTS, dtype=jnp.int32)
    flat_e = eid_t.reshape(-1)
    sorted_e, order = lax.sort((flat_e, jnp.arange(n_assign, dtype=jnp.int32)), num_keys=1)
    onehot = sorted_e[:, None] == experts[None, :]
    counts = jnp.sum(onehot, axis=0, dtype=jnp.int32)
    padded = (counts + bm - 1) // bm * bm
    start = jnp.cumsum(counts) - counts
    pend = jnp.cumsum(padded)
    pstart = pend - padded
    dest = jnp.arange(n_assign, dtype=jnp.int32) + jnp.sum(
        jnp.where(onehot, (pstart - start)[None, :], 0), axis=1)
    n_blocks = -(-(n_assign + N_EXPERTS * (bm - 1)) // bm)
    blk_start = jnp.arange(n_blocks, dtype=jnp.int32) * bm
    block_e = jnp.minimum(jnp.sum(blk_start[:, None] >= pend[None, :], axis=1), N_EXPERTS - 1)
    n_valid = pend[-1] // bm
    _, pos = lax.sort((order, dest), num_keys=1)
    t = jnp.arange(bm, dtype=jnp.int32)[None, :]
    pad_key = jnp.where(t < (padded - counts)[:, None], (pstart + counts)[:, None] + t,
                        n_blocks * bm).reshape(-1)
    pad_tok = jnp.arange(N_EXPERTS * bm, dtype=jnp.int32) % n_tok
    assert n_blocks * bm - n_assign == N_EXPERTS * bm
    _, row_tok = lax.sort((jnp.concatenate([dest, pad_key]),
                           jnp.concatenate([order % n_tok, pad_tok])), num_keys=1)
    return block_e.astype(jnp.int32), n_valid.astype(jnp.int32).reshape(1), row_tok, pos


def _moe_dense_kernel(be_ref, nv_ref, x_ref, wgu_ref, wd_ref, bg_ref, bu_ref, bd_ref, perm_ref, y_ref,
                      wg_s, wu_s, wd_s):
    i = pl.program_id(0)
    live = i < nv_ref[0]

    @pl.when(jnp.logical_and(live, jnp.logical_or(i == 0, be_ref[i] != be_ref[jnp.maximum(i - 1, 0)])))
    def _():
        perm = perm_ref[...]
        grp = 2 * LANES
        for gidx in range(2 * D_FF // grp):
            blk = wgu_ref[0, :, gidx * grp:(gidx + 1) * grp].astype(BF16)
            split = jnp.dot(blk, perm, preferred_element_type=F32).astype(BF16)
            wg_s[:, gidx * LANES:(gidx + 1) * LANES] = split[:, :LANES]
            wu_s[:, gidx * LANES:(gidx + 1) * LANES] = split[:, LANES:]
        wd_s[...] = wd_ref[0].astype(BF16)

    @pl.when(live)
    def _():
        x = _unpack_rows(x_ref[...]).astype(BF16)
        y = None
        for f0 in range(0, D_FF, MOE_BF):
            g = jnp.dot(x, wg_s[:, f0:f0 + MOE_BF], preferred_element_type=F32) + bg_ref[0, :, f0:f0 + MOE_BF]
            u = jnp.dot(x, wu_s[:, f0:f0 + MOE_BF], preferred_element_type=F32) + bu_ref[0, :, f0:f0 + MOE_BF]
            gate = jnp.minimum(g, SWIGLU_LIMIT)
            up = jnp.clip(u, -SWIGLU_LIMIT, SWIGLU_LIMIT)
            t = gate * jax.nn.sigmoid(gate * SWIGLU_ALPHA) * (up + 1.0)
            part = jnp.dot(t.astype(BF16), wd_s[f0:f0 + MOE_BF, :], preferred_element_type=F32)
            y = part if y is None else y + part
        y_ref[...] = _pack_rows(y + bd_ref[0])


def _moe_dense(x_sorted, block_e, n_valid, w_gate_up, w_down, b_gate, b_up, b_down, bm):
    n_blocks = block_e.shape[0]
    src = jnp.arange(2 * LANES)
    perm = jax.nn.one_hot((src % 2) * LANES + src // 2, 2 * LANES, dtype=BF16)
    rows = lambda i, be, nv: (jnp.minimum(i, nv[0] - 1), 0)
    wsel = lambda i, be, nv: (be[i], 0, 0)
    fixed = lambda i, be, nv: (0, 0)
    grid_spec = pltpu.PrefetchScalarGridSpec(
        num_scalar_prefetch=2,
        grid=(n_blocks,),
        in_specs=[
            pl.BlockSpec((bm, D_MODEL // 2), rows),
            pl.BlockSpec((1, D_MODEL, 2 * D_FF), wsel),
            pl.BlockSpec((1, D_FF, D_MODEL), wsel),
            pl.BlockSpec((1, 1, D_FF), wsel),
            pl.BlockSpec((1, 1, D_FF), wsel),
            pl.BlockSpec((1, 1, D_MODEL), wsel),
            pl.BlockSpec((2 * LANES, 2 * LANES), fixed),
        ],
        out_specs=pl.BlockSpec((bm, D_MODEL // 2), rows),
        scratch_shapes=[
            pltpu.VMEM((D_MODEL, D_FF), BF16),
            pltpu.VMEM((D_MODEL, D_FF), BF16),
            pltpu.VMEM((D_FF, D_MODEL), BF16),
        ],
    )
    return pl.pallas_call(
        _moe_dense_kernel,
        grid_spec=grid_spec,
        out_shape=jax.ShapeDtypeStruct(x_sorted.shape, jnp.int32),
        compiler_params=pltpu.CompilerParams(dimension_semantics=("arbitrary",)),
        name="moe_experts",
    )(block_e, n_valid, x_sorted, w_gate_up, w_down, b_gate, b_up, b_down, perm)


def _combine_kernel(h_ref, gate_ref, y0_ref, y1_ref, y2_ref, y3_ref, o_ref):
    gates = gate_ref[...]
    out = h_ref[...]
    for kk, y_ref in enumerate((y0_ref, y1_ref, y2_ref, y3_ref)):
        out = out + gates[:, kk:kk + 1] * _unpack_rows(y_ref[...])
    o_ref[...] = out


def _combine(h, gates, y, tm):
    n_tok = h.shape[0]
    nt = n_tok // tm
    row = lambda i: (i, 0)
    return pl.pallas_call(
        _combine_kernel,
        grid=(nt,),
        in_specs=[pl.BlockSpec((tm, D_MODEL), row), pl.BlockSpec((tm, LANES), row)]
        + [pl.BlockSpec((tm, D_MODEL // 2), functools.partial(lambda kk, i: (kk * nt + i, 0), kk))
           for kk in range(TOP_K)],
        out_specs=pl.BlockSpec((tm, D_MODEL), row),
        out_shape=jax.ShapeDtypeStruct((n_tok, D_MODEL), F32),
        compiler_params=pltpu.CompilerParams(dimension_semantics=("parallel",)),
        name="moe_combine",
    )(h, gates, y, y, y, y)


def kernel(x, meta_tokens, norm_mix_w, w_in, q_norm_w, k_norm_w, kv_norm_w, w_kv_up, conv_w, a_log,
           dt_bias, delta_norm_w, w_out, norm_ffn_w, w_router, b_router, w_gate_up, b_gate_up,
           w_down, b_down):
    b, s, d = x.shape
    consts = _pack_in_proj_weights(norm_mix_w[0], w_in[0], q_norm_w[0], k_norm_w[0], kv_norm_w[0],
                                   w_kv_up[0])
    cos, sin = _rope_tables(N_META + s)
    real = _in_proj(x.reshape(b * s, d), cos[N_META:], sin[N_META:], consts, VT_TILE)
    meta = _in_proj(meta_tokens, cos[:N_META], sin[:N_META], consts, N_META)
    vt = real[2]
    q, k, iq, ik, misc, qkv, z = (a.reshape(b, s, a.shape[-1]) for a in real[:2] + real[3:])
    km = jnp.pad(meta[1], ((0, LANES - N_META), (0, 0)))
    vmt = jnp.pad(meta[2][0], ((0, 0), (0, LANES - N_META)))
    o_a = _sparse_attention(q, k, vt, km, vmt, iq, ik, misc,
                            tq=LANES, kc=min(ATT_KC, s), topk=min(INDEX_TOPK, s // 4))
    lead = lambda a: jnp.pad(a, ((CHUNK - N_META, 0), (0, 0)))
    o_b = _gated_deltanet(qkv, z, misc, lead(meta[6]), lead(meta[5]), conv_w[0], a_log[0],
                          dt_bias[0], delta_norm_w[0])
    n_tok = b * s
    h, xn, gates, eid = _out_router(o_a.reshape(n_tok, A_WIDTH), o_b.reshape(n_tok, B_WIDTH),
                                    x.reshape(n_tok, d), w_out[0], norm_ffn_w[0], w_router[0],
                                    b_router[0], TOKEN_TILE)
    block_e, n_valid, row_tok, pos = _route_blocks(eid[:TOP_K], MOE_BM)
    bgu = b_gate_up[0].reshape(N_EXPERTS, 1, 2 * D_FF)
    x_sorted = _sc_gather_rows(xn, row_tok)
    y_sorted = _moe_dense(x_sorted, block_e, n_valid, w_gate_up[0], w_down[0], bgu[:, :, 0::2],
                          bgu[:, :, 1::2], b_down[0].reshape(N_EXPERTS, 1, D_MODEL), MOE_BM)
    y = _sc_gather_rows(y_sorted, pos)
    out = _combine(h, gates, y, min(COMBINE_TILE, n_tok))
    return out.reshape(b, s, d)
```

```python
import functools
import math

import jax
import jax.numpy as jnp
from jax import lax
from jax.experimental import pallas as pl
from jax.experimental.pallas import tpu as pltpu
from jax.experimental.pallas import tpu_sc as plsc

F32 = jnp.float32
BF16 = jnp.bfloat16
HIGHEST = lax.Precision.HIGHEST

D_MODEL = 1024
N_META = 16
ROPE_THETA = 10000.0
EPS = 1e-6
A_HEAD_DIM = 64
A_HEADS = 8
A_WIDTH = A_HEADS * A_HEAD_DIM
KV_RANK = 256
IDX_HEADS = 8
IDX_DIM = 64
INDEX_TOPK = 256
B_HEAD_DIM = 128
B_HEADS = 4
B_WIDTH = B_HEADS * B_HEAD_DIM
CONV_WIDTH = 4
CHUNK = 64
N_EXPERTS = 32
TOP_K = 4
D_FF = D_MODEL
SWIGLU_LIMIT = 7.0
SWIGLU_ALPHA = 1.702
IN_SPLITS = (A_WIDTH, KV_RANK, IDX_HEADS * IDX_DIM, IDX_DIM, IDX_HEADS, 3 * B_WIDTH, B_WIDTH,
             B_HEADS, B_HEADS)

LANES = 128

C_Q = 0
C_CKV = C_Q + A_WIDTH
C_IQ = C_CKV + KV_RANK
C_IK = C_IQ + IDX_HEADS * IDX_DIM
C_MISC = C_IK + LANES
C_QKV = C_MISC + LANES
C_Z = C_QKV + 3 * B_WIDTH
C_END = C_Z + B_WIDTH
MISC_W, MISC_BETA, MISC_DECAY = 0, IDX_HEADS, IDX_HEADS + B_HEADS
LOG2E = math.log2(math.e)
VT_ROWS = A_HEAD_DIM + 16


def _rope_partner(a):
    lane = lax.broadcasted_iota(jnp.int32, a.shape, 1)
    first_half = (lane % A_HEAD_DIM) < (A_HEAD_DIM // 2)
    return jnp.where(first_half, pltpu.roll(a, LANES - A_HEAD_DIM // 2, 1),
                     pltpu.roll(a, A_HEAD_DIM // 2, 1))


def _rope(a, cos, sin_signed):
    return a * cos + _rope_partner(a) * sin_signed


def _head_rms(a, head_mean, gain):
    msq = jnp.dot((a * a).astype(BF16), head_mean, preferred_element_type=F32)
    return a * lax.rsqrt(msq + EPS) * gain


def _in_proj_kernel(x_ref, nw_ref, w_ref, wkv_ref, qnw_ref, knw_ref, kvnw_ref, cos_ref, sin_ref,
                    hm_ref, q_ref, k_ref, vt_ref, iq_ref, ik_ref, misc_ref, qkv_ref, z_ref):
    x = x_ref[...]
    u = x * lax.rsqrt(jnp.mean(x * x, axis=-1, keepdims=True) + EPS) * nw_ref[...]
    ub = u.astype(BF16)

    def proj(c0, c1):
        return jnp.dot(ub, w_ref[:, c0:c1], preferred_element_type=F32)

    cos = cos_ref[...]
    sin = sin_ref[...]
    hm = hm_ref[...]

    def rope_groups(a):
        return jnp.concatenate(
            [_rope(a[:, g * LANES:(g + 1) * LANES], cos, sin) for g in range(a.shape[1] // LANES)],
            axis=1)

    q = _head_rms(proj(C_Q, C_CKV), hm, qnw_ref[...])
    q_ref[...] = (rope_groups(q) * (A_HEAD_DIM ** -0.5 * LOG2E)).astype(BF16)

    ckv = proj(C_CKV, C_IQ)
    ckv = ckv * lax.rsqrt(jnp.mean(ckv * ckv, axis=-1, keepdims=True) + EPS) * kvnw_ref[...]
    kv = jnp.dot(ckv.astype(BF16), wkv_ref[...], preferred_element_type=F32)
    k = _head_rms(kv[:, :A_WIDTH], hm, knw_ref[...])
    k_ref[...] = rope_groups(k).astype(BF16)
    tm = x.shape[0]
    ones = jnp.ones((VT_ROWS - A_HEAD_DIM, tm), BF16)
    for g in range(A_WIDTH // LANES):
        vt = kv[:, A_WIDTH + g * LANES:A_WIDTH + (g + 1) * LANES].T.astype(BF16)
        for half in range(2):
            r0 = (2 * g + half) * VT_ROWS
            vt_ref[0, r0:r0 + A_HEAD_DIM, :] = vt[half * A_HEAD_DIM:(half + 1) * A_HEAD_DIM, :]
            vt_ref[0, r0 + A_HEAD_DIM:r0 + VT_ROWS, :] = ones

    iq_ref[...] = rope_groups(proj(C_IQ, C_IK)).astype(BF16)
    ik_ref[...] = _rope(proj(C_IK, C_MISC), cos, sin).astype(BF16)

    lane = lax.broadcasted_iota(jnp.int32, (1, LANES), 1)
    w_scale = jnp.where(lane < IDX_HEADS, IDX_HEADS ** -0.5 * IDX_DIM ** -0.5, 1.0)
    misc_ref[...] = proj(C_MISC, C_QKV) * w_scale
    qkv_ref[...] = proj(C_QKV, C_Z)
    z_ref[...] = proj(C_Z, C_END)


def _in_proj(x2d, cos, sin, consts, tm):
    n = x2d.shape[0]
    n_pos_blocks = cos.shape[0] // tm
    nw, w_pack, wkv, qnw, knw, kvnw, hm = consts
    row = lambda i: (i, 0)
    fixed = lambda i: (0, 0)
    pos = lambda i: (i % n_pos_blocks, 0)
    out_widths = (A_WIDTH, A_WIDTH, None, IDX_HEADS * IDX_DIM, LANES, LANES, 3 * B_WIDTH, B_WIDTH)
    out_dtypes = (BF16, BF16, BF16, BF16, BF16, F32, F32, F32)
    vt_rows = A_HEADS * VT_ROWS
    out_specs = [pl.BlockSpec((1, vt_rows, tm), lambda i: (i, 0, 0)) if w is None
                 else pl.BlockSpec((tm, w), row) for w in out_widths]
    out_shape = [jax.ShapeDtypeStruct((n // tm, vt_rows, tm) if w is None else (n, w), dt)
                 for w, dt in zip(out_widths, out_dtypes)]
    return pl.pallas_call(
        _in_proj_kernel,
        grid=(n // tm,),
        in_specs=[
            pl.BlockSpec((tm, D_MODEL), row),
            pl.BlockSpec(nw.shape, fixed),
            pl.BlockSpec(w_pack.shape, fixed),
            pl.BlockSpec(wkv.shape, fixed),
            pl.BlockSpec(qnw.shape, fixed),
            pl.BlockSpec(knw.shape, fixed),
            pl.BlockSpec(kvnw.shape, fixed),
            pl.BlockSpec((tm, LANES), pos),
            pl.BlockSpec((tm, LANES), pos),
            pl.BlockSpec(hm.shape, fixed),
        ],
        out_specs=out_specs,
        out_shape=out_shape,
        compiler_params=pltpu.CompilerParams(dimension_semantics=("parallel",)),
        name="in_proj",
    )(x2d, nw, w_pack, wkv, qnw, knw, kvnw, cos, sin, hm)


def _pack_in_proj_weights(norm_mix_w, w_in, q_norm_w, k_norm_w, kv_norm_w, w_kv_up):
    points = []
    acc = 0
    for s in IN_SPLITS:
        points.append((acc, acc + s))
        acc += s
    a_q, a_ckv, i_q, i_k, i_w, b_qkv, b_z, b_beta, b_a = (w_in[:, a:b] for a, b in points)
    misc = jnp.concatenate(
        [i_w, b_beta, b_a, jnp.zeros((D_MODEL, LANES - IDX_HEADS - 2 * B_HEADS), w_in.dtype)], axis=1)
    w_pack = jnp.concatenate([a_q, a_ckv, i_q, i_k, i_k, misc, b_qkv, b_z], axis=1).astype(BF16)
    head_mean = jnp.kron(jnp.eye(A_HEADS, dtype=F32),
                         jnp.full((A_HEAD_DIM, A_HEAD_DIM), 1.0 / A_HEAD_DIM, F32)).astype(BF16)
    return (norm_mix_w.reshape(1, D_MODEL), w_pack, w_kv_up.astype(BF16),
            jnp.tile(q_norm_w, A_HEADS).reshape(1, A_WIDTH),
            jnp.tile(k_norm_w, A_HEADS).reshape(1, A_WIDTH),
            kv_norm_w.reshape(1, KV_RANK), head_mean)


def _rope_tables(n_pos):
    half = A_HEAD_DIM // 2
    inv_freq = ROPE_THETA ** (-jnp.arange(0, A_HEAD_DIM, 2, dtype=F32) / A_HEAD_DIM)
    ang = jnp.arange(n_pos, dtype=F32)[:, None] * inv_freq[None, :]
    cos, sin = jnp.cos(ang), jnp.sin(ang)
    cos128 = jnp.tile(cos, (1, LANES // half))
    sin128 = jnp.tile(jnp.concatenate([-sin, sin], axis=1), (1, LANES // A_HEAD_DIM))
    return cos128, sin128


NEG_INF = float("-inf")
F32_MAX = float(jnp.finfo(jnp.float32).max)
INT_MIN = -2 ** 31


VT_TILE = 512
SUBLANES = 8


def _attn_kernel(q_ref, k_ref, vt_ref, km_ref, vmt_ref, iq_ref, ik_ref, misc_ref, o_ref,
                 isc_ref, lhs_ref, qm_ref, s_ref, m_ref, acc_ref, *, tq, kc, topk, pos_bits):
    j = pl.program_id(1)
    n_kc = lax.div((j + 1) * tq + (kc - 1), kc)
    g = kc // SUBLANES
    lane = lax.broadcasted_iota(jnp.int32, (1, LANES), 1)
    lo_half = lane < A_HEAD_DIM
    n_pairs = A_WIDTH // LANES
    nt = (((1,), (1,)), ((), ()))

    iq = iq_ref[0]
    q = q_ref[0]
    zero = jnp.zeros((), BF16)
    for p in range(n_pairs):
        blk = iq[:, p * LANES:(p + 1) * LANES]
        lhs_ref[(2 * p) * tq:(2 * p + 1) * tq, :] = jnp.where(lo_half, blk, zero)
        lhs_ref[(2 * p + 1) * tq:(2 * p + 2) * tq, :] = jnp.where(lo_half, zero, blk)
        qb = q[:, p * LANES:(p + 1) * LANES]
        qm_ref[p, :tq, :] = jnp.where(lo_half, qb, zero)
        qm_ref[p, tq:, :] = jnp.where(lo_half, zero, qb)
    w_t = misc_ref[0].T

    def fold0(x3, op):
        n = x3.shape[0]
        while n > 1:
            x3 = op(x3[:n // 2], x3[n // 2:n])
            n //= 2
        return x3[0]

    q_pos = j * tq + lax.broadcasted_iota(jnp.int32, (1, tq), 1)

    def idx_body(c, carry):
        k0 = pl.multiple_of(c * kc, kc)
        r = lax.dot_general(ik_ref[0, pl.ds(k0, kc), :], lhs_ref[...], nt,
                            preferred_element_type=F32)
        s = None
        for h in range(IDX_HEADS):
            term = jnp.maximum(r[:, h * tq:(h + 1) * tq], 0.0) * w_t[MISC_W + h:MISC_W + h + 1, :]
            s = term if s is None else s + term
        k_pos = k0 + lax.broadcasted_iota(jnp.int32, (kc, 1), 0)
        isc_ref[c] = jnp.where(k_pos <= q_pos, s, NEG_INF)
        return carry
    lax.fori_loop(0, n_kc, idx_body, 0)

    def count(pred):
        def body(c, acc):
            x3 = isc_ref[c].reshape(g, SUBLANES, tq)
            return acc + fold0(jnp.where(pred(x3, c), 1.0, 0.0), jnp.add)
        acc = lax.fori_loop(0, n_kc, body, jnp.zeros((SUBLANES, tq), F32))
        return jnp.broadcast_to(jnp.sum(acc, axis=0, keepdims=True), (SUBLANES, tq))

    def key_to_float(u):
        key = u ^ jnp.int32(INT_MIN)
        bits = jnp.where(key >= 0, key, key ^ jnp.int32(0x7FFFFFFF))
        return lax.bitcast_convert_type(bits, F32)

    def bit_body(i, carry):
        u, n_u = carry
        u2 = u | lax.shift_left(jnp.int32(1), 31 - i)
        cand = key_to_float(u2)
        n_ge = count(lambda x3, c: x3 >= cand[None])
        keep = n_ge >= topk
        return jnp.where(keep, u2, u), jnp.where(keep, n_ge, n_u)
    u, n_u = lax.fori_loop(0, 32, bit_body, (jnp.zeros((SUBLANES, tq), jnp.int32),
                                             jnp.zeros((SUBLANES, tq), F32)))
    few = (u >= 0) & (u < 0x00800000)
    tau = jnp.where(few, -F32_MAX, key_to_float(u))

    n_ge = jnp.where(few, 0.0, n_u)

    @pl.when(jnp.max(n_ge) > topk)
    def _():
        n_gt = count(lambda x3, c: x3 > tau[None])
        need = topk - n_gt
        def chunk_pos(c):
            return (c * kc + lax.broadcasted_iota(jnp.int32, (g, SUBLANES, tq), 0) * SUBLANES
                    + lax.broadcasted_iota(jnp.int32, (g, SUBLANES, tq), 1))
        def pos_body(i, cut):
            cut2 = cut | lax.shift_left(jnp.int32(1), pos_bits - 1 - i)
            ties_before = count(lambda x3, c: (x3 == tau[None]) & (chunk_pos(c) < cut2[None]))
            return jnp.where(ties_before < need, cut2, cut)
        cut = lax.fori_loop(0, pos_bits, pos_body, jnp.zeros((SUBLANES, tq), jnp.int32))
        def drop_body(c, carry):
            x3 = isc_ref[c].reshape(g, SUBLANES, tq)
            drop = (x3 == tau[None]) & (chunk_pos(c) > cut[None])
            isc_ref[c] = jnp.where(drop, NEG_INF, x3).reshape(kc, tq)
            return carry
        lax.fori_loop(0, n_kc, drop_body, 0)

    tau_row = tau[0:1, :]

    def attend(key_pairs, bias, vt_slabs, first):
        n = key_pairs[0].shape[0]
        m_cur = []
        for p in range(n_pairs):
            s2 = lax.dot_general(key_pairs[p], qm_ref[p], nt, preferred_element_type=F32)
            for half in range(2):
                s = s2[:, half * tq:(half + 1) * tq] + bias
                s_ref[2 * p + half, :n, :] = s
                m8 = fold0(s.reshape(n // SUBLANES, SUBLANES, tq), jnp.maximum)
                m_cur.append(jnp.max(m8, axis=0, keepdims=True))
        for hd in range(A_HEADS):
            m_new = m_cur[hd] if first else jnp.maximum(m_ref[hd][0:1, :], m_cur[hd])
            e = jnp.exp2(s_ref[hd, :n, :] - m_new).astype(BF16)
            pv = None
            off = 0
            for slab in vt_slabs(hd):
                part = jnp.dot(slab, e[off:off + slab.shape[1], :], preferred_element_type=F32)
                pv = part if pv is None else pv + part
                off += slab.shape[1]
            if first:
                acc_ref[hd] = pv
            else:
                acc_ref[hd] = jnp.exp2(m_ref[hd][0:1, :] - m_new) * acc_ref[hd] + pv
            m_ref[hd] = jnp.broadcast_to(m_new, (SUBLANES, tq))

    meta_bias = jnp.where(lax.broadcasted_iota(jnp.int32, (LANES, 1), 0) < N_META, 0.0, NEG_INF)
    attend([km_ref[:, p * LANES:(p + 1) * LANES] for p in range(n_pairs)], meta_bias,
           lambda hd: [vmt_ref[hd * VT_ROWS:(hd + 1) * VT_ROWS, :]], True)

    def att_body(c, carry):
        k0 = pl.multiple_of(c * kc, kc)
        bias = jnp.where(isc_ref[c] >= tau_row, 0.0, NEG_INF)
        attend([k_ref[0, pl.ds(k0, kc), p * LANES:(p + 1) * LANES] for p in range(n_pairs)], bias,
               lambda hd: [vt_ref[c * (kc // VT_TILE) + t, hd * VT_ROWS:(hd + 1) * VT_ROWS, :]
                           for t in range(kc // VT_TILE)], False)
        return carry
    lax.fori_loop(0, n_kc, att_body, 0)

    for p in range(n_pairs):
        halves = []
        for hd in (2 * p, 2 * p + 1):
            acc = acc_ref[hd]
            halves.append(acc[:A_HEAD_DIM, :] / acc[A_HEAD_DIM:A_HEAD_DIM + 1, :])
        o_ref[0, :, p * LANES:(p + 1) * LANES] = jnp.concatenate(halves, axis=0).T.astype(BF16)


def _sparse_attention(q, k, vt, km, vmt, iq, ik, misc, *, tq, kc, topk):
    b, s, _ = q.shape
    assert tq % LANES == 0 and s % tq == 0 and kc % VT_TILE == 0 and s % kc == 0
    kernel = functools.partial(_attn_kernel, tq=tq, kc=kc, topk=topk, pos_bits=int(math.log2(s)))
    qblk = lambda bi, j: (bi, j, 0)
    full = lambda bi, j: (bi, 0, 0)
    fixed = lambda bi, j: (0, 0)
    vt_rows = A_HEADS * VT_ROWS
    return pl.pallas_call(
        kernel,
        grid=(b, s // tq),
        in_specs=[
            pl.BlockSpec((1, tq, A_WIDTH), qblk),
            pl.BlockSpec((1, s, A_WIDTH), full),
            pl.BlockSpec((s // VT_TILE, vt_rows, VT_TILE), full),
            pl.BlockSpec(km.shape, fixed),
            pl.BlockSpec(vmt.shape, fixed),
            pl.BlockSpec((1, tq, IDX_HEADS * IDX_DIM), qblk),
            pl.BlockSpec((1, s, LANES), full),
            pl.BlockSpec((1, tq, LANES), qblk),
        ],
        out_specs=pl.BlockSpec((1, tq, A_WIDTH), qblk),
        out_shape=jax.ShapeDtypeStruct((b, s, A_WIDTH), BF16),
        scratch_shapes=[
            pltpu.VMEM((s // kc, kc, tq), F32),
            pltpu.VMEM((IDX_HEADS * tq, LANES), BF16),
            pltpu.VMEM((A_WIDTH // LANES, 2 * tq, LANES), BF16),
            pltpu.VMEM((A_HEADS, max(kc, LANES), tq), F32),
            pltpu.VMEM((A_HEADS, SUBLANES, tq), F32),
            pltpu.VMEM((A_HEADS, VT_ROWS, tq), F32),
        ],
        compiler_params=pltpu.CompilerParams(dimension_semantics=("parallel", "arbitrary")),
        name="sparse_attention",
    )(q, k, vt, km, vmt, iq, ik, misc)


HALO = 8


def _softplus(x):
    return jnp.maximum(x, 0.0) + jnp.log1p(jnp.exp(-jnp.abs(x)))


def _bdot(a, b):
    return jnp.dot(a.astype(BF16), b.astype(BF16), preferred_element_type=F32)


def _hdot(a, b):
    return jnp.dot(a, b, precision=HIGHEST, preferred_element_type=F32)


def _dot3(a, b):
    a_hi = a.astype(BF16)
    b_hi = b.astype(BF16)
    a_lo = (a - a_hi.astype(F32)).astype(BF16)
    b_lo = (b - b_hi.astype(F32)).astype(BF16)
    dot = functools.partial(jnp.dot, preferred_element_type=F32)
    return dot(a_hi, b_hi) + (dot(a_hi, b_lo) + dot(a_lo, b_hi))


def _delta_kernel(qkv_ref, z_ref, misc_ref, qkvm_ref, miscm_ref, convw_ref, alog_ref, dtb_ref,
                  nw_ref, o_ref, state_ref, halo_ref):
    c = pl.program_id(1)
    n_pad = CHUNK - N_META
    nt = (((1,), (1,)), ((), ()))
    tn = (((0,), (0,)), ((), ()))

    @pl.when(c == 0)
    def _():
        state_ref[...] = jnp.zeros_like(state_ref)
        halo_ref[...] = jnp.zeros_like(halo_ref)

    ri = lax.broadcasted_iota(jnp.int32, (CHUNK, CHUNK), 0)
    ci = lax.broadcasted_iota(jnp.int32, (CHUNK, CHUNK), 1)
    incl = ri >= ci
    strict = ri > ci
    eye = (ri == ci).astype(F32)
    tri = incl.astype(F32)
    row = lax.broadcasted_iota(jnp.int32, (CHUNK, 1), 0)
    is_meta = c == 0
    neutral = jnp.logical_and(is_meta, row < n_pad)
    cols = lambda base, h: slice(base + h * B_HEAD_DIM, base + (h + 1) * B_HEAD_DIM)

    nb = qkv_ref.shape[0]
    heads = range(nb * B_HEADS)
    qn, kn, vb, kb, gc, decay = [], [], [], [], [], []
    for bi in range(nb):
        xin = jnp.where(is_meta, qkvm_ref[...], qkv_ref[bi])
        misc = jnp.where(is_meta, miscm_ref[...], misc_ref[bi])
        xcat = jnp.concatenate([halo_ref[bi], xin], axis=0)
        halo_ref[bi] = xin[CHUNK - HALO:, :]
        conv = None
        for tap in range(CONV_WIDTH):
            off = HALO - (CONV_WIDTH - 1) + tap
            term = xcat[off:off + CHUNK, :] * convw_ref[tap:tap + 1, :]
            conv = term if conv is None else conv + term
        xc = conv * jax.nn.sigmoid(conv)
        beta_all = jnp.where(neutral, 0.0, jax.nn.sigmoid(misc))
        g_all = jnp.where(neutral, 0.0, -jnp.exp(alog_ref[...]) * _softplus(misc + dtb_ref[...]))
        gc_all = _hdot(tri, g_all)
        gc_all_t = gc_all.T
        for h in range(B_HEADS):
            qh, kh, vh = xc[:, cols(0, h)], xc[:, cols(B_WIDTH, h)], xc[:, cols(2 * B_WIDTH, h)]
            qn.append(qh * lax.rsqrt(jnp.sum(qh * qh, axis=-1, keepdims=True) + EPS) * (B_HEAD_DIM ** -0.5))
            kn.append(kh * lax.rsqrt(jnp.sum(kh * kh, axis=-1, keepdims=True) + EPS))
            beta = jnp.broadcast_to(beta_all[:, MISC_BETA + h:MISC_BETA + h + 1], (CHUNK, B_HEAD_DIM))
            gc.append(jnp.broadcast_to(gc_all[:, MISC_DECAY + h:MISC_DECAY + h + 1], (CHUNK, B_HEAD_DIM)))
            g_col = gc[-1][:, :CHUNK]
            g_row = gc_all_t[MISC_DECAY + h:MISC_DECAY + h + 1, :]
            decay.append(jnp.where(incl, jnp.exp(jnp.where(incl, g_col - g_row, 0.0)), 0.0))
            kb.append(kn[-1] * beta)
            vb.append(vh * beta)

    kk = [lax.dot_general(kb[h].astype(BF16), kn[h].astype(BF16), nt, preferred_element_type=F32)
          for h in heads]
    qk = [lax.dot_general(qn[h].astype(BF16), kn[h].astype(BF16), nt, preferred_element_type=F32)
          for h in heads]
    intra = [jnp.where(incl, qk[h] * decay[h], 0.0) for h in heads]

    pw = [jnp.where(strict, -(kk[h] * decay[h]), 0.0) for h in heads]
    t_inv = [eye + pw[h] for h in heads]
    pw = [_dot3(pw[h], pw[h]) for h in heads]
    for _ in range(int(math.log2(CHUNK)) - 2):
        t_next = [t_inv[h] + _dot3(t_inv[h], pw[h]) for h in heads]
        pw = [_dot3(pw[h], pw[h]) for h in heads]
        t_inv = t_next
    t_inv = [t_inv[h] + _dot3(t_inv[h], pw[h]) for h in heads]

    u = [_bdot(t_inv[h], vb[h]) for h in heads]
    w = [_bdot(t_inv[h], kb[h] * jnp.exp(gc[h])) for h in heads]

    state = [state_ref[h] for h in heads]
    w_s = [_bdot(w[h], state[h]) for h in heads]
    q_s = [_bdot(qn[h] * jnp.exp(gc[h]), state[h]) for h in heads]
    v_new = [u[h] - w_s[h] for h in heads]
    o = [q_s[h] + _bdot(intra[h], v_new[h]) for h in heads]
    for h in heads:
        g_last = gc[h][CHUNK - 1:CHUNK, :]
        kd = kn[h] * jnp.exp(g_last - gc[h])
        state_ref[h] = state[h] * jnp.exp(g_last) + lax.dot_general(
            kd.astype(BF16), v_new[h].astype(BF16), tn, preferred_element_type=F32)

    for h in heads:
        bi, hh = divmod(h, B_HEADS)
        y = o[h] * lax.rsqrt(jnp.mean(o[h] * o[h], axis=-1, keepdims=True) + EPS) * nw_ref[...]
        zh = z_ref[bi, :, cols(0, hh)]
        o_ref[bi, :, cols(0, hh)] = (y * (zh * jax.nn.sigmoid(zh))).astype(BF16)


DELTA_BATCH = 4


def _gated_deltanet(qkv, z, misc, qkv_m, misc_m, conv_w, a_log, dt_bias, norm_w):
    b, s, _ = qkv.shape
    n_chunks = s // CHUNK + 1
    nb = DELTA_BATCH if b % DELTA_BATCH == 0 else 1
    blk = lambda bi, c: (bi, jnp.maximum(c - 1, 0), 0)
    fixed = lambda bi, c: (0, 0)
    lane_vec = lambda v: jnp.zeros((1, LANES), F32).at[0, MISC_DECAY:MISC_DECAY + B_HEADS].set(v)
    return pl.pallas_call(
        _delta_kernel,
        grid=(b // nb, n_chunks),
        in_specs=[
            pl.BlockSpec((nb, CHUNK, 3 * B_WIDTH), blk),
            pl.BlockSpec((nb, CHUNK, B_WIDTH), blk),
            pl.BlockSpec((nb, CHUNK, LANES), blk),
            pl.BlockSpec((CHUNK, 3 * B_WIDTH), fixed),
            pl.BlockSpec((CHUNK, LANES), fixed),
            pl.BlockSpec((CONV_WIDTH, 3 * B_WIDTH), fixed),
            pl.BlockSpec((1, LANES), fixed),
            pl.BlockSpec((1, LANES), fixed),
            pl.BlockSpec((1, B_HEAD_DIM), fixed),
        ],
        out_specs=pl.BlockSpec((nb, CHUNK, B_WIDTH), blk),
        out_shape=jax.ShapeDtypeStruct((b, s, B_WIDTH), BF16),
        scratch_shapes=[
            pltpu.VMEM((nb * B_HEADS, B_HEAD_DIM, B_HEAD_DIM), F32),
            pltpu.VMEM((nb, HALO, 3 * B_WIDTH), F32),
        ],
        compiler_params=pltpu.CompilerParams(dimension_semantics=("parallel", "arbitrary")),
        name="gated_deltanet",
    )(qkv, z, misc, qkv_m, misc_m, conv_w, lane_vec(a_log), lane_vec(dt_bias),
      norm_w.reshape(1, B_HEAD_DIM))


def _pack_rows(a):
    half = a.shape[1] // 2
    bits = lax.bitcast_convert_type(a.astype(BF16).astype(F32), jnp.int32)
    return bits[:, :half] | lax.shift_right_logical(bits[:, half:], 16)


def _unpack_rows(w):
    hi = lax.bitcast_convert_type(w & jnp.int32(-65536), F32)
    lo = lax.bitcast_convert_type(lax.shift_left(w, 16), F32)
    return jnp.concatenate([hi, lo], axis=1)


def _out_router_kernel(oa_ref, ob_ref, x_ref, wo_ref, nw_ref, wr_ref, br_ref,
                       h_ref, xn_ref, gate_ref, eid_ref):
    mix = (jnp.dot(oa_ref[...], wo_ref[:A_WIDTH, :], preferred_element_type=F32)
           + jnp.dot(ob_ref[...], wo_ref[A_WIDTH:, :], preferred_element_type=F32))
    h = x_ref[...] + mix
    h_ref[...] = h
    xn = h * lax.rsqrt(jnp.mean(h * h, axis=-1, keepdims=True) + EPS) * nw_ref[...]
    xn_ref[...] = _pack_rows(xn)
    tm = xn.shape[0]
    logits = lax.dot_general(wr_ref[...], xn, (((1,), (1,)), ((), ())), precision=HIGHEST,
                             preferred_element_type=F32) + br_ref[...]
    e_idx = lax.broadcasted_iota(jnp.int32, (N_EXPERTS, tm), 0).astype(F32)
    work = logits
    vals, idxs = [], []
    for _ in range(TOP_K):
        m = jnp.max(work, axis=0, keepdims=True)
        idx = jnp.min(jnp.where(work == m, e_idx, float(N_EXPERTS)), axis=0, keepdims=True)
        vals.append(m)
        idxs.append(idx)
        work = jnp.where(e_idx == idx, NEG_INF, work)
    exps = [jnp.exp(v - vals[0]) for v in vals]
    denom = exps[0]
    for e in exps[1:]:
        denom = denom + e
    eid_ref[...] = jnp.concatenate(idxs + [jnp.zeros((SUBLANES - TOP_K, tm), F32)], axis=0).astype(jnp.int32)
    gates_t = jnp.concatenate([e / denom for e in exps] + [jnp.zeros((LANES - TOP_K, tm), F32)], axis=0)
    gate_ref[...] = gates_t.T


def _out_router(o_a, o_b, x2d, w_out, norm_w, w_router, b_router, tm):
    n = x2d.shape[0]
    row = lambda i: (i, 0)
    fixed = lambda i: (0, 0)
    wr = w_router.T
    br = b_router.reshape(N_EXPERTS, 1)
    return pl.pallas_call(
        _out_router_kernel,
        grid=(n // tm,),
        in_specs=[
            pl.BlockSpec((tm, A_WIDTH), row),
            pl.BlockSpec((tm, B_WIDTH), row),
            pl.BlockSpec((tm, D_MODEL), row),
            pl.BlockSpec((A_WIDTH + B_WIDTH, D_MODEL), fixed),
            pl.BlockSpec((1, D_MODEL), fixed),
            pl.BlockSpec((N_EXPERTS, D_MODEL), fixed),
            pl.BlockSpec((N_EXPERTS, 1), fixed),
        ],
        out_specs=[pl.BlockSpec((tm, D_MODEL), row), pl.BlockSpec((tm, D_MODEL // 2), row),
                   pl.BlockSpec((tm, LANES), row), pl.BlockSpec((SUBLANES, tm), lambda i: (0, i))],
        out_shape=[jax.ShapeDtypeStruct((n, D_MODEL), F32), jax.ShapeDtypeStruct((n, D_MODEL // 2), jnp.int32),
                   jax.ShapeDtypeStruct((n, LANES), F32), jax.ShapeDtypeStruct((SUBLANES, n), jnp.int32)],
        compiler_params=pltpu.CompilerParams(dimension_semantics=("parallel",)),
        name="out_proj_router",
    )(o_a, o_b, x2d, w_out.astype(BF16), norm_w.reshape(1, D_MODEL), wr, br)


MOE_BM = 512
TOKEN_TILE = 512
COMBINE_TILE = 1024
ATT_KC = 512
ATT_TQ = 256
MOE_BF = 512


SC_CORES = 2
SC_SUBCORES = 16
SC_ROWS = 128


def _sc_gather_rows(table, idx):
    n_workers = SC_CORES * SC_SUBCORES
    n_rows = idx.shape[0]
    d = table.shape[1]
    assert n_rows % (n_workers * SC_ROWS) == 0
    rows_per_worker = n_rows // n_workers
    mesh = plsc.VectorSubcoreMesh(core_axis_name="c", subcore_axis_name="s",
                                  num_cores=SC_CORES, num_subcores=SC_SUBCORES)

    @functools.partial(
        pl.kernel, mesh=mesh,
        out_type=jax.ShapeDtypeStruct((n_rows, d), table.dtype),
        scratch_types=[pltpu.VMEM((SC_ROWS,), jnp.int32), pltpu.VMEM((SC_ROWS, d), table.dtype),
                       pltpu.SemaphoreType.DMA],
        name="sc_gather_rows",
    )
    def gather(table_hbm, idx_hbm, out_hbm, idx_v, rows_v, sem):
        wid = lax.axis_index("s") * SC_CORES + lax.axis_index("c")
        base = wid * rows_per_worker

        @pl.loop(0, rows_per_worker // SC_ROWS)
        def _(i):
            off = base + i * SC_ROWS
            pltpu.sync_copy(idx_hbm.at[pl.ds(off, SC_ROWS)], idx_v)
            pltpu.async_copy(table_hbm.at[idx_v], rows_v, sem).wait()
            pltpu.sync_copy(rows_v, out_hbm.at[pl.ds(off, SC_ROWS)])

    return gather(table, idx)


def _route_blocks(eid_t, bm):
    n_tok = eid_t.shape[1]
    n_assign = n_tok * TOP_K
    experts = jnp.arange(N_EXPERTS, dtype=jnp.int32)
    flat_e = eid_t.reshape(-1)
    sorted_e, order = lax.sort((flat_e, jnp.arange(n_assign, dtype=jnp.int32)), num_keys=1)
    onehot = sorted_e[:, None] == experts[None, :]
    counts = jnp.sum(onehot, axis=0, dtype=jnp.int32)
    padded = (counts + bm - 1) // bm * bm
    start = jnp.cumsum(counts) - counts
    pend = jnp.cumsum(padded)
    pstart = pend - padded
    dest = jnp.arange(n_assign, dtype=jnp.int32) + jnp.sum(
        jnp.where(onehot, (pstart - start)[None, :], 0), axis=1)
    n_blocks = -(-(n_assign + N_EXPERTS * (bm - 1)) // bm)
    blk_start = jnp.arange(n_blocks, dtype=jnp.int32) * bm
    block_e = jnp.minimum(jnp.sum(blk_start[:, None] >= pend[None, :], axis=1), N_EXPERTS - 1)
    n_valid = pend[-1] // bm
    _, pos = lax.sort((order, dest), num_keys=1)
    t = jnp.arange(bm, dtype=jnp.int32)[None, :]
    pad_key = jnp.where(t < (padded - counts)[:, None], (pstart + counts)[:, None] + t,
                        n_blocks * bm).reshape(-1)
    pad_tok = jnp.arange(N_EXPERTS * bm, dtype=jnp.int32) % n_tok
    assert n_blocks * bm - n_assign == N_EXPERTS * bm
    _, row_tok = lax.sort((jnp.concatenate([dest, pad_key]),
                           jnp.concatenate([order % n_tok, pad_tok])), num_keys=1)
    return block_e.astype(jnp.int32), n_valid.astype(jnp.int32).reshape(1), row_tok, pos


def _moe_dense_kernel(be_ref, nv_ref, x_ref, wgu_ref, wd_ref, bg_ref, bu_ref, bd_ref, perm_ref, y_ref,
                      wg_s, wu_s, wd_s):
    i = pl.program_id(0)
    live = i < nv_ref[0]

    @pl.when(jnp.logical_and(live, jnp.logical_or(i == 0, be_ref[i] != be_ref[jnp.maximum(i - 1, 0)])))
    def _():
        perm = perm_ref[...]
        grp = 2 * LANES
        for gidx in range(2 * D_FF // grp):
            blk = wgu_ref[0, :, gidx * grp:(gidx + 1) * grp].astype(BF16)
            split = jnp.dot(blk, perm, preferred_element_type=F32).astype(BF16)
            wg_s[:, gidx * LANES:(gidx + 1) * LANES] = split[:, :LANES]
            wu_s[:, gidx * LANES:(gidx + 1) * LANES] = split[:, LANES:]
        wd_s[...] = wd_ref[0].astype(BF16)

    @pl.when(live)
    def _():
        x = _unpack_rows(x_ref[...]).astype(BF16)
        y = None
        for f0 in range(0, D_FF, MOE_BF):
            g = jnp.dot(x, wg_s[:, f0:f0 + MOE_BF], preferred_element_type=F32) + bg_ref[0, :, f0:f0 + MOE_BF]
            u = jnp.dot(x, wu_s[:, f0:f0 + MOE_BF], preferred_element_type=F32) + bu_ref[0, :, f0:f0 + MOE_BF]
            gate = jnp.minimum(g, SWIGLU_LIMIT)
            up = jnp.clip(u, -SWIGLU_LIMIT, SWIGLU_LIMIT)
            t = gate * jax.nn.sigmoid(gate * SWIGLU_ALPHA) * (up + 1.0)
            part = jnp.dot(t.astype(BF16), wd_s[f0:f0 + MOE_BF, :], preferred_element_type=F32)
            y = part if y is None else y + part
        y_ref[...] = _pack_rows(y + bd_ref[0])


def _moe_dense(x_sorted, block_e, n_valid, w_gate_up, w_down, b_gate, b_up, b_down, bm):
    n_blocks = block_e.shape[0]
    src = jnp.arange(2 * LANES)
    perm = jax.nn.one_hot((src % 2) * LANES + src // 2, 2 * LANES, dtype=BF16)
    rows = lambda i, be, nv: (jnp.minimum(i, nv[0] - 1), 0)
    wsel = lambda i, be, nv: (be[i], 0, 0)
    fixed = lambda i, be, nv: (0, 0)
    grid_spec = pltpu.PrefetchScalarGridSpec(
        num_scalar_prefetch=2,
        grid=(n_blocks,),
        in_specs=[
            pl.BlockSpec((bm, D_MODEL // 2), rows),
            pl.BlockSpec((1, D_MODEL, 2 * D_FF), wsel),
            pl.BlockSpec((1, D_FF, D_MODEL), wsel),
            pl.BlockSpec((1, 1, D_FF), wsel),
            pl.BlockSpec((1, 1, D_FF), wsel),
            pl.BlockSpec((1, 1, D_MODEL), wsel),
            pl.BlockSpec((2 * LANES, 2 * LANES), fixed),
        ],
        out_specs=pl.BlockSpec((bm, D_MODEL // 2), rows),
        scratch_shapes=[
            pltpu.VMEM((D_MODEL, D_FF), BF16),
            pltpu.VMEM((D_MODEL, D_FF), BF16),
            pltpu.VMEM((D_FF, D_MODEL), BF16),
        ],
    )
    return pl.pallas_call(
        _moe_dense_kernel,
        grid_spec=grid_spec,
        out_shape=jax.ShapeDtypeStruct(x_sorted.shape, jnp.int32),
        compiler_params=pltpu.CompilerParams(dimension_semantics=("arbitrary",)),
        name="moe_experts",
    )(block_e, n_valid, x_sorted, w_gate_up, w_down, b_gate, b_up, b_down, perm)


def _combine_kernel(h_ref, gate_ref, y0_ref, y1_ref, y2_ref, y3_ref, o_ref):
    gates = gate_ref[...]
    out = h_ref[...]
    for kk, y_ref in enumerate((y0_ref, y1_ref, y2_ref, y3_ref)):
        out = out + gates[:, kk:kk + 1] * _unpack_rows(y_ref[...])
    o_ref[...] = out


def _combine(h, gates, y, tm):
    n_tok = h.shape[0]
    nt = n_tok // tm
    row = lambda i: (i, 0)
    return pl.pallas_call(
        _combine_kernel,
        grid=(nt,),
        in_specs=[pl.BlockSpec((tm, D_MODEL), row), pl.BlockSpec((tm, LANES), row)]
        + [pl.BlockSpec((tm, D_MODEL // 2), functools.partial(lambda kk, i: (kk * nt + i, 0), kk))
           for kk in range(TOP_K)],
        out_specs=pl.BlockSpec((tm, D_MODEL), row),
        out_shape=jax.ShapeDtypeStruct((n_tok, D_MODEL), F32),
        compiler_params=pltpu.CompilerParams(dimension_semantics=("parallel",)),
        name="moe_combine",
    )(h, gates, y, y, y, y)


def kernel(x, meta_tokens, norm_mix_w, w_in, q_norm_w, k_norm_w, kv_norm_w, w_kv_up, conv_w, a_log,
           dt_bias, delta_norm_w, w_out, norm_ffn_w, w_router, b_router, w_gate_up, b_gate_up,
           w_down, b_down):
    b, s, d = x.shape
    consts = _pack_in_proj_weights(norm_mix_w[0], w_in[0], q_norm_w[0], k_norm_w[0], kv_norm_w[0],
                                   w_kv_up[0])
    cos, sin = _rope_tables(N_META + s)
    real = _in_proj(x.reshape(b * s, d), cos[N_META:], sin[N_META:], consts, VT_TILE)
    meta = _in_proj(meta_tokens, cos[:N_META], sin[:N_META], consts, N_META)
    vt = real[2]
    q, k, iq, ik, misc, qkv, z = (a.reshape(b, s, a.shape[-1]) for a in real[:2] + real[3:])
    km = jnp.pad(meta[1], ((0, LANES - N_META), (0, 0)))
    vmt = jnp.pad(meta[2][0], ((0, 0), (0, LANES - N_META)))
    o_a = _sparse_attention(q, k, vt, km, vmt, iq, ik, misc,
                            tq=ATT_TQ, kc=min(ATT_KC, s), topk=min(INDEX_TOPK, s // 4))
    lead = lambda a: jnp.pad(a, ((CHUNK - N_META, 0), (0, 0)))
    o_b = _gated_deltanet(qkv, z, misc, lead(meta[6]), lead(meta[5]), conv_w[0], a_log[0],
                          dt_bias[0], delta_norm_w[0])
    n_tok = b * s
    h, xn, gates, eid = _out_router(o_a.reshape(n_tok, A_WIDTH), o_b.reshape(n_tok, B_WIDTH),
                                    x.reshape(n_tok, d), w_out[0], norm_ffn_w[0], w_router[0],
                                    b_router[0], TOKEN_TILE)
    block_e, n_valid, row_tok, pos = _route_blocks(eid[:TOP_K], MOE_BM)
    bgu = b_gate_up[0].reshape(N_EXPERTS, 1, 2 * D_FF)
    x_sorted = _sc_gather_rows(xn, row_tok)
    y_sorted = _moe_dense(x_sorted, block_e, n_valid, w_gate_up[0], w_down[0], bgu[:, :, 0::2],
                          bgu[:, :, 1::2], b_down[0].reshape(N_EXPERTS, 1, D_MODEL), MOE_BM)
    y = _sc_gather_rows(y_sorted, pos)
    out = _combine(h, gates, y, min(COMBINE_TILE, n_tok))
    return out.reshape(b, s, d)
```

```python
import functools
import math

import jax
import jax.numpy as jnp
from jax import lax
from jax.experimental import pallas as pl
from jax.experimental.pallas import tpu as pltpu
from jax.experimental.pallas import tpu_sc as plsc

F32 = jnp.float32
BF16 = jnp.bfloat16
HIGHEST = lax.Precision.HIGHEST

D_MODEL = 1024
N_META = 16
ROPE_THETA = 10000.0
EPS = 1e-6
A_HEAD_DIM = 64
A_HEADS = 8
A_WIDTH = A_HEADS * A_HEAD_DIM
KV_RANK = 256
IDX_HEADS = 8
IDX_DIM = 64
INDEX_TOPK = 256
B_HEAD_DIM = 128
B_HEADS = 4
B_WIDTH = B_HEADS * B_HEAD_DIM
CONV_WIDTH = 4
CHUNK = 64
N_EXPERTS = 32
TOP_K = 4
D_FF = D_MODEL
SWIGLU_LIMIT = 7.0
SWIGLU_ALPHA = 1.702
IN_SPLITS = (A_WIDTH, KV_RANK, IDX_HEADS * IDX_DIM, IDX_DIM, IDX_HEADS, 3 * B_WIDTH, B_WIDTH,
             B_HEADS, B_HEADS)

LANES = 128

C_Q = 0
C_CKV = C_Q + A_WIDTH
C_IQ = C_CKV + KV_RANK
C_IK = C_IQ + IDX_HEADS * IDX_DIM
C_MISC = C_IK + LANES
C_QKV = C_MISC + LANES
C_Z = C_QKV + 3 * B_WIDTH
C_END = C_Z + B_WIDTH
MISC_W, MISC_BETA, MISC_DECAY = 0, IDX_HEADS, IDX_HEADS + B_HEADS
LOG2E = math.log2(math.e)
BF16_SUBLANES = 16
VT_ROWS = A_HEAD_DIM + BF16_SUBLANES


def _rope_partner(a):
    lane = lax.broadcasted_iota(jnp.int32, a.shape, 1)
    first_half = (lane % A_HEAD_DIM) < (A_HEAD_DIM // 2)
    return jnp.where(first_half, pltpu.roll(a, LANES - A_HEAD_DIM // 2, 1),
                     pltpu.roll(a, A_HEAD_DIM // 2, 1))


def _rope(a, cos, sin_signed):
    return a * cos + _rope_partner(a) * sin_signed


def _head_rms(a, head_mean, gain):
    msq = jnp.dot((a * a).astype(BF16), head_mean, preferred_element_type=F32)
    return a * lax.rsqrt(msq + EPS) * gain


def _in_proj_kernel(x_ref, nw_ref, w_ref, wkv_ref, qnw_ref, knw_ref, kvnw_ref, cos_ref, sin_ref,
                    hm_ref, q_ref, k_ref, vt_ref, iq_ref, ik_ref, misc_ref, qkv_ref, z_ref):
    x = x_ref[...]
    u = x * lax.rsqrt(jnp.mean(x * x, axis=-1, keepdims=True) + EPS) * nw_ref[...]
    ub = u.astype(BF16)

    def proj(c0, c1):
        return jnp.dot(ub, w_ref[:, c0:c1], preferred_element_type=F32)

    cos = cos_ref[...]
    sin = sin_ref[...]
    hm = hm_ref[...]

    def rope_groups(a):
        return jnp.concatenate(
            [_rope(a[:, g * LANES:(g + 1) * LANES], cos, sin) for g in range(a.shape[1] // LANES)],
            axis=1)

    q = _head_rms(proj(C_Q, C_CKV), hm, qnw_ref[...])
    q_ref[...] = (rope_groups(q) * (A_HEAD_DIM ** -0.5 * LOG2E)).astype(BF16)

    ckv = proj(C_CKV, C_IQ)
    ckv = ckv * lax.rsqrt(jnp.mean(ckv * ckv, axis=-1, keepdims=True) + EPS) * kvnw_ref[...]
    kv = jnp.dot(ckv.astype(BF16), wkv_ref[...], preferred_element_type=F32)
    k = _head_rms(kv[:, :A_WIDTH], hm, knw_ref[...])
    k_ref[...] = rope_groups(k).astype(BF16)
    tm = x.shape[0]
    ones = jnp.ones((VT_ROWS - A_HEAD_DIM, tm), BF16)
    for g in range(A_WIDTH // LANES):
        vt = kv[:, A_WIDTH + g * LANES:A_WIDTH + (g + 1) * LANES].T.astype(BF16)
        for half in range(2):
            r0 = (2 * g + half) * VT_ROWS
            vt_ref[0, r0:r0 + A_HEAD_DIM, :] = vt[half * A_HEAD_DIM:(half + 1) * A_HEAD_DIM, :]
            vt_ref[0, r0 + A_HEAD_DIM:r0 + VT_ROWS, :] = ones

    iq_ref[...] = rope_groups(proj(C_IQ, C_IK)).astype(BF16)
    ik_ref[...] = _rope(proj(C_IK, C_MISC), cos, sin).astype(BF16)

    lane = lax.broadcasted_iota(jnp.int32, (1, LANES), 1)
    w_scale = jnp.where(lane < IDX_HEADS, IDX_HEADS ** -0.5 * IDX_DIM ** -0.5, 1.0)
    misc_ref[...] = proj(C_MISC, C_QKV) * w_scale
    qkv_ref[...] = proj(C_QKV, C_Z)
    z_ref[...] = proj(C_Z, C_END)


def _in_proj(x2d, cos, sin, consts, tm):
    n = x2d.shape[0]
    n_pos_blocks = cos.shape[0] // tm
    nw, w_pack, wkv, qnw, knw, kvnw, hm = consts
    row = lambda i: (i, 0)
    fixed = lambda i: (0, 0)
    pos = lambda i: (i % n_pos_blocks, 0)
    out_widths = (A_WIDTH, A_WIDTH, None, IDX_HEADS * IDX_DIM, LANES, LANES, 3 * B_WIDTH, B_WIDTH)
    out_dtypes = (BF16, BF16, BF16, BF16, BF16, F32, F32, F32)
    vt_rows = A_HEADS * VT_ROWS
    out_specs = [pl.BlockSpec((1, vt_rows, tm), lambda i: (i, 0, 0)) if w is None
                 else pl.BlockSpec((tm, w), row) for w in out_widths]
    out_shape = [jax.ShapeDtypeStruct((n // tm, vt_rows, tm) if w is None else (n, w), dt)
                 for w, dt in zip(out_widths, out_dtypes)]
    return pl.pallas_call(
        _in_proj_kernel,
        grid=(n // tm,),
        in_specs=[
            pl.BlockSpec((tm, D_MODEL), row),
            pl.BlockSpec(nw.shape, fixed),
            pl.BlockSpec(w_pack.shape, fixed),
            pl.BlockSpec(wkv.shape, fixed),
            pl.BlockSpec(qnw.shape, fixed),
            pl.BlockSpec(knw.shape, fixed),
            pl.BlockSpec(kvnw.shape, fixed),
            pl.BlockSpec((tm, LANES), pos),
            pl.BlockSpec((tm, LANES), pos),
            pl.BlockSpec(hm.shape, fixed),
        ],
        out_specs=out_specs,
        out_shape=out_shape,
        compiler_params=pltpu.CompilerParams(dimension_semantics=("parallel",)),
        name="in_proj",
    )(x2d, nw, w_pack, wkv, qnw, knw, kvnw, cos, sin, hm)


def _pack_in_proj_weights(norm_mix_w, w_in, q_norm_w, k_norm_w, kv_norm_w, w_kv_up):
    points = []
    acc = 0
    for s in IN_SPLITS:
        points.append((acc, acc + s))
        acc += s
    a_q, a_ckv, i_q, i_k, i_w, b_qkv, b_z, b_beta, b_a = (w_in[:, a:b] for a, b in points)
    misc = jnp.concatenate(
        [i_w, b_beta, b_a, jnp.zeros((D_MODEL, LANES - IDX_HEADS - 2 * B_HEADS), w_in.dtype)], axis=1)
    w_pack = jnp.concatenate([a_q, a_ckv, i_q, i_k, i_k, misc, b_qkv, b_z], axis=1).astype(BF16)
    head_mean = jnp.kron(jnp.eye(A_HEADS, dtype=F32),
                         jnp.full((A_HEAD_DIM, A_HEAD_DIM), 1.0 / A_HEAD_DIM, F32)).astype(BF16)
    return (norm_mix_w.reshape(1, D_MODEL), w_pack, w_kv_up.astype(BF16),
            jnp.tile(q_norm_w, A_HEADS).reshape(1, A_WIDTH),
            jnp.tile(k_norm_w, A_HEADS).reshape(1, A_WIDTH),
            kv_norm_w.reshape(1, KV_RANK), head_mean)


def _rope_tables(n_pos):
    half = A_HEAD_DIM // 2
    inv_freq = ROPE_THETA ** (-jnp.arange(0, A_HEAD_DIM, 2, dtype=F32) / A_HEAD_DIM)
    ang = jnp.arange(n_pos, dtype=F32)[:, None] * inv_freq[None, :]
    cos, sin = jnp.cos(ang), jnp.sin(ang)
    cos128 = jnp.tile(cos, (1, LANES // half))
    sin128 = jnp.tile(jnp.concatenate([-sin, sin], axis=1), (1, LANES // A_HEAD_DIM))
    return cos128, sin128


NEG_INF = float("-inf")
F32_MAX = float(jnp.finfo(jnp.float32).max)
INT_MIN = -2 ** 31


VT_TILE = 512
SUBLANES = 8
FOLD_WIDTH = 8


def _attn_kernel(q_ref, k_ref, vt_ref, km_ref, vmt_ref, iq_ref, ik_ref, misc_ref, o_ref,
                 isc_ref, lhs_ref, qm_ref, s_ref, m_ref, acc_ref, *, tq, kc, topk, pos_bits):
    j = pl.program_id(1)
    n_kc = lax.div((j + 1) * tq + (kc - 1), kc)
    g = kc // SUBLANES
    lane = lax.broadcasted_iota(jnp.int32, (1, LANES), 1)
    lo_half = lane < A_HEAD_DIM
    n_pairs = A_WIDTH // LANES
    nt = (((1,), (1,)), ((), ()))

    iq = iq_ref[0]
    q = q_ref[0]
    zero = jnp.zeros((), BF16)
    for p in range(n_pairs):
        blk = iq[:, p * LANES:(p + 1) * LANES]
        lhs_ref[(2 * p) * tq:(2 * p + 1) * tq, :] = jnp.where(lo_half, blk, zero)
        lhs_ref[(2 * p + 1) * tq:(2 * p + 2) * tq, :] = jnp.where(lo_half, zero, blk)
        qb = q[:, p * LANES:(p + 1) * LANES]
        qm_ref[p, :tq, :] = jnp.where(lo_half, qb, zero)
        qm_ref[p, tq:, :] = jnp.where(lo_half, zero, qb)
    w_t = misc_ref[0].T

    def fold0(x3, op):
        n = x3.shape[0]
        width = min(FOLD_WIDTH, n)
        acc = x3[:width]
        for i in range(width, n, width):
            acc = op(acc, x3[i:i + width])
        while width > 1:
            acc = op(acc[:width // 2], acc[width // 2:width])
            width //= 2
        return acc[0]

    q_pos = j * tq + lax.broadcasted_iota(jnp.int32, (1, tq), 1)

    def idx_body(c, carry):
        k0 = pl.multiple_of(c * kc, kc)
        r = lax.dot_general(ik_ref[0, pl.ds(k0, kc), :], lhs_ref[...], nt,
                            preferred_element_type=F32)
        s = None
        for h in range(IDX_HEADS):
            term = jnp.maximum(r[:, h * tq:(h + 1) * tq], 0.0) * w_t[MISC_W + h:MISC_W + h + 1, :]
            s = term if s is None else s + term
        k_pos = k0 + lax.broadcasted_iota(jnp.int32, (kc, 1), 0)
        isc_ref[c] = jnp.where(k_pos <= q_pos, s, NEG_INF)
        return carry
    lax.fori_loop(0, n_kc, idx_body, 0)

    def count(pred):
        def body(c, acc):
            x3 = isc_ref[c].reshape(g, SUBLANES, tq)
            return acc + fold0(jnp.where(pred(x3, c), 1.0, 0.0), jnp.add)
        acc = lax.fori_loop(0, n_kc, body, jnp.zeros((SUBLANES, tq), F32))
        return jnp.broadcast_to(jnp.sum(acc, axis=0, keepdims=True), (SUBLANES, tq))

    def key_to_float(u):
        key = u ^ jnp.int32(INT_MIN)
        bits = jnp.where(key >= 0, key, key ^ jnp.int32(0x7FFFFFFF))
        return lax.bitcast_convert_type(bits, F32)

    def bit_body(i, carry):
        u, n_u = carry
        u2 = u | lax.shift_left(jnp.int32(1), 31 - i)
        cand = key_to_float(u2)
        n_ge = count(lambda x3, c: x3 >= cand[None])
        keep = n_ge >= topk
        return jnp.where(keep, u2, u), jnp.where(keep, n_ge, n_u)
    u, n_u = lax.fori_loop(0, 32, bit_body, (jnp.zeros((SUBLANES, tq), jnp.int32),
                                             jnp.zeros((SUBLANES, tq), F32)))
    few = (u >= 0) & (u < 0x00800000)
    tau = jnp.where(few, -F32_MAX, key_to_float(u))

    n_ge = jnp.where(few, 0.0, n_u)

    @pl.when(jnp.max(n_ge) > topk)
    def _():
        n_gt = count(lambda x3, c: x3 > tau[None])
        need = topk - n_gt
        def chunk_pos(c):
            return (c * kc + lax.broadcasted_iota(jnp.int32, (g, SUBLANES, tq), 0) * SUBLANES
                    + lax.broadcasted_iota(jnp.int32, (g, SUBLANES, tq), 1))
        def pos_body(i, cut):
            cut2 = cut | lax.shift_left(jnp.int32(1), pos_bits - 1 - i)
            ties_before = count(lambda x3, c: (x3 == tau[None]) & (chunk_pos(c) < cut2[None]))
            return jnp.where(ties_before < need, cut2, cut)
        cut = lax.fori_loop(0, pos_bits, pos_body, jnp.zeros((SUBLANES, tq), jnp.int32))
        def drop_body(c, carry):
            x3 = isc_ref[c].reshape(g, SUBLANES, tq)
            drop = (x3 == tau[None]) & (chunk_pos(c) > cut[None])
            isc_ref[c] = jnp.where(drop, NEG_INF, x3).reshape(kc, tq)
            return carry
        lax.fori_loop(0, n_kc, drop_body, 0)

    tau_row = tau[0:1, :]

    def attend(key_pairs, bias, vt_slabs, first):
        n = key_pairs[0].shape[0]
        m_cur = []
        for p in range(n_pairs):
            s2 = lax.dot_general(key_pairs[p], qm_ref[p], nt, preferred_element_type=F32)
            for half in range(2):
                s = s2[:, half * tq:(half + 1) * tq] + bias
                s_ref[2 * p + half, :n, :] = s
                m8 = fold0(s.reshape(n // SUBLANES, SUBLANES, tq), jnp.maximum)
                m_cur.append(jnp.max(m8, axis=0, keepdims=True))
        for hd in range(A_HEADS):
            m_new = m_cur[hd] if first else jnp.maximum(m_ref[hd][0:1, :], m_cur[hd])
            e = jnp.exp2(s_ref[hd, :n, :] - m_new).astype(BF16)
            pv = None
            off = 0
            for slab in vt_slabs(hd):
                part = jnp.dot(slab, e[off:off + slab.shape[1], :], preferred_element_type=F32)
                pv = part if pv is None else pv + part
                off += slab.shape[1]
            if first:
                acc_ref[hd] = pv
            else:
                acc_ref[hd] = jnp.exp2(m_ref[hd][0:1, :] - m_new) * acc_ref[hd] + pv
            m_ref[hd] = jnp.broadcast_to(m_new, (SUBLANES, tq))

    meta_bias = jnp.where(lax.broadcasted_iota(jnp.int32, (LANES, 1), 0) < N_META, 0.0, NEG_INF)
    attend([km_ref[:, p * LANES:(p + 1) * LANES] for p in range(n_pairs)], meta_bias,
           lambda hd: [vmt_ref[hd * VT_ROWS:(hd + 1) * VT_ROWS, :]], True)

    def att_body(c, carry):
        k0 = pl.multiple_of(c * kc, kc)
        bias = jnp.where(isc_ref[c] >= tau_row, 0.0, NEG_INF)
        attend([k_ref[0, pl.ds(k0, kc), p * LANES:(p + 1) * LANES] for p in range(n_pairs)], bias,
               lambda hd: [vt_ref[c * (kc // VT_TILE) + t, hd * VT_ROWS:(hd + 1) * VT_ROWS, :]
                           for t in range(kc // VT_TILE)], False)
        return carry
    lax.fori_loop(0, n_kc, att_body, 0)

    for p in range(n_pairs):
        halves = []
        for hd in (2 * p, 2 * p + 1):
            acc = acc_ref[hd]
            halves.append(acc[:A_HEAD_DIM, :] / acc[A_HEAD_DIM:A_HEAD_DIM + 1, :])
        o_ref[0, :, p * LANES:(p + 1) * LANES] = jnp.concatenate(halves, axis=0).T.astype(BF16)


def _sparse_attention(q, k, vt, km, vmt, iq, ik, misc, *, tq, kc, topk):
    b, s, _ = q.shape
    assert tq % LANES == 0 and s % tq == 0 and kc % VT_TILE == 0 and s % kc == 0
    kernel = functools.partial(_attn_kernel, tq=tq, kc=kc, topk=topk, pos_bits=int(math.log2(s)))
    qblk = lambda bi, j: (bi, j, 0)
    full = lambda bi, j: (bi, 0, 0)
    fixed = lambda bi, j: (0, 0)
    vt_rows = A_HEADS * VT_ROWS
    return pl.pallas_call(
        kernel,
        grid=(b, s // tq),
        in_specs=[
            pl.BlockSpec((1, tq, A_WIDTH), qblk),
            pl.BlockSpec((1, s, A_WIDTH), full),
            pl.BlockSpec((s // VT_TILE, vt_rows, VT_TILE), full),
            pl.BlockSpec(km.shape, fixed),
            pl.BlockSpec(vmt.shape, fixed),
            pl.BlockSpec((1, tq, IDX_HEADS * IDX_DIM), qblk),
            pl.BlockSpec((1, s, LANES), full),
            pl.BlockSpec((1, tq, LANES), qblk),
        ],
        out_specs=pl.BlockSpec((1, tq, A_WIDTH), qblk),
        out_shape=jax.ShapeDtypeStruct((b, s, A_WIDTH), BF16),
        scratch_shapes=[
            pltpu.VMEM((s // kc, kc, tq), F32),
            pltpu.VMEM((IDX_HEADS * tq, LANES), BF16),
            pltpu.VMEM((A_WIDTH // LANES, 2 * tq, LANES), BF16),
            pltpu.VMEM((A_HEADS, max(kc, LANES), tq), F32),
            pltpu.VMEM((A_HEADS, SUBLANES, tq), F32),
            pltpu.VMEM((A_HEADS, VT_ROWS, tq), F32),
        ],
        compiler_params=pltpu.CompilerParams(dimension_semantics=("parallel", "arbitrary")),
        name="sparse_attention",
    )(q, k, vt, km, vmt, iq, ik, misc)


HALO = 8


def _softplus(x):
    return jnp.maximum(x, 0.0) + jnp.log1p(jnp.exp(-jnp.abs(x)))


def _bdot(a, b):
    return jnp.dot(a.astype(BF16), b.astype(BF16), preferred_element_type=F32)


def _hdot(a, b):
    return jnp.dot(a, b, precision=HIGHEST, preferred_element_type=F32)


def _dot3(a, b):
    a_hi = a.astype(BF16)
    b_hi = b.astype(BF16)
    a_lo = (a - a_hi.astype(F32)).astype(BF16)
    b_lo = (b - b_hi.astype(F32)).astype(BF16)
    dot = functools.partial(jnp.dot, preferred_element_type=F32)
    return dot(a_hi, b_hi) + (dot(a_hi, b_lo) + dot(a_lo, b_hi))


def _delta_kernel(qkv_ref, z_ref, misc_ref, qkvm_ref, miscm_ref, convw_ref, alog_ref, dtb_ref,
                  nw_ref, o_ref, state_ref, halo_ref):
    c = pl.program_id(1)
    n_pad = CHUNK - N_META
    nt = (((1,), (1,)), ((), ()))
    tn = (((0,), (0,)), ((), ()))

    @pl.when(c == 0)
    def _():
        state_ref[...] = jnp.zeros_like(state_ref)
        halo_ref[...] = jnp.zeros_like(halo_ref)

    ri = lax.broadcasted_iota(jnp.int32, (CHUNK, CHUNK), 0)
    ci = lax.broadcasted_iota(jnp.int32, (CHUNK, CHUNK), 1)
    incl = ri >= ci
    strict = ri > ci
    eye = (ri == ci).astype(F32)
    tri = incl.astype(F32)
    row = lax.broadcasted_iota(jnp.int32, (CHUNK, 1), 0)
    is_meta = c == 0
    neutral = jnp.logical_and(is_meta, row < n_pad)
    cols = lambda base, h: slice(base + h * B_HEAD_DIM, base + (h + 1) * B_HEAD_DIM)

    nb = qkv_ref.shape[0]
    heads = range(nb * B_HEADS)
    qn, kn, vb, kb, gc, decay = [], [], [], [], [], []
    for bi in range(nb):
        xin = jnp.where(is_meta, qkvm_ref[...], qkv_ref[bi])
        misc = jnp.where(is_meta, miscm_ref[...], misc_ref[bi])
        xcat = jnp.concatenate([halo_ref[bi], xin], axis=0)
        halo_ref[bi] = xin[CHUNK - HALO:, :]
        conv = None
        for tap in range(CONV_WIDTH):
            off = HALO - (CONV_WIDTH - 1) + tap
            term = xcat[off:off + CHUNK, :] * convw_ref[tap:tap + 1, :]
            conv = term if conv is None else conv + term
        xc = conv * jax.nn.sigmoid(conv)
        beta_all = jnp.where(neutral, 0.0, jax.nn.sigmoid(misc))
        g_all = jnp.where(neutral, 0.0, -jnp.exp(alog_ref[...]) * _softplus(misc + dtb_ref[...]))
        gc_all = _hdot(tri, g_all)
        gc_all_t = gc_all.T
        for h in range(B_HEADS):
            qh, kh, vh = xc[:, cols(0, h)], xc[:, cols(B_WIDTH, h)], xc[:, cols(2 * B_WIDTH, h)]
            qn.append(qh * lax.rsqrt(jnp.sum(qh * qh, axis=-1, keepdims=True) + EPS) * (B_HEAD_DIM ** -0.5))
            kn.append(kh * lax.rsqrt(jnp.sum(kh * kh, axis=-1, keepdims=True) + EPS))
            beta = jnp.broadcast_to(beta_all[:, MISC_BETA + h:MISC_BETA + h + 1], (CHUNK, B_HEAD_DIM))
            gc.append(jnp.broadcast_to(gc_all[:, MISC_DECAY + h:MISC_DECAY + h + 1], (CHUNK, B_HEAD_DIM)))
            g_col = gc[-1][:, :CHUNK]
            g_row = gc_all_t[MISC_DECAY + h:MISC_DECAY + h + 1, :]
            decay.append(jnp.where(incl, jnp.exp(jnp.where(incl, g_col - g_row, 0.0)), 0.0))
            kb.append(kn[-1] * beta)
            vb.append(vh * beta)

    kk = [lax.dot_general(kb[h].astype(BF16), kn[h].astype(BF16), nt, preferred_element_type=F32)
          for h in heads]
    qk = [lax.dot_general(qn[h].astype(BF16), kn[h].astype(BF16), nt, preferred_element_type=F32)
          for h in heads]
    intra = [jnp.where(incl, qk[h] * decay[h], 0.0) for h in heads]

    pw = [jnp.where(strict, -(kk[h] * decay[h]), 0.0) for h in heads]
    t_inv = [eye + pw[h] for h in heads]
    pw = [_dot3(pw[h], pw[h]) for h in heads]
    for _ in range(int(math.log2(CHUNK)) - 2):
        t_next = [t_inv[h] + _dot3(t_inv[h], pw[h]) for h in heads]
        pw = [_dot3(pw[h], pw[h]) for h in heads]
        t_inv = t_next
    t_inv = [t_inv[h] + _dot3(t_inv[h], pw[h]) for h in heads]

    u = [_bdot(t_inv[h], vb[h]) for h in heads]
    w = [_bdot(t_inv[h], kb[h] * jnp.exp(gc[h])) for h in heads]

    state = [state_ref[h] for h in heads]
    w_s = [_bdot(w[h], state[h]) for h in heads]
    q_s = [_bdot(qn[h] * jnp.exp(gc[h]), state[h]) for h in heads]
    v_new = [u[h] - w_s[h] for h in heads]
    o = [q_s[h] + _bdot(intra[h], v_new[h]) for h in heads]
    for h in heads:
        g_last = gc[h][CHUNK - 1:CHUNK, :]
        kd = kn[h] * jnp.exp(g_last - gc[h])
        state_ref[h] = state[h] * jnp.exp(g_last) + lax.dot_general(
            kd.astype(BF16), v_new[h].astype(BF16), tn, preferred_element_type=F32)

    for h in heads:
        bi, hh = divmod(h, B_HEADS)
        y = o[h] * lax.rsqrt(jnp.mean(o[h] * o[h], axis=-1, keepdims=True) + EPS) * nw_ref[...]
        zh = z_ref[bi, :, cols(0, hh)]
        o_ref[bi, :, cols(0, hh)] = (y * (zh * jax.nn.sigmoid(zh))).astype(BF16)


DELTA_BATCH = 4


def _gated_deltanet(qkv, z, misc, qkv_m, misc_m, conv_w, a_log, dt_bias, norm_w):
    b, s, _ = qkv.shape
    n_chunks = s // CHUNK + 1
    nb = DELTA_BATCH if b % DELTA_BATCH == 0 else 1
    blk = lambda bi, c: (bi, jnp.maximum(c - 1, 0), 0)
    fixed = lambda bi, c: (0, 0)
    lane_vec = lambda v: jnp.zeros((1, LANES), F32).at[0, MISC_DECAY:MISC_DECAY + B_HEADS].set(v)
    return pl.pallas_call(
        _delta_kernel,
        grid=(b // nb, n_chunks),
        in_specs=[
            pl.BlockSpec((nb, CHUNK, 3 * B_WIDTH), blk),
            pl.BlockSpec((nb, CHUNK, B_WIDTH), blk),
            pl.BlockSpec((nb, CHUNK, LANES), blk),
            pl.BlockSpec((CHUNK, 3 * B_WIDTH), fixed),
            pl.BlockSpec((CHUNK, LANES), fixed),
            pl.BlockSpec((CONV_WIDTH, 3 * B_WIDTH), fixed),
            pl.BlockSpec((1, LANES), fixed),
            pl.BlockSpec((1, LANES), fixed),
            pl.BlockSpec((1, B_HEAD_DIM), fixed),
        ],
        out_specs=pl.BlockSpec((nb, CHUNK, B_WIDTH), blk),
        out_shape=jax.ShapeDtypeStruct((b, s, B_WIDTH), BF16),
        scratch_shapes=[
            pltpu.VMEM((nb * B_HEADS, B_HEAD_DIM, B_HEAD_DIM), F32),
            pltpu.VMEM((nb, HALO, 3 * B_WIDTH), F32),
        ],
        compiler_params=pltpu.CompilerParams(dimension_semantics=("parallel", "arbitrary")),
        name="gated_deltanet",
    )(qkv, z, misc, qkv_m, misc_m, conv_w, lane_vec(a_log), lane_vec(dt_bias),
      norm_w.reshape(1, B_HEAD_DIM))


def _pack_rows(a):
    half = a.shape[1] // 2
    bits = lax.bitcast_convert_type(a.astype(BF16).astype(F32), jnp.int32)
    return bits[:, :half] | lax.shift_right_logical(bits[:, half:], 16)


def _unpack_rows(w):
    hi = lax.bitcast_convert_type(w & jnp.int32(-65536), F32)
    lo = lax.bitcast_convert_type(lax.shift_left(w, 16), F32)
    return jnp.concatenate([hi, lo], axis=1)


def _out_router_kernel(oa_ref, ob_ref, x_ref, wo_ref, nw_ref, wr_ref, br_ref,
                       h_ref, xn_ref, gate_ref, eid_ref):
    mix = (jnp.dot(oa_ref[...], wo_ref[:A_WIDTH, :], preferred_element_type=F32)
           + jnp.dot(ob_ref[...], wo_ref[A_WIDTH:, :], preferred_element_type=F32))
    h = x_ref[...] + mix
    h_ref[...] = h
    xn = h * lax.rsqrt(jnp.mean(h * h, axis=-1, keepdims=True) + EPS) * nw_ref[...]
    xn_ref[...] = _pack_rows(xn)
    tm = xn.shape[0]
    logits = lax.dot_general(wr_ref[...], xn, (((1,), (1,)), ((), ())), precision=HIGHEST,
                             preferred_element_type=F32) + br_ref[...]
    e_idx = lax.broadcasted_iota(jnp.int32, (N_EXPERTS, tm), 0).astype(F32)
    work = logits
    vals, idxs = [], []
    for _ in range(TOP_K):
        m = jnp.max(work, axis=0, keepdims=True)
        idx = jnp.min(jnp.where(work == m, e_idx, float(N_EXPERTS)), axis=0, keepdims=True)
        vals.append(m)
        idxs.append(idx)
        work = jnp.where(e_idx == idx, NEG_INF, work)
    exps = [jnp.exp(v - vals[0]) for v in vals]
    denom = exps[0]
    for e in exps[1:]:
        denom = denom + e
    eid_ref[...] = jnp.concatenate(idxs + [jnp.zeros((SUBLANES - TOP_K, tm), F32)], axis=0).astype(jnp.int32)
    gates_t = jnp.concatenate([e / denom for e in exps] + [jnp.zeros((LANES - TOP_K, tm), F32)], axis=0)
    gate_ref[...] = gates_t.T


def _out_router(o_a, o_b, x2d, w_out, norm_w, w_router, b_router, tm):
    n = x2d.shape[0]
    row = lambda i: (i, 0)
    fixed = lambda i: (0, 0)
    wr = w_router.T
    br = b_router.reshape(N_EXPERTS, 1)
    return pl.pallas_call(
        _out_router_kernel,
        grid=(n // tm,),
        in_specs=[
            pl.BlockSpec((tm, A_WIDTH), row),
            pl.BlockSpec((tm, B_WIDTH), row),
            pl.BlockSpec((tm, D_MODEL), row),
            pl.BlockSpec((A_WIDTH + B_WIDTH, D_MODEL), fixed),
            pl.BlockSpec((1, D_MODEL), fixed),
            pl.BlockSpec((N_EXPERTS, D_MODEL), fixed),
            pl.BlockSpec((N_EXPERTS, 1), fixed),
        ],
        out_specs=[pl.BlockSpec((tm, D_MODEL), row), pl.BlockSpec((tm, D_MODEL // 2), row),
                   pl.BlockSpec((tm, LANES), row), pl.BlockSpec((SUBLANES, tm), lambda i: (0, i))],
        out_shape=[jax.ShapeDtypeStruct((n, D_MODEL), F32), jax.ShapeDtypeStruct((n, D_MODEL // 2), jnp.int32),
                   jax.ShapeDtypeStruct((n, LANES), F32), jax.ShapeDtypeStruct((SUBLANES, n), jnp.int32)],
        compiler_params=pltpu.CompilerParams(dimension_semantics=("parallel",)),
        name="out_proj_router",
    )(o_a, o_b, x2d, w_out.astype(BF16), norm_w.reshape(1, D_MODEL), wr, br)


MOE_BM = 512
TOKEN_TILE = 512
COMBINE_TILE = 1024
ATT_KC = 512
ATT_TQ = 256
MOE_BF = 512


SC_CORES = 2
SC_SUBCORES = 16
SC_ROWS = 128


def _sc_gather_rows(table, idx):
    n_workers = SC_CORES * SC_SUBCORES
    n_rows = idx.shape[0]
    d = table.shape[1]
    assert n_rows % (n_workers * SC_ROWS) == 0
    rows_per_worker = n_rows // n_workers
    mesh = plsc.VectorSubcoreMesh(core_axis_name="c", subcore_axis_name="s",
                                  num_cores=SC_CORES, num_subcores=SC_SUBCORES)

    @functools.partial(
        pl.kernel, mesh=mesh,
        out_type=jax.ShapeDtypeStruct((n_rows, d), table.dtype),
        scratch_types=[pltpu.VMEM((SC_ROWS,), jnp.int32), pltpu.VMEM((SC_ROWS, d), table.dtype),
                       pltpu.SemaphoreType.DMA],
        name="sc_gather_rows",
    )
    def gather(table_hbm, idx_hbm, out_hbm, idx_v, rows_v, sem):
        wid = lax.axis_index("s") * SC_CORES + lax.axis_index("c")
        base = wid * rows_per_worker

        @pl.loop(0, rows_per_worker // SC_ROWS)
        def _(i):
            off = base + i * SC_ROWS
            pltpu.sync_copy(idx_hbm.at[pl.ds(off, SC_ROWS)], idx_v)
            pltpu.async_copy(table_hbm.at[idx_v], rows_v, sem).wait()
            pltpu.sync_copy(rows_v, out_hbm.at[pl.ds(off, SC_ROWS)])

    return gather(table, idx)


def _route_blocks(eid_t, bm):
    n_tok = eid_t.shape[1]
    n_assign = n_tok * TOP_K
    experts = jnp.arange(N_EXPERTS, dtype=jnp.int32)
    flat_e = eid_t.reshape(-1)
    sorted_e, order = lax.sort((flat_e, jnp.arange(n_assign, dtype=jnp.int32)), num_keys=1)
    onehot = sorted_e[:, None] == experts[None, :]
    counts = jnp.sum(onehot, axis=0, dtype=jnp.int32)
    padded = (counts + bm - 1) // bm * bm
    start = jnp.cumsum(counts) - counts
    pend = jnp.cumsum(padded)
    pstart = pend - padded
    dest = jnp.arange(n_assign, dtype=jnp.int32) + jnp.sum(
        jnp.where(onehot, (pstart - start)[None, :], 0), axis=1)
    n_blocks = -(-(n_assign + N_EXPERTS * (bm - 1)) // bm)
    blk_start = jnp.arange(n_blocks, dtype=jnp.int32) * bm
    block_e = jnp.minimum(jnp.sum(blk_start[:, None] >= pend[None, :], axis=1), N_EXPERTS - 1)
    n_valid = pend[-1] // bm
    _, pos = lax.sort((order, dest), num_keys=1)
    t = jnp.arange(bm, dtype=jnp.int32)[None, :]
    pad_key = jnp.where(t < (padded - counts)[:, None], (pstart + counts)[:, None] + t,
                        n_blocks * bm).reshape(-1)
    pad_tok = jnp.arange(N_EXPERTS * bm, dtype=jnp.int32) % n_tok
    assert n_blocks * bm - n_assign == N_EXPERTS * bm
    _, row_tok = lax.sort((jnp.concatenate([dest, pad_key]),
                           jnp.concatenate([order % n_tok, pad_tok])), num_keys=1)
    return block_e.astype(jnp.int32), n_valid.astype(jnp.int32).reshape(1), row_tok, pos


def _moe_dense_kernel(be_ref, nv_ref, x_ref, wgu_ref, wd_ref, bg_ref, bu_ref, bd_ref, perm_ref, y_ref,
                      wg_s, wu_s, wd_s):
    i = pl.program_id(0)
    live = i < nv_ref[0]

    @pl.when(jnp.logical_and(live, jnp.logical_or(i == 0, be_ref[i] != be_ref[jnp.maximum(i - 1, 0)])))
    def _():
        perm = perm_ref[...]
        grp = 2 * LANES
        for gidx in range(2 * D_FF // grp):
            blk = wgu_ref[0, :, gidx * grp:(gidx + 1) * grp].astype(BF16)
            split = jnp.dot(blk, perm, preferred_element_type=F32).astype(BF16)
            wg_s[:, gidx * LANES:(gidx + 1) * LANES] = split[:, :LANES]
            wu_s[:, gidx * LANES:(gidx + 1) * LANES] = split[:, LANES:]
        wd_s[...] = wd_ref[0].astype(BF16)

    @pl.when(live)
    def _():
        x = _unpack_rows(x_ref[...]).astype(BF16)
        y = None
        for f0 in range(0, D_FF, MOE_BF):
            g = jnp.dot(x, wg_s[:, f0:f0 + MOE_BF], preferred_element_type=F32) + bg_ref[0, :, f0:f0 + MOE_BF]
            u = jnp.dot(x, wu_s[:, f0:f0 + MOE_BF], preferred_element_type=F32) + bu_ref[0, :, f0:f0 + MOE_BF]
            gate = jnp.minimum(g, SWIGLU_LIMIT)
            up = jnp.clip(u, -SWIGLU_LIMIT, SWIGLU_LIMIT)
            t = gate * jax.nn.sigmoid(gate * SWIGLU_ALPHA) * (up + 1.0)
            part = jnp.dot(t.astype(BF16), wd_s[f0:f0 + MOE_BF, :], preferred_element_type=F32)
            y = part if y is None else y + part
        y_ref[...] = _pack_rows(y + bd_ref[0])


def _moe_dense(x_sorted, block_e, n_valid, w_gate_up, w_down, b_gate, b_up, b_down, bm):
    n_blocks = block_e.shape[0]
    src = jnp.arange(2 * LANES)
    perm = jax.nn.one_hot((src % 2) * LANES + src // 2, 2 * LANES, dtype=BF16)
    rows = lambda i, be, nv: (jnp.minimum(i, nv[0] - 1), 0)
    wsel = lambda i, be, nv: (be[i], 0, 0)
    fixed = lambda i, be, nv: (0, 0)
    grid_spec = pltpu.PrefetchScalarGridSpec(
        num_scalar_prefetch=2,
        grid=(n_blocks,),
        in_specs=[
            pl.BlockSpec((bm, D_MODEL // 2), rows),
            pl.BlockSpec((1, D_MODEL, 2 * D_FF), wsel),
            pl.BlockSpec((1, D_FF, D_MODEL), wsel),
            pl.BlockSpec((1, 1, D_FF), wsel),
            pl.BlockSpec((1, 1, D_FF), wsel),
            pl.BlockSpec((1, 1, D_MODEL), wsel),
            pl.BlockSpec((2 * LANES, 2 * LANES), fixed),
        ],
        out_specs=pl.BlockSpec((bm, D_MODEL // 2), rows),
        scratch_shapes=[
            pltpu.VMEM((D_MODEL, D_FF), BF16),
            pltpu.VMEM((D_MODEL, D_FF), BF16),
            pltpu.VMEM((D_FF, D_MODEL), BF16),
        ],
    )
    return pl.pallas_call(
        _moe_dense_kernel,
        grid_spec=grid_spec,
        out_shape=jax.ShapeDtypeStruct(x_sorted.shape, jnp.int32),
        compiler_params=pltpu.CompilerParams(dimension_semantics=("arbitrary",)),
        name="moe_experts",
    )(block_e, n_valid, x_sorted, w_gate_up, w_down, b_gate, b_up, b_down, perm)


def _combine_kernel(h_ref, gate_ref, y0_ref, y1_ref, y2_ref, y3_ref, o_ref):
    gates = gate_ref[...]
    out = h_ref[...]
    for kk, y_ref in enumerate((y0_ref, y1_ref, y2_ref, y3_ref)):
        out = out + gates[:, kk:kk + 1] * _unpack_rows(y_ref[...])
    o_ref[...] = out


def _combine(h, gates, y, tm):
    n_tok = h.shape[0]
    nt = n_tok // tm
    row = lambda i: (i, 0)
    return pl.pallas_call(
        _combine_kernel,
        grid=(nt,),
        in_specs=[pl.BlockSpec((tm, D_MODEL), row), pl.BlockSpec((tm, LANES), row)]
        + [pl.BlockSpec((tm, D_MODEL // 2), functools.partial(lambda kk, i: (kk * nt + i, 0), kk))
           for kk in range(TOP_K)],
        out_specs=pl.BlockSpec((tm, D_MODEL), row),
        out_shape=jax.ShapeDtypeStruct((n_tok, D_MODEL), F32),
        compiler_params=pltpu.CompilerParams(dimension_semantics=("parallel",)),
        name="moe_combine",
    )(h, gates, y, y, y, y)


def kernel(x, meta_tokens, norm_mix_w, w_in, q_norm_w, k_norm_w, kv_norm_w, w_kv_up, conv_w, a_log,
           dt_bias, delta_norm_w, w_out, norm_ffn_w, w_router, b_router, w_gate_up, b_gate_up,
           w_down, b_down):
    b, s, d = x.shape
    consts = _pack_in_proj_weights(norm_mix_w[0], w_in[0], q_norm_w[0], k_norm_w[0], kv_norm_w[0],
                                   w_kv_up[0])
    cos, sin = _rope_tables(N_META + s)
    real = _in_proj(x.reshape(b * s, d), cos[N_META:], sin[N_META:], consts, VT_TILE)
    meta = _in_proj(meta_tokens, cos[:N_META], sin[:N_META], consts, N_META)
    vt = real[2]
    q, k, iq, ik, misc, qkv, z = (a.reshape(b, s, a.shape[-1]) for a in real[:2] + real[3:])
    km = jnp.pad(meta[1], ((0, LANES - N_META), (0, 0)))
    vmt = jnp.pad(meta[2][0], ((0, 0), (0, LANES - N_META)))
    o_a = _sparse_attention(q, k, vt, km, vmt, iq, ik, misc,
                            tq=ATT_TQ, kc=min(ATT_KC, s), topk=min(INDEX_TOPK, s // 4))
    lead = lambda a: jnp.pad(a, ((CHUNK - N_META, 0), (0, 0)))
    o_b = _gated_deltanet(qkv, z, misc, lead(meta[6]), lead(meta[5]), conv_w[0], a_log[0],
                          dt_bias[0], delta_norm_w[0])
    n_tok = b * s
    h, xn, gates, eid = _out_router(o_a.reshape(n_tok, A_WIDTH), o_b.reshape(n_tok, B_WIDTH),
                                    x.reshape(n_tok, d), w_out[0], norm_ffn_w[0], w_router[0],
                                    b_router[0], TOKEN_TILE)
    block_e, n_valid, row_tok, pos = _route_blocks(eid[:TOP_K], MOE_BM)
    bgu = b_gate_up[0].reshape(N_EXPERTS, 1, 2 * D_FF)
    x_sorted = _sc_gather_rows(xn, row_tok)
    y_sorted = _moe_dense(x_sorted, block_e, n_valid, w_gate_up[0], w_down[0], bgu[:, :, 0::2],
                          bgu[:, :, 1::2], b_down[0].reshape(N_EXPERTS, 1, D_MODEL), MOE_BM)
    y = _sc_gather_rows(y_sorted, pos)
    out = _combine(h, gates, y, min(COMBINE_TILE, n_tok))
    return out.reshape(b, s, d)
```

```python
import functools
import math

import jax
import jax.numpy as jnp
from jax import lax
from jax.experimental import pallas as pl
from jax.experimental.pallas import tpu as pltpu
from jax.experimental.pallas import tpu_sc as plsc

F32 = jnp.float32
BF16 = jnp.bfloat16
HIGHEST = lax.Precision.HIGHEST

D_MODEL = 1024
N_META = 16
ROPE_THETA = 10000.0
EPS = 1e-6
A_HEAD_DIM = 64
A_HEADS = 8
A_WIDTH = A_HEADS * A_HEAD_DIM
KV_RANK = 256
IDX_HEADS = 8
IDX_DIM = 64
INDEX_TOPK = 256
B_HEAD_DIM = 128
B_HEADS = 4
B_WIDTH = B_HEADS * B_HEAD_DIM
CONV_WIDTH = 4
CHUNK = 64
N_EXPERTS = 32
TOP_K = 4
D_FF = D_MODEL
SWIGLU_LIMIT = 7.0
SWIGLU_ALPHA = 1.702
IN_SPLITS = (A_WIDTH, KV_RANK, IDX_HEADS * IDX_DIM, IDX_DIM, IDX_HEADS, 3 * B_WIDTH, B_WIDTH,
             B_HEADS, B_HEADS)

LANES = 128

C_Q = 0
C_CKV = C_Q + A_WIDTH
C_IQ = C_CKV + KV_RANK
C_IK = C_IQ + IDX_HEADS * IDX_DIM
C_MISC = C_IK + LANES
C_QKV = C_MISC + LANES
C_Z = C_QKV + 3 * B_WIDTH
C_END = C_Z + B_WIDTH
MISC_W, MISC_BETA, MISC_DECAY = 0, IDX_HEADS, IDX_HEADS + B_HEADS
LOG2E = math.log2(math.e)
BF16_SUBLANES = 16
VT_ROWS = A_HEAD_DIM + BF16_SUBLANES


def _rope_partner(a):
    lane = lax.broadcasted_iota(jnp.int32, a.shape, 1)
    first_half = (lane % A_HEAD_DIM) < (A_HEAD_DIM // 2)
    return jnp.where(first_half, pltpu.roll(a, LANES - A_HEAD_DIM // 2, 1),
                     pltpu.roll(a, A_HEAD_DIM // 2, 1))


def _rope(a, cos, sin_signed):
    return a * cos + _rope_partner(a) * sin_signed


def _head_rms(a, head_mean, gain):
    msq = jnp.dot((a * a).astype(BF16), head_mean, preferred_element_type=F32)
    return a * lax.rsqrt(msq + EPS) * gain


def _in_proj_kernel(x_ref, nw_ref, w_ref, wkv_ref, qnw_ref, knw_ref, kvnw_ref, cos_ref, sin_ref,
                    hm_ref, q_ref, k_ref, vt_ref, iq_ref, ik_ref, misc_ref, qkv_ref, z_ref):
    x = x_ref[...]
    u = x * lax.rsqrt(jnp.mean(x * x, axis=-1, keepdims=True) + EPS) * nw_ref[...]
    ub = u.astype(BF16)

    def proj(c0, c1):
        return jnp.dot(ub, w_ref[:, c0:c1], preferred_element_type=F32)

    cos = cos_ref[...]
    sin = sin_ref[...]
    hm = hm_ref[...]

    def rope_groups(a):
        return jnp.concatenate(
            [_rope(a[:, g * LANES:(g + 1) * LANES], cos, sin) for g in range(a.shape[1] // LANES)],
            axis=1)

    q = _head_rms(proj(C_Q, C_CKV), hm, qnw_ref[...])
    q_ref[...] = (rope_groups(q) * (A_HEAD_DIM ** -0.5 * LOG2E)).astype(BF16)

    ckv = proj(C_CKV, C_IQ)
    ckv = ckv * lax.rsqrt(jnp.mean(ckv * ckv, axis=-1, keepdims=True) + EPS) * kvnw_ref[...]
    kv = jnp.dot(ckv.astype(BF16), wkv_ref[...], preferred_element_type=F32)
    k = _head_rms(kv[:, :A_WIDTH], hm, knw_ref[...])
    k_ref[...] = rope_groups(k).astype(BF16)
    tm = x.shape[0]
    ones = jnp.ones((VT_ROWS - A_HEAD_DIM, tm), BF16)
    for g in range(A_WIDTH // LANES):
        vt = kv[:, A_WIDTH + g * LANES:A_WIDTH + (g + 1) * LANES].T.astype(BF16)
        for half in range(2):
            r0 = (2 * g + half) * VT_ROWS
            vt_ref[0, r0:r0 + A_HEAD_DIM, :] = vt[half * A_HEAD_DIM:(half + 1) * A_HEAD_DIM, :]
            vt_ref[0, r0 + A_HEAD_DIM:r0 + VT_ROWS, :] = ones

    iq_ref[...] = rope_groups(proj(C_IQ, C_IK)).astype(BF16)
    ik_ref[...] = _rope(proj(C_IK, C_MISC), cos, sin).astype(BF16)

    lane = lax.broadcasted_iota(jnp.int32, (1, LANES), 1)
    w_scale = jnp.where(lane < IDX_HEADS, IDX_HEADS ** -0.5 * IDX_DIM ** -0.5, 1.0)
    misc_ref[...] = proj(C_MISC, C_QKV) * w_scale
    qkv_ref[...] = proj(C_QKV, C_Z)
    z_ref[...] = proj(C_Z, C_END)


def _in_proj(x2d, cos, sin, consts, tm):
    n = x2d.shape[0]
    n_pos_blocks = cos.shape[0] // tm
    nw, w_pack, wkv, qnw, knw, kvnw, hm = consts
    row = lambda i: (i, 0)
    fixed = lambda i: (0, 0)
    pos = lambda i: (i % n_pos_blocks, 0)
    out_widths = (A_WIDTH, A_WIDTH, None, IDX_HEADS * IDX_DIM, LANES, LANES, 3 * B_WIDTH, B_WIDTH)
    out_dtypes = (BF16, BF16, BF16, BF16, BF16, F32, F32, F32)
    vt_rows = A_HEADS * VT_ROWS
    out_specs = [pl.BlockSpec((1, vt_rows, tm), lambda i: (i, 0, 0)) if w is None
                 else pl.BlockSpec((tm, w), row) for w in out_widths]
    out_shape = [jax.ShapeDtypeStruct((n // tm, vt_rows, tm) if w is None else (n, w), dt)
                 for w, dt in zip(out_widths, out_dtypes)]
    return pl.pallas_call(
        _in_proj_kernel,
        grid=(n // tm,),
        in_specs=[
            pl.BlockSpec((tm, D_MODEL), row),
            pl.BlockSpec(nw.shape, fixed),
            pl.BlockSpec(w_pack.shape, fixed),
            pl.BlockSpec(wkv.shape, fixed),
            pl.BlockSpec(qnw.shape, fixed),
            pl.BlockSpec(knw.shape, fixed),
            pl.BlockSpec(kvnw.shape, fixed),
            pl.BlockSpec((tm, LANES), pos),
            pl.BlockSpec((tm, LANES), pos),
            pl.BlockSpec(hm.shape, fixed),
        ],
        out_specs=out_specs,
        out_shape=out_shape,
        compiler_params=pltpu.CompilerParams(dimension_semantics=("parallel",)),
        name="in_proj",
    )(x2d, nw, w_pack, wkv, qnw, knw, kvnw, cos, sin, hm)


def _pack_in_proj_weights(norm_mix_w, w_in, q_norm_w, k_norm_w, kv_norm_w, w_kv_up):
    points = []
    acc = 0
    for s in IN_SPLITS:
        points.append((acc, acc + s))
        acc += s
    a_q, a_ckv, i_q, i_k, i_w, b_qkv, b_z, b_beta, b_a = (w_in[:, a:b] for a, b in points)
    misc = jnp.concatenate(
        [i_w, b_beta, b_a, jnp.zeros((D_MODEL, LANES - IDX_HEADS - 2 * B_HEADS), w_in.dtype)], axis=1)
    w_pack = jnp.concatenate([a_q, a_ckv, i_q, i_k, i_k, misc, b_qkv, b_z], axis=1).astype(BF16)
    head_mean = jnp.kron(jnp.eye(A_HEADS, dtype=F32),
                         jnp.full((A_HEAD_DIM, A_HEAD_DIM), 1.0 / A_HEAD_DIM, F32)).astype(BF16)
    return (norm_mix_w.reshape(1, D_MODEL), w_pack, w_kv_up.astype(BF16),
            jnp.tile(q_norm_w, A_HEADS).reshape(1, A_WIDTH),
            jnp.tile(k_norm_w, A_HEADS).reshape(1, A_WIDTH),
            kv_norm_w.reshape(1, KV_RANK), head_mean)


def _rope_tables(n_pos):
    half = A_HEAD_DIM // 2
    inv_freq = ROPE_THETA ** (-jnp.arange(0, A_HEAD_DIM, 2, dtype=F32) / A_HEAD_DIM)
    ang = jnp.arange(n_pos, dtype=F32)[:, None] * inv_freq[None, :]
    cos, sin = jnp.cos(ang), jnp.sin(ang)
    cos128 = jnp.tile(cos, (1, LANES // half))
    sin128 = jnp.tile(jnp.concatenate([-sin, sin], axis=1), (1, LANES // A_HEAD_DIM))
    return cos128, sin128


NEG_INF = float("-inf")
F32_MAX = float(jnp.finfo(jnp.float32).max)
INT_MIN = -2 ** 31


VT_TILE = 512
SUBLANES = 8
FOLD_WIDTH = 8


def _attn_kernel(q_ref, k_ref, vt_ref, km_ref, vmt_ref, iq_ref, ik_ref, misc_ref, o_ref,
                 isc_ref, lhs_ref, qm_ref, s_ref, m_ref, acc_ref, *, tq, kc, topk, pos_bits):
    j = pl.program_id(1)
    n_kc = lax.div((j + 1) * tq + (kc - 1), kc)
    g = kc // SUBLANES
    lane = lax.broadcasted_iota(jnp.int32, (1, LANES), 1)
    lo_half = lane < A_HEAD_DIM
    n_pairs = A_WIDTH // LANES
    nt = (((1,), (1,)), ((), ()))

    iq = iq_ref[0]
    q = q_ref[0]
    zero = jnp.zeros((), BF16)
    for p in range(n_pairs):
        blk = iq[:, p * LANES:(p + 1) * LANES]
        lhs_ref[(2 * p) * tq:(2 * p + 1) * tq, :] = jnp.where(lo_half, blk, zero)
        lhs_ref[(2 * p + 1) * tq:(2 * p + 2) * tq, :] = jnp.where(lo_half, zero, blk)
        qb = q[:, p * LANES:(p + 1) * LANES]
        qm_ref[p, :tq, :] = jnp.where(lo_half, qb, zero)
        qm_ref[p, tq:, :] = jnp.where(lo_half, zero, qb)
    w_t = misc_ref[0].T

    def fold0(x3, op):
        n = x3.shape[0]
        width = min(FOLD_WIDTH, n)
        acc = x3[:width]
        for i in range(width, n, width):
            acc = op(acc, x3[i:i + width])
        while width > 1:
            acc = op(acc[:width // 2], acc[width // 2:width])
            width //= 2
        return acc[0]

    q_pos = j * tq + lax.broadcasted_iota(jnp.int32, (1, tq), 1)

    def idx_body(c, carry):
        k0 = pl.multiple_of(c * kc, kc)
        r = lax.dot_general(ik_ref[0, pl.ds(k0, kc), :], lhs_ref[...], nt,
                            preferred_element_type=F32)
        s = None
        for h in range(IDX_HEADS):
            term = jnp.maximum(r[:, h * tq:(h + 1) * tq], 0.0) * w_t[MISC_W + h:MISC_W + h + 1, :]
            s = term if s is None else s + term
        k_pos = k0 + lax.broadcasted_iota(jnp.int32, (kc, 1), 0)
        isc_ref[c] = jnp.where(k_pos <= q_pos, s, NEG_INF)
        return carry
    lax.fori_loop(0, n_kc, idx_body, 0)

    def count(pred):
        def body(c, acc):
            x3 = isc_ref[c].reshape(g, SUBLANES, tq)
            return acc + fold0(jnp.where(pred(x3, c), 1.0, 0.0), jnp.add)
        acc = lax.fori_loop(0, n_kc, body, jnp.zeros((SUBLANES, tq), F32))
        return jnp.broadcast_to(jnp.sum(acc, axis=0, keepdims=True), (SUBLANES, tq))

    def key_to_float(u):
        key = u ^ jnp.int32(INT_MIN)
        bits = jnp.where(key >= 0, key, key ^ jnp.int32(0x7FFFFFFF))
        return lax.bitcast_convert_type(bits, F32)

    def bit_body(i, carry):
        u, n_u = carry
        u2 = u | lax.shift_left(jnp.int32(1), 31 - i)
        cand = key_to_float(u2)
        n_ge = count(lambda x3, c: x3 >= cand[None])
        keep = n_ge >= topk
        return jnp.where(keep, u2, u), jnp.where(keep, n_ge, n_u)
    u, n_u = lax.fori_loop(0, 32, bit_body, (jnp.zeros((SUBLANES, tq), jnp.int32),
                                             jnp.zeros((SUBLANES, tq), F32)))
    few = (u >= 0) & (u < 0x00800000)
    tau = jnp.where(few, -F32_MAX, key_to_float(u))

    n_ge = jnp.where(few, 0.0, n_u)

    @pl.when(jnp.max(n_ge) > topk)
    def _():
        n_gt = count(lambda x3, c: x3 > tau[None])
        need = topk - n_gt
        def chunk_pos(c):
            return (c * kc + lax.broadcasted_iota(jnp.int32, (g, SUBLANES, tq), 0) * SUBLANES
                    + lax.broadcasted_iota(jnp.int32, (g, SUBLANES, tq), 1))
        def pos_body(i, cut):
            cut2 = cut | lax.shift_left(jnp.int32(1), pos_bits - 1 - i)
            ties_before = count(lambda x3, c: (x3 == tau[None]) & (chunk_pos(c) < cut2[None]))
            return jnp.where(ties_before < need, cut2, cut)
        cut = lax.fori_loop(0, pos_bits, pos_body, jnp.zeros((SUBLANES, tq), jnp.int32))
        def drop_body(c, carry):
            x3 = isc_ref[c].reshape(g, SUBLANES, tq)
            drop = (x3 == tau[None]) & (chunk_pos(c) > cut[None])
            isc_ref[c] = jnp.where(drop, NEG_INF, x3).reshape(kc, tq)
            return carry
        lax.fori_loop(0, n_kc, drop_body, 0)

    tau_row = tau[0:1, :]

    def attend(key_pairs, bias, vt_slabs, first):
        n = key_pairs[0].shape[0]
        m_cur = []
        for p in range(n_pairs):
            s2 = lax.dot_general(key_pairs[p], qm_ref[p], nt, preferred_element_type=F32)
            for half in range(2):
                s = s2[:, half * tq:(half + 1) * tq] + bias
                s_ref[2 * p + half, :n, :] = s
                m8 = fold0(s.reshape(n // SUBLANES, SUBLANES, tq), jnp.maximum)
                m_cur.append(jnp.max(m8, axis=0, keepdims=True))
        for hd in range(A_HEADS):
            m_new = m_cur[hd] if first else jnp.maximum(m_ref[hd][0:1, :], m_cur[hd])
            e = jnp.exp2(s_ref[hd, :n, :] - m_new).astype(BF16)
            pv = None
            off = 0
            for slab in vt_slabs(hd):
                part = jnp.dot(slab, e[off:off + slab.shape[1], :], preferred_element_type=F32)
                pv = part if pv is None else pv + part
                off += slab.shape[1]
            if first:
                acc_ref[hd] = pv
            else:
                acc_ref[hd] = jnp.exp2(m_ref[hd][0:1, :] - m_new) * acc_ref[hd] + pv
            m_ref[hd] = jnp.broadcast_to(m_new, (SUBLANES, tq))

    meta_bias = jnp.where(lax.broadcasted_iota(jnp.int32, (LANES, 1), 0) < N_META, 0.0, NEG_INF)
    attend([km_ref[:, p * LANES:(p + 1) * LANES] for p in range(n_pairs)], meta_bias,
           lambda hd: [vmt_ref[hd * VT_ROWS:(hd + 1) * VT_ROWS, :]], True)

    def att_body(c, carry):
        k0 = pl.multiple_of(c * kc, kc)
        bias = jnp.where(isc_ref[c] >= tau_row, 0.0, NEG_INF)
        attend([k_ref[0, pl.ds(k0, kc), p * LANES:(p + 1) * LANES] for p in range(n_pairs)], bias,
               lambda hd: [vt_ref[c * (kc // VT_TILE) + t, hd * VT_ROWS:(hd + 1) * VT_ROWS, :]
                           for t in range(kc // VT_TILE)], False)
        return carry
    lax.fori_loop(0, n_kc, att_body, 0)

    for p in range(n_pairs):
        halves = []
        for hd in (2 * p, 2 * p + 1):
            acc = acc_ref[hd]
            halves.append(acc[:A_HEAD_DIM, :] / acc[A_HEAD_DIM:A_HEAD_DIM + 1, :])
        o_ref[0, :, p * LANES:(p + 1) * LANES] = jnp.concatenate(halves, axis=0).T.astype(BF16)


def _sparse_attention(q, k, vt, km, vmt, iq, ik, misc, *, tq, kc, topk):
    b, s, _ = q.shape
    assert tq % LANES == 0 and s % tq == 0 and kc % VT_TILE == 0 and s % kc == 0
    kernel = functools.partial(_attn_kernel, tq=tq, kc=kc, topk=topk, pos_bits=int(math.log2(s)))
    qblk = lambda bi, j: (bi, j, 0)
    full = lambda bi, j: (bi, 0, 0)
    fixed = lambda bi, j: (0, 0)
    vt_rows = A_HEADS * VT_ROWS
    once = pl.Buffered(1)
    return pl.pallas_call(
        kernel,
        grid=(b, s // tq),
        in_specs=[
            pl.BlockSpec((1, tq, A_WIDTH), qblk),
            pl.BlockSpec((1, s, A_WIDTH), full, pipeline_mode=once),
            pl.BlockSpec((s // VT_TILE, vt_rows, VT_TILE), full, pipeline_mode=once),
            pl.BlockSpec(km.shape, fixed),
            pl.BlockSpec(vmt.shape, fixed),
            pl.BlockSpec((1, tq, IDX_HEADS * IDX_DIM), qblk),
            pl.BlockSpec((1, s, LANES), full, pipeline_mode=once),
            pl.BlockSpec((1, tq, LANES), qblk),
        ],
        out_specs=pl.BlockSpec((1, tq, A_WIDTH), qblk),
        out_shape=jax.ShapeDtypeStruct((b, s, A_WIDTH), BF16),
        scratch_shapes=[
            pltpu.VMEM((s // kc, kc, tq), F32),
            pltpu.VMEM((IDX_HEADS * tq, LANES), BF16),
            pltpu.VMEM((A_WIDTH // LANES, 2 * tq, LANES), BF16),
            pltpu.VMEM((A_HEADS, max(kc, LANES), tq), F32),
            pltpu.VMEM((A_HEADS, SUBLANES, tq), F32),
            pltpu.VMEM((A_HEADS, VT_ROWS, tq), F32),
        ],
        compiler_params=pltpu.CompilerParams(dimension_semantics=("parallel", "arbitrary")),
        name="sparse_attention",
    )(q, k, vt, km, vmt, iq, ik, misc)


HALO = 8


def _softplus(x):
    return jnp.maximum(x, 0.0) + jnp.log1p(jnp.exp(-jnp.abs(x)))


def _bdot(a, b):
    return jnp.dot(a.astype(BF16), b.astype(BF16), preferred_element_type=F32)


def _hdot(a, b):
    return jnp.dot(a, b, precision=HIGHEST, preferred_element_type=F32)


def _dot3(a, b):
    a_hi = a.astype(BF16)
    b_hi = b.astype(BF16)
    a_lo = (a - a_hi.astype(F32)).astype(BF16)
    b_lo = (b - b_hi.astype(F32)).astype(BF16)
    dot = functools.partial(jnp.dot, preferred_element_type=F32)
    return dot(a_hi, b_hi) + (dot(a_hi, b_lo) + dot(a_lo, b_hi))


def _delta_kernel(qkv_ref, z_ref, misc_ref, qkvm_ref, miscm_ref, convw_ref, alog_ref, dtb_ref,
                  nw_ref, o_ref, state_ref, halo_ref):
    c = pl.program_id(1)
    n_pad = CHUNK - N_META
    nt = (((1,), (1,)), ((), ()))
    tn = (((0,), (0,)), ((), ()))

    @pl.when(c == 0)
    def _():
        state_ref[...] = jnp.zeros_like(state_ref)
        halo_ref[...] = jnp.zeros_like(halo_ref)

    ri = lax.broadcasted_iota(jnp.int32, (CHUNK, CHUNK), 0)
    ci = lax.broadcasted_iota(jnp.int32, (CHUNK, CHUNK), 1)
    incl = ri >= ci
    strict = ri > ci
    eye = (ri == ci).astype(F32)
    tri = incl.astype(F32)
    row = lax.broadcasted_iota(jnp.int32, (CHUNK, 1), 0)
    is_meta = c == 0
    neutral = jnp.logical_and(is_meta, row < n_pad)
    cols = lambda base, h: slice(base + h * B_HEAD_DIM, base + (h + 1) * B_HEAD_DIM)

    nb = qkv_ref.shape[0]
    heads = range(nb * B_HEADS)
    qn, kn, vb, kb, gc, decay = [], [], [], [], [], []
    for bi in range(nb):
        xin = jnp.where(is_meta, qkvm_ref[...], qkv_ref[bi])
        misc = jnp.where(is_meta, miscm_ref[...], misc_ref[bi])
        xcat = jnp.concatenate([halo_ref[bi], xin], axis=0)
        halo_ref[bi] = xin[CHUNK - HALO:, :]
        conv = None
        for tap in range(CONV_WIDTH):
            off = HALO - (CONV_WIDTH - 1) + tap
            term = xcat[off:off + CHUNK, :] * convw_ref[tap:tap + 1, :]
            conv = term if conv is None else conv + term
        xc = conv * jax.nn.sigmoid(conv)
        beta_all = jnp.where(neutral, 0.0, jax.nn.sigmoid(misc))
        g_all = jnp.where(neutral, 0.0, -jnp.exp(alog_ref[...]) * _softplus(misc + dtb_ref[...]))
        gc_all = _hdot(tri, g_all)
        gc_all_t = gc_all.T
        for h in range(B_HEADS):
            qh, kh, vh = xc[:, cols(0, h)], xc[:, cols(B_WIDTH, h)], xc[:, cols(2 * B_WIDTH, h)]
            qn.append(qh * lax.rsqrt(jnp.sum(qh * qh, axis=-1, keepdims=True) + EPS) * (B_HEAD_DIM ** -0.5))
            kn.append(kh * lax.rsqrt(jnp.sum(kh * kh, axis=-1, keepdims=True) + EPS))
            beta = jnp.broadcast_to(beta_all[:, MISC_BETA + h:MISC_BETA + h + 1], (CHUNK, B_HEAD_DIM))
            gc.append(jnp.broadcast_to(gc_all[:, MISC_DECAY + h:MISC_DECAY + h + 1], (CHUNK, B_HEAD_DIM)))
            g_col = gc[-1][:, :CHUNK]
            g_row = gc_all_t[MISC_DECAY + h:MISC_DECAY + h + 1, :]
            decay.append(jnp.where(incl, jnp.exp(jnp.where(incl, g_col - g_row, 0.0)), 0.0))
            kb.append(kn[-1] * beta)
            vb.append(vh * beta)

    kk = [lax.dot_general(kb[h].astype(BF16), kn[h].astype(BF16), nt, preferred_element_type=F32)
          for h in heads]
    qk = [lax.dot_general(qn[h].astype(BF16), kn[h].astype(BF16), nt, preferred_element_type=F32)
          for h in heads]
    intra = [jnp.where(incl, qk[h] * decay[h], 0.0) for h in heads]

    pw = [jnp.where(strict, -(kk[h] * decay[h]), 0.0) for h in heads]
    t_inv = [eye + pw[h] for h in heads]
    pw = [_dot3(pw[h], pw[h]) for h in heads]
    for _ in range(int(math.log2(CHUNK)) - 2):
        t_next = [t_inv[h] + _dot3(t_inv[h], pw[h]) for h in heads]
        pw = [_dot3(pw[h], pw[h]) for h in heads]
        t_inv = t_next
    t_inv = [t_inv[h] + _dot3(t_inv[h], pw[h]) for h in heads]

    u = [_bdot(t_inv[h], vb[h]) for h in heads]
    w = [_bdot(t_inv[h], kb[h] * jnp.exp(gc[h])) for h in heads]

    state = [state_ref[h] for h in heads]
    w_s = [_bdot(w[h], state[h]) for h in heads]
    q_s = [_bdot(qn[h] * jnp.exp(gc[h]), state[h]) for h in heads]
    v_new = [u[h] - w_s[h] for h in heads]
    o = [q_s[h] + _bdot(intra[h], v_new[h]) for h in heads]
    for h in heads:
        g_last = gc[h][CHUNK - 1:CHUNK, :]
        kd = kn[h] * jnp.exp(g_last - gc[h])
        state_ref[h] = state[h] * jnp.exp(g_last) + lax.dot_general(
            kd.astype(BF16), v_new[h].astype(BF16), tn, preferred_element_type=F32)

    for h in heads:
        bi, hh = divmod(h, B_HEADS)
        y = o[h] * lax.rsqrt(jnp.mean(o[h] * o[h], axis=-1, keepdims=True) + EPS) * nw_ref[...]
        zh = z_ref[bi, :, cols(0, hh)]
        o_ref[bi, :, cols(0, hh)] = (y * (zh * jax.nn.sigmoid(zh))).astype(BF16)


DELTA_BATCH = 4


def _gated_deltanet(qkv, z, misc, qkv_m, misc_m, conv_w, a_log, dt_bias, norm_w):
    b, s, _ = qkv.shape
    n_chunks = s // CHUNK + 1
    nb = DELTA_BATCH if b % DELTA_BATCH == 0 else 1
    blk = lambda bi, c: (bi, jnp.maximum(c - 1, 0), 0)
    fixed = lambda bi, c: (0, 0)
    lane_vec = lambda v: jnp.zeros((1, LANES), F32).at[0, MISC_DECAY:MISC_DECAY + B_HEADS].set(v)
    return pl.pallas_call(
        _delta_kernel,
        grid=(b // nb, n_chunks),
        in_specs=[
            pl.BlockSpec((nb, CHUNK, 3 * B_WIDTH), blk),
            pl.BlockSpec((nb, CHUNK, B_WIDTH), blk),
            pl.BlockSpec((nb, CHUNK, LANES), blk),
            pl.BlockSpec((CHUNK, 3 * B_WIDTH), fixed),
            pl.BlockSpec((CHUNK, LANES), fixed),
            pl.BlockSpec((CONV_WIDTH, 3 * B_WIDTH), fixed),
            pl.BlockSpec((1, LANES), fixed),
            pl.BlockSpec((1, LANES), fixed),
            pl.BlockSpec((1, B_HEAD_DIM), fixed),
        ],
        out_specs=pl.BlockSpec((nb, CHUNK, B_WIDTH), blk),
        out_shape=jax.ShapeDtypeStruct((b, s, B_WIDTH), BF16),
        scratch_shapes=[
            pltpu.VMEM((nb * B_HEADS, B_HEAD_DIM, B_HEAD_DIM), F32),
            pltpu.VMEM((nb, HALO, 3 * B_WIDTH), F32),
        ],
        compiler_params=pltpu.CompilerParams(dimension_semantics=("parallel", "arbitrary")),
        name="gated_deltanet",
    )(qkv, z, misc, qkv_m, misc_m, conv_w, lane_vec(a_log), lane_vec(dt_bias),
      norm_w.reshape(1, B_HEAD_DIM))


def _pack_rows(a):
    half = a.shape[1] // 2
    bits = lax.bitcast_convert_type(a.astype(BF16).astype(F32), jnp.int32)
    return bits[:, :half] | lax.shift_right_logical(bits[:, half:], 16)


def _unpack_rows(w):
    hi = lax.bitcast_convert_type(w & jnp.int32(-65536), F32)
    lo = lax.bitcast_convert_type(lax.shift_left(w, 16), F32)
    return jnp.concatenate([hi, lo], axis=1)


def _out_router_kernel(oa_ref, ob_ref, x_ref, wo_ref, nw_ref, wr_ref, br_ref,
                       h_ref, xn_ref, gate_ref, eid_ref):
    mix = (jnp.dot(oa_ref[...], wo_ref[:A_WIDTH, :], preferred_element_type=F32)
           + jnp.dot(ob_ref[...], wo_ref[A_WIDTH:, :], preferred_element_type=F32))
    h = x_ref[...] + mix
    h_ref[...] = h
    xn = h * lax.rsqrt(jnp.mean(h * h, axis=-1, keepdims=True) + EPS) * nw_ref[...]
    xn_ref[...] = _pack_rows(xn)
    tm = xn.shape[0]
    logits = lax.dot_general(wr_ref[...], xn, (((1,), (1,)), ((), ())), precision=HIGHEST,
                             preferred_element_type=F32) + br_ref[...]
    e_idx = lax.broadcasted_iota(jnp.int32, (N_EXPERTS, tm), 0).astype(F32)
    work = logits
    vals, idxs = [], []
    for _ in range(TOP_K):
        m = jnp.max(work, axis=0, keepdims=True)
        idx = jnp.min(jnp.where(work == m, e_idx, float(N_EXPERTS)), axis=0, keepdims=True)
        vals.append(m)
        idxs.append(idx)
        work = jnp.where(e_idx == idx, NEG_INF, work)
    exps = [jnp.exp(v - vals[0]) for v in vals]
    denom = exps[0]
    for e in exps[1:]:
        denom = denom + e
    eid_ref[...] = jnp.concatenate(idxs + [jnp.zeros((SUBLANES - TOP_K, tm), F32)], axis=0).astype(jnp.int32)
    gates_t = jnp.concatenate([e / denom for e in exps] + [jnp.zeros((LANES - TOP_K, tm), F32)], axis=0)
    gate_ref[...] = gates_t.T


def _out_router(o_a, o_b, x2d, w_out, norm_w, w_router, b_router, tm):
    n = x2d.shape[0]
    row = lambda i: (i, 0)
    fixed = lambda i: (0, 0)
    wr = w_router.T
    br = b_router.reshape(N_EXPERTS, 1)
    return pl.pallas_call(
        _out_router_kernel,
        grid=(n // tm,),
        in_specs=[
            pl.BlockSpec((tm, A_WIDTH), row),
            pl.BlockSpec((tm, B_WIDTH), row),
            pl.BlockSpec((tm, D_MODEL), row),
            pl.BlockSpec((A_WIDTH + B_WIDTH, D_MODEL), fixed),
            pl.BlockSpec((1, D_MODEL), fixed),
            pl.BlockSpec((N_EXPERTS, D_MODEL), fixed),
            pl.BlockSpec((N_EXPERTS, 1), fixed),
        ],
        out_specs=[pl.BlockSpec((tm, D_MODEL), row), pl.BlockSpec((tm, D_MODEL // 2), row),
                   pl.BlockSpec((tm, LANES), row), pl.BlockSpec((SUBLANES, tm), lambda i: (0, i))],
        out_shape=[jax.ShapeDtypeStruct((n, D_MODEL), F32), jax.ShapeDtypeStruct((n, D_MODEL // 2), jnp.int32),
                   jax.ShapeDtypeStruct((n, LANES), F32), jax.ShapeDtypeStruct((SUBLANES, n), jnp.int32)],
        compiler_params=pltpu.CompilerParams(dimension_semantics=("parallel",)),
        name="out_proj_router",
    )(o_a, o_b, x2d, w_out.astype(BF16), norm_w.reshape(1, D_MODEL), wr, br)


MOE_BM = 512
TOKEN_TILE = 512
COMBINE_TILE = 1024
ATT_KC = 512
ATT_TQ = 512
MOE_BF = 512


SC_CORES = 2
SC_SUBCORES = 16
SC_ROWS = 128


def _sc_gather_rows(table, idx):
    n_workers = SC_CORES * SC_SUBCORES
    n_rows = idx.shape[0]
    d = table.shape[1]
    assert n_rows % (n_workers * SC_ROWS) == 0
    rows_per_worker = n_rows // n_workers
    mesh = plsc.VectorSubcoreMesh(core_axis_name="c", subcore_axis_name="s",
                                  num_cores=SC_CORES, num_subcores=SC_SUBCORES)

    @functools.partial(
        pl.kernel, mesh=mesh,
        out_type=jax.ShapeDtypeStruct((n_rows, d), table.dtype),
        scratch_types=[pltpu.VMEM((SC_ROWS,), jnp.int32), pltpu.VMEM((SC_ROWS, d), table.dtype),
                       pltpu.SemaphoreType.DMA],
        name="sc_gather_rows",
    )
    def gather(table_hbm, idx_hbm, out_hbm, idx_v, rows_v, sem):
        wid = lax.axis_index("s") * SC_CORES + lax.axis_index("c")
        base = wid * rows_per_worker

        @pl.loop(0, rows_per_worker // SC_ROWS)
        def _(i):
            off = base + i * SC_ROWS
            pltpu.sync_copy(idx_hbm.at[pl.ds(off, SC_ROWS)], idx_v)
            pltpu.async_copy(table_hbm.at[idx_v], rows_v, sem).wait()
            pltpu.sync_copy(rows_v, out_hbm.at[pl.ds(off, SC_ROWS)])

    return gather(table, idx)


def _route_blocks(eid_t, bm):
    n_tok = eid_t.shape[1]
    n_assign = n_tok * TOP_K
    experts = jnp.arange(N_EXPERTS, dtype=jnp.int32)
    flat_e = eid_t.reshape(-1)
    sorted_e, order = lax.sort((flat_e, jnp.arange(n_assign, dtype=jnp.int32)), num_keys=1)
    onehot = sorted_e[:, None] == experts[None, :]
    counts = jnp.sum(onehot, axis=0, dtype=jnp.int32)
    padded = (counts + bm - 1) // bm * bm
    start = jnp.cumsum(counts) - counts
    pend = jnp.cumsum(padded)
    pstart = pend - padded
    dest = jnp.arange(n_assign, dtype=jnp.int32) + jnp.sum(
        jnp.where(onehot, (pstart - start)[None, :], 0), axis=1)
    n_blocks = -(-(n_assign + N_EXPERTS * (bm - 1)) // bm)
    blk_start = jnp.arange(n_blocks, dtype=jnp.int32) * bm
    block_e = jnp.minimum(jnp.sum(blk_start[:, None] >= pend[None, :], axis=1), N_EXPERTS - 1)
    n_valid = pend[-1] // bm
    _, pos = lax.sort((order, dest), num_keys=1)
    t = jnp.arange(bm, dtype=jnp.int32)[None, :]
    pad_key = jnp.where(t < (padded - counts)[:, None], (pstart + counts)[:, None] + t,
                        n_blocks * bm).reshape(-1)
    pad_tok = jnp.arange(N_EXPERTS * bm, dtype=jnp.int32) % n_tok
    assert n_blocks * bm - n_assign == N_EXPERTS * bm
    _, row_tok = lax.sort((jnp.concatenate([dest, pad_key]),
                           jnp.concatenate([order % n_tok, pad_tok])), num_keys=1)
    return block_e.astype(jnp.int32), n_valid.astype(jnp.int32).reshape(1), row_tok, pos


def _moe_dense_kernel(be_ref, nv_ref, x_ref, wgu_ref, wd_ref, bg_ref, bu_ref, bd_ref, perm_ref, y_ref,
                      wg_s, wu_s, wd_s):
    i = pl.program_id(0)
    live = i < nv_ref[0]

    @pl.when(jnp.logical_and(live, jnp.logical_or(i == 0, be_ref[i] != be_ref[jnp.maximum(i - 1, 0)])))
    def _():
        perm = perm_ref[...]
        grp = 2 * LANES
        for gidx in range(2 * D_FF // grp):
            blk = wgu_ref[0, :, gidx * grp:(gidx + 1) * grp].astype(BF16)
            split = jnp.dot(blk, perm, preferred_element_type=F32).astype(BF16)
            wg_s[:, gidx * LANES:(gidx + 1) * LANES] = split[:, :LANES]
            wu_s[:, gidx * LANES:(gidx + 1) * LANES] = split[:, LANES:]
        wd_s[...] = wd_ref[0].astype(BF16)

    @pl.when(live)
    def _():
        x = _unpack_rows(x_ref[...]).astype(BF16)
        y = None
        for f0 in range(0, D_FF, MOE_BF):
            g = jnp.dot(x, wg_s[:, f0:f0 + MOE_BF], preferred_element_type=F32) + bg_ref[0, :, f0:f0 + MOE_BF]
            u = jnp.dot(x, wu_s[:, f0:f0 + MOE_BF], preferred_element_type=F32) + bu_ref[0, :, f0:f0 + MOE_BF]
            gate = jnp.minimum(g, SWIGLU_LIMIT)
            up = jnp.clip(u, -SWIGLU_LIMIT, SWIGLU_LIMIT)
            t = gate * jax.nn.sigmoid(gate * SWIGLU_ALPHA) * (up + 1.0)
            part = jnp.dot(t.astype(BF16), wd_s[f0:f0 + MOE_BF, :], preferred_element_type=F32)
            y = part if y is None else y + part
        y_ref[...] = _pack_rows(y + bd_ref[0])


def _moe_dense(x_sorted, block_e, n_valid, w_gate_up, w_down, b_gate, b_up, b_down, bm):
    n_blocks = block_e.shape[0]
    src = jnp.arange(2 * LANES)
    perm = jax.nn.one_hot((src % 2) * LANES + src // 2, 2 * LANES, dtype=BF16)
    rows = lambda i, be, nv: (jnp.minimum(i, nv[0] - 1), 0)
    wsel = lambda i, be, nv: (be[i], 0, 0)
    fixed = lambda i, be, nv: (0, 0)
    grid_spec = pltpu.PrefetchScalarGridSpec(
        num_scalar_prefetch=2,
        grid=(n_blocks,),
        in_specs=[
            pl.BlockSpec((bm, D_MODEL // 2), rows),
            pl.BlockSpec((1, D_MODEL, 2 * D_FF), wsel),
            pl.BlockSpec((1, D_FF, D_MODEL), wsel),
            pl.BlockSpec((1, 1, D_FF), wsel),
            pl.BlockSpec((1, 1, D_FF), wsel),
            pl.BlockSpec((1, 1, D_MODEL), wsel),
            pl.BlockSpec((2 * LANES, 2 * LANES), fixed),
        ],
        out_specs=pl.BlockSpec((bm, D_MODEL // 2), rows),
        scratch_shapes=[
            pltpu.VMEM((D_MODEL, D_FF), BF16),
            pltpu.VMEM((D_MODEL, D_FF), BF16),
            pltpu.VMEM((D_FF, D_MODEL), BF16),
        ],
    )
    return pl.pallas_call(
        _moe_dense_kernel,
        grid_spec=grid_spec,
        out_shape=jax.ShapeDtypeStruct(x_sorted.shape, jnp.int32),
        compiler_params=pltpu.CompilerParams(dimension_semantics=("arbitrary",)),
        name="moe_experts",
    )(block_e, n_valid, x_sorted, w_gate_up, w_down, b_gate, b_up, b_down, perm)


def _combine_kernel(h_ref, gate_ref, y0_ref, y1_ref, y2_ref, y3_ref, o_ref):
    gates = gate_ref[...]
    out = h_ref[...]
    for kk, y_ref in enumerate((y0_ref, y1_ref, y2_ref, y3_ref)):
        out = out + gates[:, kk:kk + 1] * _unpack_rows(y_ref[...])
    o_ref[...] = out


def _combine(h, gates, y, tm):
    n_tok = h.shape[0]
    nt = n_tok // tm
    row = lambda i: (i, 0)
    return pl.pallas_call(
        _combine_kernel,
        grid=(nt,),
        in_specs=[pl.BlockSpec((tm, D_MODEL), row), pl.BlockSpec((tm, LANES), row)]
        + [pl.BlockSpec((tm, D_MODEL // 2), functools.partial(lambda kk, i: (kk * nt + i, 0), kk))
           for kk in range(TOP_K)],
        out_specs=pl.BlockSpec((tm, D_MODEL), row),
        out_shape=jax.ShapeDtypeStruct((n_tok, D_MODEL), F32),
        compiler_params=pltpu.CompilerParams(dimension_semantics=("parallel",)),
        name="moe_combine",
    )(h, gates, y, y, y, y)


def kernel(x, meta_tokens, norm_mix_w, w_in, q_norm_w, k_norm_w, kv_norm_w, w_kv_up, conv_w, a_log,
           dt_bias, delta_norm_w, w_out, norm_ffn_w, w_router, b_router, w_gate_up, b_gate_up,
           w_down, b_down):
    b, s, d = x.shape
    consts = _pack_in_proj_weights(norm_mix_w[0], w_in[0], q_norm_w[0], k_norm_w[0], kv_norm_w[0],
                                   w_kv_up[0])
    cos, sin = _rope_tables(N_META + s)
    real = _in_proj(x.reshape(b * s, d), cos[N_META:], sin[N_META:], consts, VT_TILE)
    meta = _in_proj(meta_tokens, cos[:N_META], sin[:N_META], consts, N_META)
    vt = real[2]
    q, k, iq, ik, misc, qkv, z = (a.reshape(b, s, a.shape[-1]) for a in real[:2] + real[3:])
    km = jnp.pad(meta[1], ((0, LANES - N_META), (0, 0)))
    vmt = jnp.pad(meta[2][0], ((0, 0), (0, LANES - N_META)))
    o_a = _sparse_attention(q, k, vt, km, vmt, iq, ik, misc,
                            tq=ATT_TQ, kc=min(ATT_KC, s), topk=min(INDEX_TOPK, s // 4))
    lead = lambda a: jnp.pad(a, ((CHUNK - N_META, 0), (0, 0)))
    o_b = _gated_deltanet(qkv, z, misc, lead(meta[6]), lead(meta[5]), conv_w[0], a_log[0],
                          dt_bias[0], delta_norm_w[0])
    n_tok = b * s
    h, xn, gates, eid = _out_router(o_a.reshape(n_tok, A_WIDTH), o_b.reshape(n_tok, B_WIDTH),
                                    x.reshape(n_tok, d), w_out[0], norm_ffn_w[0], w_router[0],
                                    b_router[0], TOKEN_TILE)
    block_e, n_valid, row_tok, pos = _route_blocks(eid[:TOP_K], MOE_BM)
    bgu = b_gate_up[0].reshape(N_EXPERTS, 1, 2 * D_FF)
    x_sorted = _sc_gather_rows(xn, row_tok)
    y_sorted = _moe_dense(x_sorted, block_e, n_valid, w_gate_up[0], w_down[0], bgu[:, :, 0::2],
                          bgu[:, :, 1::2], b_down[0].reshape(N_EXPERTS, 1, D_MODEL), MOE_BM)
    y = _sc_gather_rows(y_sorted, pos)
    out = _combine(h, gates, y, min(COMBINE_TILE, n_tok))
    return out.reshape(b, s, d)
```

```python
import functools
import math

import jax
import jax.numpy as jnp
from jax import lax
from jax.experimental import pallas as pl
from jax.experimental.pallas import tpu as pltpu
from jax.experimental.pallas import tpu_sc as plsc

F32 = jnp.float32
BF16 = jnp.bfloat16
HIGHEST = lax.Precision.HIGHEST

D_MODEL = 1024
N_META = 16
ROPE_THETA = 10000.0
EPS = 1e-6
A_HEAD_DIM = 64
A_HEADS = 8
A_WIDTH = A_HEADS * A_HEAD_DIM
KV_RANK = 256
IDX_HEADS = 8
IDX_DIM = 64
INDEX_TOPK = 256
B_HEAD_DIM = 128
B_HEADS = 4
B_WIDTH = B_HEADS * B_HEAD_DIM
CONV_WIDTH = 4
CHUNK = 64
N_EXPERTS = 32
TOP_K = 4
D_FF = D_MODEL
SWIGLU_LIMIT = 7.0
SWIGLU_ALPHA = 1.702
IN_SPLITS = (A_WIDTH, KV_RANK, IDX_HEADS * IDX_DIM, IDX_DIM, IDX_HEADS, 3 * B_WIDTH, B_WIDTH,
             B_HEADS, B_HEADS)

LANES = 128

C_Q = 0
C_CKV = C_Q + A_WIDTH
C_IQ = C_CKV + KV_RANK
C_IK = C_IQ + IDX_HEADS * IDX_DIM
C_MISC = C_IK + LANES
C_QKV = C_MISC + LANES
C_Z = C_QKV + 3 * B_WIDTH
C_END = C_Z + B_WIDTH
MISC_W, MISC_BETA, MISC_DECAY = 0, IDX_HEADS, IDX_HEADS + B_HEADS
LOG2E = math.log2(math.e)
BF16_SUBLANES = 16
VT_ROWS = A_HEAD_DIM + BF16_SUBLANES


def _rope_partner(a):
    lane = lax.broadcasted_iota(jnp.int32, a.shape, 1)
    first_half = (lane % A_HEAD_DIM) < (A_HEAD_DIM // 2)
    return jnp.where(first_half, pltpu.roll(a, LANES - A_HEAD_DIM // 2, 1),
                     pltpu.roll(a, A_HEAD_DIM // 2, 1))


def _rope(a, cos, sin_signed):
    return a * cos + _rope_partner(a) * sin_signed


def _head_rms(a, head_mean, gain):
    msq = jnp.dot((a * a).astype(BF16), head_mean, preferred_element_type=F32)
    return a * lax.rsqrt(msq + EPS) * gain


def _in_proj_kernel(x_ref, nw_ref, w_ref, wkv_ref, qnw_ref, knw_ref, kvnw_ref, cos_ref, sin_ref,
                    hm_ref, q_ref, k_ref, vt_ref, iq_ref, ik_ref, misc_ref, qkv_ref, z_ref):
    x = x_ref[...]
    u = x * lax.rsqrt(jnp.mean(x * x, axis=-1, keepdims=True) + EPS) * nw_ref[...]
    ub = u.astype(BF16)

    def proj(c0, c1):
        return jnp.dot(ub, w_ref[:, c0:c1], preferred_element_type=F32)

    cos = cos_ref[...]
    sin = sin_ref[...]
    hm = hm_ref[...]

    def rope_groups(a):
        return jnp.concatenate(
            [_rope(a[:, g * LANES:(g + 1) * LANES], cos, sin) for g in range(a.shape[1] // LANES)],
            axis=1)

    q = _head_rms(proj(C_Q, C_CKV), hm, qnw_ref[...])
    q_ref[...] = (rope_groups(q) * (A_HEAD_DIM ** -0.5 * LOG2E)).astype(BF16)

    ckv = proj(C_CKV, C_IQ)
    ckv = ckv * lax.rsqrt(jnp.mean(ckv * ckv, axis=-1, keepdims=True) + EPS) * kvnw_ref[...]
    kv = jnp.dot(ckv.astype(BF16), wkv_ref[...], preferred_element_type=F32)
    k = _head_rms(kv[:, :A_WIDTH], hm, knw_ref[...])
    k_ref[...] = rope_groups(k).astype(BF16)
    tm = x.shape[0]
    ones = jnp.ones((VT_ROWS - A_HEAD_DIM, tm), BF16)
    for g in range(A_WIDTH // LANES):
        vt = kv[:, A_WIDTH + g * LANES:A_WIDTH + (g + 1) * LANES].T.astype(BF16)
        for half in range(2):
            r0 = (2 * g + half) * VT_ROWS
            vt_ref[0, r0:r0 + A_HEAD_DIM, :] = vt[half * A_HEAD_DIM:(half + 1) * A_HEAD_DIM, :]
            vt_ref[0, r0 + A_HEAD_DIM:r0 + VT_ROWS, :] = ones

    iq_ref[...] = rope_groups(proj(C_IQ, C_IK)).astype(BF16)
    ik_ref[...] = _rope(proj(C_IK, C_MISC), cos, sin).astype(BF16)

    lane = lax.broadcasted_iota(jnp.int32, (1, LANES), 1)
    w_scale = jnp.where(lane < IDX_HEADS, IDX_HEADS ** -0.5 * IDX_DIM ** -0.5, 1.0)
    misc_ref[...] = proj(C_MISC, C_QKV) * w_scale
    qkv_ref[...] = proj(C_QKV, C_Z)
    z_ref[...] = proj(C_Z, C_END)


def _in_proj(x2d, cos, sin, consts, tm):
    n = x2d.shape[0]
    n_pos_blocks = cos.shape[0] // tm
    nw, w_pack, wkv, qnw, knw, kvnw, hm = consts
    row = lambda i: (i, 0)
    fixed = lambda i: (0, 0)
    pos = lambda i: (i % n_pos_blocks, 0)
    out_widths = (A_WIDTH, A_WIDTH, None, IDX_HEADS * IDX_DIM, LANES, LANES, 3 * B_WIDTH, B_WIDTH)
    out_dtypes = (BF16, BF16, BF16, BF16, BF16, F32, F32, F32)
    vt_rows = A_HEADS * VT_ROWS
    out_specs = [pl.BlockSpec((1, vt_rows, tm), lambda i: (i, 0, 0)) if w is None
                 else pl.BlockSpec((tm, w), row) for w in out_widths]
    out_shape = [jax.ShapeDtypeStruct((n // tm, vt_rows, tm) if w is None else (n, w), dt)
                 for w, dt in zip(out_widths, out_dtypes)]
    return pl.pallas_call(
        _in_proj_kernel,
        grid=(n // tm,),
        in_specs=[
            pl.BlockSpec((tm, D_MODEL), row),
            pl.BlockSpec(nw.shape, fixed),
            pl.BlockSpec(w_pack.shape, fixed),
            pl.BlockSpec(wkv.shape, fixed),
            pl.BlockSpec(qnw.shape, fixed),
            pl.BlockSpec(knw.shape, fixed),
            pl.BlockSpec(kvnw.shape, fixed),
            pl.BlockSpec((tm, LANES), pos),
            pl.BlockSpec((tm, LANES), pos),
            pl.BlockSpec(hm.shape, fixed),
        ],
        out_specs=out_specs,
        out_shape=out_shape,
        compiler_params=pltpu.CompilerParams(dimension_semantics=("parallel",)),
        name="in_proj",
    )(x2d, nw, w_pack, wkv, qnw, knw, kvnw, cos, sin, hm)


def _pack_in_proj_weights(norm_mix_w, w_in, q_norm_w, k_norm_w, kv_norm_w, w_kv_up):
    points = []
    acc = 0
    for s in IN_SPLITS:
        points.append((acc, acc + s))
        acc += s
    a_q, a_ckv, i_q, i_k, i_w, b_qkv, b_z, b_beta, b_a = (w_in[:, a:b] for a, b in points)
    misc = jnp.concatenate(
        [i_w, b_beta, b_a, jnp.zeros((D_MODEL, LANES - IDX_HEADS - 2 * B_HEADS), w_in.dtype)], axis=1)
    w_pack = jnp.concatenate([a_q, a_ckv, i_q, i_k, i_k, misc, b_qkv, b_z], axis=1).astype(BF16)
    head_mean = jnp.kron(jnp.eye(A_HEADS, dtype=F32),
                         jnp.full((A_HEAD_DIM, A_HEAD_DIM), 1.0 / A_HEAD_DIM, F32)).astype(BF16)
    return (norm_mix_w.reshape(1, D_MODEL), w_pack, w_kv_up.astype(BF16),
            jnp.tile(q_norm_w, A_HEADS).reshape(1, A_WIDTH),
            jnp.tile(k_norm_w, A_HEADS).reshape(1, A_WIDTH),
            kv_norm_w.reshape(1, KV_RANK), head_mean)


def _rope_tables(n_pos):
    half = A_HEAD_DIM // 2
    inv_freq = ROPE_THETA ** (-jnp.arange(0, A_HEAD_DIM, 2, dtype=F32) / A_HEAD_DIM)
    ang = jnp.arange(n_pos, dtype=F32)[:, None] * inv_freq[None, :]
    cos, sin = jnp.cos(ang), jnp.sin(ang)
    cos128 = jnp.tile(cos, (1, LANES // half))
    sin128 = jnp.tile(jnp.concatenate([-sin, sin], axis=1), (1, LANES // A_HEAD_DIM))
    return cos128, sin128


NEG_INF = float("-inf")
F32_MAX = float(jnp.finfo(jnp.float32).max)
INT_MIN = -2 ** 31


VT_TILE = 512
SUBLANES = 8
FOLD_WIDTH = 8


def _attn_kernel(q_ref, k_ref, vt_ref, km_ref, vmt_ref, iq_ref, ik_ref, misc_ref, o_ref,
                 isc_ref, lhs_ref, qm_ref, s_ref, m_ref, acc_ref, *, tq, kc, topk, pos_bits):
    j = pl.program_id(1)
    n_kc = lax.div((j + 1) * tq + (kc - 1), kc)
    g = kc // SUBLANES
    lane = lax.broadcasted_iota(jnp.int32, (1, LANES), 1)
    lo_half = lane < A_HEAD_DIM
    n_pairs = A_WIDTH // LANES
    nt = (((1,), (1,)), ((), ()))

    iq = iq_ref[0]
    q = q_ref[0]
    zero = jnp.zeros((), BF16)
    for p in range(n_pairs):
        blk = iq[:, p * LANES:(p + 1) * LANES]
        lhs_ref[(2 * p) * tq:(2 * p + 1) * tq, :] = jnp.where(lo_half, blk, zero)
        lhs_ref[(2 * p + 1) * tq:(2 * p + 2) * tq, :] = jnp.where(lo_half, zero, blk)
        qb = q[:, p * LANES:(p + 1) * LANES]
        qm_ref[p, :tq, :] = jnp.where(lo_half, qb, zero)
        qm_ref[p, tq:, :] = jnp.where(lo_half, zero, qb)
    w_t = misc_ref[0].T

    def fold0(x3, op):
        n = x3.shape[0]
        width = min(FOLD_WIDTH, n)
        acc = x3[:width]
        for i in range(width, n, width):
            acc = op(acc, x3[i:i + width])
        while width > 1:
            acc = op(acc[:width // 2], acc[width // 2:width])
            width //= 2
        return acc[0]

    q_pos = j * tq + lax.broadcasted_iota(jnp.int32, (1, tq), 1)

    def idx_body(c, carry):
        k0 = pl.multiple_of(c * kc, kc)
        r = lax.dot_general(ik_ref[0, pl.ds(k0, kc), :], lhs_ref[...], nt,
                            preferred_element_type=F32)
        s = None
        for h in range(IDX_HEADS):
            term = jnp.maximum(r[:, h * tq:(h + 1) * tq], 0.0) * w_t[MISC_W + h:MISC_W + h + 1, :]
            s = term if s is None else s + term
        k_pos = k0 + lax.broadcasted_iota(jnp.int32, (kc, 1), 0)
        isc_ref[c] = jnp.where(k_pos <= q_pos, s, NEG_INF)
        return carry
    lax.fori_loop(0, n_kc, idx_body, 0)

    grp = FOLD_WIDTH * SUBLANES
    def count(pred):
        def body(c, acc):
            for r0 in range(0, kc, grp):
                x3 = isc_ref[c, r0:r0 + grp, :].reshape(FOLD_WIDTH, SUBLANES, tq)
                acc = acc + jnp.where(pred(x3, c * kc + r0), 1.0, 0.0)
            return acc
        acc = lax.fori_loop(0, n_kc, body, jnp.zeros((FOLD_WIDTH, SUBLANES, tq), F32))
        total = jnp.sum(fold0(acc, jnp.add), axis=0, keepdims=True)
        return jnp.broadcast_to(total, (SUBLANES, tq))

    def key_to_float(u):
        key = u ^ jnp.int32(INT_MIN)
        bits = jnp.where(key >= 0, key, key ^ jnp.int32(0x7FFFFFFF))
        return lax.bitcast_convert_type(bits, F32)

    def bit_body(i, carry):
        u, n_u = carry
        u2 = u | lax.shift_left(jnp.int32(1), 31 - i)
        cand = key_to_float(u2)
        n_ge = count(lambda x3, pos0: x3 >= cand[None])
        keep = n_ge >= topk
        return jnp.where(keep, u2, u), jnp.where(keep, n_ge, n_u)
    u, n_u = lax.fori_loop(0, 32, bit_body, (jnp.zeros((SUBLANES, tq), jnp.int32),
                                             jnp.zeros((SUBLANES, tq), F32)))
    few = (u >= 0) & (u < 0x00800000)
    tau = jnp.where(few, -F32_MAX, key_to_float(u))

    n_ge = jnp.where(few, 0.0, n_u)

    @pl.when(jnp.max(n_ge) > topk)
    def _():
        n_gt = count(lambda x3, pos0: x3 > tau[None])
        need = topk - n_gt
        def key_pos(pos0, n):
            return (pos0 + lax.broadcasted_iota(jnp.int32, (n, SUBLANES, tq), 0) * SUBLANES
                    + lax.broadcasted_iota(jnp.int32, (n, SUBLANES, tq), 1))
        def pos_body(i, cut):
            cut2 = cut | lax.shift_left(jnp.int32(1), pos_bits - 1 - i)
            ties_before = count(lambda x3, pos0: (x3 == tau[None])
                                & (key_pos(pos0, FOLD_WIDTH) < cut2[None]))
            return jnp.where(ties_before < need, cut2, cut)
        cut = lax.fori_loop(0, pos_bits, pos_body, jnp.zeros((SUBLANES, tq), jnp.int32))
        def drop_body(c, carry):
            x3 = isc_ref[c].reshape(g, SUBLANES, tq)
            drop = (x3 == tau[None]) & (key_pos(c * kc, g) > cut[None])
            isc_ref[c] = jnp.where(drop, NEG_INF, x3).reshape(kc, tq)
            return carry
        lax.fori_loop(0, n_kc, drop_body, 0)

    tau_row = tau[0:1, :]

    def attend(key_pairs, bias, vt_slabs, first):
        n = key_pairs[0].shape[0]
        m_cur = []
        for p in range(n_pairs):
            s2 = lax.dot_general(key_pairs[p], qm_ref[p], nt, preferred_element_type=F32)
            for half in range(2):
                s = s2[:, half * tq:(half + 1) * tq] + bias
                s_ref[2 * p + half, :n, :] = s
                m8 = fold0(s.reshape(n // SUBLANES, SUBLANES, tq), jnp.maximum)
                m_cur.append(jnp.max(m8, axis=0, keepdims=True))
        for hd in range(A_HEADS):
            m_new = m_cur[hd] if first else jnp.maximum(m_ref[hd][0:1, :], m_cur[hd])
            e = jnp.exp2(s_ref[hd, :n, :] - m_new).astype(BF16)
            pv = None
            off = 0
            for slab in vt_slabs(hd):
                part = jnp.dot(slab, e[off:off + slab.shape[1], :], preferred_element_type=F32)
                pv = part if pv is None else pv + part
                off += slab.shape[1]
            if first:
                acc_ref[hd] = pv
            else:
                acc_ref[hd] = jnp.exp2(m_ref[hd][0:1, :] - m_new) * acc_ref[hd] + pv
            m_ref[hd] = jnp.broadcast_to(m_new, (SUBLANES, tq))

    meta_bias = jnp.where(lax.broadcasted_iota(jnp.int32, (LANES, 1), 0) < N_META, 0.0, NEG_INF)
    attend([km_ref[:, p * LANES:(p + 1) * LANES] for p in range(n_pairs)], meta_bias,
           lambda hd: [vmt_ref[hd * VT_ROWS:(hd + 1) * VT_ROWS, :]], True)

    def att_body(c, carry):
        k0 = pl.multiple_of(c * kc, kc)
        bias = jnp.where(isc_ref[c] >= tau_row, 0.0, NEG_INF)
        attend([k_ref[0, pl.ds(k0, kc), p * LANES:(p + 1) * LANES] for p in range(n_pairs)], bias,
               lambda hd: [vt_ref[c * (kc // VT_TILE) + t, hd * VT_ROWS:(hd + 1) * VT_ROWS, :]
                           for t in range(kc // VT_TILE)], False)
        return carry
    lax.fori_loop(0, n_kc, att_body, 0)

    for p in range(n_pairs):
        halves = []
        for hd in (2 * p, 2 * p + 1):
            acc = acc_ref[hd]
            halves.append(acc[:A_HEAD_DIM, :] / acc[A_HEAD_DIM:A_HEAD_DIM + 1, :])
        o_ref[0, :, p * LANES:(p + 1) * LANES] = jnp.concatenate(halves, axis=0).T.astype(BF16)


def _sparse_attention(q, k, vt, km, vmt, iq, ik, misc, *, tq, kc, topk):
    b, s, _ = q.shape
    assert tq % LANES == 0 and s % tq == 0 and kc % VT_TILE == 0 and s % kc == 0
    kernel = functools.partial(_attn_kernel, tq=tq, kc=kc, topk=topk, pos_bits=int(math.log2(s)))
    qblk = lambda bi, j: (bi, j, 0)
    full = lambda bi, j: (bi, 0, 0)
    fixed = lambda bi, j: (0, 0)
    vt_rows = A_HEADS * VT_ROWS
    return pl.pallas_call(
        kernel,
        grid=(b, s // tq),
        in_specs=[
            pl.BlockSpec((1, tq, A_WIDTH), qblk),
            pl.BlockSpec((1, s, A_WIDTH), full),
            pl.BlockSpec((s // VT_TILE, vt_rows, VT_TILE), full),
            pl.BlockSpec(km.shape, fixed),
            pl.BlockSpec(vmt.shape, fixed),
            pl.BlockSpec((1, tq, IDX_HEADS * IDX_DIM), qblk),
            pl.BlockSpec((1, s, LANES), full),
            pl.BlockSpec((1, tq, LANES), qblk),
        ],
        out_specs=pl.BlockSpec((1, tq, A_WIDTH), qblk),
        out_shape=jax.ShapeDtypeStruct((b, s, A_WIDTH), BF16),
        scratch_shapes=[
            pltpu.VMEM((s // kc, kc, tq), F32),
            pltpu.VMEM((IDX_HEADS * tq, LANES), BF16),
            pltpu.VMEM((A_WIDTH // LANES, 2 * tq, LANES), BF16),
            pltpu.VMEM((A_HEADS, max(kc, LANES), tq), F32),
            pltpu.VMEM((A_HEADS, SUBLANES, tq), F32),
            pltpu.VMEM((A_HEADS, VT_ROWS, tq), F32),
        ],
        compiler_params=pltpu.CompilerParams(dimension_semantics=("parallel", "arbitrary")),
        name="sparse_attention",
    )(q, k, vt, km, vmt, iq, ik, misc)


HALO = 8


def _softplus(x):
    return jnp.maximum(x, 0.0) + jnp.log1p(jnp.exp(-jnp.abs(x)))


def _bdot(a, b):
    return jnp.dot(a.astype(BF16), b.astype(BF16), preferred_element_type=F32)


def _hdot(a, b):
    return jnp.dot(a, b, precision=HIGHEST, preferred_element_type=F32)


def _dot3(a, b):
    a_hi = a.astype(BF16)
    b_hi = b.astype(BF16)
    a_lo = (a - a_hi.astype(F32)).astype(BF16)
    b_lo = (b - b_hi.astype(F32)).astype(BF16)
    dot = functools.partial(jnp.dot, preferred_element_type=F32)
    return dot(a_hi, b_hi) + (dot(a_hi, b_lo) + dot(a_lo, b_hi))


def _delta_kernel(qkv_ref, z_ref, misc_ref, qkvm_ref, miscm_ref, convw_ref, alog_ref, dtb_ref,
                  nw_ref, o_ref, state_ref, halo_ref):
    c = pl.program_id(1)
    n_pad = CHUNK - N_META
    nt = (((1,), (1,)), ((), ()))
    tn = (((0,), (0,)), ((), ()))

    @pl.when(c == 0)
    def _():
        state_ref[...] = jnp.zeros_like(state_ref)
        halo_ref[...] = jnp.zeros_like(halo_ref)

    ri = lax.broadcasted_iota(jnp.int32, (CHUNK, CHUNK), 0)
    ci = lax.broadcasted_iota(jnp.int32, (CHUNK, CHUNK), 1)
    incl = ri >= ci
    strict = ri > ci
    eye = (ri == ci).astype(F32)
    tri = incl.astype(F32)
    row = lax.broadcasted_iota(jnp.int32, (CHUNK, 1), 0)
    is_meta = c == 0
    neutral = jnp.logical_and(is_meta, row < n_pad)
    cols = lambda base, h: slice(base + h * B_HEAD_DIM, base + (h + 1) * B_HEAD_DIM)

    nb = qkv_ref.shape[0]
    heads = range(nb * B_HEADS)
    qn, kn, vb, kb, gc, decay = [], [], [], [], [], []
    for bi in range(nb):
        xin = jnp.where(is_meta, qkvm_ref[...], qkv_ref[bi])
        misc = jnp.where(is_meta, miscm_ref[...], misc_ref[bi])
        xcat = jnp.concatenate([halo_ref[bi], xin], axis=0)
        halo_ref[bi] = xin[CHUNK - HALO:, :]
        conv = None
        for tap in range(CONV_WIDTH):
            off = HALO - (CONV_WIDTH - 1) + tap
            term = xcat[off:off + CHUNK, :] * convw_ref[tap:tap + 1, :]
            conv = term if conv is None else conv + term
        xc = conv * jax.nn.sigmoid(conv)
        beta_all = jnp.where(neutral, 0.0, jax.nn.sigmoid(misc))
        g_all = jnp.where(neutral, 0.0, -jnp.exp(alog_ref[...]) * _softplus(misc + dtb_ref[...]))
        gc_all = _hdot(tri, g_all)
        gc_all_t = gc_all.T
        for h in range(B_HEADS):
            qh, kh, vh = xc[:, cols(0, h)], xc[:, cols(B_WIDTH, h)], xc[:, cols(2 * B_WIDTH, h)]
            qn.append(qh * lax.rsqrt(jnp.sum(qh * qh, axis=-1, keepdims=True) + EPS) * (B_HEAD_DIM ** -0.5))
            kn.append(kh * lax.rsqrt(jnp.sum(kh * kh, axis=-1, keepdims=True) + EPS))
            beta = jnp.broadcast_to(beta_all[:, MISC_BETA + h:MISC_BETA + h + 1], (CHUNK, B_HEAD_DIM))
            gc.append(jnp.broadcast_to(gc_all[:, MISC_DECAY + h:MISC_DECAY + h + 1], (CHUNK, B_HEAD_DIM)))
            g_col = gc[-1][:, :CHUNK]
            g_row = gc_all_t[MISC_DECAY + h:MISC_DECAY + h + 1, :]
            decay.append(jnp.where(incl, jnp.exp(jnp.where(incl, g_col - g_row, 0.0)), 0.0))
            kb.append(kn[-1] * beta)
            vb.append(vh * beta)

    kk = [lax.dot_general(kb[h].astype(BF16), kn[h].astype(BF16), nt, preferred_element_type=F32)
          for h in heads]
    qk = [lax.dot_general(qn[h].astype(BF16), kn[h].astype(BF16), nt, preferred_element_type=F32)
          for h in heads]
    intra = [jnp.where(incl, qk[h] * decay[h], 0.0) for h in heads]

    pw = [jnp.where(strict, -(kk[h] * decay[h]), 0.0) for h in heads]
    t_inv = [eye + pw[h] for h in heads]
    pw = [_dot3(pw[h], pw[h]) for h in heads]
    for _ in range(int(math.log2(CHUNK)) - 2):
        t_next = [t_inv[h] + _dot3(t_inv[h], pw[h]) for h in heads]
        pw = [_dot3(pw[h], pw[h]) for h in heads]
        t_inv = t_next
    t_inv = [t_inv[h] + _dot3(t_inv[h], pw[h]) for h in heads]

    u = [_bdot(t_inv[h], vb[h]) for h in heads]
    w = [_bdot(t_inv[h], kb[h] * jnp.exp(gc[h])) for h in heads]

    state = [state_ref[h] for h in heads]
    w_s = [_bdot(w[h], state[h]) for h in heads]
    q_s = [_bdot(qn[h] * jnp.exp(gc[h]), state[h]) for h in heads]
    v_new = [u[h] - w_s[h] for h in heads]
    o = [q_s[h] + _bdot(intra[h], v_new[h]) for h in heads]
    for h in heads:
        g_last = gc[h][CHUNK - 1:CHUNK, :]
        kd = kn[h] * jnp.exp(g_last - gc[h])
        state_ref[h] = state[h] * jnp.exp(g_last) + lax.dot_general(
            kd.astype(BF16), v_new[h].astype(BF16), tn, preferred_element_type=F32)

    for h in heads:
        bi, hh = divmod(h, B_HEADS)
        y = o[h] * lax.rsqrt(jnp.mean(o[h] * o[h], axis=-1, keepdims=True) + EPS) * nw_ref[...]
        zh = z_ref[bi, :, cols(0, hh)]
        o_ref[bi, :, cols(0, hh)] = (y * (zh * jax.nn.sigmoid(zh))).astype(BF16)


DELTA_BATCH = 4


def _gated_deltanet(qkv, z, misc, qkv_m, misc_m, conv_w, a_log, dt_bias, norm_w):
    b, s, _ = qkv.shape
    n_chunks = s // CHUNK + 1
    nb = DELTA_BATCH if b % DELTA_BATCH == 0 else 1
    blk = lambda bi, c: (bi, jnp.maximum(c - 1, 0), 0)
    fixed = lambda bi, c: (0, 0)
    lane_vec = lambda v: jnp.zeros((1, LANES), F32).at[0, MISC_DECAY:MISC_DECAY + B_HEADS].set(v)
    return pl.pallas_call(
        _delta_kernel,
        grid=(b // nb, n_chunks),
        in_specs=[
            pl.BlockSpec((nb, CHUNK, 3 * B_WIDTH), blk),
            pl.BlockSpec((nb, CHUNK, B_WIDTH), blk),
            pl.BlockSpec((nb, CHUNK, LANES), blk),
            pl.BlockSpec((CHUNK, 3 * B_WIDTH), fixed),
            pl.BlockSpec((CHUNK, LANES), fixed),
            pl.BlockSpec((CONV_WIDTH, 3 * B_WIDTH), fixed),
            pl.BlockSpec((1, LANES), fixed),
            pl.BlockSpec((1, LANES), fixed),
            pl.BlockSpec((1, B_HEAD_DIM), fixed),
        ],
        out_specs=pl.BlockSpec((nb, CHUNK, B_WIDTH), blk),
        out_shape=jax.ShapeDtypeStruct((b, s, B_WIDTH), BF16),
        scratch_shapes=[
            pltpu.VMEM((nb * B_HEADS, B_HEAD_DIM, B_HEAD_DIM), F32),
            pltpu.VMEM((nb, HALO, 3 * B_WIDTH), F32),
        ],
        compiler_params=pltpu.CompilerParams(dimension_semantics=("parallel", "arbitrary")),
        name="gated_deltanet",
    )(qkv, z, misc, qkv_m, misc_m, conv_w, lane_vec(a_log), lane_vec(dt_bias),
      norm_w.reshape(1, B_HEAD_DIM))


def _pack_rows(a):
    half = a.shape[1] // 2
    bits = lax.bitcast_convert_type(a.astype(BF16).astype(F32), jnp.int32)
    return bits[:, :half] | lax.shift_right_logical(bits[:, half:], 16)


def _unpack_rows(w):
    hi = lax.bitcast_convert_type(w & jnp.int32(-65536), F32)
    lo = lax.bitcast_convert_type(lax.shift_left(w, 16), F32)
    return jnp.concatenate([hi, lo], axis=1)


def _out_router_kernel(oa_ref, ob_ref, x_ref, wo_ref, nw_ref, wr_ref, br_ref,
                       h_ref, xn_ref, gate_ref, eid_ref):
    mix = (jnp.dot(oa_ref[...], wo_ref[:A_WIDTH, :], preferred_element_type=F32)
           + jnp.dot(ob_ref[...], wo_ref[A_WIDTH:, :], preferred_element_type=F32))
    h = x_ref[...] + mix
    h_ref[...] = h
    xn = h * lax.rsqrt(jnp.mean(h * h, axis=-1, keepdims=True) + EPS) * nw_ref[...]
    xn_ref[...] = _pack_rows(xn)
    tm = xn.shape[0]
    logits = lax.dot_general(wr_ref[...], xn, (((1,), (1,)), ((), ())), precision=HIGHEST,
                             preferred_element_type=F32) + br_ref[...]
    e_idx = lax.broadcasted_iota(jnp.int32, (N_EXPERTS, tm), 0).astype(F32)
    work = logits
    vals, idxs = [], []
    for _ in range(TOP_K):
        m = jnp.max(work, axis=0, keepdims=True)
        idx = jnp.min(jnp.where(work == m, e_idx, float(N_EXPERTS)), axis=0, keepdims=True)
        vals.append(m)
        idxs.append(idx)
        work = jnp.where(e_idx == idx, NEG_INF, work)
    exps = [jnp.exp(v - vals[0]) for v in vals]
    denom = exps[0]
    for e in exps[1:]:
        denom = denom + e
    eid_ref[...] = jnp.concatenate(idxs + [jnp.zeros((SUBLANES - TOP_K, tm), F32)], axis=0).astype(jnp.int32)
    gates_t = jnp.concatenate([e / denom for e in exps] + [jnp.zeros((LANES - TOP_K, tm), F32)], axis=0)
    gate_ref[...] = gates_t.T


def _out_router(o_a, o_b, x2d, w_out, norm_w, w_router, b_router, tm):
    n = x2d.shape[0]
    row = lambda i: (i, 0)
    fixed = lambda i: (0, 0)
    wr = w_router.T
    br = b_router.reshape(N_EXPERTS, 1)
    return pl.pallas_call(
        _out_router_kernel,
        grid=(n // tm,),
        in_specs=[
            pl.BlockSpec((tm, A_WIDTH), row),
            pl.BlockSpec((tm, B_WIDTH), row),
            pl.BlockSpec((tm, D_MODEL), row),
            pl.BlockSpec((A_WIDTH + B_WIDTH, D_MODEL), fixed),
            pl.BlockSpec((1, D_MODEL), fixed),
            pl.BlockSpec((N_EXPERTS, D_MODEL), fixed),
            pl.BlockSpec((N_EXPERTS, 1), fixed),
        ],
        out_specs=[pl.BlockSpec((tm, D_MODEL), row), pl.BlockSpec((tm, D_MODEL // 2), row),
                   pl.BlockSpec((tm, LANES), row), pl.BlockSpec((SUBLANES, tm), lambda i: (0, i))],
        out_shape=[jax.ShapeDtypeStruct((n, D_MODEL), F32), jax.ShapeDtypeStruct((n, D_MODEL // 2), jnp.int32),
                   jax.ShapeDtypeStruct((n, LANES), F32), jax.ShapeDtypeStruct((SUBLANES, n), jnp.int32)],
        compiler_params=pltpu.CompilerParams(dimension_semantics=("parallel",)),
        name="out_proj_router",
    )(o_a, o_b, x2d, w_out.astype(BF16), norm_w.reshape(1, D_MODEL), wr, br)


MOE_BM = 512
TOKEN_TILE = 512
COMBINE_TILE = 1024
ATT_KC = 512
ATT_TQ = 256
MOE_BF = 512


SC_CORES = 2
SC_SUBCORES = 16
SC_ROWS = 128


def _sc_gather_rows(table, idx):
    n_workers = SC_CORES * SC_SUBCORES
    n_rows = idx.shape[0]
    d = table.shape[1]
    assert n_rows % (n_workers * SC_ROWS) == 0
    rows_per_worker = n_rows // n_workers
    mesh = plsc.VectorSubcoreMesh(core_axis_name="c", subcore_axis_name="s",
                                  num_cores=SC_CORES, num_subcores=SC_SUBCORES)

    @functools.partial(
        pl.kernel, mesh=mesh,
        out_type=jax.ShapeDtypeStruct((n_rows, d), table.dtype),
        scratch_types=[pltpu.VMEM((SC_ROWS,), jnp.int32), pltpu.VMEM((SC_ROWS, d), table.dtype),
                       pltpu.SemaphoreType.DMA],
        name="sc_gather_rows",
    )
    def gather(table_hbm, idx_hbm, out_hbm, idx_v, rows_v, sem):
        wid = lax.axis_index("s") * SC_CORES + lax.axis_index("c")
        base = wid * rows_per_worker

        @pl.loop(0, rows_per_worker // SC_ROWS)
        def _(i):
            off = base + i * SC_ROWS
            pltpu.sync_copy(idx_hbm.at[pl.ds(off, SC_ROWS)], idx_v)
            pltpu.async_copy(table_hbm.at[idx_v], rows_v, sem).wait()
            pltpu.sync_copy(rows_v, out_hbm.at[pl.ds(off, SC_ROWS)])

    return gather(table, idx)


def _route_blocks(eid_t, bm):
    n_tok = eid_t.shape[1]
    n_assign = n_tok * TOP_K
    experts = jnp.arange(N_EXPERTS, dtype=jnp.int32)
    flat_e = eid_t.reshape(-1)
    sorted_e, order = lax.sort((flat_e, jnp.arange(n_assign, dtype=jnp.int32)), num_keys=1)
    onehot = sorted_e[:, None] == experts[None, :]
    counts = jnp.sum(onehot, axis=0, dtype=jnp.int32)
    padded = (counts + bm - 1) // bm * bm
    start = jnp.cumsum(counts) - counts
    pend = jnp.cumsum(padded)
    pstart = pend - padded
    dest = jnp.arange(n_assign, dtype=jnp.int32) + jnp.sum(
        jnp.where(onehot, (pstart - start)[None, :], 0), axis=1)
    n_blocks = -(-(n_assign + N_EXPERTS * (bm - 1)) // bm)
    blk_start = jnp.arange(n_blocks, dtype=jnp.int32) * bm
    block_e = jnp.minimum(jnp.sum(blk_start[:, None] >= pend[None, :], axis=1), N_EXPERTS - 1)
    n_valid = pend[-1] // bm
    _, pos = lax.sort((order, dest), num_keys=1)
    t = jnp.arange(bm, dtype=jnp.int32)[None, :]
    pad_key = jnp.where(t < (padded - counts)[:, None], (pstart + counts)[:, None] + t,
                        n_blocks * bm).reshape(-1)
    pad_tok = jnp.arange(N_EXPERTS * bm, dtype=jnp.int32) % n_tok
    assert n_blocks * bm - n_assign == N_EXPERTS * bm
    _, row_tok = lax.sort((jnp.concatenate([dest, pad_key]),
                           jnp.concatenate([order % n_tok, pad_tok])), num_keys=1)
    return block_e.astype(jnp.int32), n_valid.astype(jnp.int32).reshape(1), row_tok, pos


def _moe_dense_kernel(be_ref, nv_ref, x_ref, wgu_ref, wd_ref, bg_ref, bu_ref, bd_ref, perm_ref, y_ref,
                      wg_s, wu_s, wd_s):
    i = pl.program_id(0)
    live = i < nv_ref[0]

    @pl.when(jnp.logical_and(live, jnp.logical_or(i == 0, be_ref[i] != be_ref[jnp.maximum(i - 1, 0)])))
    def _():
        perm = perm_ref[...]
        grp = 2 * LANES
        for gidx in range(2 * D_FF // grp):
            blk = wgu_ref[0, :, gidx * grp:(gidx + 1) * grp].astype(BF16)
            split = jnp.dot(blk, perm, preferred_element_type=F32).astype(BF16)
            wg_s[:, gidx * LANES:(gidx + 1) * LANES] = split[:, :LANES]
            wu_s[:, gidx * LANES:(gidx + 1) * LANES] = split[:, LANES:]
        wd_s[...] = wd_ref[0].astype(BF16)

    @pl.when(live)
    def _():
        x = _unpack_rows(x_ref[...]).astype(BF16)
        y = None
        for f0 in range(0, D_FF, MOE_BF):
            g = jnp.dot(x, wg_s[:, f0:f0 + MOE_BF], preferred_element_type=F32) + bg_ref[0, :, f0:f0 + MOE_BF]
            u = jnp.dot(x, wu_s[:, f0:f0 + MOE_BF], preferred_element_type=F32) + bu_ref[0, :, f0:f0 + MOE_BF]
            gate = jnp.minimum(g, SWIGLU_LIMIT)
            up = jnp.clip(u, -SWIGLU_LIMIT, SWIGLU_LIMIT)
            t = gate * jax.nn.sigmoid(gate * SWIGLU_ALPHA) * (up + 1.0)
            part = jnp.dot(t.astype(BF16), wd_s[f0:f0 + MOE_BF, :], preferred_element_type=F32)
            y = part if y is None else y + part
        y_ref[...] = _pack_rows(y + bd_ref[0])


def _moe_dense(x_sorted, block_e, n_valid, w_gate_up, w_down, b_gate, b_up, b_down, bm):
    n_blocks = block_e.shape[0]
    src = jnp.arange(2 * LANES)
    perm = jax.nn.one_hot((src % 2) * LANES + src // 2, 2 * LANES, dtype=BF16)
    rows = lambda i, be, nv: (jnp.minimum(i, nv[0] - 1), 0)
    wsel = lambda i, be, nv: (be[i], 0, 0)
    fixed = lambda i, be, nv: (0, 0)
    grid_spec = pltpu.PrefetchScalarGridSpec(
        num_scalar_prefetch=2,
        grid=(n_blocks,),
        in_specs=[
            pl.BlockSpec((bm, D_MODEL // 2), rows),
            pl.BlockSpec((1, D_MODEL, 2 * D_FF), wsel),
            pl.BlockSpec((1, D_FF, D_MODEL), wsel),
            pl.BlockSpec((1, 1, D_FF), wsel),
            pl.BlockSpec((1, 1, D_FF), wsel),
            pl.BlockSpec((1, 1, D_MODEL), wsel),
            pl.BlockSpec((2 * LANES, 2 * LANES), fixed),
        ],
        out_specs=pl.BlockSpec((bm, D_MODEL // 2), rows),
        scratch_shapes=[
            pltpu.VMEM((D_MODEL, D_FF), BF16),
            pltpu.VMEM((D_MODEL, D_FF), BF16),
            pltpu.VMEM((D_FF, D_MODEL), BF16),
        ],
    )
    return pl.pallas_call(
        _moe_dense_kernel,
        grid_spec=grid_spec,
        out_shape=jax.ShapeDtypeStruct(x_sorted.shape, jnp.int32),
        compiler_params=pltpu.CompilerParams(dimension_semantics=("arbitrary",)),
        name="moe_experts",
    )(block_e, n_valid, x_sorted, w_gate_up, w_down, b_gate, b_up, b_down, perm)


def _combine_kernel(h_ref, gate_ref, y0_ref, y1_ref, y2_ref, y3_ref, o_ref):
    gates = gate_ref[...]
    out = h_ref[...]
    for kk, y_ref in enumerate((y0_ref, y1_ref, y2_ref, y3_ref)):
        out = out + gates[:, kk:kk + 1] * _unpack_rows(y_ref[...])
    o_ref[...] = out


def _combine(h, gates, y, tm):
    n_tok = h.shape[0]
    nt = n_tok // tm
    row = lambda i: (i, 0)
    return pl.pallas_call(
        _combine_kernel,
        grid=(nt,),
        in_specs=[pl.BlockSpec((tm, D_MODEL), row), pl.BlockSpec((tm, LANES), row)]
        + [pl.BlockSpec((tm, D_MODEL // 2), functools.partial(lambda kk, i: (kk * nt + i, 0), kk))
           for kk in range(TOP_K)],
        out_specs=pl.BlockSpec((tm, D_MODEL), row),
        out_shape=jax.ShapeDtypeStruct((n_tok, D_MODEL), F32),
        compiler_params=pltpu.CompilerParams(dimension_semantics=("parallel",)),
        name="moe_combine",
    )(h, gates, y, y, y, y)


def kernel(x, meta_tokens, norm_mix_w, w_in, q_norm_w, k_norm_w, kv_norm_w, w_kv_up, conv_w, a_log,
           dt_bias, delta_norm_w, w_out, norm_ffn_w, w_router, b_router, w_gate_up, b_gate_up,
           w_down, b_down):
    b, s, d = x.shape
    consts = _pack_in_proj_weights(norm_mix_w[0], w_in[0], q_norm_w[0], k_norm_w[0], kv_norm_w[0],
                                   w_kv_up[0])
    cos, sin = _rope_tables(N_META + s)
    real = _in_proj(x.reshape(b * s, d), cos[N_META:], sin[N_META:], consts, VT_TILE)
    meta = _in_proj(meta_tokens, cos[:N_META], sin[:N_META], consts, N_META)
    vt = real[2]
    q, k, iq, ik, misc, qkv, z = (a.reshape(b, s, a.shape[-1]) for a in real[:2] + real[3:])
    km = jnp.pad(meta[1], ((0, LANES - N_META), (0, 0)))
    vmt = jnp.pad(meta[2][0], ((0, 0), (0, LANES - N_META)))
    o_a = _sparse_attention(q, k, vt, km, vmt, iq, ik, misc,
                            tq=ATT_TQ, kc=min(ATT_KC, s), topk=min(INDEX_TOPK, s // 4))
    lead = lambda a: jnp.pad(a, ((CHUNK - N_META, 0), (0, 0)))
    o_b = _gated_deltanet(qkv, z, misc, lead(meta[6]), lead(meta[5]), conv_w[0], a_log[0],
                          dt_bias[0], delta_norm_w[0])
    n_tok = b * s
    h, xn, gates, eid = _out_router(o_a.reshape(n_tok, A_WIDTH), o_b.reshape(n_tok, B_WIDTH),
                                    x.reshape(n_tok, d), w_out[0], norm_ffn_w[0], w_router[0],
                                    b_router[0], TOKEN_TILE)
    block_e, n_valid, row_tok, pos = _route_blocks(eid[:TOP_K], MOE_BM)
    bgu = b_gate_up[0].reshape(N_EXPERTS, 1, 2 * D_FF)
    x_sorted = _sc_gather_rows(xn, row_tok)
    y_sorted = _moe_dense(x_sorted, block_e, n_valid, w_gate_up[0], w_down[0], bgu[:, :, 0::2],
                          bgu[:, :, 1::2], b_down[0].reshape(N_EXPERTS, 1, D_MODEL), MOE_BM)
    y = _sc_gather_rows(y_sorted, pos)
    out = _combine(h, gates, y, min(COMBINE_TILE, n_tok))
    return out.reshape(b, s, d)
```

```python
import functools
import math

import jax
import jax.numpy as jnp
from jax import lax
from jax.experimental import pallas as pl
from jax.experimental.pallas import tpu as pltpu
from jax.experimental.pallas import tpu_sc as plsc

F32 = jnp.float32
BF16 = jnp.bfloat16
HIGHEST = lax.Precision.HIGHEST

D_MODEL = 1024
N_META = 16
ROPE_THETA = 10000.0
EPS = 1e-6
A_HEAD_DIM = 64
A_HEADS = 8
A_WIDTH = A_HEADS * A_HEAD_DIM
KV_RANK = 256
IDX_HEADS = 8
IDX_DIM = 64
INDEX_TOPK = 256
B_HEAD_DIM = 128
B_HEADS = 4
B_WIDTH = B_HEADS * B_HEAD_DIM
CONV_WIDTH = 4
CHUNK = 64
N_EXPERTS = 32
TOP_K = 4
D_FF = D_MODEL
SWIGLU_LIMIT = 7.0
SWIGLU_ALPHA = 1.702
IN_SPLITS = (A_WIDTH, KV_RANK, IDX_HEADS * IDX_DIM, IDX_DIM, IDX_HEADS, 3 * B_WIDTH, B_WIDTH,
             B_HEADS, B_HEADS)

LANES = 128

C_Q = 0
C_CKV = C_Q + A_WIDTH
C_IQ = C_CKV + KV_RANK
C_IK = C_IQ + IDX_HEADS * IDX_DIM
C_MISC = C_IK + LANES
C_QKV = C_MISC + LANES
C_Z = C_QKV + 3 * B_WIDTH
C_END = C_Z + B_WIDTH
MISC_W, MISC_BETA, MISC_DECAY = 0, IDX_HEADS, IDX_HEADS + B_HEADS
LOG2E = math.log2(math.e)
BF16_SUBLANES = 16
VT_ROWS = A_HEAD_DIM + BF16_SUBLANES


def _rope_partner(a):
    lane = lax.broadcasted_iota(jnp.int32, a.shape, 1)
    first_half = (lane % A_HEAD_DIM) < (A_HEAD_DIM // 2)
    return jnp.where(first_half, pltpu.roll(a, LANES - A_HEAD_DIM // 2, 1),
                     pltpu.roll(a, A_HEAD_DIM // 2, 1))


def _rope(a, cos, sin_signed):
    return a * cos + _rope_partner(a) * sin_signed


def _head_rms(a, head_mean, gain):
    msq = jnp.dot((a * a).astype(BF16), head_mean, preferred_element_type=F32)
    return a * lax.rsqrt(msq + EPS) * gain


def _in_proj_kernel(x_ref, nw_ref, w_ref, wkv_ref, qnw_ref, knw_ref, kvnw_ref, cos_ref, sin_ref,
                    hm_ref, q_ref, k_ref, vt_ref, iq_ref, ik_ref, misc_ref, qkv_ref, z_ref):
    x = x_ref[...]
    u = x * lax.rsqrt(jnp.mean(x * x, axis=-1, keepdims=True) + EPS) * nw_ref[...]
    ub = u.astype(BF16)

    def proj(c0, c1):
        return jnp.dot(ub, w_ref[:, c0:c1], preferred_element_type=F32)

    cos = cos_ref[...]
    sin = sin_ref[...]
    hm = hm_ref[...]

    def rope_groups(a):
        return jnp.concatenate(
            [_rope(a[:, g * LANES:(g + 1) * LANES], cos, sin) for g in range(a.shape[1] // LANES)],
            axis=1)

    q = _head_rms(proj(C_Q, C_CKV), hm, qnw_ref[...])
    q_ref[...] = (rope_groups(q) * (A_HEAD_DIM ** -0.5 * LOG2E)).astype(BF16)

    ckv = proj(C_CKV, C_IQ)
    ckv = ckv * lax.rsqrt(jnp.mean(ckv * ckv, axis=-1, keepdims=True) + EPS) * kvnw_ref[...]
    kv = jnp.dot(ckv.astype(BF16), wkv_ref[...], preferred_element_type=F32)
    k = _head_rms(kv[:, :A_WIDTH], hm, knw_ref[...])
    k_ref[...] = rope_groups(k).astype(BF16)
    tm = x.shape[0]
    ones = jnp.ones((VT_ROWS - A_HEAD_DIM, tm), BF16)
    for g in range(A_WIDTH // LANES):
        vt = kv[:, A_WIDTH + g * LANES:A_WIDTH + (g + 1) * LANES].T.astype(BF16)
        for half in range(2):
            r0 = (2 * g + half) * VT_ROWS
            vt_ref[0, r0:r0 + A_HEAD_DIM, :] = vt[half * A_HEAD_DIM:(half + 1) * A_HEAD_DIM, :]
            vt_ref[0, r0 + A_HEAD_DIM:r0 + VT_ROWS, :] = ones

    iq_ref[...] = rope_groups(proj(C_IQ, C_IK)).astype(BF16)
    ik_ref[...] = _rope(proj(C_IK, C_MISC), cos, sin).astype(BF16)

    lane = lax.broadcasted_iota(jnp.int32, (1, LANES), 1)
    w_scale = jnp.where(lane < IDX_HEADS, IDX_HEADS ** -0.5 * IDX_DIM ** -0.5, 1.0)
    misc_ref[...] = proj(C_MISC, C_QKV) * w_scale
    qkv_ref[...] = proj(C_QKV, C_Z)
    z_ref[...] = proj(C_Z, C_END)


def _in_proj(x2d, cos, sin, consts, tm):
    n = x2d.shape[0]
    n_pos_blocks = cos.shape[0] // tm
    nw, w_pack, wkv, qnw, knw, kvnw, hm = consts
    row = lambda i: (i, 0)
    fixed = lambda i: (0, 0)
    pos = lambda i: (i % n_pos_blocks, 0)
    out_widths = (A_WIDTH, A_WIDTH, None, IDX_HEADS * IDX_DIM, LANES, LANES, 3 * B_WIDTH, B_WIDTH)
    out_dtypes = (BF16, BF16, BF16, BF16, BF16, F32, F32, F32)
    vt_rows = A_HEADS * VT_ROWS
    out_specs = [pl.BlockSpec((1, vt_rows, tm), lambda i: (i, 0, 0)) if w is None
                 else pl.BlockSpec((tm, w), row) for w in out_widths]
    out_shape = [jax.ShapeDtypeStruct((n // tm, vt_rows, tm) if w is None else (n, w), dt)
                 for w, dt in zip(out_widths, out_dtypes)]
    return pl.pallas_call(
        _in_proj_kernel,
        grid=(n // tm,),
        in_specs=[
            pl.BlockSpec((tm, D_MODEL), row),
            pl.BlockSpec(nw.shape, fixed),
            pl.BlockSpec(w_pack.shape, fixed),
            pl.BlockSpec(wkv.shape, fixed),
            pl.BlockSpec(qnw.shape, fixed),
            pl.BlockSpec(knw.shape, fixed),
            pl.BlockSpec(kvnw.shape, fixed),
            pl.BlockSpec((tm, LANES), pos),
            pl.BlockSpec((tm, LANES), pos),
            pl.BlockSpec(hm.shape, fixed),
        ],
        out_specs=out_specs,
        out_shape=out_shape,
        compiler_params=pltpu.CompilerParams(dimension_semantics=("parallel",)),
        name="in_proj",
    )(x2d, nw, w_pack, wkv, qnw, knw, kvnw, cos, sin, hm)


def _pack_in_proj_weights(norm_mix_w, w_in, q_norm_w, k_norm_w, kv_norm_w, w_kv_up):
    points = []
    acc = 0
    for s in IN_SPLITS:
        points.append((acc, acc + s))
        acc += s
    a_q, a_ckv, i_q, i_k, i_w, b_qkv, b_z, b_beta, b_a = (w_in[:, a:b] for a, b in points)
    misc = jnp.concatenate(
        [i_w, b_beta, b_a, jnp.zeros((D_MODEL, LANES - IDX_HEADS - 2 * B_HEADS), w_in.dtype)], axis=1)
    w_pack = jnp.concatenate([a_q, a_ckv, i_q, i_k, i_k, misc, b_qkv, b_z], axis=1).astype(BF16)
    head_mean = jnp.kron(jnp.eye(A_HEADS, dtype=F32),
                         jnp.full((A_HEAD_DIM, A_HEAD_DIM), 1.0 / A_HEAD_DIM, F32)).astype(BF16)
    return (norm_mix_w.reshape(1, D_MODEL), w_pack, w_kv_up.astype(BF16),
            jnp.tile(q_norm_w, A_HEADS).reshape(1, A_WIDTH),
            jnp.tile(k_norm_w, A_HEADS).reshape(1, A_WIDTH),
            kv_norm_w.reshape(1, KV_RANK), head_mean)


def _rope_tables(n_pos):
    half = A_HEAD_DIM // 2
    inv_freq = ROPE_THETA ** (-jnp.arange(0, A_HEAD_DIM, 2, dtype=F32) / A_HEAD_DIM)
    ang = jnp.arange(n_pos, dtype=F32)[:, None] * inv_freq[None, :]
    cos, sin = jnp.cos(ang), jnp.sin(ang)
    cos128 = jnp.tile(cos, (1, LANES // half))
    sin128 = jnp.tile(jnp.concatenate([-sin, sin], axis=1), (1, LANES // A_HEAD_DIM))
    return cos128, sin128


NEG_INF = float("-inf")
F32_MAX = float(jnp.finfo(jnp.float32).max)
INT_MIN = -2 ** 31


VT_TILE = 512
SUBLANES = 8
FOLD_WIDTH = 8


def _attn_kernel(q_ref, k_ref, vt_ref, km_ref, vmt_ref, iq_ref, ik_ref, misc_ref, o_ref,
                 isc_ref, lhs_ref, qm_ref, s_ref, m_ref, acc_ref, *, tq, kc, topk, pos_bits):
    j = pl.program_id(1)
    n_kc = lax.div((j + 1) * tq + (kc - 1), kc)
    g = kc // SUBLANES
    lane = lax.broadcasted_iota(jnp.int32, (1, LANES), 1)
    lo_half = lane < A_HEAD_DIM
    n_pairs = A_WIDTH // LANES
    nt = (((1,), (1,)), ((), ()))

    iq = iq_ref[0]
    q = q_ref[0]
    zero = jnp.zeros((), BF16)
    for p in range(n_pairs):
        blk = iq[:, p * LANES:(p + 1) * LANES]
        lhs_ref[(2 * p) * tq:(2 * p + 1) * tq, :] = jnp.where(lo_half, blk, zero)
        lhs_ref[(2 * p + 1) * tq:(2 * p + 2) * tq, :] = jnp.where(lo_half, zero, blk)
        qb = q[:, p * LANES:(p + 1) * LANES]
        qm_ref[p, :tq, :] = jnp.where(lo_half, qb, zero)
        qm_ref[p, tq:, :] = jnp.where(lo_half, zero, qb)
    w_t = misc_ref[0].T

    def fold0(x3, op):
        n = x3.shape[0]
        width = min(FOLD_WIDTH, n)
        acc = x3[:width]
        for i in range(width, n, width):
            acc = op(acc, x3[i:i + width])
        while width > 1:
            acc = op(acc[:width // 2], acc[width // 2:width])
            width //= 2
        return acc[0]

    q_pos = j * tq + lax.broadcasted_iota(jnp.int32, (1, tq), 1)

    def idx_body(c, carry):
        k0 = pl.multiple_of(c * kc, kc)
        r = lax.dot_general(ik_ref[0, pl.ds(k0, kc), :], lhs_ref[...], nt,
                            preferred_element_type=F32)
        s = None
        for h in range(IDX_HEADS):
            term = jnp.maximum(r[:, h * tq:(h + 1) * tq], 0.0) * w_t[MISC_W + h:MISC_W + h + 1, :]
            s = term if s is None else s + term
        k_pos = k0 + lax.broadcasted_iota(jnp.int32, (kc, 1), 0)
        isc_ref[c] = jnp.where(k_pos <= q_pos, s, NEG_INF)
        return carry
    lax.fori_loop(0, n_kc, idx_body, 0)

    grp = FOLD_WIDTH * SUBLANES
    def count(pred):
        def body(c, acc):
            for r0 in range(0, kc, grp):
                x3 = isc_ref[c, r0:r0 + grp, :].reshape(FOLD_WIDTH, SUBLANES, tq)
                acc = acc + jnp.where(pred(x3, c * kc + r0), 1.0, 0.0)
            return acc
        acc = lax.fori_loop(0, n_kc, body, jnp.zeros((FOLD_WIDTH, SUBLANES, tq), F32))
        total = jnp.sum(fold0(acc, jnp.add), axis=0, keepdims=True)
        return jnp.broadcast_to(total, (SUBLANES, tq))

    def key_to_float(u):
        key = u ^ jnp.int32(INT_MIN)
        bits = jnp.where(key >= 0, key, key ^ jnp.int32(0x7FFFFFFF))
        return lax.bitcast_convert_type(bits, F32)

    def bit_body(i, carry):
        u, n_u = carry
        u2 = u | lax.shift_left(jnp.int32(1), 31 - i)
        cand = key_to_float(u2)
        n_ge = count(lambda x3, pos0: x3 >= cand[None])
        keep = n_ge >= topk
        return jnp.where(keep, u2, u), jnp.where(keep, n_ge, n_u)
    u, n_u = lax.fori_loop(0, 32, bit_body, (jnp.zeros((SUBLANES, tq), jnp.int32),
                                             jnp.zeros((SUBLANES, tq), F32)))
    few = (u >= 0) & (u < 0x00800000)
    tau = jnp.where(few, -F32_MAX, key_to_float(u))

    n_ge = jnp.where(few, 0.0, n_u)

    @pl.when(jnp.max(n_ge) > topk)
    def _():
        n_gt = count(lambda x3, pos0: x3 > tau[None])
        need = topk - n_gt
        def key_pos(pos0, n):
            return (pos0 + lax.broadcasted_iota(jnp.int32, (n, SUBLANES, tq), 0) * SUBLANES
                    + lax.broadcasted_iota(jnp.int32, (n, SUBLANES, tq), 1))
        def pos_body(i, cut):
            cut2 = cut | lax.shift_left(jnp.int32(1), pos_bits - 1 - i)
            ties_before = count(lambda x3, pos0: (x3 == tau[None])
                                & (key_pos(pos0, FOLD_WIDTH) < cut2[None]))
            return jnp.where(ties_before < need, cut2, cut)
        cut = lax.fori_loop(0, pos_bits, pos_body, jnp.zeros((SUBLANES, tq), jnp.int32))
        def drop_body(c, carry):
            x3 = isc_ref[c].reshape(g, SUBLANES, tq)
            drop = (x3 == tau[None]) & (key_pos(c * kc, g) > cut[None])
            isc_ref[c] = jnp.where(drop, NEG_INF, x3).reshape(kc, tq)
            return carry
        lax.fori_loop(0, n_kc, drop_body, 0)

    tau_row = tau[0:1, :]

    def attend(key_pairs, bias, vt_slabs, first):
        n = key_pairs[0].shape[0]
        m_cur = []
        for p in range(n_pairs):
            s2 = lax.dot_general(key_pairs[p], qm_ref[p], nt, preferred_element_type=F32)
            for half in range(2):
                m_acc = None
                for r0 in range(0, n, grp):
                    piece = s2[r0:r0 + grp, half * tq:(half + 1) * tq] + bias[r0:r0 + grp]
                    s_ref[2 * p + half, r0:r0 + grp, :] = piece
                    p3 = piece.reshape(FOLD_WIDTH, SUBLANES, tq)
                    m_acc = p3 if m_acc is None else jnp.maximum(m_acc, p3)
                m_cur.append(jnp.max(fold0(m_acc, jnp.maximum), axis=0, keepdims=True))
        for hd in range(A_HEADS):
            m_new = m_cur[hd] if first else jnp.maximum(m_ref[hd][0:1, :], m_cur[hd])
            e = jnp.exp2(s_ref[hd, :n, :] - m_new).astype(BF16)
            pv = None
            off = 0
            for slab in vt_slabs(hd):
                part = jnp.dot(slab, e[off:off + slab.shape[1], :], preferred_element_type=F32)
                pv = part if pv is None else pv + part
                off += slab.shape[1]
            if first:
                acc_ref[hd] = pv
            else:
                acc_ref[hd] = jnp.exp2(m_ref[hd][0:1, :] - m_new) * acc_ref[hd] + pv
            m_ref[hd] = jnp.broadcast_to(m_new, (SUBLANES, tq))

    meta_bias = jnp.where(lax.broadcasted_iota(jnp.int32, (LANES, 1), 0) < N_META, 0.0, NEG_INF)
    attend([km_ref[:, p * LANES:(p + 1) * LANES] for p in range(n_pairs)], meta_bias,
           lambda hd: [vmt_ref[hd * VT_ROWS:(hd + 1) * VT_ROWS, :]], True)

    def att_body(c, carry):
        k0 = pl.multiple_of(c * kc, kc)
        bias = jnp.where(isc_ref[c] >= tau_row, 0.0, NEG_INF)
        attend([k_ref[0, pl.ds(k0, kc), p * LANES:(p + 1) * LANES] for p in range(n_pairs)], bias,
               lambda hd: [vt_ref[c * (kc // VT_TILE) + t, hd * VT_ROWS:(hd + 1) * VT_ROWS, :]
                           for t in range(kc // VT_TILE)], False)
        return carry
    lax.fori_loop(0, n_kc, att_body, 0)

    for p in range(n_pairs):
        halves = []
        for hd in (2 * p, 2 * p + 1):
            acc = acc_ref[hd]
            halves.append(acc[:A_HEAD_DIM, :] / acc[A_HEAD_DIM:A_HEAD_DIM + 1, :])
        o_ref[0, :, p * LANES:(p + 1) * LANES] = jnp.concatenate(halves, axis=0).T.astype(BF16)


def _sparse_attention(q, k, vt, km, vmt, iq, ik, misc, *, tq, kc, topk):
    b, s, _ = q.shape
    assert tq % LANES == 0 and s % tq == 0 and kc % VT_TILE == 0 and s % kc == 0
    kernel = functools.partial(_attn_kernel, tq=tq, kc=kc, topk=topk, pos_bits=int(math.log2(s)))
    qblk = lambda bi, j: (bi, j, 0)
    full = lambda bi, j: (bi, 0, 0)
    fixed = lambda bi, j: (0, 0)
    vt_rows = A_HEADS * VT_ROWS
    return pl.pallas_call(
        kernel,
        grid=(b, s // tq),
        in_specs=[
            pl.BlockSpec((1, tq, A_WIDTH), qblk),
            pl.BlockSpec((1, s, A_WIDTH), full),
            pl.BlockSpec((s // VT_TILE, vt_rows, VT_TILE), full),
            pl.BlockSpec(km.shape, fixed),
            pl.BlockSpec(vmt.shape, fixed),
            pl.BlockSpec((1, tq, IDX_HEADS * IDX_DIM), qblk),
            pl.BlockSpec((1, s, LANES), full),
            pl.BlockSpec((1, tq, LANES), qblk),
        ],
        out_specs=pl.BlockSpec((1, tq, A_WIDTH), qblk),
        out_shape=jax.ShapeDtypeStruct((b, s, A_WIDTH), BF16),
        scratch_shapes=[
            pltpu.VMEM((s // kc, kc, tq), F32),
            pltpu.VMEM((IDX_HEADS * tq, LANES), BF16),
            pltpu.VMEM((A_WIDTH // LANES, 2 * tq, LANES), BF16),
            pltpu.VMEM((A_HEADS, max(kc, LANES), tq), F32),
            pltpu.VMEM((A_HEADS, SUBLANES, tq), F32),
            pltpu.VMEM((A_HEADS, VT_ROWS, tq), F32),
        ],
        compiler_params=pltpu.CompilerParams(dimension_semantics=("parallel", "arbitrary")),
        name="sparse_attention",
    )(q, k, vt, km, vmt, iq, ik, misc)


HALO = 8


def _softplus(x):
    return jnp.maximum(x, 0.0) + jnp.log1p(jnp.exp(-jnp.abs(x)))


def _bdot(a, b):
    return jnp.dot(a.astype(BF16), b.astype(BF16), preferred_element_type=F32)


def _hdot(a, b):
    return jnp.dot(a, b, precision=HIGHEST, preferred_element_type=F32)


def _dot3(a, b):
    a_hi = a.astype(BF16)
    b_hi = b.astype(BF16)
    a_lo = (a - a_hi.astype(F32)).astype(BF16)
    b_lo = (b - b_hi.astype(F32)).astype(BF16)
    dot = functools.partial(jnp.dot, preferred_element_type=F32)
    return dot(a_hi, b_hi) + (dot(a_hi, b_lo) + dot(a_lo, b_hi))


def _delta_kernel(qkv_ref, z_ref, misc_ref, qkvm_ref, miscm_ref, convw_ref, alog_ref, dtb_ref,
                  nw_ref, o_ref, state_ref, halo_ref):
    c = pl.program_id(1)
    n_pad = CHUNK - N_META
    nt = (((1,), (1,)), ((), ()))
    tn = (((0,), (0,)), ((), ()))

    @pl.when(c == 0)
    def _():
        state_ref[...] = jnp.zeros_like(state_ref)
        halo_ref[...] = jnp.zeros_like(halo_ref)

    ri = lax.broadcasted_iota(jnp.int32, (CHUNK, CHUNK), 0)
    ci = lax.broadcasted_iota(jnp.int32, (CHUNK, CHUNK), 1)
    incl = ri >= ci
    strict = ri > ci
    eye = (ri == ci).astype(F32)
    tri = incl.astype(F32)
    row = lax.broadcasted_iota(jnp.int32, (CHUNK, 1), 0)
    is_meta = c == 0
    neutral = jnp.logical_and(is_meta, row < n_pad)
    cols = lambda base, h: slice(base + h * B_HEAD_DIM, base + (h + 1) * B_HEAD_DIM)

    nb = qkv_ref.shape[0]
    heads = range(nb * B_HEADS)
    qn, kn, vb, kb, gc, decay = [], [], [], [], [], []
    for bi in range(nb):
        xin = jnp.where(is_meta, qkvm_ref[...], qkv_ref[bi])
        misc = jnp.where(is_meta, miscm_ref[...], misc_ref[bi])
        xcat = jnp.concatenate([halo_ref[bi], xin], axis=0)
        halo_ref[bi] = xin[CHUNK - HALO:, :]
        conv = None
        for tap in range(CONV_WIDTH):
            off = HALO - (CONV_WIDTH - 1) + tap
            term = xcat[off:off + CHUNK, :] * convw_ref[tap:tap + 1, :]
            conv = term if conv is None else conv + term
        xc = conv * jax.nn.sigmoid(conv)
        beta_all = jnp.where(neutral, 0.0, jax.nn.sigmoid(misc))
        g_all = jnp.where(neutral, 0.0, -jnp.exp(alog_ref[...]) * _softplus(misc + dtb_ref[...]))
        gc_all = _hdot(tri, g_all)
        gc_all_t = gc_all.T
        for h in range(B_HEADS):
            qh, kh, vh = xc[:, cols(0, h)], xc[:, cols(B_WIDTH, h)], xc[:, cols(2 * B_WIDTH, h)]
            qn.append(qh * lax.rsqrt(jnp.sum(qh * qh, axis=-1, keepdims=True) + EPS) * (B_HEAD_DIM ** -0.5))
            kn.append(kh * lax.rsqrt(jnp.sum(kh * kh, axis=-1, keepdims=True) + EPS))
            beta = jnp.broadcast_to(beta_all[:, MISC_BETA + h:MISC_BETA + h + 1], (CHUNK, B_HEAD_DIM))
            gc.append(jnp.broadcast_to(gc_all[:, MISC_DECAY + h:MISC_DECAY + h + 1], (CHUNK, B_HEAD_DIM)))
            g_col = gc[-1][:, :CHUNK]
            g_row = gc_all_t[MISC_DECAY + h:MISC_DECAY + h + 1, :]
            decay.append(jnp.where(incl, jnp.exp(jnp.where(incl, g_col - g_row, 0.0)), 0.0))
            kb.append(kn[-1] * beta)
            vb.append(vh * beta)

    kk = [lax.dot_general(kb[h].astype(BF16), kn[h].astype(BF16), nt, preferred_element_type=F32)
          for h in heads]
    qk = [lax.dot_general(qn[h].astype(BF16), kn[h].astype(BF16), nt, preferred_element_type=F32)
          for h in heads]
    intra = [jnp.where(incl, qk[h] * decay[h], 0.0) for h in heads]

    pw = [jnp.where(strict, -(kk[h] * decay[h]), 0.0) for h in heads]
    t_inv = [eye + pw[h] for h in heads]
    pw = [_dot3(pw[h], pw[h]) for h in heads]
    for _ in range(int(math.log2(CHUNK)) - 2):
        t_next = [t_inv[h] + _dot3(t_inv[h], pw[h]) for h in heads]
        pw = [_dot3(pw[h], pw[h]) for h in heads]
        t_inv = t_next
    t_inv = [t_inv[h] + _dot3(t_inv[h], pw[h]) for h in heads]

    u = [_bdot(t_inv[h], vb[h]) for h in heads]
    w = [_bdot(t_inv[h], kb[h] * jnp.exp(gc[h])) for h in heads]

    state = [state_ref[h] for h in heads]
    w_s = [_bdot(w[h], state[h]) for h in heads]
    q_s = [_bdot(qn[h] * jnp.exp(gc[h]), state[h]) for h in heads]
    v_new = [u[h] - w_s[h] for h in heads]
    o = [q_s[h] + _bdot(intra[h], v_new[h]) for h in heads]
    for h in heads:
        g_last = gc[h][CHUNK - 1:CHUNK, :]
        kd = kn[h] * jnp.exp(g_last - gc[h])
        state_ref[h] = state[h] * jnp.exp(g_last) + lax.dot_general(
            kd.astype(BF16), v_new[h].astype(BF16), tn, preferred_element_type=F32)

    for h in heads:
        bi, hh = divmod(h, B_HEADS)
        y = o[h] * lax.rsqrt(jnp.mean(o[h] * o[h], axis=-1, keepdims=True) + EPS) * nw_ref[...]
        zh = z_ref[bi, :, cols(0, hh)]
        o_ref[bi, :, cols(0, hh)] = (y * (zh * jax.nn.sigmoid(zh))).astype(BF16)


DELTA_BATCH = 4


def _gated_deltanet(qkv, z, misc, qkv_m, misc_m, conv_w, a_log, dt_bias, norm_w):
    b, s, _ = qkv.shape
    n_chunks = s // CHUNK + 1
    nb = DELTA_BATCH if b % DELTA_BATCH == 0 else 1
    blk = lambda bi, c: (bi, jnp.maximum(c - 1, 0), 0)
    fixed = lambda bi, c: (0, 0)
    lane_vec = lambda v: jnp.zeros((1, LANES), F32).at[0, MISC_DECAY:MISC_DECAY + B_HEADS].set(v)
    return pl.pallas_call(
        _delta_kernel,
        grid=(b // nb, n_chunks),
        in_specs=[
            pl.BlockSpec((nb, CHUNK, 3 * B_WIDTH), blk),
            pl.BlockSpec((nb, CHUNK, B_WIDTH), blk),
            pl.BlockSpec((nb, CHUNK, LANES), blk),
            pl.BlockSpec((CHUNK, 3 * B_WIDTH), fixed),
            pl.BlockSpec((CHUNK, LANES), fixed),
            pl.BlockSpec((CONV_WIDTH, 3 * B_WIDTH), fixed),
            pl.BlockSpec((1, LANES), fixed),
            pl.BlockSpec((1, LANES), fixed),
            pl.BlockSpec((1, B_HEAD_DIM), fixed),
        ],
        out_specs=pl.BlockSpec((nb, CHUNK, B_WIDTH), blk),
        out_shape=jax.ShapeDtypeStruct((b, s, B_WIDTH), BF16),
        scratch_shapes=[
            pltpu.VMEM((nb * B_HEADS, B_HEAD_DIM, B_HEAD_DIM), F32),
            pltpu.VMEM((nb, HALO, 3 * B_WIDTH), F32),
        ],
        compiler_params=pltpu.CompilerParams(dimension_semantics=("parallel", "arbitrary")),
        name="gated_deltanet",
    )(qkv, z, misc, qkv_m, misc_m, conv_w, lane_vec(a_log), lane_vec(dt_bias),
      norm_w.reshape(1, B_HEAD_DIM))


def _pack_rows(a):
    half = a.shape[1] // 2
    bits = lax.bitcast_convert_type(a.astype(BF16).astype(F32), jnp.int32)
    return bits[:, :half] | lax.shift_right_logical(bits[:, half:], 16)


def _unpack_rows(w):
    hi = lax.bitcast_convert_type(w & jnp.int32(-65536), F32)
    lo = lax.bitcast_convert_type(lax.shift_left(w, 16), F32)
    return jnp.concatenate([hi, lo], axis=1)


def _out_router_kernel(oa_ref, ob_ref, x_ref, wo_ref, nw_ref, wr_ref, br_ref,
                       h_ref, xn_ref, gate_ref, eid_ref):
    mix = (jnp.dot(oa_ref[...], wo_ref[:A_WIDTH, :], preferred_element_type=F32)
           + jnp.dot(ob_ref[...], wo_ref[A_WIDTH:, :], preferred_element_type=F32))
    h = x_ref[...] + mix
    h_ref[...] = h
    xn = h * lax.rsqrt(jnp.mean(h * h, axis=-1, keepdims=True) + EPS) * nw_ref[...]
    xn_ref[...] = _pack_rows(xn)
    tm = xn.shape[0]
    logits = lax.dot_general(wr_ref[...], xn, (((1,), (1,)), ((), ())), precision=HIGHEST,
                             preferred_element_type=F32) + br_ref[...]
    e_idx = lax.broadcasted_iota(jnp.int32, (N_EXPERTS, tm), 0).astype(F32)
    work = logits
    vals, idxs = [], []
    for _ in range(TOP_K):
        m = jnp.max(work, axis=0, keepdims=True)
        idx = jnp.min(jnp.where(work == m, e_idx, float(N_EXPERTS)), axis=0, keepdims=True)
        vals.append(m)
        idxs.append(idx)
        work = jnp.where(e_idx == idx, NEG_INF, work)
    exps = [jnp.exp(v - vals[0]) for v in vals]
    denom = exps[0]
    for e in exps[1:]:
        denom = denom + e
    eid_ref[...] = jnp.concatenate(idxs + [jnp.zeros((SUBLANES - TOP_K, tm), F32)], axis=0).astype(jnp.int32)
    gates_t = jnp.concatenate([e / denom for e in exps] + [jnp.zeros((LANES - TOP_K, tm), F32)], axis=0)
    gate_ref[...] = gates_t.T


def _out_router(o_a, o_b, x2d, w_out, norm_w, w_router, b_router, tm):
    n = x2d.shape[0]
    row = lambda i: (i, 0)
    fixed = lambda i: (0, 0)
    wr = w_router.T
    br = b_router.reshape(N_EXPERTS, 1)
    return pl.pallas_call(
        _out_router_kernel,
        grid=(n // tm,),
        in_specs=[
            pl.BlockSpec((tm, A_WIDTH), row),
            pl.BlockSpec((tm, B_WIDTH), row),
            pl.BlockSpec((tm, D_MODEL), row),
            pl.BlockSpec((A_WIDTH + B_WIDTH, D_MODEL), fixed),
            pl.BlockSpec((1, D_MODEL), fixed),
            pl.BlockSpec((N_EXPERTS, D_MODEL), fixed),
            pl.BlockSpec((N_EXPERTS, 1), fixed),
        ],
        out_specs=[pl.BlockSpec((tm, D_MODEL), row), pl.BlockSpec((tm, D_MODEL // 2), row),
                   pl.BlockSpec((tm, LANES), row), pl.BlockSpec((SUBLANES, tm), lambda i: (0, i))],
        out_shape=[jax.ShapeDtypeStruct((n, D_MODEL), F32), jax.ShapeDtypeStruct((n, D_MODEL // 2), jnp.int32),
                   jax.ShapeDtypeStruct((n, LANES), F32), jax.ShapeDtypeStruct((SUBLANES, n), jnp.int32)],
        compiler_params=pltpu.CompilerParams(dimension_semantics=("parallel",)),
        name="out_proj_router",
    )(o_a, o_b, x2d, w_out.astype(BF16), norm_w.reshape(1, D_MODEL), wr, br)


MOE_BM = 512
TOKEN_TILE = 512
COMBINE_TILE = 1024
ATT_KC = 512
ATT_TQ = 256
MOE_BF = 512


SC_CORES = 2
SC_SUBCORES = 16
SC_ROWS = 128


def _sc_gather_rows(table, idx):
    n_workers = SC_CORES * SC_SUBCORES
    n_rows = idx.shape[0]
    d = table.shape[1]
    assert n_rows % (n_workers * SC_ROWS) == 0
    rows_per_worker = n_rows // n_workers
    mesh = plsc.VectorSubcoreMesh(core_axis_name="c", subcore_axis_name="s",
                                  num_cores=SC_CORES, num_subcores=SC_SUBCORES)

    @functools.partial(
        pl.kernel, mesh=mesh,
        out_type=jax.ShapeDtypeStruct((n_rows, d), table.dtype),
        scratch_types=[pltpu.VMEM((SC_ROWS,), jnp.int32), pltpu.VMEM((SC_ROWS, d), table.dtype),
                       pltpu.SemaphoreType.DMA],
        name="sc_gather_rows",
    )
    def gather(table_hbm, idx_hbm, out_hbm, idx_v, rows_v, sem):
        wid = lax.axis_index("s") * SC_CORES + lax.axis_index("c")
        base = wid * rows_per_worker

        @pl.loop(0, rows_per_worker // SC_ROWS)
        def _(i):
            off = base + i * SC_ROWS
            pltpu.sync_copy(idx_hbm.at[pl.ds(off, SC_ROWS)], idx_v)
            pltpu.async_copy(table_hbm.at[idx_v], rows_v, sem).wait()
            pltpu.sync_copy(rows_v, out_hbm.at[pl.ds(off, SC_ROWS)])

    return gather(table, idx)


def _route_blocks(eid_t, bm):
    n_tok = eid_t.shape[1]
    n_assign = n_tok * TOP_K
    experts = jnp.arange(N_EXPERTS, dtype=jnp.int32)
    flat_e = eid_t.reshape(-1)
    sorted_e, order = lax.sort((flat_e, jnp.arange(n_assign, dtype=jnp.int32)), num_keys=1)
    onehot = sorted_e[:, None] == experts[None, :]
    counts = jnp.sum(onehot, axis=0, dtype=jnp.int32)
    padded = (counts + bm - 1) // bm * bm
    start = jnp.cumsum(counts) - counts
    pend = jnp.cumsum(padded)
    pstart = pend - padded
    dest = jnp.arange(n_assign, dtype=jnp.int32) + jnp.sum(
        jnp.where(onehot, (pstart - start)[None, :], 0), axis=1)
    n_blocks = -(-(n_assign + N_EXPERTS * (bm - 1)) // bm)
    blk_start = jnp.arange(n_blocks, dtype=jnp.int32) * bm
    block_e = jnp.minimum(jnp.sum(blk_start[:, None] >= pend[None, :], axis=1), N_EXPERTS - 1)
    n_valid = pend[-1] // bm
    _, pos = lax.sort((order, dest), num_keys=1)
    t = jnp.arange(bm, dtype=jnp.int32)[None, :]
    pad_key = jnp.where(t < (padded - counts)[:, None], (pstart + counts)[:, None] + t,
                        n_blocks * bm).reshape(-1)
    pad_tok = jnp.arange(N_EXPERTS * bm, dtype=jnp.int32) % n_tok
    assert n_blocks * bm - n_assign == N_EXPERTS * bm
    _, row_tok = lax.sort((jnp.concatenate([dest, pad_key]),
                           jnp.concatenate([order % n_tok, pad_tok])), num_keys=1)
    return block_e.astype(jnp.int32), n_valid.astype(jnp.int32).reshape(1), row_tok, pos


def _moe_dense_kernel(be_ref, nv_ref, x_ref, wgu_ref, wd_ref, bg_ref, bu_ref, bd_ref, perm_ref, y_ref,
                      wg_s, wu_s, wd_s):
    i = pl.program_id(0)
    live = i < nv_ref[0]

    @pl.when(jnp.logical_and(live, jnp.logical_or(i == 0, be_ref[i] != be_ref[jnp.maximum(i - 1, 0)])))
    def _():
        perm = perm_ref[...]
        grp = 2 * LANES
        for gidx in range(2 * D_FF // grp):
            blk = wgu_ref[0, :, gidx * grp:(gidx + 1) * grp].astype(BF16)
            split = jnp.dot(blk, perm, preferred_element_type=F32).astype(BF16)
            wg_s[:, gidx * LANES:(gidx + 1) * LANES] = split[:, :LANES]
            wu_s[:, gidx * LANES:(gidx + 1) * LANES] = split[:, LANES:]
        wd_s[...] = wd_ref[0].astype(BF16)

    @pl.when(live)
    def _():
        x = _unpack_rows(x_ref[...]).astype(BF16)
        y = None
        for f0 in range(0, D_FF, MOE_BF):
            g = jnp.dot(x, wg_s[:, f0:f0 + MOE_BF], preferred_element_type=F32) + bg_ref[0, :, f0:f0 + MOE_BF]
            u = jnp.dot(x, wu_s[:, f0:f0 + MOE_BF], preferred_element_type=F32) + bu_ref[0, :, f0:f0 + MOE_BF]
            gate = jnp.minimum(g, SWIGLU_LIMIT)
            up = jnp.clip(u, -SWIGLU_LIMIT, SWIGLU_LIMIT)
            t = gate * jax.nn.sigmoid(gate * SWIGLU_ALPHA) * (up + 1.0)
            part = jnp.dot(t.astype(BF16), wd_s[f0:f0 + MOE_BF, :], preferred_element_type=F32)
            y = part if y is None else y + part
        y_ref[...] = _pack_rows(y + bd_ref[0])


def _moe_dense(x_sorted, block_e, n_valid, w_gate_up, w_down, b_gate, b_up, b_down, bm):
    n_blocks = block_e.shape[0]
    src = jnp.arange(2 * LANES)
    perm = jax.nn.one_hot((src % 2) * LANES + src // 2, 2 * LANES, dtype=BF16)
    rows = lambda i, be, nv: (jnp.minimum(i, nv[0] - 1), 0)
    wsel = lambda i, be, nv: (be[i], 0, 0)
    fixed = lambda i, be, nv: (0, 0)
    grid_spec = pltpu.PrefetchScalarGridSpec(
        num_scalar_prefetch=2,
        grid=(n_blocks,),
        in_specs=[
            pl.BlockSpec((bm, D_MODEL // 2), rows),
            pl.BlockSpec((1, D_MODEL, 2 * D_FF), wsel),
            pl.BlockSpec((1, D_FF, D_MODEL), wsel),
            pl.BlockSpec((1, 1, D_FF), wsel),
            pl.BlockSpec((1, 1, D_FF), wsel),
            pl.BlockSpec((1, 1, D_MODEL), wsel),
            pl.BlockSpec((2 * LANES, 2 * LANES), fixed),
        ],
        out_specs=pl.BlockSpec((bm, D_MODEL // 2), rows),
        scratch_shapes=[
            pltpu.VMEM((D_MODEL, D_FF), BF16),
            pltpu.VMEM((D_MODEL, D_FF), BF16),
            pltpu.VMEM((D_FF, D_MODEL), BF16),
        ],
    )
    return pl.pallas_call(
        _moe_dense_kernel,
        grid_spec=grid_spec,
        out_shape=jax.ShapeDtypeStruct(x_sorted.shape, jnp.int32),
        compiler_params=pltpu.CompilerParams(dimension_semantics=("arbitrary",)),
        name="moe_experts",
    )(block_e, n_valid, x_sorted, w_gate_up, w_down, b_gate, b_up, b_down, perm)


def _combine_kernel(h_ref, gate_ref, y0_ref, y1_ref, y2_ref, y3_ref, o_ref):
    gates = gate_ref[...]
    out = h_ref[...]
    for kk, y_ref in enumerate((y0_ref, y1_ref, y2_ref, y3_ref)):
        out = out + gates[:, kk:kk + 1] * _unpack_rows(y_ref[...])
    o_ref[...] = out


def _combine(h, gates, y, tm):
    n_tok = h.shape[0]
    nt = n_tok // tm
    row = lambda i: (i, 0)
    return pl.pallas_call(
        _combine_kernel,
        grid=(nt,),
        in_specs=[pl.BlockSpec((tm, D_MODEL), row), pl.BlockSpec((tm, LANES), row)]
        + [pl.BlockSpec((tm, D_MODEL // 2), functools.partial(lambda kk, i: (kk * nt + i, 0), kk))
           for kk in range(TOP_K)],
        out_specs=pl.BlockSpec((tm, D_MODEL), row),
        out_shape=jax.ShapeDtypeStruct((n_tok, D_MODEL), F32),
        compiler_params=pltpu.CompilerParams(dimension_semantics=("parallel",)),
        name="moe_combine",
    )(h, gates, y, y, y, y)


def kernel(x, meta_tokens, norm_mix_w, w_in, q_norm_w, k_norm_w, kv_norm_w, w_kv_up, conv_w, a_log,
           dt_bias, delta_norm_w, w_out, norm_ffn_w, w_router, b_router, w_gate_up, b_gate_up,
           w_down, b_down):
    b, s, d = x.shape
    consts = _pack_in_proj_weights(norm_mix_w[0], w_in[0], q_norm_w[0], k_norm_w[0], kv_norm_w[0],
                                   w_kv_up[0])
    cos, sin = _rope_tables(N_META + s)
    real = _in_proj(x.reshape(b * s, d), cos[N_META:], sin[N_META:], consts, VT_TILE)
    meta = _in_proj(meta_tokens, cos[:N_META], sin[:N_META], consts, N_META)
    vt = real[2]
    q, k, iq, ik, misc, qkv, z = (a.reshape(b, s, a.shape[-1]) for a in real[:2] + real[3:])
    km = jnp.pad(meta[1], ((0, LANES - N_META), (0, 0)))
    vmt = jnp.pad(meta[2][0], ((0, 0), (0, LANES - N_META)))
    o_a = _sparse_attention(q, k, vt, km, vmt, iq, ik, misc,
                            tq=ATT_TQ, kc=min(ATT_KC, s), topk=min(INDEX_TOPK, s // 4))
    lead = lambda a: jnp.pad(a, ((CHUNK - N_META, 0), (0, 0)))
    o_b = _gated_deltanet(qkv, z, misc, lead(meta[6]), lead(meta[5]), conv_w[0], a_log[0],
                          dt_bias[0], delta_norm_w[0])
    n_tok = b * s
    h, xn, gates, eid = _out_router(o_a.reshape(n_tok, A_WIDTH), o_b.reshape(n_tok, B_WIDTH),
                                    x.reshape(n_tok, d), w_out[0], norm_ffn_w[0], w_router[0],
                                    b_router[0], TOKEN_TILE)
    block_e, n_valid, row_tok, pos = _route_blocks(eid[:TOP_K], MOE_BM)
    bgu = b_gate_up[0].reshape(N_EXPERTS, 1, 2 * D_FF)
    x_sorted = _sc_gather_rows(xn, row_tok)
    y_sorted = _moe_dense(x_sorted, block_e, n_valid, w_gate_up[0], w_down[0], bgu[:, :, 0::2],
                          bgu[:, :, 1::2], b_down[0].reshape(N_EXPERTS, 1, D_MODEL), MOE_BM)
    y = _sc_gather_rows(y_sorted, pos)
    out = _combine(h, gates, y, min(COMBINE_TILE, n_tok))
    return out.reshape(b, s, d)
```
